```python
import jax, jax.numpy as jnp
from jax import lax
import numpy as np

D_MODEL = 2048
BATCH = 2
SEQ = 4096
DEPTH = 1
DEC_BATCH = 8
DEC_SEQ = 1
PAST_LEN = 16384
PAGE_SIZE = 128

N_HEADS = 8
N_KV_HEADS = 2
HEAD_DIM = 128
Q_PER_KV = N_HEADS // N_KV_HEADS
ATTN_WIDTH = N_HEADS * HEAD_DIM
KV_WIDTH = N_KV_HEADS * HEAD_DIM
IDX_HEADS = 16
IDX_DIM = 128
TOPK_MAX = 256
Q_BLOCK = 128
GMLP_WIDTH = 1024
GMLP_GROUPS = 8
GMLP_GROUP_DIM = GMLP_WIDTH // GMLP_GROUPS
CHUNK = 128
RMS_EPS = 1e-6
LN_EPS = 1e-5
SPLIT_SIZES = (ATTN_WIDTH, KV_WIDTH, KV_WIDTH, IDX_HEADS * IDX_DIM, IDX_HEADS, IDX_DIM, ATTN_WIDTH,
               GMLP_WIDTH, GMLP_WIDTH, GMLP_WIDTH, D_MODEL, D_MODEL)
IN_WIDTH = sum(SPLIT_SIZES)

kernel_name = "dsa_gmlp_gated_hybrid_step"


def rmsnorm(x, g):
    x32 = x.astype(jnp.float32)
    y = x32 * lax.rsqrt(jnp.mean(x32 * x32, axis=-1, keepdims=True) + RMS_EPS)
    return (y * g.astype(jnp.float32)).astype(x.dtype)


def layernorm(x, g, b):
    x32 = x.astype(jnp.float32)
    mu = jnp.mean(x32, axis=-1, keepdims=True)
    xc = x32 - mu
    y = xc * lax.rsqrt(jnp.mean(xc * xc, axis=-1, keepdims=True) + LN_EPS)
    return (y * g.astype(jnp.float32) + b.astype(jnp.float32)).astype(x.dtype)


def in_projection(x, norm_g, w_in):
    h = rmsnorm(x, norm_g) @ w_in
    offs = [int(o) for o in np.cumsum(SPLIT_SIZES)[:-1]]
    return jnp.split(h, offs, axis=-1)


def attn_heads(q, k, v, q_idx, w_idx):
    B, T = q.shape[:2]
    q = q.reshape(B, T, N_KV_HEADS, Q_PER_KV, HEAD_DIM)
    k = k.reshape(B, T, N_KV_HEADS, HEAD_DIM)
    v = v.reshape(B, T, N_KV_HEADS, HEAD_DIM)
    q_idx = q_idx.reshape(B, T, IDX_HEADS, IDX_DIM)
    w_idx = w_idx * (IDX_HEADS ** -0.5)
    return q, k, v, q_idx, w_idx


def dsa_select(q_idx, w_idx, k_idx, q_pos, top_k):
    s = jnp.einsum('bthd,bsd->bths', q_idx, k_idx, preferred_element_type=jnp.float32) * (IDX_DIM ** -0.5)
    score = jnp.einsum('bths,bth->bts', jax.nn.relu(s), w_idx.astype(jnp.float32))
    L = k_idx.shape[1]
    admissible = jnp.arange(L, dtype=jnp.int32)[None, :] <= q_pos[:, None]
    score = jnp.where(admissible[None], score, -jnp.inf)
    _, idx = lax.top_k(score, top_k)
    valid = idx <= q_pos[None, :, None]
    return idx, valid


def sparse_attend(q, k_sel, v_sel, valid):
    B, T = q.shape[:2]
    s = jnp.einsum('btkgd,btskd->btkgs', q, k_sel, preferred_element_type=jnp.float32) * (HEAD_DIM ** -0.5)
    s = jnp.where(valid[:, :, None, None, :], s, -jnp.inf)
    p = jax.nn.softmax(s, axis=-1).astype(v_sel.dtype)
    o = jnp.einsum('btkgs,btskd->btkgd', p, v_sel)
    return o.reshape(B, T, ATTN_WIDTH)


def prompt_attention(q, k, v, q_idx, w_idx, k_idx):
    B, S = q.shape[:2]
    top_k = min(TOPK_MAX, S // 4)
    nb = S // Q_BLOCK

    def to_blocks(a):
        return jnp.moveaxis(a.reshape(B, nb, Q_BLOCK, *a.shape[2:]), 1, 0)

    def one_block(args):
        qb, qib, wb, start = args
        pos = start + jnp.arange(Q_BLOCK, dtype=jnp.int32)
        idx, valid = dsa_select(qib, wb, k_idx, pos, top_k)
        k_sel = jax.vmap(lambda kb, ib: kb[ib])(k, idx)
        v_sel = jax.vmap(lambda vb, ib: vb[ib])(v, idx)
        return sparse_attend(qb, k_sel, v_sel, valid)

    starts = jnp.arange(nb, dtype=jnp.int32) * Q_BLOCK
    out = lax.map(one_block, (to_blocks(q), to_blocks(q_idx), to_blocks(w_idx), starts))
    return jnp.moveaxis(out, 0, 1).reshape(B, S, ATTN_WIDTH)


def gather_rows(pool, new, idx, page_table):
    DB = idx.shape[0]
    past = page_table.shape[1] * PAGE_SIZE
    bidx = jnp.arange(DB)[:, None, None]
    pidx = jnp.minimum(idx, past - 1)
    phys = page_table[bidx, pidx // PAGE_SIZE]
    past_rows = pool[phys, pidx % PAGE_SIZE]
    new_rows = new[bidx, jnp.clip(idx - past, 0, new.shape[1] - 1)]
    is_past = (idx < past).reshape(idx.shape + (1,) * (pool.ndim - 2))
    return jnp.where(is_past, past_rows, new_rows)


def sample_attention(q, k_new, v_new, q_idx, w_idx, k_idx_new, pool_k, pool_v, pool_k_idx, page_table):
    DB, T = q.shape[:2]
    past = page_table.shape[1] * PAGE_SIZE
    k_idx_past = pool_k_idx[page_table].reshape(DB, past, IDX_DIM)
    k_idx_all = jnp.concatenate([k_idx_past, k_idx_new.astype(k_idx_past.dtype)], axis=1)
    top_k = min(TOPK_MAX, (past + T) // 4)
    pos = past + jnp.arange(T, dtype=jnp.int32)
    idx, valid = dsa_select(q_idx, w_idx, k_idx_all, pos, top_k)
    k_sel = gather_rows(pool_k, k_new.astype(pool_k.dtype), idx, page_table)
    v_sel = gather_rows(pool_v, v_new.astype(pool_v.dtype), idx, page_table)
    return sparse_attend(q, k_sel, v_sel, valid)


def gmlp_mix(u, v, ln_g, ln_b, w_s, b_s):
    B, T, W = u.shape
    u = jax.nn.gelu(u, approximate=False)
    vn = layernorm(jax.nn.gelu(v, approximate=False), ln_g, ln_b)
    c = min(T, CHUNK)
    vc = vn.reshape(B, T // c, c, GMLP_GROUPS, GMLP_GROUP_DIM)
    wm = jnp.tril(w_s[:, :c, :c])
    mixed = jnp.einsum('gts,bcsgd->bctgd', wm, vc) + b_s[:, :c].T[None, None, :, :, None]
    return u * mixed.reshape(B, T, W), vn


def merge_out(x, o_a, z_a, o_b, z_b, g_a, g_b, w_proj_a, w_proj_b, w_out):
    y_a = (o_a * jax.nn.silu(z_a)) @ w_proj_a
    y_b = (o_b * jax.nn.silu(z_b)) @ w_proj_b
    mix = jax.nn.sigmoid(g_a) * y_a + jax.nn.sigmoid(g_b) * y_b
    return x + mix @ w_out


def setup_inputs(seed: int = 0) -> dict:
    key = jax.random.key(seed)
    ks = jax.random.split(key, 20)
    n_pages = PAST_LEN // PAGE_SIZE
    used = DEC_BATCH * n_pages
    n_pool = used + (used + 3) // 4
    f = jnp.float32
    x_prompt = jax.random.normal(ks[0], (BATCH, SEQ, D_MODEL), f)
    x_sample = jax.random.normal(ks[1], (DEC_BATCH, DEC_SEQ, D_MODEL), f)
    cache_k = jax.random.normal(ks[2], (DEPTH, n_pool, PAGE_SIZE, N_KV_HEADS, HEAD_DIM), f)
    cache_v = jax.random.normal(ks[3], (DEPTH, n_pool, PAGE_SIZE, N_KV_HEADS, HEAD_DIM), f)
    cache_k_idx = jax.random.normal(ks[4], (DEPTH, n_pool, PAGE_SIZE, IDX_DIM), f)
    page_table = jax.random.permutation(ks[5], n_pool)[:used].reshape(DEC_BATCH, n_pages).astype(jnp.int32)
    norm_in_g = 1.0 + 0.02 * jax.random.normal(ks[6], (DEPTH, D_MODEL), f)
    w_in = jax.random.normal(ks[7], (DEPTH, D_MODEL, IN_WIDTH), f) * D_MODEL ** -0.5
    w_proj_a = jax.random.normal(ks[8], (DEPTH, ATTN_WIDTH, D_MODEL), f) * ATTN_WIDTH ** -0.5
    w_proj_b = jax.random.normal(ks[9], (DEPTH, GMLP_WIDTH, D_MODEL), f) * GMLP_WIDTH ** -0.5
    w_out = jax.random.normal(ks[10], (DEPTH, D_MODEL, D_MODEL), f) * D_MODEL ** -0.5
    ln_g = 1.0 + 0.02 * jax.random.normal(ks[11], (DEPTH, GMLP_WIDTH), f)
    ln_b = 0.02 * jax.random.normal(ks[12], (DEPTH, GMLP_WIDTH), f)
    w_spatial = jax.random.normal(ks[13], (DEPTH, GMLP_GROUPS, CHUNK, CHUNK), f) * CHUNK ** -0.5
    b_spatial = 1.0 + 0.01 * jax.random.normal(ks[14], (DEPTH, GMLP_GROUPS, CHUNK), f)
    norm_f_g = 1.0 + 0.02 * jax.random.normal(ks[15], (D_MODEL,), f)
    return {"x_prompt": x_prompt, "x_sample": x_sample, "cache_k": cache_k, "cache_v": cache_v,
            "cache_k_idx": cache_k_idx, "page_table": page_table, "norm_in_g": norm_in_g, "w_in": w_in,
            "w_proj_a": w_proj_a, "w_proj_b": w_proj_b, "w_out": w_out, "ln_g": ln_g, "ln_b": ln_b,
            "w_spatial": w_spatial, "b_spatial": b_spatial, "norm_f_g": norm_f_g}


def reference(x_prompt, x_sample, cache_k, cache_v, cache_k_idx, page_table, norm_in_g, w_in,
              w_proj_a, w_proj_b, w_out, ln_g, ln_b, w_spatial, b_spatial, norm_f_g):
    xp, xs = x_prompt, x_sample
    kp_l, vp_l, kip_l, ks_l, vs_l, kis_l, gv_l = [], [], [], [], [], [], []
    for l in range(DEPTH):
        q, k, v, qi, wi, ki, za, u, vb, zb, ga, gb = in_projection(xp, norm_in_g[l], w_in[l])
        q, k, v, qi, wi = attn_heads(q, k, v, qi, wi)
        o_a = prompt_attention(q, k, v, qi, wi, ki)
        o_b, _ = gmlp_mix(u, vb, ln_g[l], ln_b[l], w_spatial[l], b_spatial[l])
        xp = merge_out(xp, o_a, za, o_b, zb, ga, gb, w_proj_a[l], w_proj_b[l], w_out[l])
        kp_l.append(k)
        vp_l.append(v)
        kip_l.append(ki)
        q, k, v, qi, wi, ki, za, u, vb, zb, ga, gb = in_projection(xs, norm_in_g[l], w_in[l])
        q, k, v, qi, wi = attn_heads(q, k, v, qi, wi)
        o_a = sample_attention(q, k, v, qi, wi, ki, cache_k[l], cache_v[l], cache_k_idx[l], page_table)
        o_b, vn = gmlp_mix(u, vb, ln_g[l], ln_b[l], w_spatial[l], b_spatial[l])
        xs = merge_out(xs, o_a, za, o_b, zb, ga, gb, w_proj_a[l], w_proj_b[l], w_out[l])
        ks_l.append(k)
        vs_l.append(v)
        kis_l.append(ki)
        gv_l.append(vn)
    y_prompt = rmsnorm(xp, norm_f_g)
    y_sample = rmsnorm(xs, norm_f_g)
    new_k_prompt = jnp.stack(kp_l)
    new_v_prompt = jnp.stack(vp_l)
    new_k_idx_prompt = jnp.stack(kip_l)
    new_k_sample = jnp.stack(ks_l)
    new_v_sample = jnp.stack(vs_l)
    new_k_idx_sample = jnp.stack(kis_l)
    new_gmlp_v_sample = jnp.stack(gv_l)
    return (y_prompt, y_sample, new_k_prompt, new_v_prompt, new_k_idx_prompt,
            new_k_sample, new_v_sample, new_k_idx_sample, new_gmlp_v_sample)
```

```python
import functools

import jax
import jax.numpy as jnp
import numpy as np
from jax import lax
from jax.experimental import pallas as pl
from jax.experimental.pallas import tpu as pltpu

F32 = jnp.float32
BF16 = jnp.bfloat16
I32 = jnp.int32

D_MODEL = 2048
N_HEADS = 8
N_KV_HEADS = 2
HEAD_DIM = 128
Q_PER_KV = N_HEADS // N_KV_HEADS
ATTN_WIDTH = N_HEADS * HEAD_DIM
KV_WIDTH = N_KV_HEADS * HEAD_DIM
IDX_HEADS = 16
IDX_DIM = 128
TOPK_MAX = 256
PAGE_SIZE = 128
GMLP_WIDTH = 1024
GMLP_GROUPS = 8
GMLP_GROUP_DIM = GMLP_WIDTH // GMLP_GROUPS
CHUNK = 128
RMS_EPS = 1e-6
LN_EPS = 1e-5

HQ_WIDTH = ATTN_WIDTH + IDX_HEADS * IDX_DIM
KVI_WIDTH = 768
KVI_K, KVI_V, KVI_KI, KVI_WI = 0, 256, 512, 640
GATE_WIDTH = 4 * GMLP_WIDTH + 2 * D_MODEL

Q_BLOCK = 128
KEY_CHUNK = 512
NEG_SENTINEL = float(np.finfo(np.float32).min)
INT_MIN = -(2 ** 31)
IDX_W_SCALE = float(IDX_HEADS ** -0.5 * IDX_DIM ** -0.5)
ATTN_SCALE = float(HEAD_DIM ** -0.5)
VMEM_LIMIT = 56 * 1024 * 1024


def _dot_nt(a, b):
    return lax.dot_general(a, b, (((1,), (1,)), ((), ())), preferred_element_type=F32)


def _ordered_bits_to_f32(u):
    bits = jnp.where(u < 0, u ^ INT_MIN, ~u)
    return lax.bitcast_convert_type(bits, F32)


def _gelu(x):
    return 0.5 * x * (1.0 + lax.erf(x * float(np.sqrt(0.5))))


def _silu(x):
    return x * jax.nn.sigmoid(x)


def _inproj_kernel(x_ref, g_ref, w_ref, o_ref, xn_ref):
    @pl.when(pl.program_id(1) == 0)
    def _():
        x = x_ref[...]
        ms = jnp.mean(x * x, axis=-1, keepdims=True)
        xn_ref[...] = ((x * lax.rsqrt(ms + RMS_EPS)) * g_ref[...]).astype(BF16)

    o_ref[...] = jnp.dot(xn_ref[...], w_ref[...], preferred_element_type=F32).astype(o_ref.dtype)


def _inproj(x2d, g, w, out_dtype, tm, tn):
    rows = x2d.shape[0]
    n = w.shape[1]
    return pl.pallas_call(
        _inproj_kernel,
        grid=(rows // tm, n // tn),
        in_specs=[
            pl.BlockSpec((tm, D_MODEL), lambda i, j: (i, 0)),
            pl.BlockSpec((1, D_MODEL), lambda i, j: (0, 0)),
            pl.BlockSpec((D_MODEL, tn), lambda i, j: (0, j)),
        ],
        out_specs=pl.BlockSpec((tm, tn), lambda i, j: (i, j)),
        out_shape=jax.ShapeDtypeStruct((rows, n), out_dtype),
        scratch_shapes=[pltpu.VMEM((tm, D_MODEL), BF16)],
        compiler_params=pltpu.CompilerParams(
            dimension_semantics=("arbitrary", "arbitrary"), vmem_limit_bytes=VMEM_LIMIT),
        name="inproj",
    )(x2d, g, w)


def _prompt_attn_kernel(q_ref, qi0_ref, qi1_ref, kvq_ref, kvf_ref, o_ref,
                        kb_ref, vb_ref, kib_ref, sc_ref, jlim_ref, m_ref, l_ref, acc_ref):
    i = pl.program_id(1)
    seq = kvf_ref.shape[0]

    @pl.when(i == 0)
    def _():
        kb_ref[...] = kvf_ref[:, KVI_K:KVI_K + KV_WIDTH].astype(BF16)
        vb_ref[...] = kvf_ref[:, KVI_V:KVI_V + KV_WIDTH].astype(BF16)
        kib_ref[...] = kvf_ref[:, KVI_KI:KVI_KI + IDX_DIM].astype(BF16)

    n_chunks = (i * Q_BLOCK) // KEY_CHUNK + 1
    t_pos = i * Q_BLOCK + lax.broadcasted_iota(I32, (Q_BLOCK, 1), 0)
    col0 = lax.broadcasted_iota(I32, (1, KEY_CHUNK), 1)

    w_idx = kvq_ref[:, KVI_WI:KVI_WI + IDX_HEADS] * IDX_W_SCALE

    def score_body(c, carry):
        k0 = pl.multiple_of(c * KEY_CHUNK, KEY_CHUNK)
        kic = kib_ref[pl.ds(k0, KEY_CHUNK), :]
        acc = jnp.zeros((Q_BLOCK, KEY_CHUNK), F32)
        for h in range(IDX_HEADS):
            qref = qi0_ref if h < IDX_HEADS // 2 else qi1_ref
            hh = h % (IDX_HEADS // 2)
            s = _dot_nt(qref[:, hh * IDX_DIM:(hh + 1) * IDX_DIM], kic)
            acc = acc + w_idx[:, h:h + 1] * jnp.maximum(s, 0.0)
        sc_ref[c] = jnp.where(k0 + col0 <= t_pos, acc, NEG_SENTINEL)
        return carry

    lax.fori_loop(0, n_chunks, score_body, 0)

    def count(pred):
        def body(c, cnt):
            return cnt + jnp.sum(pred(sc_ref[c], c).astype(F32), axis=-1, keepdims=True)
        return lax.fori_loop(0, n_chunks, body, jnp.zeros((Q_BLOCK, 1), F32))

    def bit_body(j, u):
        cand = u | lax.shift_left(jnp.int32(1), 31 - j)
        thr = _ordered_bits_to_f32(cand)
        cnt = count(lambda blk, c: blk >= thr)
        return jnp.where(cnt >= float(TOPK_MAX), cand, u)

    u = lax.fori_loop(0, 32, bit_body, jnp.zeros((Q_BLOCK, 1), I32))
    thr = _ordered_bits_to_f32(u)
    n_ge = count(lambda blk, c: blk >= thr)
    n_gt = count(lambda blk, c: blk > thr)
    need = float(TOPK_MAX) - n_gt

    jlim_ref[...] = jnp.full((Q_BLOCK, 1), 2 * seq, I32)

    @pl.when(jnp.max(n_ge) > float(TOPK_MAX))
    def _():
        def jbit_body(j, jl):
            cand = jl | lax.shift_left(jnp.int32(1), 12 - j)
            cnt = count(lambda blk, c: (blk == thr) & (c * KEY_CHUNK + col0 < cand))
            return jnp.where(cnt <= need, cand, jl)
        jlim_ref[...] = lax.fori_loop(0, 13, jbit_body, jnp.zeros((Q_BLOCK, 1), I32))

    jlim = jlim_ref[...]

    m_ref[...] = jnp.full(m_ref.shape, -jnp.inf, F32)
    l_ref[...] = jnp.zeros(l_ref.shape, F32)
    acc_ref[...] = jnp.zeros(acc_ref.shape, F32)
    rows = Q_PER_KV * Q_BLOCK
    q_groups = [
        jnp.concatenate([q_ref[:, (g * Q_PER_KV + h) * HEAD_DIM:(g * Q_PER_KV + h + 1) * HEAD_DIM]
                         for h in range(Q_PER_KV)], axis=0)
        for g in range(N_KV_HEADS)
    ]

    def attn_body(c, carry):
        k0 = pl.multiple_of(c * KEY_CHUNK, KEY_CHUNK)
        blk = sc_ref[c]
        col = k0 + col0
        sel = ((blk > thr) | ((blk == thr) & (col < jlim))) & (col <= t_pos)
        for g in range(N_KV_HEADS):
            kc = kb_ref[pl.ds(k0, KEY_CHUNK), g * HEAD_DIM:(g + 1) * HEAD_DIM]
            vc = vb_ref[pl.ds(k0, KEY_CHUNK), g * HEAD_DIM:(g + 1) * HEAD_DIM]
            s = (_dot_nt(q_groups[g], kc) * ATTN_SCALE).reshape(Q_PER_KV, Q_BLOCK, KEY_CHUNK)
            s = jnp.where(sel[None], s, -jnp.inf).reshape(rows, KEY_CHUNK)
            m_old = m_ref[g]
            m_new = jnp.maximum(m_old, jnp.max(s, axis=-1, keepdims=True))
            m_safe = jnp.where(m_new == -jnp.inf, 0.0, m_new)
            alpha = jnp.exp(m_old - m_safe)
            p = jnp.exp(s - m_safe)
            l_ref[g] = alpha * l_ref[g] + jnp.sum(p, axis=-1, keepdims=True)
            acc_ref[g] = alpha * acc_ref[g] + jnp.dot(p.astype(BF16), vc, preferred_element_type=F32)
            m_ref[g] = m_new
        return carry

    lax.fori_loop(0, n_chunks, attn_body, 0)

    for g in range(N_KV_HEADS):
        o = acc_ref[g] / l_ref[g]
        for h in range(Q_PER_KV):
            hd = (g * Q_PER_KV + h) * HEAD_DIM
            o_ref[:, hd:hd + HEAD_DIM] = o[h * Q_BLOCK:(h + 1) * Q_BLOCK, :]


def _prompt_attn(hq, kvi, batch, seq):
    nb = seq // Q_BLOCK
    rows = Q_PER_KV * Q_BLOCK
    return pl.pallas_call(
        _prompt_attn_kernel,
        grid=(batch, nb),
        in_specs=[
            pl.BlockSpec((Q_BLOCK, ATTN_WIDTH), lambda b, i: (b * nb + i, 0)),
            pl.BlockSpec((Q_BLOCK, ATTN_WIDTH), lambda b, i: (b * nb + i, 1)),
            pl.BlockSpec((Q_BLOCK, ATTN_WIDTH), lambda b, i: (b * nb + i, 2)),
            pl.BlockSpec((Q_BLOCK, KVI_WIDTH), lambda b, i: (b * nb + i, 0)),
            pl.BlockSpec((seq, KVI_WIDTH), lambda b, i: (b, 0)),
        ],
        out_specs=pl.BlockSpec((Q_BLOCK, ATTN_WIDTH), lambda b, i: (b * nb + i, 0)),
        out_shape=jax.ShapeDtypeStruct((batch * seq, ATTN_WIDTH), F32),
        scratch_shapes=[
            pltpu.VMEM((seq, KV_WIDTH), BF16),
            pltpu.VMEM((seq, KV_WIDTH), BF16),
            pltpu.VMEM((seq, IDX_DIM), BF16),
            pltpu.VMEM((seq // KEY_CHUNK, Q_BLOCK, KEY_CHUNK), F32),
            pltpu.VMEM((Q_BLOCK, 1), I32),
            pltpu.VMEM((N_KV_HEADS, rows, 1), F32),
            pltpu.VMEM((N_KV_HEADS, rows, 1), F32),
            pltpu.VMEM((N_KV_HEADS, rows, HEAD_DIM), F32),
        ],
        compiler_params=pltpu.CompilerParams(
            dimension_semantics=("arbitrary", "arbitrary"), vmem_limit_bytes=VMEM_LIMIT),
        name="prompt_attn",
    )(hq, hq, hq, kvi, kvi)


PAGES_PER_STEP = 8


def _sample_select_kernel(pt_ref, qi_ref, kvs_ref, *refs):
    page_refs = refs[:PAGES_PER_STEP]
    idx_ref, meta_ref, sc_ref, rank_ref = refs[PAGES_PER_STEP:]
    p = pl.program_id(1)
    n_steps = pl.num_programs(1)
    n_pages = n_steps * PAGES_PER_STEP

    qi = qi_ref[0]
    w_col = kvs_ref[0, :, KVI_WI:KVI_WI + 1] * IDX_W_SCALE

    def key_scores(keys_bf16):
        s = _dot_nt(qi, keys_bf16)
        return jnp.sum(w_col * jnp.maximum(s, 0.0), axis=0, keepdims=True)

    for r in range(PAGES_PER_STEP):
        row = key_scores(page_refs[r][0, 0].astype(BF16))
        sc_ref[pl.ds(p * PAGES_PER_STEP + r, 1), :] = row

    @pl.when(p == n_steps - 1)
    def _():
        sc = sc_ref[...]
        ki_new = kvs_ref[0, :, KVI_KI:KVI_KI + IDX_DIM].astype(BF16)
        s_new = key_scores(ki_new)[:, 0:1]

        def total(x):
            return jnp.sum(jnp.sum(x.astype(F32), axis=1, keepdims=True), axis=0, keepdims=True)

        def bit_body(j, u):
            cand = u | lax.shift_left(jnp.int32(1), 31 - j)
            thr = _ordered_bits_to_f32(cand)
            cnt = total(sc >= thr) + (s_new >= thr).astype(F32)
            return jnp.where(cnt >= float(TOPK_MAX), cand, u)

        u = lax.fori_loop(0, 32, bit_body, jnp.zeros((1, 1), I32))
        thr = _ordered_bits_to_f32(u)
        n_gt = total(sc > thr) + (s_new > thr).astype(F32)
        need = float(TOPK_MAX) - n_gt
        flat = (lax.broadcasted_iota(I32, sc.shape, 0) * PAGE_SIZE
                + lax.broadcasted_iota(I32, sc.shape, 1))
        tie = sc == thr

        def jbit_body(j, jl):
            cand = jl | lax.shift_left(jnp.int32(1), 14 - j)
            cnt = total(tie & (flat < cand))
            return jnp.where(cnt <= need, cand, jl)

        jlim = lax.fori_loop(0, 15, jbit_body, jnp.zeros((1, 1), I32))
        sel = (sc > thr) | (tie & (flat < jlim))
        n_pool = total(sel)

        selb = jnp.where(sel, 1.0, 0.0).astype(BF16)
        ri = lax.broadcasted_iota(I32, (PAGE_SIZE, PAGE_SIZE), 0)
        ci = lax.broadcasted_iota(I32, (PAGE_SIZE, PAGE_SIZE), 1)
        incl = jnp.dot(selb, jnp.where(ri <= ci, 1.0, 0.0).astype(BF16), preferred_element_type=F32)
        tot = jnp.broadcast_to(incl[:, PAGE_SIZE - 1:PAGE_SIZE], (n_pages, PAGE_SIZE)).astype(BF16)
        rp = lax.broadcasted_iota(I32, (n_pages, n_pages), 0)
        cp = lax.broadcasted_iota(I32, (n_pages, n_pages), 1)
        offs = jnp.dot(jnp.where(cp < rp, 1.0, 0.0).astype(BF16), tot, preferred_element_type=F32)
        rank_ref[...] = jnp.where(sel, offs + incl - 1.0, -1.0)

        slot = lax.broadcasted_iota(I32, (TOPK_MAX, PAGE_SIZE), 0).astype(F32)
        off_i = lax.broadcasted_iota(I32, (1, PAGE_SIZE), 1)

        def gather_body(pg, acc):
            rk = rank_ref[pl.ds(pg, 1), :]
            pos = (off_i + pg * PAGE_SIZE).astype(F32)
            return acc + jnp.where(rk == slot, pos, 0.0)

        acc = lax.fori_loop(0, n_pages, gather_body, jnp.zeros((TOPK_MAX, PAGE_SIZE), F32))
        idx_ref[0] = jnp.sum(acc, axis=1, keepdims=True).astype(I32)
        meta_ref[0] = jnp.broadcast_to(n_pool, (8, PAGE_SIZE)).astype(I32)


def _sample_select(page_table, qi_s, kvi_s16, pool_ki):
    db, n_pages = page_table.shape
    n_steps = n_pages // PAGES_PER_STEP

    def page_spec(r):
        return pl.BlockSpec(
            (1, 1, PAGE_SIZE, IDX_DIM),
            lambda b, p, pt: (0, pt[b, p * PAGES_PER_STEP + r], 0, 0))

    grid_spec = pltpu.PrefetchScalarGridSpec(
        num_scalar_prefetch=1,
        grid=(db, n_steps),
        in_specs=[
            pl.BlockSpec((1, IDX_HEADS, IDX_DIM), lambda b, p, pt: (b, 0, 0)),
            pl.BlockSpec((1, IDX_HEADS, KVI_WIDTH), lambda b, p, pt: (b, 0, 0)),
        ] + [page_spec(r) for r in range(PAGES_PER_STEP)],
        out_specs=[
            pl.BlockSpec((1, TOPK_MAX, 1), lambda b, p, pt: (b, 0, 0)),
            pl.BlockSpec((1, 8, PAGE_SIZE), lambda b, p, pt: (b, 0, 0)),
        ],
        scratch_shapes=[
            pltpu.VMEM((n_pages, PAGE_SIZE), F32),
            pltpu.VMEM((n_pages, PAGE_SIZE), F32),
        ],
    )
    return pl.pallas_call(
        _sample_select_kernel,
        grid_spec=grid_spec,
        out_shape=[
            jax.ShapeDtypeStruct((db, TOPK_MAX, 1), I32),
            jax.ShapeDtypeStruct((db, 8, PAGE_SIZE), I32),
        ],
        compiler_params=pltpu.CompilerParams(
            dimension_semantics=("arbitrary", "arbitrary"), vmem_limit_bytes=VMEM_LIMIT),
        name="sample_select",
    )(page_table, qi_s, kvi_s16, *([pool_ki] * PAGES_PER_STEP))


def _sample_attend_kernel(idx_ref, npool_ref, pt_ref, q_ref, kvs_ref, pk_ref, pv_ref, o_ref,
                          kbuf, vbuf, kflat, vflat, sem):
    b = pl.program_id(0)

    def row_copies(r):
        ix = idx_ref[b, r]
        page = pt_ref[b, lax.shift_right_logical(ix, 7)]
        off = ix & (PAGE_SIZE - 1)
        return (pltpu.make_async_copy(pk_ref.at[0, page, off], kbuf.at[r], sem.at[0]),
                pltpu.make_async_copy(pv_ref.at[0, page, off], vbuf.at[r], sem.at[1]))

    def start_body(r, carry):
        ck, cv = row_copies(r)
        ck.start()
        cv.start()
        return carry

    def wait_body(r, carry):
        ck, cv = row_copies(r)
        ck.wait()
        cv.wait()
        return carry

    lax.fori_loop(0, TOPK_MAX, start_body, 0)
    lax.fori_loop(0, TOPK_MAX, wait_body, 0)

    def head_copies(g):
        return (pltpu.make_async_copy(kbuf.at[:, g, :], kflat.at[g], sem.at[2]),
                pltpu.make_async_copy(vbuf.at[:, g, :], vflat.at[g], sem.at[3]))

    for g in range(N_KV_HEADS):
        for cp in head_copies(g):
            cp.start()
    for g in range(N_KV_HEADS):
        for cp in head_copies(g):
            cp.wait()

    n_pool = npool_ref[b]
    slot = lax.broadcasted_iota(I32, (1, TOPK_MAX), 1)
    slot_ok = slot < n_pool
    new_ok = n_pool < TOPK_MAX
    q_all = q_ref[0].astype(F32)
    for g in range(N_KV_HEADS):
        qg = q_all[g * Q_PER_KV:(g + 1) * Q_PER_KV, :].astype(BF16)
        ks = kflat[g].astype(BF16)
        vs = vflat[g].astype(BF16)
        k_new = kvs_ref[0, 0:1, KVI_K + g * HEAD_DIM:KVI_K + (g + 1) * HEAD_DIM].astype(BF16)
        v_new = kvs_ref[0, 0:1, KVI_V + g * HEAD_DIM:KVI_V + (g + 1) * HEAD_DIM].astype(BF16)
        s = jnp.where(slot_ok, _dot_nt(qg, ks) * ATTN_SCALE, -jnp.inf)
        s_new = jnp.sum(qg.astype(F32) * k_new.astype(F32), axis=-1, keepdims=True) * ATTN_SCALE
        s_new = jnp.where(new_ok, s_new, -jnp.inf)
        m = jnp.maximum(jnp.max(s, axis=-1, keepdims=True), s_new)
        p = jnp.exp(s - m)
        p_new = jnp.exp(s_new - m)
        denom = jnp.sum(p, axis=-1, keepdims=True) + p_new
        pn = (p / denom).astype(BF16)
        pn_new = (p_new / denom).astype(BF16).astype(F32)
        o = jnp.dot(pn, vs, preferred_element_type=F32) + pn_new * v_new.astype(F32)
        o_ref[0, g * Q_PER_KV:(g + 1) * Q_PER_KV, :] = o


def _sample_attend(idx, n_pool, page_table, q_s, kvi_s16, pool_k, pool_v):
    db = page_table.shape[0]
    grid_spec = pltpu.PrefetchScalarGridSpec(
        num_scalar_prefetch=3,
        grid=(db,),
        in_specs=[
            pl.BlockSpec((1, N_HEADS, HEAD_DIM), lambda b, *_: (b, 0, 0)),
            pl.BlockSpec((1, IDX_HEADS, KVI_WIDTH), lambda b, *_: (b, 0, 0)),
            pl.BlockSpec(memory_space=pl.ANY),
            pl.BlockSpec(memory_space=pl.ANY),
        ],
        out_specs=pl.BlockSpec((1, N_HEADS, HEAD_DIM), lambda b, *_: (b, 0, 0)),
        scratch_shapes=[
            pltpu.VMEM((TOPK_MAX, N_KV_HEADS, HEAD_DIM), F32),
            pltpu.VMEM((TOPK_MAX, N_KV_HEADS, HEAD_DIM), F32),
            pltpu.VMEM((N_KV_HEADS, TOPK_MAX, HEAD_DIM), F32),
            pltpu.VMEM((N_KV_HEADS, TOPK_MAX, HEAD_DIM), F32),
            pltpu.SemaphoreType.DMA((4,)),
        ],
    )
    return pl.pallas_call(
        _sample_attend_kernel,
        grid_spec=grid_spec,
        out_shape=jax.ShapeDtypeStruct((db, N_HEADS, HEAD_DIM), F32),
        compiler_params=pltpu.CompilerParams(dimension_semantics=("arbitrary",)),
        name="sample_attend",
    )(idx, n_pool, page_table, q_s, kvi_s16, pool_k, pool_v)


def _merge_kernel(chunked, oa_ref, za_ref, u_ref, v_ref, zb_ref, ga_ref, gb_ref, x_ref,
                  wpa_ref, wpb_ref, wout_ref, lng_ref, lnb_ref, ws_ref, bs_ref, gf_ref,
                  y_ref, vn_ref, ob_ref):
    tm = x_ref.shape[0]
    u = _gelu(u_ref[...])
    v = _gelu(v_ref[...])
    mu = jnp.mean(v, axis=-1, keepdims=True)
    vc = v - mu
    vn = (vc * lax.rsqrt(jnp.mean(vc * vc, axis=-1, keepdims=True) + LN_EPS)) * lng_ref[...] + lnb_ref[...]
    vn_ref[...] = vn

    if chunked:
        vnb = vn.astype(BF16)
        ri = lax.broadcasted_iota(I32, (CHUNK, CHUNK), 0)
        ci = lax.broadcasted_iota(I32, (CHUNK, CHUNK), 1)
        for g in range(GMLP_GROUPS):
            wm = jnp.where(ri >= ci, ws_ref[g], 0.0).astype(BF16)
            lo = g * GMLP_GROUP_DIM
            for c in range(tm // CHUNK):
                mixed = jnp.dot(wm, vnb[c * CHUNK:(c + 1) * CHUNK, lo:lo + GMLP_GROUP_DIM],
                                preferred_element_type=F32) + bs_ref[:, g:g + 1]
                ob_ref[c * CHUNK:(c + 1) * CHUNK, lo:lo + GMLP_GROUP_DIM] = (
                    u[c * CHUNK:(c + 1) * CHUNK, lo:lo + GMLP_GROUP_DIM] * mixed)
    else:
        ob_ref[...] = u * (ws_ref[...] * vn + bs_ref[...])

    ha = (oa_ref[...] * _silu(za_ref[...])).astype(BF16)
    hb = (ob_ref[...] * _silu(zb_ref[...])).astype(BF16)
    ya = jnp.dot(ha, wpa_ref[...], preferred_element_type=F32)
    yb = jnp.dot(hb, wpb_ref[...], preferred_element_type=F32)
    mix = jax.nn.sigmoid(ga_ref[...]) * ya + jax.nn.sigmoid(gb_ref[...]) * yb
    out = x_ref[...] + jnp.dot(mix.astype(BF16), wout_ref[...], preferred_element_type=F32)
    ms = jnp.mean(out * out, axis=-1, keepdims=True)
    y_ref[...] = (out * lax.rsqrt(ms + RMS_EPS)) * gf_ref[...]


def _merge(chunked, o_a, gate, x2d, wpa, wpb, wout, lng, lnb, ws, bs, gf, tm):
    rows = x2d.shape[0]
    gw = GMLP_WIDTH

    def const(shape):
        return pl.BlockSpec(shape, lambda i: (0,) * len(shape), pipeline_mode=pl.Buffered(1))

    return pl.pallas_call(
        functools.partial(_merge_kernel, chunked),
        grid=(rows // tm,),
        in_specs=[
            pl.BlockSpec((tm, ATTN_WIDTH), lambda i: (i, 0)),
            pl.BlockSpec((tm, gw), lambda i: (i, 0)),
            pl.BlockSpec((tm, gw), lambda i: (i, 1)),
            pl.BlockSpec((tm, gw), lambda i: (i, 2)),
            pl.BlockSpec((tm, gw), lambda i: (i, 3)),
            pl.BlockSpec((tm, D_MODEL), lambda i: (i, 2)),
            pl.BlockSpec((tm, D_MODEL), lambda i: (i, 3)),
            pl.BlockSpec((tm, D_MODEL), lambda i: (i, 0)),
            const(wpa.shape), const(wpb.shape), const(wout.shape),
            const(lng.shape), const(lnb.shape), const(ws.shape), const(bs.shape), const(gf.shape),
        ],
        out_specs=[
            pl.BlockSpec((tm, D_MODEL), lambda i: (i, 0)),
            pl.BlockSpec((tm, gw), lambda i: (i, 0)),
        ],
        out_shape=[
            jax.ShapeDtypeStruct((rows, D_MODEL), F32),
            jax.ShapeDtypeStruct((rows, gw), F32),
        ],
        scratch_shapes=[pltpu.VMEM((tm, gw), F32)],
        compiler_params=pltpu.CompilerParams(
            dimension_semantics=("arbitrary",), vmem_limit_bytes=VMEM_LIMIT),
        name="merge_prompt" if chunked else "merge_sample",
    )(o_a, gate, gate, gate, gate, gate, gate, x2d, wpa, wpb, wout, lng, lnb, ws, bs, gf)


def _split_weights(w_in_l):
    offs = np.cumsum([0, ATTN_WIDTH, KV_WIDTH, KV_WIDTH, IDX_HEADS * IDX_DIM, IDX_HEADS, IDX_DIM, ATTN_WIDTH,
                      GMLP_WIDTH, GMLP_WIDTH, GMLP_WIDTH, D_MODEL, D_MODEL])
    q, k, v, qi, wi, ki, za, u, vb, zb, ga, gb = [w_in_l[:, offs[n]:offs[n + 1]] for n in range(12)]
    pad = jnp.zeros((D_MODEL, KVI_WIDTH - KVI_WI - IDX_HEADS), w_in_l.dtype)
    w_hq = jnp.concatenate([q, qi], axis=1).astype(BF16)
    w_kvi = jnp.concatenate([k, v, ki, wi, pad], axis=1).astype(BF16)
    w_gate = jnp.concatenate([za, u, vb, zb, ga, gb], axis=1).astype(BF16)
    return w_hq, w_kvi, w_gate


def kernel(x_prompt, x_sample, cache_k, cache_v, cache_k_idx, page_table, norm_in_g, w_in,
           w_proj_a, w_proj_b, w_out, ln_g, ln_b, w_spatial, b_spatial, norm_f_g):
    depth = w_in.shape[0]
    assert depth == 1, "single trunk layer"
    batch, seq, _ = x_prompt.shape
    db, dseq, _ = x_sample.shape
    assert dseq == 1
    l = 0
    w_hq, w_kvi, w_gate = _split_weights(w_in[l])
    wpa = w_proj_a[l].astype(BF16)
    wpb = w_proj_b[l].astype(BF16)
    wout = w_out[l].astype(BF16)
    g_in = norm_in_g[l].reshape(1, D_MODEL)
    lng = ln_g[l].reshape(1, GMLP_WIDTH)
    lnb = ln_b[l].reshape(1, GMLP_WIDTH)
    gf = norm_f_g.reshape(1, D_MODEL)

    xp = x_prompt.reshape(batch * seq, D_MODEL)
    hq_p = _inproj(xp, g_in, w_hq, BF16, 1024, 1024)
    kvi_p = _inproj(xp, g_in, w_kvi, F32, 1024, KVI_WIDTH)
    gate_p = _inproj(xp, g_in, w_gate, F32, 1024, 1024)
    oa_p = _prompt_attn(hq_p, kvi_p, batch, seq)
    bs_t = b_spatial[l].T
    y_p, _ = _merge(True, oa_p, gate_p, xp, wpa, wpb, wout, lng, lnb, w_spatial[l], bs_t, gf, 256)

    xs = x_sample.reshape(db, D_MODEL)
    hq_s = _inproj(xs, g_in, w_hq, BF16, db, 1024)
    kvi_s = _inproj(xs, g_in, w_kvi, F32, db, KVI_WIDTH)
    gate_s = _inproj(xs, g_in, w_gate, F32, db, 1024)
    q_s = hq_s[:, :ATTN_WIDTH].reshape(db, N_HEADS, HEAD_DIM)
    qi_s = hq_s[:, ATTN_WIDTH:].reshape(db, IDX_HEADS, IDX_DIM)
    wi_col = kvi_s[:, KVI_WI:KVI_WI + IDX_HEADS].reshape(db, IDX_HEADS, 1)
    kvi_s16 = jnp.broadcast_to(kvi_s[:, None, :], (db, IDX_HEADS, KVI_WIDTH))
    kvi_s16 = lax.dynamic_update_slice(kvi_s16, wi_col, (0, 0, KVI_WI))
    idx, meta = _sample_select(page_table, qi_s, kvi_s16, cache_k_idx)
    oa_s = _sample_attend(idx.reshape(db, TOPK_MAX), meta[:, 0, 0], page_table, q_s, kvi_s16,
                          cache_k, cache_v).reshape(db, ATTN_WIDTH)
    ws0 = jnp.repeat(w_spatial[l][:, 0, 0], GMLP_GROUP_DIM).reshape(1, GMLP_WIDTH)
    bs0 = jnp.repeat(b_spatial[l][:, 0], GMLP_GROUP_DIM).reshape(1, GMLP_WIDTH)
    y_s, vn_s = _merge(False, oa_s, gate_s, xs, wpa, wpb, wout, lng, lnb, ws0, bs0, gf, db)

    def kv_out(kvi, lead):
        k = kvi[:, KVI_K:KVI_K + KV_WIDTH].reshape((1,) + lead + (N_KV_HEADS, HEAD_DIM))
        v = kvi[:, KVI_V:KVI_V + KV_WIDTH].reshape((1,) + lead + (N_KV_HEADS, HEAD_DIM))
        ki = kvi[:, KVI_KI:KVI_KI + IDX_DIM].reshape((1,) + lead + (IDX_DIM,))
        return k, v, ki

    k_p, v_p, ki_p = kv_out(kvi_p, (batch, seq))
    k_s, v_s, ki_s = kv_out(kvi_s, (db, dseq))
    return (y_p.reshape(batch, seq, D_MODEL), y_s.reshape(db, dseq, D_MODEL),
            k_p, v_p, ki_p, k_s, v_s, ki_s, vn_s.reshape(1, db, dseq, GMLP_WIDTH))
```

```python
import functools

import jax
import jax.numpy as jnp
import numpy as np
from jax import lax
from jax.experimental import pallas as pl
from jax.experimental.pallas import tpu as pltpu

F32 = jnp.float32
BF16 = jnp.bfloat16
I32 = jnp.int32

D_MODEL = 2048
N_HEADS = 8
N_KV_HEADS = 2
HEAD_DIM = 128
Q_PER_KV = N_HEADS // N_KV_HEADS
ATTN_WIDTH = N_HEADS * HEAD_DIM
KV_WIDTH = N_KV_HEADS * HEAD_DIM
IDX_HEADS = 16
IDX_DIM = 128
TOPK_MAX = 256
PAGE_SIZE = 128
GMLP_WIDTH = 1024
GMLP_GROUPS = 8
GMLP_GROUP_DIM = GMLP_WIDTH // GMLP_GROUPS
CHUNK = 128
RMS_EPS = 1e-6
LN_EPS = 1e-5

HQ_WIDTH = ATTN_WIDTH + IDX_HEADS * IDX_DIM
KVI_WIDTH = 768
KVI_K, KVI_V, KVI_KI, KVI_WI = 0, 256, 512, 640
GATE_WIDTH = 4 * GMLP_WIDTH + 2 * D_MODEL

Q_BLOCK = 128
KEY_CHUNK = 512
COUNT_ROWS = 64
NEG_SENTINEL = float(np.finfo(np.float32).min)
NEG_ABOVE_SENTINEL = float(np.nextafter(np.float32(NEG_SENTINEL), np.float32(0.0)))
INT_MIN = -(2 ** 31)
IDX_W_SCALE = float(IDX_HEADS ** -0.5 * IDX_DIM ** -0.5)
ATTN_SCALE = float(HEAD_DIM ** -0.5)
ATTN_SCALE_LOG2E = float(HEAD_DIM ** -0.5 * np.log2(np.e))
VMEM_LIMIT = 56 * 1024 * 1024


def _dot_nt(a, b):
    return lax.dot_general(a, b, (((1,), (1,)), ((), ())), preferred_element_type=F32)


def _ordered_bits_to_f32(u):
    bits = jnp.where(u < 0, u ^ INT_MIN, ~u)
    return lax.bitcast_convert_type(bits, F32)


def _gelu(x):
    return 0.5 * x * (1.0 + lax.erf(x * float(np.sqrt(0.5))))


def _silu(x):
    return x * jax.nn.sigmoid(x)


def _inproj_kernel(x_ref, g_ref, w_ref, o_ref, xn_ref):
    @pl.when(pl.program_id(1) == 0)
    def _():
        x = x_ref[...]
        ms = jnp.mean(x * x, axis=-1, keepdims=True)
        xn_ref[...] = ((x * lax.rsqrt(ms + RMS_EPS)) * g_ref[...]).astype(BF16)

    o_ref[...] = jnp.dot(xn_ref[...], w_ref[...], preferred_element_type=F32).astype(o_ref.dtype)


def _inproj(x2d, g, w, out_dtype, tm, tn):
    rows = x2d.shape[0]
    n = w.shape[1]
    return pl.pallas_call(
        _inproj_kernel,
        grid=(rows // tm, n // tn),
        in_specs=[
            pl.BlockSpec((tm, D_MODEL), lambda i, j: (i, 0)),
            pl.BlockSpec((1, D_MODEL), lambda i, j: (0, 0)),
            pl.BlockSpec((D_MODEL, tn), lambda i, j: (0, j)),
        ],
        out_specs=pl.BlockSpec((tm, tn), lambda i, j: (i, j)),
        out_shape=jax.ShapeDtypeStruct((rows, n), out_dtype),
        scratch_shapes=[pltpu.VMEM((tm, D_MODEL), BF16)],
        compiler_params=pltpu.CompilerParams(
            dimension_semantics=("arbitrary", "arbitrary"), vmem_limit_bytes=VMEM_LIMIT),
        name="inproj",
    )(x2d, g, w)


def _inproj_t_kernel(x_ref, g_ref, wt_ref, o_ref):
    x = x_ref[...]
    ms = jnp.mean(x * x, axis=-1, keepdims=True)
    xn = ((x * lax.rsqrt(ms + RMS_EPS)) * g_ref[...]).astype(BF16)
    o_ref[...] = _dot_nt(wt_ref[...], xn).astype(o_ref.dtype)


def _inproj_t(x2d, g, wt, tm):
    rows = x2d.shape[0]
    n = wt.shape[0]
    return pl.pallas_call(
        _inproj_t_kernel,
        grid=(rows // tm,),
        in_specs=[
            pl.BlockSpec((tm, D_MODEL), lambda i: (i, 0)),
            pl.BlockSpec((1, D_MODEL), lambda i: (0, 0)),
            pl.BlockSpec((n, D_MODEL), lambda i: (0, 0), pipeline_mode=pl.Buffered(1)),
        ],
        out_specs=pl.BlockSpec((n, tm), lambda i: (0, i)),
        out_shape=jax.ShapeDtypeStruct((n, rows), BF16),
        compiler_params=pltpu.CompilerParams(
            dimension_semantics=("arbitrary",), vmem_limit_bytes=VMEM_LIMIT),
        name="inproj_t",
    )(x2d, g, wt)


def _prompt_attn_kernel(qt_ref, qit0_ref, qit1_ref, vt_ref, kvq_ref, kvf_ref, o_ref,
                        kb_ref, kib_ref, vtc_ref, sc_ref, jlim_ref, bias_ref, acc_ref):
    i = pl.program_id(1)
    seq = kvf_ref.shape[0]

    @pl.when(i == 0)
    def _():
        kb_ref[...] = kvf_ref[:, KVI_K:KVI_K + KV_WIDTH].astype(BF16)
        kib_ref[...] = kvf_ref[:, KVI_KI:KVI_KI + IDX_DIM].astype(BF16)
        for c in range(seq // KEY_CHUNK):
            vtc_ref[c] = vt_ref[:, c * KEY_CHUNK:(c + 1) * KEY_CHUNK]

    n_chunks = (i * Q_BLOCK) // KEY_CHUNK + 1
    q_pos = i * Q_BLOCK + lax.broadcasted_iota(I32, (1, Q_BLOCK), 1)
    key0 = lax.broadcasted_iota(I32, (KEY_CHUNK, 1), 0)

    w_t = kvq_ref[:, KVI_WI:KVI_WI + Q_BLOCK].T[:IDX_HEADS, :] * IDX_W_SCALE
    half = IDX_HEADS // 2
    qit_all = jnp.concatenate(
        [(qit0_ref if h < half else qit1_ref)[(h % half) * IDX_DIM:(h % half + 1) * IDX_DIM, :]
         for h in range(IDX_HEADS)], axis=1)

    def score_body(c, carry):
        k0 = pl.multiple_of(c * KEY_CHUNK, KEY_CHUNK)
        res = jnp.dot(kib_ref[pl.ds(k0, KEY_CHUNK), :], qit_all, preferred_element_type=F32)
        acc = jnp.zeros((KEY_CHUNK, Q_BLOCK), F32)
        for h in range(IDX_HEADS):
            acc = acc + w_t[h:h + 1, :] * jnp.maximum(res[:, h * Q_BLOCK:(h + 1) * Q_BLOCK], 0.0)
        sc_ref[c] = jnp.where(k0 + key0 <= q_pos, acc, NEG_SENTINEL)
        return carry

    lax.fori_loop(0, n_chunks, score_body, 0)

    def count(pred):
        def body(c, cnt):
            ones = jnp.where(pred(sc_ref[c], c), 1.0, 0.0)
            return cnt + ones.reshape(KEY_CHUNK // COUNT_ROWS, COUNT_ROWS, Q_BLOCK).sum(axis=0)
        part = lax.fori_loop(0, n_chunks, body, jnp.zeros((COUNT_ROWS, Q_BLOCK), F32))
        return jnp.sum(part, axis=0, keepdims=True)

    def bit_body(j, u):
        cand = u | lax.shift_left(jnp.int32(1), 31 - j)
        thr = _ordered_bits_to_f32(cand)
        cnt = count(lambda blk, c: blk >= thr)
        return jnp.where(cnt >= float(TOPK_MAX), cand, u)

    u = lax.fori_loop(0, 32, bit_body, jnp.zeros((1, Q_BLOCK), I32))
    thr = _ordered_bits_to_f32(u)
    n_ge = count(lambda blk, c: blk >= thr)
    n_gt = count(lambda blk, c: blk > thr)
    need = float(TOPK_MAX) - n_gt
    thr_adm = jnp.maximum(thr, NEG_ABOVE_SENTINEL)
    excess = jnp.max(jnp.where((n_ge > float(TOPK_MAX)) & (thr > NEG_SENTINEL), 1.0, 0.0)) > 0.0

    jlim_ref[...] = jnp.full((1, Q_BLOCK), 2 * seq, I32)

    @pl.when(excess)
    def _():
        def jbit_body(j, jl):
            cand = jl | lax.shift_left(jnp.int32(1), 12 - j)
            cnt = count(lambda blk, c: (blk == thr) & (c * KEY_CHUNK + key0 < cand))
            return jnp.where(cnt <= need, cand, jl)
        jlim_ref[...] = lax.fori_loop(0, 13, jbit_body, jnp.zeros((1, Q_BLOCK), I32))

    jlim = jlim_ref[...]

    acc_ref[...] = jnp.zeros(acc_ref.shape, F32)
    cols = Q_PER_KV * Q_BLOCK
    qt_groups = [
        jnp.concatenate([qt_ref[(g * Q_PER_KV + h) * HEAD_DIM:(g * Q_PER_KV + h + 1) * HEAD_DIM, :]
                         for h in range(Q_PER_KV)], axis=1)
        for g in range(N_KV_HEADS)
    ]

    def attn_body(c, carry):
        k0 = pl.multiple_of(c * KEY_CHUNK, KEY_CHUNK)
        blk = sc_ref[c]

        @pl.when(jnp.logical_not(excess))
        def _():
            bias_ref[...] = jnp.where(blk >= thr_adm, 0.0, -jnp.inf)

        @pl.when(excess)
        def _():
            keep = (blk >= thr_adm) & ((blk > thr) | (k0 + key0 < jlim))
            bias_ref[...] = jnp.where(keep, 0.0, -jnp.inf)

        bias = jnp.concatenate([bias_ref[...]] * Q_PER_KV, axis=1)
        out = []
        for g in range(N_KV_HEADS):
            m_old, l_old = carry[2 * g], carry[2 * g + 1]
            kc = kb_ref[pl.ds(k0, KEY_CHUNK), g * HEAD_DIM:(g + 1) * HEAD_DIM]
            s = jnp.dot(kc, qt_groups[g], preferred_element_type=F32) * ATTN_SCALE_LOG2E + bias
            m_new = jnp.maximum(m_old, jnp.max(s, axis=0, keepdims=True))
            m_safe = jnp.where(m_new == -jnp.inf, 0.0, m_new)
            alpha = jnp.exp2(m_old - m_safe)
            p = jnp.exp2(s - m_safe)
            l_new = alpha * l_old + jnp.sum(p, axis=0, keepdims=True)
            vtc = vtc_ref[c, g * HEAD_DIM:(g + 1) * HEAD_DIM, :]
            acc_ref[g] = alpha * acc_ref[g] + jnp.dot(vtc, p.astype(BF16), preferred_element_type=F32)
            out += [m_new, l_new]
        return tuple(out)

    init = (jnp.full((1, cols), -jnp.inf, F32), jnp.zeros((1, cols), F32)) * N_KV_HEADS
    stats = lax.fori_loop(0, n_chunks, attn_body, init)

    for g in range(N_KV_HEADS):
        o = acc_ref[g] / stats[2 * g + 1]
        for h in range(Q_PER_KV):
            hd = (g * Q_PER_KV + h) * HEAD_DIM
            o_ref[:, hd:hd + HEAD_DIM] = o[:, h * Q_BLOCK:(h + 1) * Q_BLOCK].T


def _prompt_attn(hqv_t, kvi, batch, seq):
    nb = seq // Q_BLOCK
    cols = Q_PER_KV * Q_BLOCK
    return pl.pallas_call(
        _prompt_attn_kernel,
        grid=(batch, nb),
        in_specs=[
            pl.BlockSpec((ATTN_WIDTH, Q_BLOCK), lambda b, i: (0, b * nb + i)),
            pl.BlockSpec((ATTN_WIDTH, Q_BLOCK), lambda b, i: (1, b * nb + i)),
            pl.BlockSpec((ATTN_WIDTH, Q_BLOCK), lambda b, i: (2, b * nb + i)),
            pl.BlockSpec((KV_WIDTH, seq), lambda b, i: (HQ_WIDTH // KV_WIDTH, b)),
            pl.BlockSpec((Q_BLOCK, KVI_WIDTH), lambda b, i: (b * nb + i, 0)),
            pl.BlockSpec((seq, KVI_WIDTH), lambda b, i: (b, 0)),
        ],
        out_specs=pl.BlockSpec((Q_BLOCK, ATTN_WIDTH), lambda b, i: (b * nb + i, 0)),
        out_shape=jax.ShapeDtypeStruct((batch * seq, ATTN_WIDTH), F32),
        scratch_shapes=[
            pltpu.VMEM((seq, KV_WIDTH), BF16),
            pltpu.VMEM((seq, IDX_DIM), BF16),
            pltpu.VMEM((seq // KEY_CHUNK, KV_WIDTH, KEY_CHUNK), BF16),
            pltpu.VMEM((seq // KEY_CHUNK, KEY_CHUNK, Q_BLOCK), F32),
            pltpu.VMEM((1, Q_BLOCK), I32),
            pltpu.VMEM((KEY_CHUNK, Q_BLOCK), F32),
            pltpu.VMEM((N_KV_HEADS, HEAD_DIM, cols), F32),
        ],
        compiler_params=pltpu.CompilerParams(
            dimension_semantics=("arbitrary", "arbitrary"), vmem_limit_bytes=VMEM_LIMIT),
        name="prompt_attn",
    )(hqv_t, hqv_t, hqv_t, hqv_t, kvi, kvi)


PAGES_PER_STEP = 8


def _sample_select_kernel(pt_ref, qi_ref, kvs_ref, *refs):
    page_refs = refs[:PAGES_PER_STEP]
    idx_ref, meta_ref, sc_ref, rank_ref = refs[PAGES_PER_STEP:]
    p = pl.program_id(1)
    n_steps = pl.num_programs(1)
    n_pages = n_steps * PAGES_PER_STEP

    qi = qi_ref[0]
    w_col = kvs_ref[0, :, KVI_WI:KVI_WI + 1] * IDX_W_SCALE

    def key_scores(keys_bf16):
        s = _dot_nt(qi, keys_bf16)
        return jnp.sum(w_col * jnp.maximum(s, 0.0), axis=0, keepdims=True)

    for r in range(PAGES_PER_STEP):
        row = key_scores(page_refs[r][0, 0].astype(BF16))
        sc_ref[pl.ds(p * PAGES_PER_STEP + r, 1), :] = row

    @pl.when(p == n_steps - 1)
    def _():
        sc = sc_ref[...]
        ki_new = kvs_ref[0, :, KVI_KI:KVI_KI + IDX_DIM].astype(BF16)
        s_new = key_scores(ki_new)[:, 0:1]

        def total(x):
            return jnp.sum(jnp.sum(x.astype(F32), axis=1, keepdims=True), axis=0, keepdims=True)

        def bit_body(j, u):
            cand = u | lax.shift_left(jnp.int32(1), 31 - j)
            thr = _ordered_bits_to_f32(cand)
            cnt = total(sc >= thr) + (s_new >= thr).astype(F32)
            return jnp.where(cnt >= float(TOPK_MAX), cand, u)

        u = lax.fori_loop(0, 32, bit_body, jnp.zeros((1, 1), I32))
        thr = _ordered_bits_to_f32(u)
        n_gt = total(sc > thr) + (s_new > thr).astype(F32)
        need = float(TOPK_MAX) - n_gt
        flat = (lax.broadcasted_iota(I32, sc.shape, 0) * PAGE_SIZE
                + lax.broadcasted_iota(I32, sc.shape, 1))
        tie = sc == thr

        def jbit_body(j, jl):
            cand = jl | lax.shift_left(jnp.int32(1), 14 - j)
            cnt = total(tie & (flat < cand))
            return jnp.where(cnt <= need, cand, jl)

        jlim = lax.fori_loop(0, 15, jbit_body, jnp.zeros((1, 1), I32))
        sel = (sc > thr) | (tie & (flat < jlim))
        n_pool = total(sel)

        selb = jnp.where(sel, 1.0, 0.0).astype(BF16)
        ri = lax.broadcasted_iota(I32, (PAGE_SIZE, PAGE_SIZE), 0)
        ci = lax.broadcasted_iota(I32, (PAGE_SIZE, PAGE_SIZE), 1)
        incl = jnp.dot(selb, jnp.where(ri <= ci, 1.0, 0.0).astype(BF16), preferred_element_type=F32)
        tot = jnp.broadcast_to(incl[:, PAGE_SIZE - 1:PAGE_SIZE], (n_pages, PAGE_SIZE)).astype(BF16)
        rp = lax.broadcasted_iota(I32, (n_pages, n_pages), 0)
        cp = lax.broadcasted_iota(I32, (n_pages, n_pages), 1)
        offs = jnp.dot(jnp.where(cp < rp, 1.0, 0.0).astype(BF16), tot, preferred_element_type=F32)
        rank_ref[...] = jnp.where(sel, offs + incl - 1.0, -1.0)

        slot = lax.broadcasted_iota(I32, (TOPK_MAX, PAGE_SIZE), 0).astype(F32)
        off_i = lax.broadcasted_iota(I32, (1, PAGE_SIZE), 1)

        def gather_body(pg, acc):
            rk = rank_ref[pl.ds(pg, 1), :]
            pos = (off_i + pg * PAGE_SIZE).astype(F32)
            return acc + jnp.where(rk == slot, pos, 0.0)

        acc = lax.fori_loop(0, n_pages, gather_body, jnp.zeros((TOPK_MAX, PAGE_SIZE), F32))
        idx_ref[0] = jnp.sum(acc, axis=1, keepdims=True).astype(I32)
        meta_ref[0] = jnp.broadcast_to(n_pool, (8, PAGE_SIZE)).astype(I32)


def _sample_select(page_table, qi_s, kvi_s16, pool_ki):
    db, n_pages = page_table.shape
    n_steps = n_pages // PAGES_PER_STEP

    def page_spec(r):
        return pl.BlockSpec(
            (1, 1, PAGE_SIZE, IDX_DIM),
            lambda b, p, pt: (0, pt[b, p * PAGES_PER_STEP + r], 0, 0))

    grid_spec = pltpu.PrefetchScalarGridSpec(
        num_scalar_prefetch=1,
        grid=(db, n_steps),
        in_specs=[
            pl.BlockSpec((1, IDX_HEADS, IDX_DIM), lambda b, p, pt: (b, 0, 0)),
            pl.BlockSpec((1, IDX_HEADS, KVI_WIDTH), lambda b, p, pt: (b, 0, 0)),
        ] + [page_spec(r) for r in range(PAGES_PER_STEP)],
        out_specs=[
            pl.BlockSpec((1, TOPK_MAX, 1), lambda b, p, pt: (b, 0, 0)),
            pl.BlockSpec((1, 8, PAGE_SIZE), lambda b, p, pt: (b, 0, 0)),
        ],
        scratch_shapes=[
            pltpu.VMEM((n_pages, PAGE_SIZE), F32),
            pltpu.VMEM((n_pages, PAGE_SIZE), F32),
        ],
    )
    return pl.pallas_call(
        _sample_select_kernel,
        grid_spec=grid_spec,
        out_shape=[
            jax.ShapeDtypeStruct((db, TOPK_MAX, 1), I32),
            jax.ShapeDtypeStruct((db, 8, PAGE_SIZE), I32),
        ],
        compiler_params=pltpu.CompilerParams(
            dimension_semantics=("arbitrary", "arbitrary"), vmem_limit_bytes=VMEM_LIMIT),
        name="sample_select",
    )(page_table, qi_s, kvi_s16, *([pool_ki] * PAGES_PER_STEP))


def _sample_attend_kernel(idx_ref, npool_ref, pt_ref, q_ref, kvs_ref, pk_ref, pv_ref, o_ref,
                          kbuf, vbuf, kflat, vflat, sem):
    b = pl.program_id(0)

    def row_copies(r):
        ix = idx_ref[b, r]
        page = pt_ref[b, lax.shift_right_logical(ix, 7)]
        off = ix & (PAGE_SIZE - 1)
        return (pltpu.make_async_copy(pk_ref.at[0, page, off], kbuf.at[r], sem.at[0]),
                pltpu.make_async_copy(pv_ref.at[0, page, off], vbuf.at[r], sem.at[1]))

    def start_body(r, carry):
        ck, cv = row_copies(r)
        ck.start()
        cv.start()
        return carry

    def wait_body(r, carry):
        ck, cv = row_copies(r)
        ck.wait()
        cv.wait()
        return carry

    lax.fori_loop(0, TOPK_MAX, start_body, 0)
    lax.fori_loop(0, TOPK_MAX, wait_body, 0)

    def head_copies(g):
        return (pltpu.make_async_copy(kbuf.at[:, g, :], kflat.at[g], sem.at[2]),
                pltpu.make_async_copy(vbuf.at[:, g, :], vflat.at[g], sem.at[3]))

    for g in range(N_KV_HEADS):
        for cp in head_copies(g):
            cp.start()
    for g in range(N_KV_HEADS):
        for cp in head_copies(g):
            cp.wait()

    n_pool = npool_ref[b]
    slot = lax.broadcasted_iota(I32, (1, TOPK_MAX), 1)
    slot_ok = slot < n_pool
    new_ok = n_pool < TOPK_MAX
    q_all = q_ref[0].astype(F32)
    for g in range(N_KV_HEADS):
        qg = q_all[g * Q_PER_KV:(g + 1) * Q_PER_KV, :].astype(BF16)
        ks = kflat[g].astype(BF16)
        vs = vflat[g].astype(BF16)
        k_new = kvs_ref[0, 0:1, KVI_K + g * HEAD_DIM:KVI_K + (g + 1) * HEAD_DIM].astype(BF16)
        v_new = kvs_ref[0, 0:1, KVI_V + g * HEAD_DIM:KVI_V + (g + 1) * HEAD_DIM].astype(BF16)
        s = jnp.where(slot_ok, _dot_nt(qg, ks) * ATTN_SCALE, -jnp.inf)
        s_new = jnp.sum(qg.astype(F32) * k_new.astype(F32), axis=-1, keepdims=True) * ATTN_SCALE
        s_new = jnp.where(new_ok, s_new, -jnp.inf)
        m = jnp.maximum(jnp.max(s, axis=-1, keepdims=True), s_new)
        p = jnp.exp(s - m)
        p_new = jnp.exp(s_new - m)
        denom = jnp.sum(p, axis=-1, keepdims=True) + p_new
        pn = (p / denom).astype(BF16)
        pn_new = (p_new / denom).astype(BF16).astype(F32)
        o = jnp.dot(pn, vs, preferred_element_type=F32) + pn_new * v_new.astype(F32)
        o_ref[0, g * Q_PER_KV:(g + 1) * Q_PER_KV, :] = o


def _sample_attend(idx, n_pool, page_table, q_s, kvi_s16, pool_k, pool_v):
    db = page_table.shape[0]
    grid_spec = pltpu.PrefetchScalarGridSpec(
        num_scalar_prefetch=3,
        grid=(db,),
        in_specs=[
            pl.BlockSpec((1, N_HEADS, HEAD_DIM), lambda b, *_: (b, 0, 0)),
            pl.BlockSpec((1, IDX_HEADS, KVI_WIDTH), lambda b, *_: (b, 0, 0)),
            pl.BlockSpec(memory_space=pl.ANY),
            pl.BlockSpec(memory_space=pl.ANY),
        ],
        out_specs=pl.BlockSpec((1, N_HEADS, HEAD_DIM), lambda b, *_: (b, 0, 0)),
        scratch_shapes=[
            pltpu.VMEM((TOPK_MAX, N_KV_HEADS, HEAD_DIM), F32),
            pltpu.VMEM((TOPK_MAX, N_KV_HEADS, HEAD_DIM), F32),
            pltpu.VMEM((N_KV_HEADS, TOPK_MAX, HEAD_DIM), F32),
            pltpu.VMEM((N_KV_HEADS, TOPK_MAX, HEAD_DIM), F32),
            pltpu.SemaphoreType.DMA((4,)),
        ],
    )
    return pl.pallas_call(
        _sample_attend_kernel,
        grid_spec=grid_spec,
        out_shape=jax.ShapeDtypeStruct((db, N_HEADS, HEAD_DIM), F32),
        compiler_params=pltpu.CompilerParams(dimension_semantics=("arbitrary",)),
        name="sample_attend",
    )(idx, n_pool, page_table, q_s, kvi_s16, pool_k, pool_v)


def _merge_kernel(chunked, oa_ref, za_ref, u_ref, v_ref, zb_ref, ga_ref, gb_ref, x_ref,
                  wpa_ref, wpb_ref, wout_ref, lng_ref, lnb_ref, ws_ref, bs_ref, gf_ref,
                  y_ref, vn_ref, ob_ref):
    tm = x_ref.shape[0]
    u = _gelu(u_ref[...])
    v = _gelu(v_ref[...])
    mu = jnp.mean(v, axis=-1, keepdims=True)
    vc = v - mu
    vn = (vc * lax.rsqrt(jnp.mean(vc * vc, axis=-1, keepdims=True) + LN_EPS)) * lng_ref[...] + lnb_ref[...]
    vn_ref[...] = vn

    if chunked:
        vnb = vn.astype(BF16)
        ri = lax.broadcasted_iota(I32, (CHUNK, CHUNK), 0)
        ci = lax.broadcasted_iota(I32, (CHUNK, CHUNK), 1)
        for g in range(GMLP_GROUPS):
            wm = jnp.where(ri >= ci, ws_ref[g], 0.0).astype(BF16)
            lo = g * GMLP_GROUP_DIM
            for c in range(tm // CHUNK):
                mixed = jnp.dot(wm, vnb[c * CHUNK:(c + 1) * CHUNK, lo:lo + GMLP_GROUP_DIM],
                                preferred_element_type=F32) + bs_ref[:, g:g + 1]
                ob_ref[c * CHUNK:(c + 1) * CHUNK, lo:lo + GMLP_GROUP_DIM] = (
                    u[c * CHUNK:(c + 1) * CHUNK, lo:lo + GMLP_GROUP_DIM] * mixed)
    else:
        ob_ref[...] = u * (ws_ref[...] * vn + bs_ref[...])

    ha = (oa_ref[...] * _silu(za_ref[...])).astype(BF16)
    hb = (ob_ref[...] * _silu(zb_ref[...])).astype(BF16)
    ya = jnp.dot(ha, wpa_ref[...], preferred_element_type=F32)
    yb = jnp.dot(hb, wpb_ref[...], preferred_element_type=F32)
    mix = jax.nn.sigmoid(ga_ref[...]) * ya + jax.nn.sigmoid(gb_ref[...]) * yb
    out = x_ref[...] + jnp.dot(mix.astype(BF16), wout_ref[...], preferred_element_type=F32)
    ms = jnp.mean(out * out, axis=-1, keepdims=True)
    y_ref[...] = (out * lax.rsqrt(ms + RMS_EPS)) * gf_ref[...]


def _merge(chunked, o_a, gate, x2d, wpa, wpb, wout, lng, lnb, ws, bs, gf, tm):
    rows = x2d.shape[0]
    gw = GMLP_WIDTH

    def const(shape):
        return pl.BlockSpec(shape, lambda i: (0,) * len(shape), pipeline_mode=pl.Buffered(1))

    return pl.pallas_call(
        functools.partial(_merge_kernel, chunked),
        grid=(rows // tm,),
        in_specs=[
            pl.BlockSpec((tm, ATTN_WIDTH), lambda i: (i, 0)),
            pl.BlockSpec((tm, gw), lambda i: (i, 0)),
            pl.BlockSpec((tm, gw), lambda i: (i, 1)),
            pl.BlockSpec((tm, gw), lambda i: (i, 2)),
            pl.BlockSpec((tm, gw), lambda i: (i, 3)),
            pl.BlockSpec((tm, D_MODEL), lambda i: (i, 2)),
            pl.BlockSpec((tm, D_MODEL), lambda i: (i, 3)),
            pl.BlockSpec((tm, D_MODEL), lambda i: (i, 0)),
            const(wpa.shape), const(wpb.shape), const(wout.shape),
            const(lng.shape), const(lnb.shape), const(ws.shape), const(bs.shape), const(gf.shape),
        ],
        out_specs=[
            pl.BlockSpec((tm, D_MODEL), lambda i: (i, 0)),
            pl.BlockSpec((tm, gw), lambda i: (i, 0)),
        ],
        out_shape=[
            jax.ShapeDtypeStruct((rows, D_MODEL), F32),
            jax.ShapeDtypeStruct((rows, gw), F32),
        ],
        scratch_shapes=[pltpu.VMEM((tm, gw), F32)],
        compiler_params=pltpu.CompilerParams(
            dimension_semantics=("arbitrary",), vmem_limit_bytes=VMEM_LIMIT),
        name="merge_prompt" if chunked else "merge_sample",
    )(o_a, gate, gate, gate, gate, gate, gate, x2d, wpa, wpb, wout, lng, lnb, ws, bs, gf)


def _split_weights(w_in_l):
    offs = np.cumsum([0, ATTN_WIDTH, KV_WIDTH, KV_WIDTH, IDX_HEADS * IDX_DIM, IDX_HEADS, IDX_DIM, ATTN_WIDTH,
                      GMLP_WIDTH, GMLP_WIDTH, GMLP_WIDTH, D_MODEL, D_MODEL])
    q, k, v, qi, wi, ki, za, u, vb, zb, ga, gb = [w_in_l[:, offs[n]:offs[n + 1]] for n in range(12)]
    pad = jnp.zeros((D_MODEL, KVI_WIDTH - KVI_WI - IDX_HEADS), w_in_l.dtype)
    w_hq = jnp.concatenate([q, qi], axis=1).astype(BF16)
    w_hqv_t = jnp.concatenate([q, qi, v], axis=1).T.astype(BF16)
    w_kvi = jnp.concatenate([k, v, ki, wi, pad], axis=1).astype(BF16)
    w_gate = jnp.concatenate([za, u, vb, zb, ga, gb], axis=1).astype(BF16)
    return w_hq, w_hqv_t, w_kvi, w_gate


def kernel(x_prompt, x_sample, cache_k, cache_v, cache_k_idx, page_table, norm_in_g, w_in,
           w_proj_a, w_proj_b, w_out, ln_g, ln_b, w_spatial, b_spatial, norm_f_g):
    depth = w_in.shape[0]
    assert depth == 1, "single trunk layer"
    batch, seq, _ = x_prompt.shape
    db, dseq, _ = x_sample.shape
    assert dseq == 1
    l = 0
    w_hq, w_hqv_t, w_kvi, w_gate = _split_weights(w_in[l])
    wpa = w_proj_a[l].astype(BF16)
    wpb = w_proj_b[l].astype(BF16)
    wout = w_out[l].astype(BF16)
    g_in = norm_in_g[l].reshape(1, D_MODEL)
    lng = ln_g[l].reshape(1, GMLP_WIDTH)
    lnb = ln_b[l].reshape(1, GMLP_WIDTH)
    gf = norm_f_g.reshape(1, D_MODEL)

    xp = x_prompt.reshape(batch * seq, D_MODEL)
    hqv_t = _inproj_t(xp, g_in, w_hqv_t, 512)
    kvi_p = _inproj(xp, g_in, w_kvi, F32, 1024, KVI_WIDTH)
    gate_p = _inproj(xp, g_in, w_gate, F32, 1024, 1024)
    oa_p = _prompt_attn(hqv_t, kvi_p, batch, seq)
    bs_t = b_spatial[l].T
    y_p, _ = _merge(True, oa_p, gate_p, xp, wpa, wpb, wout, lng, lnb, w_spatial[l], bs_t, gf, 256)

    xs = x_sample.reshape(db, D_MODEL)
    hq_s = _inproj(xs, g_in, w_hq, BF16, db, 1024)
    kvi_s = _inproj(xs, g_in, w_kvi, F32, db, KVI_WIDTH)
    gate_s = _inproj(xs, g_in, w_gate, F32, db, 1024)
    q_s = hq_s[:, :ATTN_WIDTH].reshape(db, N_HEADS, HEAD_DIM)
    qi_s = hq_s[:, ATTN_WIDTH:].reshape(db, IDX_HEADS, IDX_DIM)
    wi_col = kvi_s[:, KVI_WI:KVI_WI + IDX_HEADS].reshape(db, IDX_HEADS, 1)
    kvi_s16 = jnp.broadcast_to(kvi_s[:, None, :], (db, IDX_HEADS, KVI_WIDTH))
    kvi_s16 = lax.dynamic_update_slice(kvi_s16, wi_col, (0, 0, KVI_WI))
    idx, meta = _sample_select(page_table, qi_s, kvi_s16, cache_k_idx)
    oa_s = _sample_attend(idx.reshape(db, TOPK_MAX), meta[:, 0, 0], page_table, q_s, kvi_s16,
                          cache_k, cache_v).reshape(db, ATTN_WIDTH)
    ws0 = jnp.repeat(w_spatial[l][:, 0, 0], GMLP_GROUP_DIM).reshape(1, GMLP_WIDTH)
    bs0 = jnp.repeat(b_spatial[l][:, 0], GMLP_GROUP_DIM).reshape(1, GMLP_WIDTH)
    y_s, vn_s = _merge(False, oa_s, gate_s, xs, wpa, wpb, wout, lng, lnb, ws0, bs0, gf, db)

    def kv_out(kvi, lead):
        k = kvi[:, KVI_K:KVI_K + KV_WIDTH].reshape((1,) + lead + (N_KV_HEADS, HEAD_DIM))
        v = kvi[:, KVI_V:KVI_V + KV_WIDTH].reshape((1,) + lead + (N_KV_HEADS, HEAD_DIM))
        ki = kvi[:, KVI_KI:KVI_KI + IDX_DIM].reshape((1,) + lead + (IDX_DIM,))
        return k, v, ki

    k_p, v_p, ki_p = kv_out(kvi_p, (batch, seq))
    k_s, v_s, ki_s = kv_out(kvi_s, (db, dseq))
    return (y_p.reshape(batch, seq, D_MODEL), y_s.reshape(db, dseq, D_MODEL),
            k_p, v_p, ki_p, k_s, v_s, ki_s, vn_s.reshape(1, db, dseq, GMLP_WIDTH))
```

```python
import functools

import jax
import jax.numpy as jnp
import numpy as np
from jax import lax
from jax.experimental import pallas as pl
from jax.experimental.pallas import tpu as pltpu

F32 = jnp.float32
BF16 = jnp.bfloat16
I32 = jnp.int32

D_MODEL = 2048
N_HEADS = 8
N_KV_HEADS = 2
HEAD_DIM = 128
Q_PER_KV = N_HEADS // N_KV_HEADS
ATTN_WIDTH = N_HEADS * HEAD_DIM
KV_WIDTH = N_KV_HEADS * HEAD_DIM
IDX_HEADS = 16
IDX_DIM = 128
TOPK_MAX = 256
PAGE_SIZE = 128
GMLP_WIDTH = 1024
GMLP_GROUPS = 8
GMLP_GROUP_DIM = GMLP_WIDTH // GMLP_GROUPS
CHUNK = 128
RMS_EPS = 1e-6
LN_EPS = 1e-5

LANES = 128
COL_Q = 0
COL_K = COL_Q + ATTN_WIDTH
COL_V = COL_K + KV_WIDTH
COL_QI = COL_V + KV_WIDTH
COL_WI = COL_QI + IDX_HEADS * IDX_DIM
COL_KI = COL_WI + IDX_HEADS
GATE_COL0 = COL_KI + IDX_DIM
GATE_WIDTH = 4 * GMLP_WIDTH + 2 * D_MODEL
KIW_WIDTH = 256
KIW_WI, KIW_KI = 0, COL_KI - COL_WI
KVI_WIDTH = 768
KVI_K, KVI_V, KVI_KI, KVI_WI = 0, 256, 512, 640

Q_BLOCK = 128
KEY_CHUNK = 512
COUNT_ROWS = 64
NEG_SENTINEL = float(np.finfo(np.float32).min)
NEG_ABOVE_SENTINEL = float(np.nextafter(np.float32(NEG_SENTINEL), np.float32(0.0)))
INT_MIN = -(2 ** 31)
IDX_W_SCALE = float(IDX_HEADS ** -0.5 * IDX_DIM ** -0.5)
ATTN_SCALE = float(HEAD_DIM ** -0.5)
ATTN_SCALE_LOG2E = float(HEAD_DIM ** -0.5 * np.log2(np.e))
VMEM_LIMIT = 56 * 1024 * 1024


def _dot_nt(a, b):
    return lax.dot_general(a, b, (((1,), (1,)), ((), ())), preferred_element_type=F32)


def _ordered_bits_to_f32(u):
    bits = jnp.where(u < 0, u ^ INT_MIN, ~u)
    return lax.bitcast_convert_type(bits, F32)


def _gelu(x):
    return 0.5 * x * (1.0 + lax.erf(x * float(np.sqrt(0.5))))


def _silu(x):
    return x * jax.nn.sigmoid(x)


def _rmsnorm_kernel(x_ref, g_ref, o_ref):
    x = x_ref[...]
    ms = jnp.mean(x * x, axis=-1, keepdims=True)
    o_ref[...] = ((x * lax.rsqrt(ms + RMS_EPS)) * g_ref[...]).astype(o_ref.dtype)


def _rmsnorm_bf16(x2d, g, tm):
    rows = x2d.shape[0]
    return pl.pallas_call(
        _rmsnorm_kernel,
        grid=(rows // tm,),
        in_specs=[pl.BlockSpec((tm, D_MODEL), lambda i: (i, 0)),
                  pl.BlockSpec((1, D_MODEL), lambda i: (0, 0))],
        out_specs=pl.BlockSpec((tm, D_MODEL), lambda i: (i, 0)),
        out_shape=jax.ShapeDtypeStruct((rows, D_MODEL), BF16),
        compiler_params=pltpu.CompilerParams(
            dimension_semantics=("arbitrary",), vmem_limit_bytes=VMEM_LIMIT),
        name="rmsnorm_in",
    )(x2d, g)


def _inproj_nat_kernel(xn_ref, w_ref, o_ref, wb_ref):
    @pl.when(pl.program_id(1) == 0)
    def _():
        wb_ref[...] = w_ref[0].astype(BF16)

    o_ref[...] = jnp.dot(xn_ref[...], wb_ref[...], preferred_element_type=F32).astype(o_ref.dtype)


def _inproj_nat(xn, w_in, col_start, n_cols, tn, tm, out_dtype):
    rows = xn.shape[0]
    assert col_start % tn == 0 and n_cols % tn == 0 and rows % tm == 0
    j0 = col_start // tn
    return pl.pallas_call(
        _inproj_nat_kernel,
        grid=(n_cols // tn, rows // tm),
        in_specs=[
            pl.BlockSpec((tm, D_MODEL), lambda j, i: (i, 0)),
            pl.BlockSpec((1, D_MODEL, tn), lambda j, i: (0, 0, j0 + j)),
        ],
        out_specs=pl.BlockSpec((tm, tn), lambda j, i: (i, j)),
        out_shape=jax.ShapeDtypeStruct((rows, n_cols), out_dtype),
        scratch_shapes=[pltpu.VMEM((D_MODEL, tn), BF16)],
        compiler_params=pltpu.CompilerParams(
            dimension_semantics=("arbitrary", "arbitrary"), vmem_limit_bytes=VMEM_LIMIT),
        name="inproj_nat",
    )(xn, w_in)


def _inproj_t_kernel(xn_ref, w_ref, o_ref, wt_ref):
    @pl.when(pl.program_id(1) == 0)
    def _():
        wt_ref[...] = w_ref[0].T.astype(BF16)

    o_ref[...] = _dot_nt(wt_ref[...], xn_ref[...]).astype(o_ref.dtype)


def _inproj_t(xn, w_in, n_cols, tn, tm):
    rows = xn.shape[0]
    assert n_cols % tn == 0 and rows % tm == 0
    return pl.pallas_call(
        _inproj_t_kernel,
        grid=(n_cols // tn, rows // tm),
        in_specs=[
            pl.BlockSpec((tm, D_MODEL), lambda j, i: (i, 0)),
            pl.BlockSpec((1, D_MODEL, tn), lambda j, i: (0, 0, j)),
        ],
        out_specs=pl.BlockSpec((tn, tm), lambda j, i: (j, i)),
        out_shape=jax.ShapeDtypeStruct((n_cols, rows), BF16),
        scratch_shapes=[pltpu.VMEM((tn, D_MODEL), BF16)],
        compiler_params=pltpu.CompilerParams(
            dimension_semantics=("arbitrary", "arbitrary"), vmem_limit_bytes=VMEM_LIMIT),
        name="inproj_t",
    )(xn, w_in)


GATE_TILE = 1024
GATE_SHIFT = GATE_COL0 % LANES
GATE_BLOCKS = GATE_TILE // LANES + 1


def _inproj_gate_kernel(xn_ref, *refs):
    w_refs = refs[:GATE_BLOCKS]
    o_ref, wb_ref = refs[GATE_BLOCKS:]

    @pl.when(pl.program_id(1) == 0)
    def _():
        wide = jnp.concatenate([w[0] for w in w_refs], axis=1)
        wb_ref[...] = wide[:, GATE_SHIFT:GATE_SHIFT + GATE_TILE].astype(BF16)

    o_ref[...] = jnp.dot(xn_ref[...], wb_ref[...], preferred_element_type=F32)


def _inproj_gate(xn, w_in, tm):
    rows = xn.shape[0]
    assert rows % tm == 0
    b0 = GATE_COL0 // LANES
    per_tile = GATE_TILE // LANES

    def w_spec(t):
        return pl.BlockSpec((1, D_MODEL, LANES), lambda j, i: (0, 0, b0 + per_tile * j + t))

    return pl.pallas_call(
        _inproj_gate_kernel,
        grid=(GATE_WIDTH // GATE_TILE, rows // tm),
        in_specs=[pl.BlockSpec((tm, D_MODEL), lambda j, i: (i, 0))] + [w_spec(t) for t in range(GATE_BLOCKS)],
        out_specs=pl.BlockSpec((tm, GATE_TILE), lambda j, i: (i, j)),
        out_shape=jax.ShapeDtypeStruct((rows, GATE_WIDTH), F32),
        scratch_shapes=[pltpu.VMEM((D_MODEL, GATE_TILE), BF16)],
        compiler_params=pltpu.CompilerParams(
            dimension_semantics=("arbitrary", "arbitrary"), vmem_limit_bytes=VMEM_LIMIT),
        name="inproj_gate",
    )(xn, *([w_in] * GATE_BLOCKS))


def _prompt_attn_kernel(qt_ref, qit0_ref, qit1_ref, qit2_ref, qit3_ref, vt_ref, kiwq_ref, kvf_ref, kiwf_ref,
                        o_ref, kb_ref, kib_ref, vtc_ref, sc_ref, jlim_ref, bias_ref, acc_ref):
    i = pl.program_id(1)
    seq = kvf_ref.shape[0]
    qit_refs = (qit0_ref, qit1_ref, qit2_ref, qit3_ref)
    heads_per_ref = IDX_HEADS // len(qit_refs)

    @pl.when(i == 0)
    def _():
        kb_ref[...] = kvf_ref[:, 0:KV_WIDTH].astype(BF16)
        kib_ref[...] = kiwf_ref[:, KIW_KI:KIW_KI + IDX_DIM].astype(BF16)
        for c in range(seq // KEY_CHUNK):
            vtc_ref[c] = vt_ref[:, c * KEY_CHUNK:(c + 1) * KEY_CHUNK]

    n_chunks = (i * Q_BLOCK) // KEY_CHUNK + 1
    q_pos = i * Q_BLOCK + lax.broadcasted_iota(I32, (1, Q_BLOCK), 1)
    key0 = lax.broadcasted_iota(I32, (KEY_CHUNK, 1), 0)

    w_t = kiwq_ref[:, KIW_WI:KIW_WI + Q_BLOCK].T[:IDX_HEADS, :] * IDX_W_SCALE
    qit_all = jnp.concatenate(
        [qit_refs[h // heads_per_ref][(h % heads_per_ref) * IDX_DIM:(h % heads_per_ref + 1) * IDX_DIM, :]
         for h in range(IDX_HEADS)], axis=1)

    def score_body(c, carry):
        k0 = pl.multiple_of(c * KEY_CHUNK, KEY_CHUNK)
        res = jnp.dot(kib_ref[pl.ds(k0, KEY_CHUNK), :], qit_all, preferred_element_type=F32)
        acc = jnp.zeros((KEY_CHUNK, Q_BLOCK), F32)
        for h in range(IDX_HEADS):
            acc = acc + w_t[h:h + 1, :] * jnp.maximum(res[:, h * Q_BLOCK:(h + 1) * Q_BLOCK], 0.0)
        sc_ref[c] = jnp.where(k0 + key0 <= q_pos, acc, NEG_SENTINEL)
        return carry

    lax.fori_loop(0, n_chunks, score_body, 0)

    def count(pred):
        def body(c, cnt):
            ones = jnp.where(pred(sc_ref[c], c), 1.0, 0.0)
            return cnt + ones.reshape(KEY_CHUNK // COUNT_ROWS, COUNT_ROWS, Q_BLOCK).sum(axis=0)
        part = lax.fori_loop(0, n_chunks, body, jnp.zeros((COUNT_ROWS, Q_BLOCK), F32))
        return jnp.sum(part, axis=0, keepdims=True)

    def bit_body(j, u):
        cand = u | lax.shift_left(jnp.int32(1), 31 - j)
        thr = _ordered_bits_to_f32(cand)
        cnt = count(lambda blk, c: blk >= thr)
        return jnp.where(cnt >= float(TOPK_MAX), cand, u)

    u = lax.fori_loop(0, 32, bit_body, jnp.zeros((1, Q_BLOCK), I32))
    thr = _ordered_bits_to_f32(u)
    n_ge = count(lambda blk, c: blk >= thr)
    n_gt = count(lambda blk, c: blk > thr)
    need = float(TOPK_MAX) - n_gt
    thr_adm = jnp.maximum(thr, NEG_ABOVE_SENTINEL)
    excess = jnp.max(jnp.where((n_ge > float(TOPK_MAX)) & (thr > NEG_SENTINEL), 1.0, 0.0)) > 0.0

    jlim_ref[...] = jnp.full((1, Q_BLOCK), 2 * seq, I32)

    @pl.when(excess)
    def _():
        def jbit_body(j, jl):
            cand = jl | lax.shift_left(jnp.int32(1), 12 - j)
            cnt = count(lambda blk, c: (blk == thr) & (c * KEY_CHUNK + key0 < cand))
            return jnp.where(cnt <= need, cand, jl)
        jlim_ref[...] = lax.fori_loop(0, 13, jbit_body, jnp.zeros((1, Q_BLOCK), I32))

    jlim = jlim_ref[...]

    acc_ref[...] = jnp.zeros(acc_ref.shape, F32)
    cols = Q_PER_KV * Q_BLOCK
    qt_groups = [
        jnp.concatenate([qt_ref[(g * Q_PER_KV + h) * HEAD_DIM:(g * Q_PER_KV + h + 1) * HEAD_DIM, :]
                         for h in range(Q_PER_KV)], axis=1)
        for g in range(N_KV_HEADS)
    ]

    def attn_body(c, carry):
        k0 = pl.multiple_of(c * KEY_CHUNK, KEY_CHUNK)
        blk = sc_ref[c]

        @pl.when(jnp.logical_not(excess))
        def _():
            bias_ref[...] = jnp.where(blk >= thr_adm, 0.0, -jnp.inf)

        @pl.when(excess)
        def _():
            keep = (blk >= thr_adm) & ((blk > thr) | (k0 + key0 < jlim))
            bias_ref[...] = jnp.where(keep, 0.0, -jnp.inf)

        bias = jnp.concatenate([bias_ref[...]] * Q_PER_KV, axis=1)
        out = []
        for g in range(N_KV_HEADS):
            m_old, l_old = carry[2 * g], carry[2 * g + 1]
            kc = kb_ref[pl.ds(k0, KEY_CHUNK), g * HEAD_DIM:(g + 1) * HEAD_DIM]
            s = jnp.dot(kc, qt_groups[g], preferred_element_type=F32) * ATTN_SCALE_LOG2E + bias
            m_new = jnp.maximum(m_old, jnp.max(s, axis=0, keepdims=True))
            m_safe = jnp.where(m_new == -jnp.inf, 0.0, m_new)
            alpha = jnp.exp2(m_old - m_safe)
            p = jnp.exp2(s - m_safe)
            l_new = alpha * l_old + jnp.sum(p, axis=0, keepdims=True)
            vtc = vtc_ref[c, g * HEAD_DIM:(g + 1) * HEAD_DIM, :]
            acc_ref[g] = alpha * acc_ref[g] + jnp.dot(vtc, p.astype(BF16), preferred_element_type=F32)
            out += [m_new, l_new]
        return tuple(out)

    init = (jnp.full((1, cols), -jnp.inf, F32), jnp.zeros((1, cols), F32)) * N_KV_HEADS
    stats = lax.fori_loop(0, n_chunks, attn_body, init)

    for g in range(N_KV_HEADS):
        o = acc_ref[g] / stats[2 * g + 1]
        for h in range(Q_PER_KV):
            hd = (g * Q_PER_KV + h) * HEAD_DIM
            o_ref[:, hd:hd + HEAD_DIM] = o[:, h * Q_BLOCK:(h + 1) * Q_BLOCK].T


def _prompt_attn(h_t, kv, kiw, batch, seq):
    nb = seq // Q_BLOCK
    cols = Q_PER_KV * Q_BLOCK
    qi_rows = IDX_HEADS * IDX_DIM // 4
    qi_b0 = COL_QI // qi_rows

    def qi_spec(t):
        return pl.BlockSpec((qi_rows, Q_BLOCK), lambda b, i: (qi_b0 + t, b * nb + i))

    return pl.pallas_call(
        _prompt_attn_kernel,
        grid=(batch, nb),
        in_specs=[
            pl.BlockSpec((ATTN_WIDTH, Q_BLOCK), lambda b, i: (COL_Q // ATTN_WIDTH, b * nb + i)),
            qi_spec(0), qi_spec(1), qi_spec(2), qi_spec(3),
            pl.BlockSpec((KV_WIDTH, seq), lambda b, i: (COL_V // KV_WIDTH, b)),
            pl.BlockSpec((Q_BLOCK, KIW_WIDTH), lambda b, i: (b * nb + i, 0)),
            pl.BlockSpec((seq, 2 * KV_WIDTH), lambda b, i: (b, 0)),
            pl.BlockSpec((seq, KIW_WIDTH), lambda b, i: (b, 0)),
        ],
        out_specs=pl.BlockSpec((Q_BLOCK, ATTN_WIDTH), lambda b, i: (b * nb + i, 0)),
        out_shape=jax.ShapeDtypeStruct((batch * seq, ATTN_WIDTH), F32),
        scratch_shapes=[
            pltpu.VMEM((seq, KV_WIDTH), BF16),
            pltpu.VMEM((seq, IDX_DIM), BF16),
            pltpu.VMEM((seq // KEY_CHUNK, KV_WIDTH, KEY_CHUNK), BF16),
            pltpu.VMEM((seq // KEY_CHUNK, KEY_CHUNK, Q_BLOCK), F32),
            pltpu.VMEM((1, Q_BLOCK), I32),
            pltpu.VMEM((KEY_CHUNK, Q_BLOCK), F32),
            pltpu.VMEM((N_KV_HEADS, HEAD_DIM, cols), F32),
        ],
        compiler_params=pltpu.CompilerParams(
            dimension_semantics=("arbitrary", "arbitrary"), vmem_limit_bytes=VMEM_LIMIT),
        name="prompt_attn",
    )(h_t, h_t, h_t, h_t, h_t, h_t, kiw, kv, kiw)


PAGES_PER_STEP = 32


def _sample_select_kernel(pt_ref, qi_ref, kvs_ref, *refs):
    page_refs = refs[:PAGES_PER_STEP]
    idx_ref, meta_ref, sc_ref, snew_ref = refs[PAGES_PER_STEP:]
    b = pl.program_id(0)
    p = pl.program_id(1)
    nb = pl.num_programs(0)
    n_steps = pl.num_programs(1)
    n_pages, db, _ = sc_ref.shape

    qi = qi_ref[0]
    w_col = kvs_ref[0, :, KVI_WI:KVI_WI + 1] * IDX_W_SCALE

    def key_scores(keys_bf16):
        s = _dot_nt(qi, keys_bf16)
        return jnp.sum(w_col * jnp.maximum(s, 0.0), axis=0, keepdims=True)

    keys = jnp.concatenate([page_refs[r][0, 0] for r in range(PAGES_PER_STEP)], axis=0).astype(BF16)
    row = key_scores(keys)
    for r in range(PAGES_PER_STEP):
        sc_ref[p * PAGES_PER_STEP + r, pl.ds(b, 1), :] = row[:, r * PAGE_SIZE:(r + 1) * PAGE_SIZE]

    @pl.when(p == n_steps - 1)
    def _():
        ki_new = kvs_ref[0, :, KVI_KI:KVI_KI + IDX_DIM].astype(BF16)
        s_new = key_scores(ki_new)[:, 0:1]
        snew_ref[pl.ds(b, 1), :] = jnp.broadcast_to(s_new, (1, PAGE_SIZE))

    @pl.when((b == nb - 1) & (p == n_steps - 1))
    def _():
        sc = sc_ref[...]
        s_new = snew_ref[:, 0:1]

        def total(x):
            return jnp.sum(jnp.sum(x, axis=0), axis=1, keepdims=True)

        def bit_body(j, u):
            cand = u | lax.shift_left(jnp.int32(1), 31 - j)
            thr = _ordered_bits_to_f32(cand)
            cnt = total(jnp.where(sc >= thr[None], 1.0, 0.0)) + jnp.where(s_new >= thr, 1.0, 0.0)
            return jnp.where(cnt >= float(TOPK_MAX), cand, u)

        u = lax.fori_loop(0, 32, bit_body, jnp.zeros((db, 1), I32))
        thr = _ordered_bits_to_f32(u)
        n_gt = total(jnp.where(sc > thr[None], 1.0, 0.0)) + jnp.where(s_new > thr, 1.0, 0.0)
        need = float(TOPK_MAX) - n_gt
        flat = (lax.broadcasted_iota(I32, sc.shape, 0) * PAGE_SIZE
                + lax.broadcasted_iota(I32, sc.shape, 2))
        tie = sc == thr[None]

        def jbit_body(j, jl):
            cand = jl | lax.shift_left(jnp.int32(1), 14 - j)
            cnt = total(jnp.where(tie & (flat < cand[None]), 1.0, 0.0))
            return jnp.where(cnt <= need, cand, jl)

        jlim = lax.fori_loop(0, 15, jbit_body, jnp.zeros((db, 1), I32))
        sel = (sc > thr[None]) | (tie & (flat < jlim[None]))
        self32 = jnp.where(sel, 1.0, 0.0)
        n_pool = total(self32)

        n_rows = n_pages * db
        selb = self32.astype(BF16).reshape(n_rows, PAGE_SIZE)
        ri = lax.broadcasted_iota(I32, (PAGE_SIZE, PAGE_SIZE), 0)
        ci = lax.broadcasted_iota(I32, (PAGE_SIZE, PAGE_SIZE), 1)
        incl = jnp.dot(selb, jnp.where(ri <= ci, 1.0, 0.0).astype(BF16), preferred_element_type=F32)
        tot = jnp.broadcast_to(incl[:, PAGE_SIZE - 1:PAGE_SIZE], (n_rows, PAGE_SIZE)).astype(BF16)
        rr = lax.broadcasted_iota(I32, (n_rows, n_rows), 0)
        cc = lax.broadcasted_iota(I32, (n_rows, n_rows), 1)
        earlier_page_same_row = ((rr & (db - 1)) == (cc & (db - 1))) & (cc < (rr & -db))
        offs = jnp.dot(jnp.where(earlier_page_same_row, 1.0, 0.0).astype(BF16), tot,
                       preferred_element_type=F32)
        ends_rows = offs + tot.astype(F32)
        r_i = lax.broadcasted_iota(I32, (n_rows, PAGE_SIZE), 0)
        l_i = lax.broadcasted_iota(I32, (n_rows, PAGE_SIZE), 1)
        log2_db = db.bit_length() - 1
        diag = jnp.where(l_i == lax.shift_right_logical(r_i, log2_db), ends_rows, 0.0).astype(BF16)
        pick = jnp.where((lax.broadcasted_iota(I32, (db, n_rows), 1) & (db - 1))
                         == lax.broadcasted_iota(I32, (db, n_rows), 0), 1.0, 0.0).astype(BF16)
        ends = jnp.dot(pick, diag, preferred_element_type=F32)

        table = jnp.concatenate([incl.astype(BF16), offs.astype(BF16)], axis=1)
        slot = lax.broadcasted_iota(I32, (TOPK_MAX, PAGE_SIZE), 0).astype(F32)
        rows_lane = lax.broadcasted_iota(I32, (TOPK_MAX, n_rows), 1)
        for row_i in range(db):
            page_s = jnp.sum(jnp.where(ends[row_i:row_i + 1, :] <= slot, 1.0, 0.0), axis=1, keepdims=True)
            mrow = page_s.astype(I32) * db + row_i
            onehot = jnp.where(rows_lane == mrow, 1.0, 0.0).astype(BF16)
            got = jnp.dot(onehot, table, preferred_element_type=F32)
            k = slot - got[:, PAGE_SIZE:]
            off_s = jnp.sum(jnp.where(got[:, :PAGE_SIZE] <= k, 1.0, 0.0), axis=1, keepdims=True)
            pos = page_s * float(PAGE_SIZE) + off_s
            valid = slot[:, 0:1] < n_pool[row_i:row_i + 1, :]
            idx_ref[row_i] = jnp.where(valid, pos, 0.0).astype(I32)
        meta_ref[...] = jnp.broadcast_to(n_pool, (db, PAGE_SIZE)).astype(I32)


def _sample_select(page_table, qi_s, kvi_s16, pool_ki):
    db, n_pages = page_table.shape
    assert db & (db - 1) == 0, "row index arithmetic in the rank matrix assumes a power-of-two row count"
    assert n_pages == PAGE_SIZE, "page ends are laid out one page per lane"
    n_steps = n_pages // PAGES_PER_STEP

    def page_spec(r):
        return pl.BlockSpec(
            (1, 1, PAGE_SIZE, IDX_DIM),
            lambda b, p, pt: (0, pt[b, p * PAGES_PER_STEP + r], 0, 0))

    grid_spec = pltpu.PrefetchScalarGridSpec(
        num_scalar_prefetch=1,
        grid=(db, n_steps),
        in_specs=[
            pl.BlockSpec((1, IDX_HEADS, IDX_DIM), lambda b, p, pt: (b, 0, 0)),
            pl.BlockSpec((1, IDX_HEADS, KVI_WIDTH), lambda b, p, pt: (b, 0, 0)),
        ] + [page_spec(r) for r in range(PAGES_PER_STEP)],
        out_specs=[
            pl.BlockSpec((db, TOPK_MAX, 1), lambda b, p, pt: (0, 0, 0)),
            pl.BlockSpec((db, PAGE_SIZE), lambda b, p, pt: (0, 0)),
        ],
        scratch_shapes=[
            pltpu.VMEM((n_pages, db, PAGE_SIZE), F32),
            pltpu.VMEM((db, PAGE_SIZE), F32),
        ],
    )
    return pl.pallas_call(
        _sample_select_kernel,
        grid_spec=grid_spec,
        out_shape=[
            jax.ShapeDtypeStruct((db, TOPK_MAX, 1), I32),
            jax.ShapeDtypeStruct((db, PAGE_SIZE), I32),
        ],
        compiler_params=pltpu.CompilerParams(
            dimension_semantics=("arbitrary", "arbitrary"), vmem_limit_bytes=VMEM_LIMIT),
        name="sample_select",
    )(page_table, qi_s, kvi_s16, *([pool_ki] * PAGES_PER_STEP))


def _sample_attend_kernel(idx_ref, npool_ref, pt_ref, q_ref, kvs_ref, pk_ref, pv_ref, o_ref,
                          kbuf, vbuf, kflat, vflat, sem):
    b = pl.program_id(0)

    def row_copies(r):
        ix = idx_ref[b, r]
        page = pt_ref[b, lax.shift_right_logical(ix, 7)]
        off = ix & (PAGE_SIZE - 1)
        return (pltpu.make_async_copy(pk_ref.at[0, page, off], kbuf.at[r], sem.at[0]),
                pltpu.make_async_copy(pv_ref.at[0, page, off], vbuf.at[r], sem.at[1]))

    def start_body(r, carry):
        ck, cv = row_copies(r)
        ck.start()
        cv.start()
        return carry

    def wait_body(r, carry):
        ck, cv = row_copies(r)
        ck.wait()
        cv.wait()
        return carry

    lax.fori_loop(0, TOPK_MAX, start_body, 0)
    lax.fori_loop(0, TOPK_MAX, wait_body, 0)

    def head_copies(g):
        return (pltpu.make_async_copy(kbuf.at[:, g, :], kflat.at[g], sem.at[2]),
                pltpu.make_async_copy(vbuf.at[:, g, :], vflat.at[g], sem.at[3]))

    for g in range(N_KV_HEADS):
        for cp in head_copies(g):
            cp.start()
    for g in range(N_KV_HEADS):
        for cp in head_copies(g):
            cp.wait()

    n_pool = npool_ref[b]
    slot = lax.broadcasted_iota(I32, (1, TOPK_MAX), 1)
    slot_ok = slot < n_pool
    new_ok = n_pool < TOPK_MAX
    q_all = q_ref[0].astype(F32)
    for g in range(N_KV_HEADS):
        qg = q_all[g * Q_PER_KV:(g + 1) * Q_PER_KV, :].astype(BF16)
        ks = kflat[g].astype(BF16)
        vs = vflat[g].astype(BF16)
        k_new = kvs_ref[0, 0:1, KVI_K + g * HEAD_DIM:KVI_K + (g + 1) * HEAD_DIM].astype(BF16)
        v_new = kvs_ref[0, 0:1, KVI_V + g * HEAD_DIM:KVI_V + (g + 1) * HEAD_DIM].astype(BF16)
        s = jnp.where(slot_ok, _dot_nt(qg, ks) * ATTN_SCALE, -jnp.inf)
        s_new = jnp.sum(qg.astype(F32) * k_new.astype(F32), axis=-1, keepdims=True) * ATTN_SCALE
        s_new = jnp.where(new_ok, s_new, -jnp.inf)
        m = jnp.maximum(jnp.max(s, axis=-1, keepdims=True), s_new)
        p = jnp.exp(s - m)
        p_new = jnp.exp(s_new - m)
        denom = jnp.sum(p, axis=-1, keepdims=True) + p_new
        pn = (p / denom).astype(BF16)
        pn_new = (p_new / denom).astype(BF16).astype(F32)
        o = jnp.dot(pn, vs, preferred_element_type=F32) + pn_new * v_new.astype(F32)
        o_ref[0, g * Q_PER_KV:(g + 1) * Q_PER_KV, :] = o


def _sample_attend(idx, n_pool, page_table, q_s, kvi_s16, pool_k, pool_v):
    db = page_table.shape[0]
    grid_spec = pltpu.PrefetchScalarGridSpec(
        num_scalar_prefetch=3,
        grid=(db,),
        in_specs=[
            pl.BlockSpec((1, N_HEADS, HEAD_DIM), lambda b, *_: (b, 0, 0)),
            pl.BlockSpec((1, IDX_HEADS, KVI_WIDTH), lambda b, *_: (b, 0, 0)),
            pl.BlockSpec(memory_space=pl.ANY),
            pl.BlockSpec(memory_space=pl.ANY),
        ],
        out_specs=pl.BlockSpec((1, N_HEADS, HEAD_DIM), lambda b, *_: (b, 0, 0)),
        scratch_shapes=[
            pltpu.VMEM((TOPK_MAX, N_KV_HEADS, HEAD_DIM), F32),
            pltpu.VMEM((TOPK_MAX, N_KV_HEADS, HEAD_DIM), F32),
            pltpu.VMEM((N_KV_HEADS, TOPK_MAX, HEAD_DIM), F32),
            pltpu.VMEM((N_KV_HEADS, TOPK_MAX, HEAD_DIM), F32),
            pltpu.SemaphoreType.DMA((4,)),
        ],
    )
    return pl.pallas_call(
        _sample_attend_kernel,
        grid_spec=grid_spec,
        out_shape=jax.ShapeDtypeStruct((db, N_HEADS, HEAD_DIM), F32),
        compiler_params=pltpu.CompilerParams(dimension_semantics=("arbitrary",)),
        name="sample_attend",
    )(idx, n_pool, page_table, q_s, kvi_s16, pool_k, pool_v)


def _merge_kernel(chunked, oa_ref, za_ref, u_ref, v_ref, zb_ref, ga_ref, gb_ref, x_ref,
                  wpa_ref, wpb_ref, wout_ref, lng_ref, lnb_ref, ws_ref, bs_ref, gf_ref,
                  y_ref, vn_ref, ob_ref):
    tm = x_ref.shape[0]
    u = _gelu(u_ref[...])
    v = _gelu(v_ref[...])
    mu = jnp.mean(v, axis=-1, keepdims=True)
    vc = v - mu
    vn = (vc * lax.rsqrt(jnp.mean(vc * vc, axis=-1, keepdims=True) + LN_EPS)) * lng_ref[...] + lnb_ref[...]
    vn_ref[...] = vn

    if chunked:
        vnb = vn.astype(BF16)
        ri = lax.broadcasted_iota(I32, (CHUNK, CHUNK), 0)
        ci = lax.broadcasted_iota(I32, (CHUNK, CHUNK), 1)
        for g in range(GMLP_GROUPS):
            wm = jnp.where(ri >= ci, ws_ref[g], 0.0).astype(BF16)
            lo = g * GMLP_GROUP_DIM
            for c in range(tm // CHUNK):
                mixed = jnp.dot(wm, vnb[c * CHUNK:(c + 1) * CHUNK, lo:lo + GMLP_GROUP_DIM],
                                preferred_element_type=F32) + bs_ref[:, g:g + 1]
                ob_ref[c * CHUNK:(c + 1) * CHUNK, lo:lo + GMLP_GROUP_DIM] = (
                    u[c * CHUNK:(c + 1) * CHUNK, lo:lo + GMLP_GROUP_DIM] * mixed)
    else:
        ob_ref[...] = u * (ws_ref[...] * vn + bs_ref[...])

    ha = (oa_ref[...] * _silu(za_ref[...])).astype(BF16)
    hb = (ob_ref[...] * _silu(zb_ref[...])).astype(BF16)
    ya = jnp.dot(ha, wpa_ref[...], preferred_element_type=F32)
    yb = jnp.dot(hb, wpb_ref[...], preferred_element_type=F32)
    mix = jax.nn.sigmoid(ga_ref[...]) * ya + jax.nn.sigmoid(gb_ref[...]) * yb
    out = x_ref[...] + jnp.dot(mix.astype(BF16), wout_ref[...], preferred_element_type=F32)
    ms = jnp.mean(out * out, axis=-1, keepdims=True)
    y_ref[...] = (out * lax.rsqrt(ms + RMS_EPS)) * gf_ref[...]


def _merge(chunked, o_a, gate, x2d, wpa, wpb, wout, lng, lnb, ws, bs, gf, tm):
    rows = x2d.shape[0]
    gw = GMLP_WIDTH

    def const(shape):
        return pl.BlockSpec(shape, lambda i: (0,) * len(shape), pipeline_mode=pl.Buffered(1))

    return pl.pallas_call(
        functools.partial(_merge_kernel, chunked),
        grid=(rows // tm,),
        in_specs=[
            pl.BlockSpec((tm, ATTN_WIDTH), lambda i: (i, 0)),
            pl.BlockSpec((tm, gw), lambda i: (i, 0)),
            pl.BlockSpec((tm, gw), lambda i: (i, 1)),
            pl.BlockSpec((tm, gw), lambda i: (i, 2)),
            pl.BlockSpec((tm, gw), lambda i: (i, 3)),
            pl.BlockSpec((tm, D_MODEL), lambda i: (i, 2)),
            pl.BlockSpec((tm, D_MODEL), lambda i: (i, 3)),
            pl.BlockSpec((tm, D_MODEL), lambda i: (i, 0)),
            const(wpa.shape), const(wpb.shape), const(wout.shape),
            const(lng.shape), const(lnb.shape), const(ws.shape), const(bs.shape), const(gf.shape),
        ],
        out_specs=[
            pl.BlockSpec((tm, D_MODEL), lambda i: (i, 0)),
            pl.BlockSpec((tm, gw), lambda i: (i, 0)),
        ],
        out_shape=[
            jax.ShapeDtypeStruct((rows, D_MODEL), F32),
            jax.ShapeDtypeStruct((rows, gw), F32),
        ],
        scratch_shapes=[pltpu.VMEM((tm, gw), F32)],
        compiler_params=pltpu.CompilerParams(
            dimension_semantics=("arbitrary",), vmem_limit_bytes=VMEM_LIMIT),
        name="merge_prompt" if chunked else "merge_sample",
    )(o_a, gate, gate, gate, gate, gate, gate, x2d, wpa, wpb, wout, lng, lnb, ws, bs, gf)


def kernel(x_prompt, x_sample, cache_k, cache_v, cache_k_idx, page_table, norm_in_g, w_in,
           w_proj_a, w_proj_b, w_out, ln_g, ln_b, w_spatial, b_spatial, norm_f_g):
    depth = w_in.shape[0]
    assert depth == 1, "single trunk layer"
    batch, seq, _ = x_prompt.shape
    db, dseq, _ = x_sample.shape
    assert dseq == 1
    l = 0
    wpa = w_proj_a[l].astype(BF16)
    wpb = w_proj_b[l].astype(BF16)
    wout = w_out[l].astype(BF16)
    g_in = norm_in_g[l].reshape(1, D_MODEL)
    lng = ln_g[l].reshape(1, GMLP_WIDTH)
    lnb = ln_b[l].reshape(1, GMLP_WIDTH)
    gf = norm_f_g.reshape(1, D_MODEL)

    xp = x_prompt.reshape(batch * seq, D_MODEL)
    xn_p = _rmsnorm_bf16(xp, g_in, 1024)
    h_t = _inproj_t(xn_p, w_in, COL_WI, 512, 1024)
    kv_p = _inproj_nat(xn_p, w_in, COL_K, 2 * KV_WIDTH, 2 * KV_WIDTH, 1024, F32)
    kiw_p = _inproj_nat(xn_p, w_in, COL_WI, KIW_WIDTH, KIW_WIDTH, 1024, F32)
    gate_p = _inproj_gate(xn_p, w_in, 1024)
    oa_p = _prompt_attn(h_t, kv_p, kiw_p, batch, seq)
    bs_t = b_spatial[l].T
    y_p, _ = _merge(True, oa_p, gate_p, xp, wpa, wpb, wout, lng, lnb, w_spatial[l], bs_t, gf, 256)

    xs = x_sample.reshape(db, D_MODEL)
    xn_s = _rmsnorm_bf16(xs, g_in, db)
    h_s = _inproj_nat(xn_s, w_in, 0, COL_WI, 512, db, F32)
    kiw_s = _inproj_nat(xn_s, w_in, COL_WI, KIW_WIDTH, KIW_WIDTH, db, F32)
    gate_s = _inproj_gate(xn_s, w_in, db)
    kvi_s = jnp.concatenate(
        [h_s[:, COL_K:COL_QI], kiw_s[:, KIW_KI:KIW_KI + IDX_DIM], kiw_s[:, KIW_WI:KIW_WI + IDX_HEADS],
         jnp.zeros((db, KVI_WIDTH - KVI_WI - IDX_HEADS), F32)], axis=1)
    q_s = h_s[:, COL_Q:COL_K].astype(BF16).reshape(db, N_HEADS, HEAD_DIM)
    qi_s = h_s[:, COL_QI:COL_WI].astype(BF16).reshape(db, IDX_HEADS, IDX_DIM)
    wi_col = kvi_s[:, KVI_WI:KVI_WI + IDX_HEADS].reshape(db, IDX_HEADS, 1)
    kvi_s16 = jnp.broadcast_to(kvi_s[:, None, :], (db, IDX_HEADS, KVI_WIDTH))
    kvi_s16 = lax.dynamic_update_slice(kvi_s16, wi_col, (0, 0, KVI_WI))
    idx, meta = _sample_select(page_table, qi_s, kvi_s16, cache_k_idx)
    oa_s = _sample_attend(idx.reshape(db, TOPK_MAX), meta[:, 0], page_table, q_s, kvi_s16,
                          cache_k, cache_v).reshape(db, ATTN_WIDTH)
    ws0 = jnp.repeat(w_spatial[l][:, 0, 0], GMLP_GROUP_DIM).reshape(1, GMLP_WIDTH)
    bs0 = jnp.repeat(b_spatial[l][:, 0], GMLP_GROUP_DIM).reshape(1, GMLP_WIDTH)
    y_s, vn_s = _merge(False, oa_s, gate_s, xs, wpa, wpb, wout, lng, lnb, ws0, bs0, gf, db)

    def kv_out(kvi, lead):
        k = kvi[:, KVI_K:KVI_K + KV_WIDTH].reshape((1,) + lead + (N_KV_HEADS, HEAD_DIM))
        v = kvi[:, KVI_V:KVI_V + KV_WIDTH].reshape((1,) + lead + (N_KV_HEADS, HEAD_DIM))
        ki = kvi[:, KVI_KI:KVI_KI + IDX_DIM].reshape((1,) + lead + (IDX_DIM,))
        return k, v, ki

    k_p = kv_p[:, :KV_WIDTH].reshape(1, batch, seq, N_KV_HEADS, HEAD_DIM)
    v_p = kv_p[:, KV_WIDTH:].reshape(1, batch, seq, N_KV_HEADS, HEAD_DIM)
    ki_p = kiw_p[:, KIW_KI:KIW_KI + IDX_DIM].reshape(1, batch, seq, IDX_DIM)
    k_s, v_s, ki_s = kv_out(kvi_s, (db, dseq))
    return (y_p.reshape(batch, seq, D_MODEL), y_s.reshape(db, dseq, D_MODEL),
            k_p, v_p, ki_p, k_s, v_s, ki_s, vn_s.reshape(1, db, dseq, GMLP_WIDTH))
```

```python
import functools

import jax
import jax.numpy as jnp
import numpy as np
from jax import lax
from jax.experimental import pallas as pl
from jax.experimental.pallas import tpu as pltpu

F32 = jnp.float32
BF16 = jnp.bfloat16
I32 = jnp.int32

D_MODEL = 2048
N_HEADS = 8
N_KV_HEADS = 2
HEAD_DIM = 128
Q_PER_KV = N_HEADS // N_KV_HEADS
ATTN_WIDTH = N_HEADS * HEAD_DIM
KV_WIDTH = N_KV_HEADS * HEAD_DIM
IDX_HEADS = 16
IDX_DIM = 128
TOPK_MAX = 256
PAGE_SIZE = 128
GMLP_WIDTH = 1024
GMLP_GROUPS = 8
GMLP_GROUP_DIM = GMLP_WIDTH // GMLP_GROUPS
CHUNK = 128
RMS_EPS = 1e-6
LN_EPS = 1e-5

SUBLANES = 8
COL_Q = 0
COL_K = COL_Q + ATTN_WIDTH
COL_V = COL_K + KV_WIDTH
COL_QI = COL_V + KV_WIDTH
COL_WI = COL_QI + IDX_HEADS * IDX_DIM
COL_KI = COL_WI + IDX_HEADS
GATE_COL0 = COL_KI + IDX_DIM
GATE_WIDTH = 4 * GMLP_WIDTH + 2 * D_MODEL
KVI_WIDTH = 768
KVI_K, KVI_V, KVI_KI, KVI_WI = 0, 256, 512, 640

Q_BLOCK = 128
KEY_CHUNK = 512
COUNT_ROWS = 64
NEG_SENTINEL = float(np.finfo(np.float32).min)
NEG_ABOVE_SENTINEL = float(np.nextafter(np.float32(NEG_SENTINEL), np.float32(0.0)))
INT_MIN = -(2 ** 31)
IDX_W_SCALE = float(IDX_HEADS ** -0.5 * IDX_DIM ** -0.5)
ATTN_SCALE = float(HEAD_DIM ** -0.5)
ATTN_SCALE_LOG2E = float(HEAD_DIM ** -0.5 * np.log2(np.e))
VMEM_LIMIT = 56 * 1024 * 1024


def _dot_nt(a, b):
    return lax.dot_general(a, b, (((1,), (1,)), ((), ())), preferred_element_type=F32)


def _ordered_bits_to_f32(u):
    bits = jnp.where(u < 0, u ^ INT_MIN, ~u)
    exp = lax.shift_right_logical(bits, 23) & 0xFF
    mant = 1.0 + (bits & 0x7FFFFF).astype(F32) * float(2.0 ** -23)
    e1 = exp - 1
    p = jnp.full(u.shape, float(2.0 ** -126), F32)
    for b in range(7):
        p = jnp.where(((e1 >> b) & 1) == 1, p * float(2.0 ** (2 ** b)), p)
    top = ((e1 >> 7) & 1) == 1
    p = jnp.where(top, p * float(2.0 ** 64), p)
    p = jnp.where(top, p * float(2.0 ** 64), p)
    mag = jnp.where(exp == 0, 0.0, p * mant)
    return jnp.where(bits < 0, -mag, mag)


def _gelu(x):
    return 0.5 * x * (1.0 + lax.erf(x * float(np.sqrt(0.5))))


def _silu(x):
    return x * jax.nn.sigmoid(x)


def _rmsnorm_kernel(x_ref, g_ref, o_ref):
    x = x_ref[...]
    ms = jnp.mean(x * x, axis=-1, keepdims=True)
    o_ref[...] = ((x * lax.rsqrt(ms + RMS_EPS)) * g_ref[...]).astype(o_ref.dtype)


def _rmsnorm_bf16(x2d, g, tm):
    rows = x2d.shape[0]
    return pl.pallas_call(
        _rmsnorm_kernel,
        grid=(rows // tm,),
        in_specs=[pl.BlockSpec((tm, D_MODEL), lambda i: (i, 0)),
                  pl.BlockSpec((1, D_MODEL), lambda i: (0, 0))],
        out_specs=pl.BlockSpec((tm, D_MODEL), lambda i: (i, 0)),
        out_shape=jax.ShapeDtypeStruct((rows, D_MODEL), BF16),
        compiler_params=pltpu.CompilerParams(
            dimension_semantics=("arbitrary",), vmem_limit_bytes=VMEM_LIMIT),
        name="rmsnorm_in",
    )(x2d, g)


def _inproj_kernel(feature_major, xn_ref, w_ref, o_ref, wb_ref):
    @pl.when(pl.program_id(1) == 0)
    def _():
        wb_ref[...] = w_ref[...].astype(BF16)

    if feature_major:
        o_ref[...] = _dot_nt(wb_ref[...], xn_ref[...]).astype(o_ref.dtype)
    else:
        o_ref[...] = _dot_nt(xn_ref[...], wb_ref[...]).astype(o_ref.dtype)


def _inproj(xn, wt, row_start, n_rows, tn, tm, out_dtype, feature_major):
    rows = xn.shape[0]
    assert n_rows % tn == 0 and rows % tm == 0 and row_start % SUBLANES == 0
    if row_start % tn == 0:
        j0 = row_start // tn
        w_spec = pl.BlockSpec((tn, D_MODEL), lambda j, i: (j0 + j, 0))
    else:
        w_spec = pl.BlockSpec((pl.Element(tn), pl.Element(D_MODEL)), lambda j, i: ((row_start // SUBLANES + j * (tn // SUBLANES)) * SUBLANES, 0))
    if feature_major:
        out_spec = pl.BlockSpec((tn, tm), lambda j, i: (j, i))
        out_shape = (n_rows, rows)
    else:
        out_spec = pl.BlockSpec((tm, tn), lambda j, i: (i, j))
        out_shape = (rows, n_rows)
    return pl.pallas_call(
        functools.partial(_inproj_kernel, feature_major),
        grid=(n_rows // tn, rows // tm),
        in_specs=[pl.BlockSpec((tm, D_MODEL), lambda j, i: (i, 0)), w_spec],
        out_specs=out_spec,
        out_shape=jax.ShapeDtypeStruct(out_shape, out_dtype),
        scratch_shapes=[pltpu.VMEM((tn, D_MODEL), BF16)],
        compiler_params=pltpu.CompilerParams(
            dimension_semantics=("arbitrary", "arbitrary"), vmem_limit_bytes=VMEM_LIMIT),
        name="inproj_t" if feature_major else "inproj",
    )(xn, wt)


def _prompt_attn_kernel(qt_ref, qit0_ref, qit1_ref, qit2_ref, qit3_ref, vt_ref, wit_ref, kvf_ref, kif_ref,
                        o_ref, kb_ref, kib_ref, vtc_ref, sc_ref, jlim_ref, bias_ref, acc_ref):
    i = pl.program_id(1)
    seq = kvf_ref.shape[0]
    qit_refs = (qit0_ref, qit1_ref, qit2_ref, qit3_ref)
    heads_per_ref = IDX_HEADS // len(qit_refs)

    @pl.when(i == 0)
    def _():
        kb_ref[...] = kvf_ref[:, 0:KV_WIDTH].astype(BF16)
        kib_ref[...] = kif_ref[...].astype(BF16)
        for c in range(seq // KEY_CHUNK):
            vtc_ref[c] = vt_ref[:, c * KEY_CHUNK:(c + 1) * KEY_CHUNK]

    n_chunks = (i * Q_BLOCK) // KEY_CHUNK + 1
    q_pos = i * Q_BLOCK + lax.broadcasted_iota(I32, (1, Q_BLOCK), 1)
    key0 = lax.broadcasted_iota(I32, (KEY_CHUNK, 1), 0)

    w_t = wit_ref[...] * IDX_W_SCALE
    qit_all = jnp.concatenate(
        [qit_refs[h // heads_per_ref][(h % heads_per_ref) * IDX_DIM:(h % heads_per_ref + 1) * IDX_DIM, :]
         for h in range(IDX_HEADS)], axis=1)

    def score_body(c, carry):
        k0 = pl.multiple_of(c * KEY_CHUNK, KEY_CHUNK)
        res = jnp.dot(kib_ref[pl.ds(k0, KEY_CHUNK), :], qit_all, preferred_element_type=F32)
        acc = jnp.zeros((KEY_CHUNK, Q_BLOCK), F32)
        for h in range(IDX_HEADS):
            acc = acc + w_t[h:h + 1, :] * jnp.maximum(res[:, h * Q_BLOCK:(h + 1) * Q_BLOCK], 0.0)
        sc_ref[c] = jnp.where(k0 + key0 <= q_pos, acc, NEG_SENTINEL)
        return carry

    lax.fori_loop(0, n_chunks, score_body, 0)

    def count(pred):
        def body(c, cnt):
            ones = jnp.where(pred(sc_ref[c], c), 1.0, 0.0)
            return cnt + ones.reshape(KEY_CHUNK // COUNT_ROWS, COUNT_ROWS, Q_BLOCK).sum(axis=0)
        part = lax.fori_loop(0, n_chunks, body, jnp.zeros((COUNT_ROWS, Q_BLOCK), F32))
        return jnp.sum(part, axis=0, keepdims=True)

    def bit_body(j, u):
        cand = u | lax.shift_left(jnp.int32(1), 31 - j)
        thr = _ordered_bits_to_f32(cand)
        cnt = count(lambda blk, c: blk >= thr)
        return jnp.where(cnt >= float(TOPK_MAX), cand, u)

    u = lax.fori_loop(0, 32, bit_body, jnp.zeros((1, Q_BLOCK), I32))
    thr = _ordered_bits_to_f32(u)
    n_ge = count(lambda blk, c: blk >= thr)
    n_gt = count(lambda blk, c: blk > thr)
    need = float(TOPK_MAX) - n_gt
    thr_adm = jnp.maximum(thr, NEG_ABOVE_SENTINEL)
    excess = jnp.max(jnp.where((n_ge > float(TOPK_MAX)) & (thr > NEG_SENTINEL), 1.0, 0.0)) > 0.0

    jlim_ref[...] = jnp.full((1, Q_BLOCK), 2 * seq, I32)

    @pl.when(excess)
    def _():
        def jbit_body(j, jl):
            cand = jl | lax.shift_left(jnp.int32(1), 12 - j)
            cnt = count(lambda blk, c: (blk == thr) & (c * KEY_CHUNK + key0 < cand))
            return jnp.where(cnt <= need, cand, jl)
        jlim_ref[...] = lax.fori_loop(0, 13, jbit_body, jnp.zeros((1, Q_BLOCK), I32))

    jlim = jlim_ref[...]

    acc_ref[...] = jnp.zeros(acc_ref.shape, F32)
    cols = Q_PER_KV * Q_BLOCK
    qt_groups = [
        jnp.concatenate([qt_ref[(g * Q_PER_KV + h) * HEAD_DIM:(g * Q_PER_KV + h + 1) * HEAD_DIM, :]
                         for h in range(Q_PER_KV)], axis=1)
        for g in range(N_KV_HEADS)
    ]

    def attn_body(c, carry):
        k0 = pl.multiple_of(c * KEY_CHUNK, KEY_CHUNK)
        blk = sc_ref[c]

        @pl.when(jnp.logical_not(excess))
        def _():
            bias_ref[...] = jnp.where(blk >= thr_adm, 0.0, -jnp.inf)

        @pl.when(excess)
        def _():
            keep = (blk >= thr_adm) & ((blk > thr) | (k0 + key0 < jlim))
            bias_ref[...] = jnp.where(keep, 0.0, -jnp.inf)

        bias = jnp.concatenate([bias_ref[...]] * Q_PER_KV, axis=1)
        out = []
        for g in range(N_KV_HEADS):
            m_old, l_old = carry[2 * g], carry[2 * g + 1]
            kc = kb_ref[pl.ds(k0, KEY_CHUNK), g * HEAD_DIM:(g + 1) * HEAD_DIM]
            s = jnp.dot(kc, qt_groups[g], preferred_element_type=F32) * ATTN_SCALE_LOG2E + bias
            m_new = jnp.maximum(m_old, jnp.max(s, axis=0, keepdims=True))
            m_safe = jnp.where(m_new == -jnp.inf, 0.0, m_new)
            alpha = jnp.exp2(m_old - m_safe)
            p = jnp.exp2(s - m_safe)
            l_new = alpha * l_old + jnp.sum(p, axis=0, keepdims=True)
            vtc = vtc_ref[c, g * HEAD_DIM:(g + 1) * HEAD_DIM, :]
            acc_ref[g] = alpha * acc_ref[g] + jnp.dot(vtc, p.astype(BF16), preferred_element_type=F32)
            out += [m_new, l_new]
        return tuple(out)

    init = (jnp.full((1, cols), -jnp.inf, F32), jnp.zeros((1, cols), F32)) * N_KV_HEADS
    stats = lax.fori_loop(0, n_chunks, attn_body, init)

    for g in range(N_KV_HEADS):
        o = acc_ref[g] / stats[2 * g + 1]
        for h in range(Q_PER_KV):
            hd = (g * Q_PER_KV + h) * HEAD_DIM
            o_ref[:, hd:hd + HEAD_DIM] = o[:, h * Q_BLOCK:(h + 1) * Q_BLOCK].T


def _prompt_attn(h_t, wi_t, kv, ki, batch, seq):
    nb = seq // Q_BLOCK
    cols = Q_PER_KV * Q_BLOCK
    qi_rows = IDX_HEADS * IDX_DIM // 4
    qi_b0 = COL_QI // qi_rows

    def qi_spec(t):
        return pl.BlockSpec((qi_rows, Q_BLOCK), lambda b, i: (qi_b0 + t, b * nb + i))

    return pl.pallas_call(
        _prompt_attn_kernel,
        grid=(batch, nb),
        in_specs=[
            pl.BlockSpec((ATTN_WIDTH, Q_BLOCK), lambda b, i: (COL_Q // ATTN_WIDTH, b * nb + i)),
            qi_spec(0), qi_spec(1), qi_spec(2), qi_spec(3),
            pl.BlockSpec((KV_WIDTH, seq), lambda b, i: (COL_V // KV_WIDTH, b)),
            pl.BlockSpec((IDX_HEADS, Q_BLOCK), lambda b, i: (0, b * nb + i)),
            pl.BlockSpec((seq, 2 * KV_WIDTH), lambda b, i: (b, 0)),
            pl.BlockSpec((seq, IDX_DIM), lambda b, i: (b, 0)),
        ],
        out_specs=pl.BlockSpec((Q_BLOCK, ATTN_WIDTH), lambda b, i: (b * nb + i, 0)),
        out_shape=jax.ShapeDtypeStruct((batch * seq, ATTN_WIDTH), F32),
        scratch_shapes=[
            pltpu.VMEM((seq, KV_WIDTH), BF16),
            pltpu.VMEM((seq, IDX_DIM), BF16),
            pltpu.VMEM((seq // KEY_CHUNK, KV_WIDTH, KEY_CHUNK), BF16),
            pltpu.VMEM((seq // KEY_CHUNK, KEY_CHUNK, Q_BLOCK), F32),
            pltpu.VMEM((1, Q_BLOCK), I32),
            pltpu.VMEM((KEY_CHUNK, Q_BLOCK), F32),
            pltpu.VMEM((N_KV_HEADS, HEAD_DIM, cols), F32),
        ],
        compiler_params=pltpu.CompilerParams(
            dimension_semantics=("arbitrary", "arbitrary"), vmem_limit_bytes=VMEM_LIMIT),
        name="prompt_attn",
    )(h_t, h_t, h_t, h_t, h_t, h_t, wi_t, kv, ki)


PAGES_PER_STEP = 32


def _sample_select_kernel(pt_ref, qi_ref, kvs_ref, *refs):
    page_refs = refs[:PAGES_PER_STEP]
    idx_ref, meta_ref, sc_ref, snew_ref = refs[PAGES_PER_STEP:]
    b = pl.program_id(0)
    p = pl.program_id(1)
    nb = pl.num_programs(0)
    n_steps = pl.num_programs(1)
    n_pages, db, _ = sc_ref.shape

    qi = qi_ref[0]
    w_col = kvs_ref[0, :, KVI_WI:KVI_WI + 1] * IDX_W_SCALE

    def key_scores(keys_bf16):
        s = _dot_nt(qi, keys_bf16)
        return jnp.sum(w_col * jnp.maximum(s, 0.0), axis=0, keepdims=True)

    keys = jnp.concatenate([page_refs[r][0, 0] for r in range(PAGES_PER_STEP)], axis=0).astype(BF16)
    row = key_scores(keys)
    for r in range(PAGES_PER_STEP):
        sc_ref[p * PAGES_PER_STEP + r, pl.ds(b, 1), :] = row[:, r * PAGE_SIZE:(r + 1) * PAGE_SIZE]

    @pl.when(p == n_steps - 1)
    def _():
        ki_new = kvs_ref[0, :, KVI_KI:KVI_KI + IDX_DIM].astype(BF16)
        s_new = key_scores(ki_new)[:, 0:1]
        snew_ref[pl.ds(b, 1), :] = jnp.broadcast_to(s_new, (1, PAGE_SIZE))

    @pl.when((b == nb - 1) & (p == n_steps - 1))
    def _():
        sc = sc_ref[...]
        s_new = snew_ref[:, 0:1]

        def total(x):
            return jnp.sum(jnp.sum(x, axis=0), axis=1, keepdims=True)

        def bit_body(j, u):
            cand = u | lax.shift_left(jnp.int32(1), 31 - j)
            thr = _ordered_bits_to_f32(cand)
            cnt = total(jnp.where(sc >= thr[None], 1.0, 0.0)) + jnp.where(s_new >= thr, 1.0, 0.0)
            return jnp.where(cnt >= float(TOPK_MAX), cand, u)

        u = lax.fori_loop(0, 32, bit_body, jnp.zeros((db, 1), I32))
        thr = _ordered_bits_to_f32(u)
        n_gt = total(jnp.where(sc > thr[None], 1.0, 0.0)) + jnp.where(s_new > thr, 1.0, 0.0)
        need = float(TOPK_MAX) - n_gt
        flat = (lax.broadcasted_iota(I32, sc.shape, 0) * PAGE_SIZE
                + lax.broadcasted_iota(I32, sc.shape, 2))
        tie = sc == thr[None]

        def jbit_body(j, jl):
            cand = jl | lax.shift_left(jnp.int32(1), 14 - j)
            cnt = total(jnp.where(tie & (flat < cand[None]), 1.0, 0.0))
            return jnp.where(cnt <= need, cand, jl)

        jlim = lax.fori_loop(0, 15, jbit_body, jnp.zeros((db, 1), I32))
        sel = (sc > thr[None]) | (tie & (flat < jlim[None]))
        self32 = jnp.where(sel, 1.0, 0.0)
        n_pool = total(self32)

        n_rows = n_pages * db
        selb = self32.astype(BF16).reshape(n_rows, PAGE_SIZE)
        ri = lax.broadcasted_iota(I32, (PAGE_SIZE, PAGE_SIZE), 0)
        ci = lax.broadcasted_iota(I32, (PAGE_SIZE, PAGE_SIZE), 1)
        incl = jnp.dot(selb, jnp.where(ri <= ci, 1.0, 0.0).astype(BF16), preferred_element_type=F32)
        tot = jnp.broadcast_to(incl[:, PAGE_SIZE - 1:PAGE_SIZE], (n_rows, PAGE_SIZE)).astype(BF16)
        rr = lax.broadcasted_iota(I32, (n_rows, n_rows), 0)
        cc = lax.broadcasted_iota(I32, (n_rows, n_rows), 1)
        earlier_page_same_row = ((rr & (db - 1)) == (cc & (db - 1))) & (cc < (rr & -db))
        offs = jnp.dot(jnp.where(earlier_page_same_row, 1.0, 0.0).astype(BF16), tot,
                       preferred_element_type=F32)
        ends_rows = offs + tot.astype(F32)
        r_i = lax.broadcasted_iota(I32, (n_rows, PAGE_SIZE), 0)
        l_i = lax.broadcasted_iota(I32, (n_rows, PAGE_SIZE), 1)
        log2_db = db.bit_length() - 1
        diag = jnp.where(l_i == lax.shift_right_logical(r_i, log2_db), ends_rows, 0.0).astype(BF16)
        pick = jnp.where((lax.broadcasted_iota(I32, (db, n_rows), 1) & (db - 1))
                         == lax.broadcasted_iota(I32, (db, n_rows), 0), 1.0, 0.0).astype(BF16)
        ends = jnp.dot(pick, diag, preferred_element_type=F32)

        table = jnp.concatenate([incl.astype(BF16), offs.astype(BF16)], axis=1)
        slot = lax.broadcasted_iota(I32, (TOPK_MAX, PAGE_SIZE), 0).astype(F32)
        rows_lane = lax.broadcasted_iota(I32, (TOPK_MAX, n_rows), 1)
        for row_i in range(db):
            page_s = jnp.sum(jnp.where(ends[row_i:row_i + 1, :] <= slot, 1.0, 0.0), axis=1, keepdims=True)
            mrow = page_s.astype(I32) * db + row_i
            onehot = jnp.where(rows_lane == mrow, 1.0, 0.0).astype(BF16)
            got = jnp.dot(onehot, table, preferred_element_type=F32)
            k = slot - got[:, PAGE_SIZE:]
            off_s = jnp.sum(jnp.where(got[:, :PAGE_SIZE] <= k, 1.0, 0.0), axis=1, keepdims=True)
            pos = page_s * float(PAGE_SIZE) + off_s
            valid = slot[:, 0:1] < n_pool[row_i:row_i + 1, :]
            idx_ref[row_i] = jnp.where(valid, pos, 0.0).astype(I32)
        meta_ref[...] = jnp.broadcast_to(n_pool, (db, PAGE_SIZE)).astype(I32)


def _sample_select(page_table, qi_s, kvi_s16, pool_ki):
    db, n_pages = page_table.shape
    assert db & (db - 1) == 0, "row index arithmetic in the rank matrix assumes a power-of-two row count"
    assert n_pages == PAGE_SIZE, "page ends are laid out one page per lane"
    n_steps = n_pages // PAGES_PER_STEP

    def page_spec(r):
        return pl.BlockSpec(
            (1, 1, PAGE_SIZE, IDX_DIM),
            lambda b, p, pt: (0, pt[b, p * PAGES_PER_STEP + r], 0, 0))

    grid_spec = pltpu.PrefetchScalarGridSpec(
        num_scalar_prefetch=1,
        grid=(db, n_steps),
        in_specs=[
            pl.BlockSpec((1, IDX_HEADS, IDX_DIM), lambda b, p, pt: (b, 0, 0)),
            pl.BlockSpec((1, IDX_HEADS, KVI_WIDTH), lambda b, p, pt: (b, 0, 0)),
        ] + [page_spec(r) for r in range(PAGES_PER_STEP)],
        out_specs=[
            pl.BlockSpec((db, TOPK_MAX, 1), lambda b, p, pt: (0, 0, 0)),
            pl.BlockSpec((db, PAGE_SIZE), lambda b, p, pt: (0, 0)),
        ],
        scratch_shapes=[
            pltpu.VMEM((n_pages, db, PAGE_SIZE), F32),
            pltpu.VMEM((db, PAGE_SIZE), F32),
        ],
    )
    return pl.pallas_call(
        _sample_select_kernel,
        grid_spec=grid_spec,
        out_shape=[
            jax.ShapeDtypeStruct((db, TOPK_MAX, 1), I32),
            jax.ShapeDtypeStruct((db, PAGE_SIZE), I32),
        ],
        compiler_params=pltpu.CompilerParams(
            dimension_semantics=("arbitrary", "arbitrary"), vmem_limit_bytes=VMEM_LIMIT),
        name="sample_select",
    )(page_table, qi_s, kvi_s16, *([pool_ki] * PAGES_PER_STEP))


def _sample_attend_kernel(idx_ref, npool_ref, pt_ref, q_ref, kvs_ref, pk_ref, pv_ref, o_ref,
                          kbuf, vbuf, kflat, vflat, sem):
    b = pl.program_id(0)

    def row_copies(r):
        ix = idx_ref[b, r]
        page = pt_ref[b, lax.shift_right_logical(ix, 7)]
        off = ix & (PAGE_SIZE - 1)
        return (pltpu.make_async_copy(pk_ref.at[0, page, off], kbuf.at[r], sem.at[0]),
                pltpu.make_async_copy(pv_ref.at[0, page, off], vbuf.at[r], sem.at[1]))

    def start_body(r, carry):
        ck, cv = row_copies(r)
        ck.start()
        cv.start()
        return carry

    def wait_body(r, carry):
        ck, cv = row_copies(r)
        ck.wait()
        cv.wait()
        return carry

    lax.fori_loop(0, TOPK_MAX, start_body, 0)
    lax.fori_loop(0, TOPK_MAX, wait_body, 0)

    def head_copies(g):
        return (pltpu.make_async_copy(kbuf.at[:, g, :], kflat.at[g], sem.at[2]),
                pltpu.make_async_copy(vbuf.at[:, g, :], vflat.at[g], sem.at[3]))

    for g in range(N_KV_HEADS):
        for cp in head_copies(g):
            cp.start()
    for g in range(N_KV_HEADS):
        for cp in head_copies(g):
            cp.wait()

    n_pool = npool_ref[b]
    slot = lax.broadcasted_iota(I32, (1, TOPK_MAX), 1)
    slot_ok = slot < n_pool
    new_ok = n_pool < TOPK_MAX
    q_all = q_ref[0].astype(F32)
    for g in range(N_KV_HEADS):
        qg = q_all[g * Q_PER_KV:(g + 1) * Q_PER_KV, :].astype(BF16)
        ks = kflat[g].astype(BF16)
        vs = vflat[g].astype(BF16)
        k_new = kvs_ref[0, 0:1, KVI_K + g * HEAD_DIM:KVI_K + (g + 1) * HEAD_DIM].astype(BF16)
        v_new = kvs_ref[0, 0:1, KVI_V + g * HEAD_DIM:KVI_V + (g + 1) * HEAD_DIM].astype(BF16)
        s = jnp.where(slot_ok, _dot_nt(qg, ks) * ATTN_SCALE, -jnp.inf)
        s_new = jnp.sum(qg.astype(F32) * k_new.astype(F32), axis=-1, keepdims=True) * ATTN_SCALE
        s_new = jnp.where(new_ok, s_new, -jnp.inf)
        m = jnp.maximum(jnp.max(s, axis=-1, keepdims=True), s_new)
        p = jnp.exp(s - m)
        p_new = jnp.exp(s_new - m)
        denom = jnp.sum(p, axis=-1, keepdims=True) + p_new
        pn = (p / denom).astype(BF16)
        pn_new = (p_new / denom).astype(BF16).astype(F32)
        o = jnp.dot(pn, vs, preferred_element_type=F32) + pn_new * v_new.astype(F32)
        o_ref[0, g * Q_PER_KV:(g + 1) * Q_PER_KV, :] = o


def _sample_attend(idx, n_pool, page_table, q_s, kvi_s16, pool_k, pool_v):
    db = page_table.shape[0]
    grid_spec = pltpu.PrefetchScalarGridSpec(
        num_scalar_prefetch=3,
        grid=(db,),
        in_specs=[
            pl.BlockSpec((1, N_HEADS, HEAD_DIM), lambda b, *_: (b, 0, 0)),
            pl.BlockSpec((1, IDX_HEADS, KVI_WIDTH), lambda b, *_: (b, 0, 0)),
            pl.BlockSpec(memory_space=pl.ANY),
            pl.BlockSpec(memory_space=pl.ANY),
        ],
        out_specs=pl.BlockSpec((1, N_HEADS, HEAD_DIM), lambda b, *_: (b, 0, 0)),
        scratch_shapes=[
            pltpu.VMEM((TOPK_MAX, N_KV_HEADS, HEAD_DIM), F32),
            pltpu.VMEM((TOPK_MAX, N_KV_HEADS, HEAD_DIM), F32),
            pltpu.VMEM((N_KV_HEADS, TOPK_MAX, HEAD_DIM), F32),
            pltpu.VMEM((N_KV_HEADS, TOPK_MAX, HEAD_DIM), F32),
            pltpu.SemaphoreType.DMA((4,)),
        ],
    )
    return pl.pallas_call(
        _sample_attend_kernel,
        grid_spec=grid_spec,
        out_shape=jax.ShapeDtypeStruct((db, N_HEADS, HEAD_DIM), F32),
        compiler_params=pltpu.CompilerParams(dimension_semantics=("arbitrary",)),
        name="sample_attend",
    )(idx, n_pool, page_table, q_s, kvi_s16, pool_k, pool_v)


def _merge_kernel(chunked, oa_ref, za_ref, u_ref, v_ref, zb_ref, ga_ref, gb_ref, x_ref,
                  wpa_ref, wpb_ref, wout_ref, lng_ref, lnb_ref, ws_ref, bs_ref, gf_ref,
                  y_ref, vn_ref, ob_ref):
    tm = x_ref.shape[0]
    u = _gelu(u_ref[...])
    v = _gelu(v_ref[...])
    mu = jnp.mean(v, axis=-1, keepdims=True)
    vc = v - mu
    vn = (vc * lax.rsqrt(jnp.mean(vc * vc, axis=-1, keepdims=True) + LN_EPS)) * lng_ref[...] + lnb_ref[...]
    vn_ref[...] = vn

    if chunked:
        vnb = vn.astype(BF16)
        ri = lax.broadcasted_iota(I32, (CHUNK, CHUNK), 0)
        ci = lax.broadcasted_iota(I32, (CHUNK, CHUNK), 1)
        for g in range(GMLP_GROUPS):
            wm = jnp.where(ri >= ci, ws_ref[g], 0.0).astype(BF16)
            lo = g * GMLP_GROUP_DIM
            for c in range(tm // CHUNK):
                mixed = jnp.dot(wm, vnb[c * CHUNK:(c + 1) * CHUNK, lo:lo + GMLP_GROUP_DIM],
                                preferred_element_type=F32) + bs_ref[:, g:g + 1]
                ob_ref[c * CHUNK:(c + 1) * CHUNK, lo:lo + GMLP_GROUP_DIM] = (
                    u[c * CHUNK:(c + 1) * CHUNK, lo:lo + GMLP_GROUP_DIM] * mixed)
    else:
        ob_ref[...] = u * (ws_ref[...] * vn + bs_ref[...])

    ha = (oa_ref[...] * _silu(za_ref[...])).astype(BF16)
    hb = (ob_ref[...] * _silu(zb_ref[...])).astype(BF16)
    ya = jnp.dot(ha, wpa_ref[...], preferred_element_type=F32)
    yb = jnp.dot(hb, wpb_ref[...], preferred_element_type=F32)
    mix = jax.nn.sigmoid(ga_ref[...]) * ya + jax.nn.sigmoid(gb_ref[...]) * yb
    out = x_ref[...] + jnp.dot(mix.astype(BF16), wout_ref[...], preferred_element_type=F32)
    ms = jnp.mean(out * out, axis=-1, keepdims=True)
    y_ref[...] = (out * lax.rsqrt(ms + RMS_EPS)) * gf_ref[...]


def _merge(chunked, o_a, gate, x2d, wpa, wpb, wout, lng, lnb, ws, bs, gf, tm):
    rows = x2d.shape[0]
    gw = GMLP_WIDTH

    def const(shape):
        return pl.BlockSpec(shape, lambda i: (0,) * len(shape), pipeline_mode=pl.Buffered(1))

    return pl.pallas_call(
        functools.partial(_merge_kernel, chunked),
        grid=(rows // tm,),
        in_specs=[
            pl.BlockSpec((tm, ATTN_WIDTH), lambda i: (i, 0)),
            pl.BlockSpec((tm, gw), lambda i: (i, 0)),
            pl.BlockSpec((tm, gw), lambda i: (i, 1)),
            pl.BlockSpec((tm, gw), lambda i: (i, 2)),
            pl.BlockSpec((tm, gw), lambda i: (i, 3)),
            pl.BlockSpec((tm, D_MODEL), lambda i: (i, 2)),
            pl.BlockSpec((tm, D_MODEL), lambda i: (i, 3)),
            pl.BlockSpec((tm, D_MODEL), lambda i: (i, 0)),
            const(wpa.shape), const(wpb.shape), const(wout.shape),
            const(lng.shape), const(lnb.shape), const(ws.shape), const(bs.shape), const(gf.shape),
        ],
        out_specs=[
            pl.BlockSpec((tm, D_MODEL), lambda i: (i, 0)),
            pl.BlockSpec((tm, gw), lambda i: (i, 0)),
        ],
        out_shape=[
            jax.ShapeDtypeStruct((rows, D_MODEL), F32),
            jax.ShapeDtypeStruct((rows, gw), F32),
        ],
        scratch_shapes=[pltpu.VMEM((tm, gw), F32)],
        compiler_params=pltpu.CompilerParams(
            dimension_semantics=("arbitrary",), vmem_limit_bytes=VMEM_LIMIT),
        name="merge_prompt" if chunked else "merge_sample",
    )(o_a, gate, gate, gate, gate, gate, gate, x2d, wpa, wpb, wout, lng, lnb, ws, bs, gf)


def kernel(x_prompt, x_sample, cache_k, cache_v, cache_k_idx, page_table, norm_in_g, w_in,
           w_proj_a, w_proj_b, w_out, ln_g, ln_b, w_spatial, b_spatial, norm_f_g):
    depth = w_in.shape[0]
    assert depth == 1, "single trunk layer"
    batch, seq, _ = x_prompt.shape
    db, dseq, _ = x_sample.shape
    assert dseq == 1
    l = 0
    wpa = w_proj_a[l].astype(BF16)
    wpb = w_proj_b[l].astype(BF16)
    wout = w_out[l].astype(BF16)
    g_in = norm_in_g[l].reshape(1, D_MODEL)
    lng = ln_g[l].reshape(1, GMLP_WIDTH)
    lnb = ln_b[l].reshape(1, GMLP_WIDTH)
    gf = norm_f_g.reshape(1, D_MODEL)

    xp = x_prompt.reshape(batch * seq, D_MODEL)
    wt = w_in[l].T
    xn_p = _rmsnorm_bf16(xp, g_in, 1024)
    h_t = _inproj(xn_p, wt, COL_Q, COL_WI, 512, 1024, BF16, True)
    wi_t = _inproj(xn_p, wt, COL_WI, IDX_HEADS, IDX_HEADS, 1024, F32, True)
    kv_p = _inproj(xn_p, wt, COL_K, 2 * KV_WIDTH, 2 * KV_WIDTH, 1024, F32, False)
    ki_p2 = _inproj(xn_p, wt, COL_KI, IDX_DIM, IDX_DIM, 1024, F32, False)
    gate_p = _inproj(xn_p, wt, GATE_COL0, GATE_WIDTH, 1024, 1024, F32, False)
    oa_p = _prompt_attn(h_t, wi_t, kv_p, ki_p2, batch, seq)
    bs_t = b_spatial[l].T
    y_p, _ = _merge(True, oa_p, gate_p, xp, wpa, wpb, wout, lng, lnb, w_spatial[l], bs_t, gf, 256)

    xs = x_sample.reshape(db, D_MODEL)
    xn_s = _rmsnorm_bf16(xs, g_in, db)
    h_s = _inproj(xn_s, wt, COL_Q, COL_WI, 512, db, F32, False)
    wi_s = _inproj(xn_s, wt, COL_WI, IDX_HEADS, IDX_HEADS, db, F32, True).T
    ki_s2 = _inproj(xn_s, wt, COL_KI, IDX_DIM, IDX_DIM, db, F32, False)
    gate_s = _inproj(xn_s, wt, GATE_COL0, GATE_WIDTH, 1024, db, F32, False)
    kvi_s = jnp.concatenate(
        [h_s[:, COL_K:COL_QI], ki_s2, wi_s, jnp.zeros((db, KVI_WIDTH - KVI_WI - IDX_HEADS), F32)], axis=1)
    q_s = h_s[:, COL_Q:COL_K].astype(BF16).reshape(db, N_HEADS, HEAD_DIM)
    qi_s = h_s[:, COL_QI:COL_WI].astype(BF16).reshape(db, IDX_HEADS, IDX_DIM)
    wi_col = kvi_s[:, KVI_WI:KVI_WI + IDX_HEADS].reshape(db, IDX_HEADS, 1)
    kvi_s16 = jnp.broadcast_to(kvi_s[:, None, :], (db, IDX_HEADS, KVI_WIDTH))
    kvi_s16 = lax.dynamic_update_slice(kvi_s16, wi_col, (0, 0, KVI_WI))
    idx, meta = _sample_select(page_table, qi_s, kvi_s16, cache_k_idx)
    oa_s = _sample_attend(idx.reshape(db, TOPK_MAX), meta[:, 0], page_table, q_s, kvi_s16,
                          cache_k, cache_v).reshape(db, ATTN_WIDTH)
    ws0 = jnp.repeat(w_spatial[l][:, 0, 0], GMLP_GROUP_DIM).reshape(1, GMLP_WIDTH)
    bs0 = jnp.repeat(b_spatial[l][:, 0], GMLP_GROUP_DIM).reshape(1, GMLP_WIDTH)
    y_s, vn_s = _merge(False, oa_s, gate_s, xs, wpa, wpb, wout, lng, lnb, ws0, bs0, gf, db)

    def kv_out(kvi, lead):
        k = kvi[:, KVI_K:KVI_K + KV_WIDTH].reshape((1,) + lead + (N_KV_HEADS, HEAD_DIM))
        v = kvi[:, KVI_V:KVI_V + KV_WIDTH].reshape((1,) + lead + (N_KV_HEADS, HEAD_DIM))
        ki = kvi[:, KVI_KI:KVI_KI + IDX_DIM].reshape((1,) + lead + (IDX_DIM,))
        return k, v, ki

    k_p = kv_p[:, :KV_WIDTH].reshape(1, batch, seq, N_KV_HEADS, HEAD_DIM)
    v_p = kv_p[:, KV_WIDTH:].reshape(1, batch, seq, N_KV_HEADS, HEAD_DIM)
    ki_p = ki_p2.reshape(1, batch, seq, IDX_DIM)
    k_s, v_s, ki_s = kv_out(kvi_s, (db, dseq))
    return (y_p.reshape(batch, seq, D_MODEL), y_s.reshape(db, dseq, D_MODEL),
            k_p, v_p, ki_p, k_s, v_s, ki_s, vn_s.reshape(1, db, dseq, GMLP_WIDTH))
```

```python
import functools

import jax
import jax.numpy as jnp
import numpy as np
from jax import lax
from jax.experimental import pallas as pl
from jax.experimental.pallas import tpu as pltpu

F32 = jnp.float32
BF16 = jnp.bfloat16
I32 = jnp.int32

D_MODEL = 2048
N_HEADS = 8
N_KV_HEADS = 2
HEAD_DIM = 128
Q_PER_KV = N_HEADS // N_KV_HEADS
ATTN_WIDTH = N_HEADS * HEAD_DIM
KV_WIDTH = N_KV_HEADS * HEAD_DIM
IDX_HEADS = 16
IDX_DIM = 128
TOPK_MAX = 256
PAGE_SIZE = 128
GMLP_WIDTH = 1024
GMLP_GROUPS = 8
GMLP_GROUP_DIM = GMLP_WIDTH // GMLP_GROUPS
CHUNK = 128
RMS_EPS = 1e-6
LN_EPS = 1e-5

SUBLANES = 8
COL_Q = 0
COL_K = COL_Q + ATTN_WIDTH
COL_V = COL_K + KV_WIDTH
COL_QI = COL_V + KV_WIDTH
COL_WI = COL_QI + IDX_HEADS * IDX_DIM
COL_KI = COL_WI + IDX_HEADS
GATE_COL0 = COL_KI + IDX_DIM
GATE_WIDTH = 4 * GMLP_WIDTH + 2 * D_MODEL
KVI_WIDTH = 768
KVI_K, KVI_V, KVI_KI, KVI_WI = 0, 256, 512, 640

Q_BLOCK = 128
KEY_CHUNK = 512
COUNT_ROWS = 64
ONES_ROWS = 16
SIGN_EXP_BITS = 9
SEARCH_ALWAYS_BITS = 16
SEARCH_GROUP_BITS = 4
NEG_SENTINEL = float(np.finfo(np.float32).min)
NEG_ABOVE_SENTINEL = float(np.nextafter(np.float32(NEG_SENTINEL), np.float32(0.0)))
INT_MIN = -(2 ** 31)
IDX_W_SCALE = float(IDX_HEADS ** -0.5 * IDX_DIM ** -0.5)
ATTN_SCALE = float(HEAD_DIM ** -0.5)
ATTN_SCALE_LOG2E = float(HEAD_DIM ** -0.5 * np.log2(np.e))
VMEM_LIMIT = 56 * 1024 * 1024


def _dot_nt(a, b):
    return lax.dot_general(a, b, (((1,), (1,)), ((), ())), preferred_element_type=F32)


def _ordered_bits_to_f32(u):
    signed_pow, zero_exp = _sign_exp_parts(u)
    return _with_mantissa(u, signed_pow, zero_exp)


def _sign_exp_parts(u):
    bits = jnp.where(u < 0, u ^ INT_MIN, ~u)
    exp = lax.shift_right_logical(bits, 23) & 0xFF
    e1 = exp - 1
    p = jnp.full(u.shape, float(2.0 ** -126), F32)
    for b in range(7):
        p = jnp.where(((e1 >> b) & 1) == 1, p * float(2.0 ** (2 ** b)), p)
    top = ((e1 >> 7) & 1) == 1
    p = jnp.where(top, p * float(2.0 ** 64), p)
    p = jnp.where(top, p * float(2.0 ** 64), p)
    return jnp.where(bits < 0, -p, p), exp == 0


def _with_mantissa(u, signed_pow, zero_exp):
    low = jnp.where(u < 0, u, ~u) & 0x7FFFFF
    mant = 1.0 + low.astype(F32) * float(2.0 ** -23)
    return jnp.where(zero_exp, 0.0, signed_pow * mant)


def _gelu(x):
    return 0.5 * x * (1.0 + lax.erf(x * float(np.sqrt(0.5))))


def _silu(x):
    return x * jax.nn.sigmoid(x)


def _rmsnorm_kernel(x_ref, g_ref, o_ref):
    x = x_ref[...]
    ms = jnp.mean(x * x, axis=-1, keepdims=True)
    o_ref[...] = ((x * lax.rsqrt(ms + RMS_EPS)) * g_ref[...]).astype(o_ref.dtype)


def _rmsnorm_bf16(x2d, g, tm):
    rows = x2d.shape[0]
    return pl.pallas_call(
        _rmsnorm_kernel,
        grid=(rows // tm,),
        in_specs=[pl.BlockSpec((tm, D_MODEL), lambda i: (i, 0)),
                  pl.BlockSpec((1, D_MODEL), lambda i: (0, 0))],
        out_specs=pl.BlockSpec((tm, D_MODEL), lambda i: (i, 0)),
        out_shape=jax.ShapeDtypeStruct((rows, D_MODEL), BF16),
        compiler_params=pltpu.CompilerParams(
            dimension_semantics=("arbitrary",), vmem_limit_bytes=VMEM_LIMIT),
        name="rmsnorm_in",
    )(x2d, g)


def _inproj_kernel(feature_major, xn_ref, w_ref, o_ref, wb_ref):
    @pl.when(pl.program_id(1) == 0)
    def _():
        wb_ref[...] = w_ref[...].astype(BF16)

    if feature_major:
        o_ref[...] = _dot_nt(wb_ref[...], xn_ref[...]).astype(o_ref.dtype)
    else:
        o_ref[...] = _dot_nt(xn_ref[...], wb_ref[...]).astype(o_ref.dtype)


def _inproj(xn, wt, row_start, n_rows, tn, tm, out_dtype, feature_major):
    rows = xn.shape[0]
    assert n_rows % tn == 0 and rows % tm == 0 and row_start % SUBLANES == 0
    if row_start % tn == 0:
        j0 = row_start // tn
        w_spec = pl.BlockSpec((tn, D_MODEL), lambda j, i: (j0 + j, 0))
    else:
        w_spec = pl.BlockSpec((pl.Element(tn), pl.Element(D_MODEL)), lambda j, i: ((row_start // SUBLANES + j * (tn // SUBLANES)) * SUBLANES, 0))
    if feature_major:
        out_spec = pl.BlockSpec((tn, tm), lambda j, i: (j, i))
        out_shape = (n_rows, rows)
    else:
        out_spec = pl.BlockSpec((tm, tn), lambda j, i: (i, j))
        out_shape = (rows, n_rows)
    return pl.pallas_call(
        functools.partial(_inproj_kernel, feature_major),
        grid=(n_rows // tn, rows // tm),
        in_specs=[pl.BlockSpec((tm, D_MODEL), lambda j, i: (i, 0)), w_spec],
        out_specs=out_spec,
        out_shape=jax.ShapeDtypeStruct(out_shape, out_dtype),
        scratch_shapes=[pltpu.VMEM((tn, D_MODEL), BF16)],
        compiler_params=pltpu.CompilerParams(
            dimension_semantics=("arbitrary", "arbitrary"), vmem_limit_bytes=VMEM_LIMIT),
        name="inproj_t" if feature_major else "inproj",
    )(xn, wt)


def _prompt_attn_kernel(qt_ref, qit0_ref, qit1_ref, qit2_ref, qit3_ref, vt_ref, wit_ref, kvf_ref, kif_ref,
                        o_ref, kb_ref, kib_ref, vtc_ref, sc_ref, sa_ref, sb_ref, acc_ref, ml_ref, u_ref, cnt_ref):
    i = pl.program_id(1)
    seq = kvf_ref.shape[0]
    qit_refs = (qit0_ref, qit1_ref, qit2_ref, qit3_ref)
    heads_per_ref = IDX_HEADS // len(qit_refs)

    @pl.when(i == 0)
    def _():
        kb_ref[...] = kvf_ref[:, 0:KV_WIDTH].astype(BF16)
        kib_ref[...] = kif_ref[...].astype(BF16)
        for c in range(seq // KEY_CHUNK):
            for g in range(N_KV_HEADS):
                vtc_ref[c, g, :HEAD_DIM, :] = vt_ref[g * HEAD_DIM:(g + 1) * HEAD_DIM,
                                                     c * KEY_CHUNK:(c + 1) * KEY_CHUNK]
                vtc_ref[c, g, HEAD_DIM:, :] = jnp.ones((ONES_ROWS, KEY_CHUNK), BF16)

    n_chunks = (i * Q_BLOCK) // KEY_CHUNK + 1
    q_pos = i * Q_BLOCK + lax.broadcasted_iota(I32, (1, Q_BLOCK), 1)
    key0 = lax.broadcasted_iota(I32, (KEY_CHUNK, 1), 0)

    w_t = wit_ref[...] * IDX_W_SCALE
    qit_all = jnp.concatenate(
        [qit_refs[h // heads_per_ref][(h % heads_per_ref) * IDX_DIM:(h % heads_per_ref + 1) * IDX_DIM, :]
         for h in range(IDX_HEADS)], axis=1)

    def score_body(c, carry):
        k0 = pl.multiple_of(c * KEY_CHUNK, KEY_CHUNK)
        res = jnp.dot(kib_ref[pl.ds(k0, KEY_CHUNK), :], qit_all, preferred_element_type=F32)
        acc = jnp.zeros((KEY_CHUNK, Q_BLOCK), F32)
        for h in range(IDX_HEADS):
            acc = acc + w_t[h:h + 1, :] * jnp.maximum(res[:, h * Q_BLOCK:(h + 1) * Q_BLOCK], 0.0)
        sc_ref[c] = jnp.where(k0 + key0 <= q_pos, acc, NEG_SENTINEL)
        return carry

    lax.fori_loop(0, n_chunks, score_body, 0)

    def count(pred):
        def body(c, cnt):
            ones = jnp.where(pred(sc_ref[c], c), 1.0, 0.0)
            return cnt + ones.reshape(KEY_CHUNK // COUNT_ROWS, COUNT_ROWS, Q_BLOCK).sum(axis=0)
        part = lax.fori_loop(0, n_chunks, body, jnp.zeros((COUNT_ROWS, Q_BLOCK), F32))
        return jnp.sum(part, axis=0, keepdims=True)

    def search(j0, n, u, cnt_u, decode):
        def body(j, carry):
            u, cnt_u = carry
            cand = u | lax.shift_left(jnp.int32(1), 31 - j)
            thr_c = decode(cand)
            cnt = count(lambda blk, c: blk >= thr_c)
            ok = cnt >= float(TOPK_MAX)
            return jnp.where(ok, cand, u), jnp.where(ok, cnt, cnt_u)
        return lax.fori_loop(j0, j0 + n, body, (u, cnt_u))

    u = jnp.zeros((1, Q_BLOCK), I32)
    cnt_u = (u + n_chunks * KEY_CHUNK).astype(F32)
    u, cnt_u = search(0, SIGN_EXP_BITS, u, cnt_u, _ordered_bits_to_f32)
    signed_pow, zero_exp = _sign_exp_parts(u)

    def decode_mantissa(cand):
        return _with_mantissa(cand, signed_pow, zero_exp)

    u, cnt_u = search(SIGN_EXP_BITS, SEARCH_ALWAYS_BITS - SIGN_EXP_BITS, u, cnt_u, decode_mantissa)
    u_ref[...] = u
    cnt_ref[...] = cnt_u
    for j0 in range(SEARCH_ALWAYS_BITS, 32, SEARCH_GROUP_BITS):
        @pl.when(jnp.max(jnp.where(cnt_ref[...] == float(TOPK_MAX), 0.0, 1.0)) > 0.0)
        def _():
            u_g, cnt_g = search(j0, SEARCH_GROUP_BITS, u_ref[...], cnt_ref[...], decode_mantissa)
            u_ref[...] = u_g
            cnt_ref[...] = cnt_g
    u = u_ref[...]
    n_ge = cnt_ref[...]
    thr = decode_mantissa(u)
    thr_adm = jnp.maximum(thr, NEG_ABOVE_SENTINEL)
    excess = jnp.max(jnp.where((n_ge > float(TOPK_MAX)) & (thr > NEG_SENTINEL), 1.0, 0.0)) > 0.0

    @pl.when(excess)
    def _():
        need = float(TOPK_MAX) - count(lambda blk, c: blk > thr)

        def jbit_body(j, jl):
            cand = jl | lax.shift_left(jnp.int32(1), 12 - j)
            cnt = count(lambda blk, c: (blk == thr) & (c * KEY_CHUNK + key0 < cand))
            return jnp.where(cnt <= need, cand, jl)
        jlim = lax.fori_loop(0, 13, jbit_body, jnp.zeros((1, Q_BLOCK), I32))

        def demote_body(c, carry):
            blk = sc_ref[c]
            dropped = (blk == thr) & (thr > NEG_SENTINEL) & (c * KEY_CHUNK + key0 >= jlim)
            sc_ref[c] = jnp.where(dropped, NEG_SENTINEL, blk)
            return carry
        lax.fori_loop(0, n_chunks, demote_body, 0)

    acc_ref[...] = jnp.zeros(acc_ref.shape, F32)
    cols = Q_PER_KV * Q_BLOCK
    qt_groups = [
        jnp.concatenate([qt_ref[(g * Q_PER_KV + h) * HEAD_DIM:(g * Q_PER_KV + h + 1) * HEAD_DIM, :]
                         for h in range(Q_PER_KV)], axis=1)
        for g in range(N_KV_HEADS)
    ]

    last_chunk = seq // KEY_CHUNK - 1

    def qk_scores(c, dst_ref):
        k0 = pl.multiple_of(jnp.minimum(c, last_chunk) * KEY_CHUNK, KEY_CHUNK)
        for g in range(N_KV_HEADS):
            kc = kb_ref[pl.ds(k0, KEY_CHUNK), g * HEAD_DIM:(g + 1) * HEAD_DIM]
            dst_ref[g] = jnp.dot(kc, qt_groups[g], preferred_element_type=F32)

    def softmax_pv(c, src_ref):
        bias1 = jnp.where(sc_ref[c] >= thr_adm, 0.0, -jnp.inf)
        bias = jnp.concatenate([bias1] * Q_PER_KV, axis=1)
        for g in range(N_KV_HEADS):
            m_old = ml_ref[2 * g:2 * g + 1, :]
            l_old = ml_ref[2 * g + 1:2 * g + 2, :]
            s = src_ref[g] * ATTN_SCALE_LOG2E + bias
            m_new = jnp.maximum(m_old, jnp.max(s, axis=0, keepdims=True))
            m_safe = jnp.where(m_new == -jnp.inf, 0.0, m_new)
            alpha = jnp.exp2(m_old - m_safe)
            p = jnp.exp2(s - m_safe).astype(BF16)
            pv = jnp.dot(vtc_ref[c, g], p, preferred_element_type=F32)
            acc_ref[g] = alpha * acc_ref[g] + pv[:HEAD_DIM]
            ml_ref[2 * g:2 * g + 1, :] = m_new
            ml_ref[2 * g + 1:2 * g + 2, :] = alpha * l_old + pv[HEAD_DIM:HEAD_DIM + 1]

    def pair_body(t, carry):
        qk_scores(2 * t + 1, sb_ref)
        softmax_pv(2 * t, sa_ref)
        qk_scores(2 * t + 2, sa_ref)
        softmax_pv(2 * t + 1, sb_ref)
        return carry

    for g in range(N_KV_HEADS):
        ml_ref[2 * g:2 * g + 1, :] = jnp.full((1, cols), -jnp.inf, F32)
        ml_ref[2 * g + 1:2 * g + 2, :] = jnp.zeros((1, cols), F32)
    qk_scores(0, sa_ref)
    lax.fori_loop(0, n_chunks // 2, pair_body, 0)

    @pl.when(n_chunks % 2 == 1)
    def _():
        softmax_pv(n_chunks - 1, sa_ref)

    for g in range(N_KV_HEADS):
        o = acc_ref[g] / ml_ref[2 * g + 1:2 * g + 2, :]
        for h in range(Q_PER_KV):
            hd = (g * Q_PER_KV + h) * HEAD_DIM
            o_ref[:, hd:hd + HEAD_DIM] = o[:, h * Q_BLOCK:(h + 1) * Q_BLOCK].T


def _prompt_attn(h_t, wi_t, kv, ki, batch, seq):
    nb = seq // Q_BLOCK
    cols = Q_PER_KV * Q_BLOCK
    qi_rows = IDX_HEADS * IDX_DIM // 4
    qi_b0 = COL_QI // qi_rows

    def qi_spec(t):
        return pl.BlockSpec((qi_rows, Q_BLOCK), lambda b, i: (qi_b0 + t, b * nb + i))

    return pl.pallas_call(
        _prompt_attn_kernel,
        grid=(batch, nb),
        in_specs=[
            pl.BlockSpec((ATTN_WIDTH, Q_BLOCK), lambda b, i: (COL_Q // ATTN_WIDTH, b * nb + i)),
            qi_spec(0), qi_spec(1), qi_spec(2), qi_spec(3),
            pl.BlockSpec((KV_WIDTH, seq), lambda b, i: (COL_V // KV_WIDTH, b)),
            pl.BlockSpec((IDX_HEADS, Q_BLOCK), lambda b, i: (0, b * nb + i)),
            pl.BlockSpec((seq, 2 * KV_WIDTH), lambda b, i: (b, 0)),
            pl.BlockSpec((seq, IDX_DIM), lambda b, i: (b, 0)),
        ],
        out_specs=pl.BlockSpec((Q_BLOCK, ATTN_WIDTH), lambda b, i: (b * nb + i, 0)),
        out_shape=jax.ShapeDtypeStruct((batch * seq, ATTN_WIDTH), F32),
        scratch_shapes=[
            pltpu.VMEM((seq, KV_WIDTH), BF16),
            pltpu.VMEM((seq, IDX_DIM), BF16),
            pltpu.VMEM((seq // KEY_CHUNK, N_KV_HEADS, HEAD_DIM + ONES_ROWS, KEY_CHUNK), BF16),
            pltpu.VMEM((seq // KEY_CHUNK, KEY_CHUNK, Q_BLOCK), F32),
            pltpu.VMEM((N_KV_HEADS, KEY_CHUNK, cols), F32),
            pltpu.VMEM((N_KV_HEADS, KEY_CHUNK, cols), F32),
            pltpu.VMEM((N_KV_HEADS, HEAD_DIM, cols), F32),
            pltpu.VMEM((2 * N_KV_HEADS, cols), F32),
            pltpu.VMEM((1, Q_BLOCK), I32),
            pltpu.VMEM((1, Q_BLOCK), F32),
        ],
        compiler_params=pltpu.CompilerParams(
            dimension_semantics=("arbitrary", "arbitrary"), vmem_limit_bytes=VMEM_LIMIT),
        name="prompt_attn",
    )(h_t, h_t, h_t, h_t, h_t, h_t, wi_t, kv, ki)


PAGES_PER_STEP = 32


def _sample_select_kernel(pt_ref, qi_ref, kvs_ref, *refs):
    page_refs = refs[:PAGES_PER_STEP]
    idx_ref, meta_ref, sc_ref, snew_ref = refs[PAGES_PER_STEP:]
    b = pl.program_id(0)
    p = pl.program_id(1)
    nb = pl.num_programs(0)
    n_steps = pl.num_programs(1)
    n_pages, db, _ = sc_ref.shape

    qi = qi_ref[0]
    w_col = kvs_ref[0, :, KVI_WI:KVI_WI + 1] * IDX_W_SCALE

    def key_scores(keys_bf16):
        s = _dot_nt(qi, keys_bf16)
        return jnp.sum(w_col * jnp.maximum(s, 0.0), axis=0, keepdims=True)

    keys = jnp.concatenate([page_refs[r][0, 0] for r in range(PAGES_PER_STEP)], axis=0).astype(BF16)
    row = key_scores(keys)
    for r in range(PAGES_PER_STEP):
        sc_ref[p * PAGES_PER_STEP + r, pl.ds(b, 1), :] = row[:, r * PAGE_SIZE:(r + 1) * PAGE_SIZE]

    @pl.when(p == n_steps - 1)
    def _():
        ki_new = kvs_ref[0, :, KVI_KI:KVI_KI + IDX_DIM].astype(BF16)
        s_new = key_scores(ki_new)[:, 0:1]
        snew_ref[pl.ds(b, 1), :] = jnp.broadcast_to(s_new, (1, PAGE_SIZE))

    @pl.when((b == nb - 1) & (p == n_steps - 1))
    def _():
        sc = sc_ref[...]
        s_new = snew_ref[:, 0:1]

        def total(x):
            return jnp.sum(jnp.sum(x, axis=0), axis=1, keepdims=True)

        def bit_body(j, u):
            cand = u | lax.shift_left(jnp.int32(1), 31 - j)
            thr = _ordered_bits_to_f32(cand)
            cnt = total(jnp.where(sc >= thr[None], 1.0, 0.0)) + jnp.where(s_new >= thr, 1.0, 0.0)
            return jnp.where(cnt >= float(TOPK_MAX), cand, u)

        u = lax.fori_loop(0, 32, bit_body, jnp.zeros((db, 1), I32))
        thr = _ordered_bits_to_f32(u)
        n_gt = total(jnp.where(sc > thr[None], 1.0, 0.0)) + jnp.where(s_new > thr, 1.0, 0.0)
        need = float(TOPK_MAX) - n_gt
        flat = (lax.broadcasted_iota(I32, sc.shape, 0) * PAGE_SIZE
                + lax.broadcasted_iota(I32, sc.shape, 2))
        tie = sc == thr[None]

        def jbit_body(j, jl):
            cand = jl | lax.shift_left(jnp.int32(1), 14 - j)
            cnt = total(jnp.where(tie & (flat < cand[None]), 1.0, 0.0))
            return jnp.where(cnt <= need, cand, jl)

        jlim = lax.fori_loop(0, 15, jbit_body, jnp.zeros((db, 1), I32))
        sel = (sc > thr[None]) | (tie & (flat < jlim[None]))
        self32 = jnp.where(sel, 1.0, 0.0)
        n_pool = total(self32)

        n_rows = n_pages * db
        selb = self32.astype(BF16).reshape(n_rows, PAGE_SIZE)
        ri = lax.broadcasted_iota(I32, (PAGE_SIZE, PAGE_SIZE), 0)
        ci = lax.broadcasted_iota(I32, (PAGE_SIZE, PAGE_SIZE), 1)
        incl = jnp.dot(selb, jnp.where(ri <= ci, 1.0, 0.0).astype(BF16), preferred_element_type=F32)
        tot = jnp.broadcast_to(incl[:, PAGE_SIZE - 1:PAGE_SIZE], (n_rows, PAGE_SIZE)).astype(BF16)
        rr = lax.broadcasted_iota(I32, (n_rows, n_rows), 0)
        cc = lax.broadcasted_iota(I32, (n_rows, n_rows), 1)
        earlier_page_same_row = ((rr & (db - 1)) == (cc & (db - 1))) & (cc < (rr & -db))
        offs = jnp.dot(jnp.where(earlier_page_same_row, 1.0, 0.0).astype(BF16), tot,
                       preferred_element_type=F32)
        ends_rows = offs + tot.astype(F32)
        r_i = lax.broadcasted_iota(I32, (n_rows, PAGE_SIZE), 0)
        l_i = lax.broadcasted_iota(I32, (n_rows, PAGE_SIZE), 1)
        log2_db = db.bit_length() - 1
        diag = jnp.where(l_i == lax.shift_right_logical(r_i, log2_db), ends_rows, 0.0).astype(BF16)
        pick = jnp.where((lax.broadcasted_iota(I32, (db, n_rows), 1) & (db - 1))
                         == lax.broadcasted_iota(I32, (db, n_rows), 0), 1.0, 0.0).astype(BF16)
        ends = jnp.dot(pick, diag, preferred_element_type=F32)

        table = jnp.concatenate([incl.astype(BF16), offs.astype(BF16)], axis=1)
        slot = lax.broadcasted_iota(I32, (TOPK_MAX, PAGE_SIZE), 0).astype(F32)
        rows_lane = lax.broadcasted_iota(I32, (TOPK_MAX, n_rows), 1)
        for row_i in range(db):
            page_s = jnp.sum(jnp.where(ends[row_i:row_i + 1, :] <= slot, 1.0, 0.0), axis=1, keepdims=True)
            mrow = page_s.astype(I32) * db + row_i
            onehot = jnp.where(rows_lane == mrow, 1.0, 0.0).astype(BF16)
            got = jnp.dot(onehot, table, preferred_element_type=F32)
            k = slot - got[:, PAGE_SIZE:]
            off_s = jnp.sum(jnp.where(got[:, :PAGE_SIZE] <= k, 1.0, 0.0), axis=1, keepdims=True)
            pos = page_s * float(PAGE_SIZE) + off_s
            valid = slot[:, 0:1] < n_pool[row_i:row_i + 1, :]
            idx_ref[row_i] = jnp.where(valid, pos, 0.0).astype(I32)
        meta_ref[...] = jnp.broadcast_to(n_pool, (db, PAGE_SIZE)).astype(I32)


def _sample_select(page_table, qi_s, kvi_s16, pool_ki):
    db, n_pages = page_table.shape
    assert db & (db - 1) == 0, "row index arithmetic in the rank matrix assumes a power-of-two row count"
    assert n_pages == PAGE_SIZE, "page ends are laid out one page per lane"
    n_steps = n_pages // PAGES_PER_STEP

    def page_spec(r):
        return pl.BlockSpec(
            (1, 1, PAGE_SIZE, IDX_DIM),
            lambda b, p, pt: (0, pt[b, p * PAGES_PER_STEP + r], 0, 0))

    grid_spec = pltpu.PrefetchScalarGridSpec(
        num_scalar_prefetch=1,
        grid=(db, n_steps),
        in_specs=[
            pl.BlockSpec((1, IDX_HEADS, IDX_DIM), lambda b, p, pt: (b, 0, 0)),
            pl.BlockSpec((1, IDX_HEADS, KVI_WIDTH), lambda b, p, pt: (b, 0, 0)),
        ] + [page_spec(r) for r in range(PAGES_PER_STEP)],
        out_specs=[
            pl.BlockSpec((db, TOPK_MAX, 1), lambda b, p, pt: (0, 0, 0)),
            pl.BlockSpec((db, PAGE_SIZE), lambda b, p, pt: (0, 0)),
        ],
        scratch_shapes=[
            pltpu.VMEM((n_pages, db, PAGE_SIZE), F32),
            pltpu.VMEM((db, PAGE_SIZE), F32),
        ],
    )
    return pl.pallas_call(
        _sample_select_kernel,
        grid_spec=grid_spec,
        out_shape=[
            jax.ShapeDtypeStruct((db, TOPK_MAX, 1), I32),
            jax.ShapeDtypeStruct((db, PAGE_SIZE), I32),
        ],
        compiler_params=pltpu.CompilerParams(
            dimension_semantics=("arbitrary", "arbitrary"), vmem_limit_bytes=VMEM_LIMIT),
        name="sample_select",
    )(page_table, qi_s, kvi_s16, *([pool_ki] * PAGES_PER_STEP))


def _sample_attend_kernel(idx_ref, npool_ref, pt_ref, q_ref, kvs_ref, pk_ref, pv_ref, o_ref,
                          kbuf, vbuf, kflat, vflat, sem):
    b = pl.program_id(0)

    def row_copies(r):
        ix = idx_ref[b, r]
        page = pt_ref[b, lax.shift_right_logical(ix, 7)]
        off = ix & (PAGE_SIZE - 1)
        return (pltpu.make_async_copy(pk_ref.at[0, page, off], kbuf.at[r], sem.at[0]),
                pltpu.make_async_copy(pv_ref.at[0, page, off], vbuf.at[r], sem.at[1]))

    def start_body(r, carry):
        ck, cv = row_copies(r)
        ck.start()
        cv.start()
        return carry

    def wait_body(r, carry):
        ck, cv = row_copies(r)
        ck.wait()
        cv.wait()
        return carry

    lax.fori_loop(0, TOPK_MAX, start_body, 0)
    lax.fori_loop(0, TOPK_MAX, wait_body, 0)

    def head_copies(g):
        return (pltpu.make_async_copy(kbuf.at[:, g, :], kflat.at[g], sem.at[2]),
                pltpu.make_async_copy(vbuf.at[:, g, :], vflat.at[g], sem.at[3]))

    for g in range(N_KV_HEADS):
        for cp in head_copies(g):
            cp.start()
    for g in range(N_KV_HEADS):
        for cp in head_copies(g):
            cp.wait()

    n_pool = npool_ref[b]
    slot = lax.broadcasted_iota(I32, (1, TOPK_MAX), 1)
    slot_ok = slot < n_pool
    new_ok = n_pool < TOPK_MAX
    q_all = q_ref[0].astype(F32)
    for g in range(N_KV_HEADS):
        qg = q_all[g * Q_PER_KV:(g + 1) * Q_PER_KV, :].astype(BF16)
        ks = kflat[g].astype(BF16)
        vs = vflat[g].astype(BF16)
        k_new = kvs_ref[0, 0:1, KVI_K + g * HEAD_DIM:KVI_K + (g + 1) * HEAD_DIM].astype(BF16)
        v_new = kvs_ref[0, 0:1, KVI_V + g * HEAD_DIM:KVI_V + (g + 1) * HEAD_DIM].astype(BF16)
        s = jnp.where(slot_ok, _dot_nt(qg, ks) * ATTN_SCALE, -jnp.inf)
        s_new = jnp.sum(qg.astype(F32) * k_new.astype(F32), axis=-1, keepdims=True) * ATTN_SCALE
        s_new = jnp.where(new_ok, s_new, -jnp.inf)
        m = jnp.maximum(jnp.max(s, axis=-1, keepdims=True), s_new)
        p = jnp.exp(s - m)
        p_new = jnp.exp(s_new - m)
        denom = jnp.sum(p, axis=-1, keepdims=True) + p_new
        pn = (p / denom).astype(BF16)
        pn_new = (p_new / denom).astype(BF16).astype(F32)
        o = jnp.dot(pn, vs, preferred_element_type=F32) + pn_new * v_new.astype(F32)
        o_ref[0, g * Q_PER_KV:(g + 1) * Q_PER_KV, :] = o


def _sample_attend(idx, n_pool, page_table, q_s, kvi_s16, pool_k, pool_v):
    db = page_table.shape[0]
    grid_spec = pltpu.PrefetchScalarGridSpec(
        num_scalar_prefetch=3,
        grid=(db,),
        in_specs=[
            pl.BlockSpec((1, N_HEADS, HEAD_DIM), lambda b, *_: (b, 0, 0)),
            pl.BlockSpec((1, IDX_HEADS, KVI_WIDTH), lambda b, *_: (b, 0, 0)),
            pl.BlockSpec(memory_space=pl.ANY),
            pl.BlockSpec(memory_space=pl.ANY),
        ],
        out_specs=pl.BlockSpec((1, N_HEADS, HEAD_DIM), lambda b, *_: (b, 0, 0)),
        scratch_shapes=[
            pltpu.VMEM((TOPK_MAX, N_KV_HEADS, HEAD_DIM), F32),
            pltpu.VMEM((TOPK_MAX, N_KV_HEADS, HEAD_DIM), F32),
            pltpu.VMEM((N_KV_HEADS, TOPK_MAX, HEAD_DIM), F32),
            pltpu.VMEM((N_KV_HEADS, TOPK_MAX, HEAD_DIM), F32),
            pltpu.SemaphoreType.DMA((4,)),
        ],
    )
    return pl.pallas_call(
        _sample_attend_kernel,
        grid_spec=grid_spec,
        out_shape=jax.ShapeDtypeStruct((db, N_HEADS, HEAD_DIM), F32),
        compiler_params=pltpu.CompilerParams(dimension_semantics=("arbitrary",)),
        name="sample_attend",
    )(idx, n_pool, page_table, q_s, kvi_s16, pool_k, pool_v)


def _merge_kernel(chunked, oa_ref, za_ref, u_ref, v_ref, zb_ref, ga_ref, gb_ref, x_ref,
                  wpa_ref, wpb_ref, wout_ref, lng_ref, lnb_ref, ws_ref, bs_ref, gf_ref,
                  y_ref, vn_ref, ob_ref):
    tm = x_ref.shape[0]
    u = _gelu(u_ref[...])
    v = _gelu(v_ref[...])
    mu = jnp.mean(v, axis=-1, keepdims=True)
    vc = v - mu
    vn = (vc * lax.rsqrt(jnp.mean(vc * vc, axis=-1, keepdims=True) + LN_EPS)) * lng_ref[...] + lnb_ref[...]
    vn_ref[...] = vn

    if chunked:
        vnb = vn.astype(BF16)
        ri = lax.broadcasted_iota(I32, (CHUNK, CHUNK), 0)
        ci = lax.broadcasted_iota(I32, (CHUNK, CHUNK), 1)
        for g in range(GMLP_GROUPS):
            wm = jnp.where(ri >= ci, ws_ref[g], 0.0).astype(BF16)
            lo = g * GMLP_GROUP_DIM
            for c in range(tm // CHUNK):
                mixed = jnp.dot(wm, vnb[c * CHUNK:(c + 1) * CHUNK, lo:lo + GMLP_GROUP_DIM],
                                preferred_element_type=F32) + bs_ref[:, g:g + 1]
                ob_ref[c * CHUNK:(c + 1) * CHUNK, lo:lo + GMLP_GROUP_DIM] = (
                    u[c * CHUNK:(c + 1) * CHUNK, lo:lo + GMLP_GROUP_DIM] * mixed)
    else:
        ob_ref[...] = u * (ws_ref[...] * vn + bs_ref[...])

    ha = (oa_ref[...] * _silu(za_ref[...])).astype(BF16)
    hb = (ob_ref[...] * _silu(zb_ref[...])).astype(BF16)
    ya = jnp.dot(ha, wpa_ref[...], preferred_element_type=F32)
    yb = jnp.dot(hb, wpb_ref[...], preferred_element_type=F32)
    mix = jax.nn.sigmoid(ga_ref[...]) * ya + jax.nn.sigmoid(gb_ref[...]) * yb
    out = x_ref[...] + jnp.dot(mix.astype(BF16), wout_ref[...], preferred_element_type=F32)
    ms = jnp.mean(out * out, axis=-1, keepdims=True)
    y_ref[...] = (out * lax.rsqrt(ms + RMS_EPS)) * gf_ref[...]


def _merge(chunked, o_a, gate, x2d, wpa, wpb, wout, lng, lnb, ws, bs, gf, tm):
    rows = x2d.shape[0]
    gw = GMLP_WIDTH

    def const(shape):
        return pl.BlockSpec(shape, lambda i: (0,) * len(shape), pipeline_mode=pl.Buffered(1))

    return pl.pallas_call(
        functools.partial(_merge_kernel, chunked),
        grid=(rows // tm,),
        in_specs=[
            pl.BlockSpec((tm, ATTN_WIDTH), lambda i: (i, 0)),
            pl.BlockSpec((tm, gw), lambda i: (i, 0)),
            pl.BlockSpec((tm, gw), lambda i: (i, 1)),
            pl.BlockSpec((tm, gw), lambda i: (i, 2)),
            pl.BlockSpec((tm, gw), lambda i: (i, 3)),
            pl.BlockSpec((tm, D_MODEL), lambda i: (i, 2)),
            pl.BlockSpec((tm, D_MODEL), lambda i: (i, 3)),
            pl.BlockSpec((tm, D_MODEL), lambda i: (i, 0)),
            const(wpa.shape), const(wpb.shape), const(wout.shape),
            const(lng.shape), const(lnb.shape), const(ws.shape), const(bs.shape), const(gf.shape),
        ],
        out_specs=[
            pl.BlockSpec((tm, D_MODEL), lambda i: (i, 0)),
            pl.BlockSpec((tm, gw), lambda i: (i, 0)),
        ],
        out_shape=[
            jax.ShapeDtypeStruct((rows, D_MODEL), F32),
            jax.ShapeDtypeStruct((rows, gw), F32),
        ],
        scratch_shapes=[pltpu.VMEM((tm, gw), F32)],
        compiler_params=pltpu.CompilerParams(
            dimension_semantics=("arbitrary",), vmem_limit_bytes=VMEM_LIMIT),
        name="merge_prompt" if chunked else "merge_sample",
    )(o_a, gate, gate, gate, gate, gate, gate, x2d, wpa, wpb, wout, lng, lnb, ws, bs, gf)


def kernel(x_prompt, x_sample, cache_k, cache_v, cache_k_idx, page_table, norm_in_g, w_in,
           w_proj_a, w_proj_b, w_out, ln_g, ln_b, w_spatial, b_spatial, norm_f_g):
    depth = w_in.shape[0]
    assert depth == 1, "single trunk layer"
    batch, seq, _ = x_prompt.shape
    db, dseq, _ = x_sample.shape
    assert dseq == 1
    l = 0
    wpa = w_proj_a[l].astype(BF16)
    wpb = w_proj_b[l].astype(BF16)
    wout = w_out[l].astype(BF16)
    g_in = norm_in_g[l].reshape(1, D_MODEL)
    lng = ln_g[l].reshape(1, GMLP_WIDTH)
    lnb = ln_b[l].reshape(1, GMLP_WIDTH)
    gf = norm_f_g.reshape(1, D_MODEL)

    xp = x_prompt.reshape(batch * seq, D_MODEL)
    wt = w_in[l].T
    xn_p = _rmsnorm_bf16(xp, g_in, 1024)
    h_t = _inproj(xn_p, wt, COL_Q, COL_WI, 512, 1024, BF16, True)
    wi_t = _inproj(xn_p, wt, COL_WI, IDX_HEADS, IDX_HEADS, 1024, F32, True)
    kv_p = _inproj(xn_p, wt, COL_K, 2 * KV_WIDTH, 2 * KV_WIDTH, 1024, F32, False)
    ki_p2 = _inproj(xn_p, wt, COL_KI, IDX_DIM, IDX_DIM, 1024, F32, False)
    gate_p = _inproj(xn_p, wt, GATE_COL0, GATE_WIDTH, 1024, 1024, F32, False)
    oa_p = _prompt_attn(h_t, wi_t, kv_p, ki_p2, batch, seq)
    bs_t = b_spatial[l].T
    y_p, _ = _merge(True, oa_p, gate_p, xp, wpa, wpb, wout, lng, lnb, w_spatial[l], bs_t, gf, 256)

    xs = x_sample.reshape(db, D_MODEL)
    xn_s = _rmsnorm_bf16(xs, g_in, db)
    h_s = _inproj(xn_s, wt, COL_Q, COL_WI, 512, db, F32, False)
    wi_s = _inproj(xn_s, wt, COL_WI, IDX_HEADS, IDX_HEADS, db, F32, True).T
    ki_s2 = _inproj(xn_s, wt, COL_KI, IDX_DIM, IDX_DIM, db, F32, False)
    gate_s = _inproj(xn_s, wt, GATE_COL0, GATE_WIDTH, 1024, db, F32, False)
    kvi_s = jnp.concatenate(
        [h_s[:, COL_K:COL_QI], ki_s2, wi_s, jnp.zeros((db, KVI_WIDTH - KVI_WI - IDX_HEADS), F32)], axis=1)
    q_s = h_s[:, COL_Q:COL_K].astype(BF16).reshape(db, N_HEADS, HEAD_DIM)
    qi_s = h_s[:, COL_QI:COL_WI].astype(BF16).reshape(db, IDX_HEADS, IDX_DIM)
    wi_col = kvi_s[:, KVI_WI:KVI_WI + IDX_HEADS].reshape(db, IDX_HEADS, 1)
    kvi_s16 = jnp.broadcast_to(kvi_s[:, None, :], (db, IDX_HEADS, KVI_WIDTH))
    kvi_s16 = lax.dynamic_update_slice(kvi_s16, wi_col, (0, 0, KVI_WI))
    idx, meta = _sample_select(page_table, qi_s, kvi_s16, cache_k_idx)
    oa_s = _sample_attend(idx.reshape(db, TOPK_MAX), meta[:, 0], page_table, q_s, kvi_s16,
                          cache_k, cache_v).reshape(db, ATTN_WIDTH)
    ws0 = jnp.repeat(w_spatial[l][:, 0, 0], GMLP_GROUP_DIM).reshape(1, GMLP_WIDTH)
    bs0 = jnp.repeat(b_spatial[l][:, 0], GMLP_GROUP_DIM).reshape(1, GMLP_WIDTH)
    y_s, vn_s = _merge(False, oa_s, gate_s, xs, wpa, wpb, wout, lng, lnb, ws0, bs0, gf, db)

    def kv_out(kvi, lead):
        k = kvi[:, KVI_K:KVI_K + KV_WIDTH].reshape((1,) + lead + (N_KV_HEADS, HEAD_DIM))
        v = kvi[:, KVI_V:KVI_V + KV_WIDTH].reshape((1,) + lead + (N_KV_HEADS, HEAD_DIM))
        ki = kvi[:, KVI_KI:KVI_KI + IDX_DIM].reshape((1,) + lead + (IDX_DIM,))
        return k, v, ki

    k_p = kv_p[:, :KV_WIDTH].reshape(1, batch, seq, N_KV_HEADS, HEAD_DIM)
    v_p = kv_p[:, KV_WIDTH:].reshape(1, batch, seq, N_KV_HEADS, HEAD_DIM)
    ki_p = ki_p2.reshape(1, batch, seq, IDX_DIM)
    k_s, v_s, ki_s = kv_out(kvi_s, (db, dseq))
    return (y_p.reshape(batch, seq, D_MODEL), y_s.reshape(db, dseq, D_MODEL),
            k_p, v_p, ki_p, k_s, v_s, ki_s, vn_s.reshape(1, db, dseq, GMLP_WIDTH))
```

```python
import functools

import jax
import jax.numpy as jnp
import numpy as np
from jax import lax
from jax.experimental import pallas as pl
from jax.experimental.pallas import tpu as pltpu

F32 = jnp.float32
BF16 = jnp.bfloat16
I32 = jnp.int32

D_MODEL = 2048
N_HEADS = 8
N_KV_HEADS = 2
HEAD_DIM = 128
Q_PER_KV = N_HEADS // N_KV_HEADS
ATTN_WIDTH = N_HEADS * HEAD_DIM
KV_WIDTH = N_KV_HEADS * HEAD_DIM
IDX_HEADS = 16
IDX_DIM = 128
TOPK_MAX = 256
PAGE_SIZE = 128
GMLP_WIDTH = 1024
GMLP_GROUPS = 8
GMLP_GROUP_DIM = GMLP_WIDTH // GMLP_GROUPS
CHUNK = 128
RMS_EPS = 1e-6
LN_EPS = 1e-5

SUBLANES = 8
COL_Q = 0
COL_K = COL_Q + ATTN_WIDTH
COL_V = COL_K + KV_WIDTH
COL_QI = COL_V + KV_WIDTH
COL_WI = COL_QI + IDX_HEADS * IDX_DIM
COL_KI = COL_WI + IDX_HEADS
GATE_COL0 = COL_KI + IDX_DIM
GATE_WIDTH = 4 * GMLP_WIDTH + 2 * D_MODEL
KVI_WIDTH = 768
KVI_K, KVI_V, KVI_KI, KVI_WI = 0, 256, 512, 640

Q_BLOCK = 128
KEY_CHUNK = 512
COUNT_ROWS = 64
ONES_ROWS = 16
SIGN_EXP_BITS = 9
SEARCH_ALWAYS_BITS = 16
SEARCH_GROUP_BITS = 4
NEG_SENTINEL = float(np.finfo(np.float32).min)
NEG_ABOVE_SENTINEL = float(np.nextafter(np.float32(NEG_SENTINEL), np.float32(0.0)))
INT_MIN = -(2 ** 31)
IDX_W_SCALE = float(IDX_HEADS ** -0.5 * IDX_DIM ** -0.5)
ATTN_SCALE = float(HEAD_DIM ** -0.5)
ATTN_SCALE_LOG2E = float(HEAD_DIM ** -0.5 * np.log2(np.e))
VMEM_LIMIT = 56 * 1024 * 1024


def _dot_nt(a, b):
    return lax.dot_general(a, b, (((1,), (1,)), ((), ())), preferred_element_type=F32)


def _ordered_bits_to_f32(u):
    signed_pow, zero_exp = _sign_exp_parts(u)
    return _with_mantissa(u, signed_pow, zero_exp)


def _sign_exp_parts(u):
    bits = jnp.where(u < 0, u ^ INT_MIN, ~u)
    exp = lax.shift_right_logical(bits, 23) & 0xFF
    e1 = exp - 1
    p = jnp.full(u.shape, float(2.0 ** -126), F32)
    for b in range(7):
        p = jnp.where(((e1 >> b) & 1) == 1, p * float(2.0 ** (2 ** b)), p)
    top = ((e1 >> 7) & 1) == 1
    p = jnp.where(top, p * float(2.0 ** 64), p)
    p = jnp.where(top, p * float(2.0 ** 64), p)
    return jnp.where(bits < 0, -p, p), exp == 0


def _with_mantissa(u, signed_pow, zero_exp):
    low = jnp.where(u < 0, u, ~u) & 0x7FFFFF
    mant = 1.0 + low.astype(F32) * float(2.0 ** -23)
    return jnp.where(zero_exp, 0.0, signed_pow * mant)


def _gelu(x):
    return 0.5 * x * (1.0 + lax.erf(x * float(np.sqrt(0.5))))


def _silu(x):
    return x * jax.nn.sigmoid(x)


def _rmsnorm_kernel(x_ref, g_ref, o_ref):
    x = x_ref[...]
    ms = jnp.mean(x * x, axis=-1, keepdims=True)
    o_ref[...] = ((x * lax.rsqrt(ms + RMS_EPS)) * g_ref[...]).astype(o_ref.dtype)


def _rmsnorm_bf16(x2d, g, tm):
    rows = x2d.shape[0]
    return pl.pallas_call(
        _rmsnorm_kernel,
        grid=(rows // tm,),
        in_specs=[pl.BlockSpec((tm, D_MODEL), lambda i: (i, 0)),
                  pl.BlockSpec((1, D_MODEL), lambda i: (0, 0))],
        out_specs=pl.BlockSpec((tm, D_MODEL), lambda i: (i, 0)),
        out_shape=jax.ShapeDtypeStruct((rows, D_MODEL), BF16),
        compiler_params=pltpu.CompilerParams(
            dimension_semantics=("arbitrary",), vmem_limit_bytes=VMEM_LIMIT),
        name="rmsnorm_in",
    )(x2d, g)


def _inproj_kernel(feature_major, xn_ref, w_ref, o_ref, wb_ref):
    @pl.when(pl.program_id(1) == 0)
    def _():
        wb_ref[...] = w_ref[...].astype(BF16)

    if feature_major:
        o_ref[...] = _dot_nt(wb_ref[...], xn_ref[...]).astype(o_ref.dtype)
    else:
        o_ref[...] = _dot_nt(xn_ref[...], wb_ref[...]).astype(o_ref.dtype)


def _inproj(xn, wt, row_start, n_rows, tn, tm, out_dtype, feature_major):
    rows = xn.shape[0]
    assert n_rows % tn == 0 and rows % tm == 0 and row_start % SUBLANES == 0
    if row_start % tn == 0:
        j0 = row_start // tn
        w_spec = pl.BlockSpec((tn, D_MODEL), lambda j, i: (j0 + j, 0))
    else:
        w_spec = pl.BlockSpec((pl.Element(tn), pl.Element(D_MODEL)), lambda j, i: ((row_start // SUBLANES + j * (tn // SUBLANES)) * SUBLANES, 0))
    if feature_major:
        out_spec = pl.BlockSpec((tn, tm), lambda j, i: (j, i))
        out_shape = (n_rows, rows)
    else:
        out_spec = pl.BlockSpec((tm, tn), lambda j, i: (i, j))
        out_shape = (rows, n_rows)
    return pl.pallas_call(
        functools.partial(_inproj_kernel, feature_major),
        grid=(n_rows // tn, rows // tm),
        in_specs=[pl.BlockSpec((tm, D_MODEL), lambda j, i: (i, 0)), w_spec],
        out_specs=out_spec,
        out_shape=jax.ShapeDtypeStruct(out_shape, out_dtype),
        scratch_shapes=[pltpu.VMEM((tn, D_MODEL), BF16)],
        compiler_params=pltpu.CompilerParams(
            dimension_semantics=("arbitrary", "arbitrary"), vmem_limit_bytes=VMEM_LIMIT),
        name="inproj_t" if feature_major else "inproj",
    )(xn, wt)


def _inproj_kvi_kernel(xn_ref, wkv_ref, wwi_ref, wki_ref, kv_ref, ki_ref, wit_ref, wkvb_ref, wwib_ref, wkib_ref):
    @pl.when(pl.program_id(0) == 0)
    def _():
        wkvb_ref[...] = wkv_ref[...].astype(BF16)
        wwib_ref[...] = wwi_ref[...].astype(BF16)
        wkib_ref[...] = wki_ref[...].astype(BF16)

    xn = xn_ref[...]
    kv_ref[...] = _dot_nt(xn, wkvb_ref[...])
    ki_ref[...] = _dot_nt(xn, wkib_ref[...])
    wit_ref[...] = _dot_nt(wwib_ref[...], xn)


def _inproj_kvi(xn, wt, tm):
    rows = xn.shape[0]
    assert rows % tm == 0 and COL_K % (2 * KV_WIDTH) == 0 and COL_WI % IDX_HEADS == 0
    return pl.pallas_call(
        _inproj_kvi_kernel,
        grid=(rows // tm,),
        in_specs=[
            pl.BlockSpec((tm, D_MODEL), lambda i: (i, 0)),
            pl.BlockSpec((2 * KV_WIDTH, D_MODEL), lambda i: (COL_K // (2 * KV_WIDTH), 0)),
            pl.BlockSpec((IDX_HEADS, D_MODEL), lambda i: (COL_WI // IDX_HEADS, 0)),
            pl.BlockSpec((pl.Element(IDX_DIM), pl.Element(D_MODEL)), lambda i: (COL_KI, 0)),
        ],
        out_specs=[
            pl.BlockSpec((tm, 2 * KV_WIDTH), lambda i: (i, 0)),
            pl.BlockSpec((tm, IDX_DIM), lambda i: (i, 0)),
            pl.BlockSpec((IDX_HEADS, tm), lambda i: (0, i)),
        ],
        out_shape=[
            jax.ShapeDtypeStruct((rows, 2 * KV_WIDTH), F32),
            jax.ShapeDtypeStruct((rows, IDX_DIM), F32),
            jax.ShapeDtypeStruct((IDX_HEADS, rows), F32),
        ],
        scratch_shapes=[
            pltpu.VMEM((2 * KV_WIDTH, D_MODEL), BF16),
            pltpu.VMEM((IDX_HEADS, D_MODEL), BF16),
            pltpu.VMEM((IDX_DIM, D_MODEL), BF16),
        ],
        compiler_params=pltpu.CompilerParams(
            dimension_semantics=("arbitrary",), vmem_limit_bytes=VMEM_LIMIT),
        name="inproj_kvi",
    )(xn, wt, wt, wt)


def _prompt_attn_kernel(qt_ref, qit0_ref, qit1_ref, qit2_ref, qit3_ref, vt_ref, wit_ref, kvf_ref, kif_ref,
                        o_ref, kb_ref, kib_ref, vtc_ref, sc_ref, sa_ref, sb_ref, acc_ref, ml_ref, u_ref, cnt_ref):
    i = pl.program_id(1)
    seq = kvf_ref.shape[0]
    qit_refs = (qit0_ref, qit1_ref, qit2_ref, qit3_ref)
    heads_per_ref = IDX_HEADS // len(qit_refs)

    @pl.when(i == 0)
    def _():
        kb_ref[...] = kvf_ref[:, 0:KV_WIDTH].astype(BF16)
        kib_ref[...] = kif_ref[...].astype(BF16)
        for c in range(seq // KEY_CHUNK):
            for g in range(N_KV_HEADS):
                vtc_ref[c, g, :HEAD_DIM, :] = vt_ref[g * HEAD_DIM:(g + 1) * HEAD_DIM,
                                                     c * KEY_CHUNK:(c + 1) * KEY_CHUNK]
                vtc_ref[c, g, HEAD_DIM:, :] = jnp.ones((ONES_ROWS, KEY_CHUNK), BF16)

    n_chunks = (i * Q_BLOCK) // KEY_CHUNK + 1
    q_pos = i * Q_BLOCK + lax.broadcasted_iota(I32, (1, Q_BLOCK), 1)
    key0 = lax.broadcasted_iota(I32, (KEY_CHUNK, 1), 0)

    w_t = wit_ref[...] * IDX_W_SCALE
    qit_all = jnp.concatenate(
        [qit_refs[h // heads_per_ref][(h % heads_per_ref) * IDX_DIM:(h % heads_per_ref + 1) * IDX_DIM, :]
         for h in range(IDX_HEADS)], axis=1)

    def score_body(c, carry):
        k0 = pl.multiple_of(c * KEY_CHUNK, KEY_CHUNK)
        res = jnp.dot(kib_ref[pl.ds(k0, KEY_CHUNK), :], qit_all, preferred_element_type=F32)
        acc = jnp.zeros((KEY_CHUNK, Q_BLOCK), F32)
        for h in range(IDX_HEADS):
            acc = acc + w_t[h:h + 1, :] * jnp.maximum(res[:, h * Q_BLOCK:(h + 1) * Q_BLOCK], 0.0)
        sc_ref[c] = jnp.where(k0 + key0 <= q_pos, acc, NEG_SENTINEL)
        return carry

    lax.fori_loop(0, n_chunks, score_body, 0)

    def count(pred):
        def body(c, cnt):
            ones = jnp.where(pred(sc_ref[c], c), 1.0, 0.0)
            return cnt + ones.reshape(KEY_CHUNK // COUNT_ROWS, COUNT_ROWS, Q_BLOCK).sum(axis=0)
        part = lax.fori_loop(0, n_chunks, body, jnp.zeros((COUNT_ROWS, Q_BLOCK), F32))
        return jnp.sum(part, axis=0, keepdims=True)

    def search(j0, n, u, cnt_u, decode):
        def body(j, carry):
            u, cnt_u = carry
            cand = u | lax.shift_left(jnp.int32(1), 31 - j)
            thr_c = decode(cand)
            cnt = count(lambda blk, c: blk >= thr_c)
            ok = cnt >= float(TOPK_MAX)
            return jnp.where(ok, cand, u), jnp.where(ok, cnt, cnt_u)
        return lax.fori_loop(j0, j0 + n, body, (u, cnt_u))

    u = jnp.zeros((1, Q_BLOCK), I32)
    cnt_u = (u + n_chunks * KEY_CHUNK).astype(F32)
    u, cnt_u = search(0, SIGN_EXP_BITS, u, cnt_u, _ordered_bits_to_f32)
    signed_pow, zero_exp = _sign_exp_parts(u)

    def decode_mantissa(cand):
        return _with_mantissa(cand, signed_pow, zero_exp)

    u, cnt_u = search(SIGN_EXP_BITS, SEARCH_ALWAYS_BITS - SIGN_EXP_BITS, u, cnt_u, decode_mantissa)
    u_ref[...] = u
    cnt_ref[...] = cnt_u
    for j0 in range(SEARCH_ALWAYS_BITS, 32, SEARCH_GROUP_BITS):
        @pl.when(jnp.max(jnp.where(cnt_ref[...] == float(TOPK_MAX), 0.0, 1.0)) > 0.0)
        def _():
            u_g, cnt_g = search(j0, SEARCH_GROUP_BITS, u_ref[...], cnt_ref[...], decode_mantissa)
            u_ref[...] = u_g
            cnt_ref[...] = cnt_g
    u = u_ref[...]
    n_ge = cnt_ref[...]
    thr = decode_mantissa(u)
    thr_adm = jnp.maximum(thr, NEG_ABOVE_SENTINEL)
    excess = jnp.max(jnp.where((n_ge > float(TOPK_MAX)) & (thr > NEG_SENTINEL), 1.0, 0.0)) > 0.0

    @pl.when(excess)
    def _():
        need = float(TOPK_MAX) - count(lambda blk, c: blk > thr)

        def jbit_body(j, jl):
            cand = jl | lax.shift_left(jnp.int32(1), 12 - j)
            cnt = count(lambda blk, c: (blk == thr) & (c * KEY_CHUNK + key0 < cand))
            return jnp.where(cnt <= need, cand, jl)
        jlim = lax.fori_loop(0, 13, jbit_body, jnp.zeros((1, Q_BLOCK), I32))

        def demote_body(c, carry):
            blk = sc_ref[c]
            dropped = (blk == thr) & (thr > NEG_SENTINEL) & (c * KEY_CHUNK + key0 >= jlim)
            sc_ref[c] = jnp.where(dropped, NEG_SENTINEL, blk)
            return carry
        lax.fori_loop(0, n_chunks, demote_body, 0)

    acc_ref[...] = jnp.zeros(acc_ref.shape, F32)
    cols = Q_PER_KV * Q_BLOCK
    qt_groups = [
        jnp.concatenate([qt_ref[(g * Q_PER_KV + h) * HEAD_DIM:(g * Q_PER_KV + h + 1) * HEAD_DIM, :]
                         for h in range(Q_PER_KV)], axis=1)
        for g in range(N_KV_HEADS)
    ]

    last_chunk = seq // KEY_CHUNK - 1

    def qk_scores(c, dst_ref):
        k0 = pl.multiple_of(jnp.minimum(c, last_chunk) * KEY_CHUNK, KEY_CHUNK)
        for g in range(N_KV_HEADS):
            kc = kb_ref[pl.ds(k0, KEY_CHUNK), g * HEAD_DIM:(g + 1) * HEAD_DIM]
            dst_ref[g] = jnp.dot(kc, qt_groups[g], preferred_element_type=F32)

    def softmax_pv(c, src_ref):
        bias1 = jnp.where(sc_ref[c] >= thr_adm, 0.0, -jnp.inf)
        bias = jnp.concatenate([bias1] * Q_PER_KV, axis=1)
        for g in range(N_KV_HEADS):
            m_old = ml_ref[2 * g:2 * g + 1, :]
            l_old = ml_ref[2 * g + 1:2 * g + 2, :]
            s = src_ref[g] * ATTN_SCALE_LOG2E + bias
            m_new = jnp.maximum(m_old, jnp.max(s, axis=0, keepdims=True))
            m_safe = jnp.where(m_new == -jnp.inf, 0.0, m_new)
            alpha = jnp.exp2(m_old - m_safe)
            p = jnp.exp2(s - m_safe).astype(BF16)
            pv = jnp.dot(vtc_ref[c, g], p, preferred_element_type=F32)
            acc_ref[g] = alpha * acc_ref[g] + pv[:HEAD_DIM]
            ml_ref[2 * g:2 * g + 1, :] = m_new
            ml_ref[2 * g + 1:2 * g + 2, :] = alpha * l_old + pv[HEAD_DIM:HEAD_DIM + 1]

    def pair_body(t, carry):
        qk_scores(2 * t + 1, sb_ref)
        softmax_pv(2 * t, sa_ref)
        qk_scores(2 * t + 2, sa_ref)
        softmax_pv(2 * t + 1, sb_ref)
        return carry

    for g in range(N_KV_HEADS):
        ml_ref[2 * g:2 * g + 1, :] = jnp.full((1, cols), -jnp.inf, F32)
        ml_ref[2 * g + 1:2 * g + 2, :] = jnp.zeros((1, cols), F32)
    qk_scores(0, sa_ref)
    lax.fori_loop(0, n_chunks // 2, pair_body, 0)

    @pl.when(n_chunks % 2 == 1)
    def _():
        softmax_pv(n_chunks - 1, sa_ref)

    for g in range(N_KV_HEADS):
        o = acc_ref[g] / ml_ref[2 * g + 1:2 * g + 2, :]
        for h in range(Q_PER_KV):
            hd = (g * Q_PER_KV + h) * HEAD_DIM
            o_ref[:, hd:hd + HEAD_DIM] = o[:, h * Q_BLOCK:(h + 1) * Q_BLOCK].T


def _prompt_attn(h_t, wi_t, kv, ki, batch, seq):
    nb = seq // Q_BLOCK
    cols = Q_PER_KV * Q_BLOCK
    qi_rows = IDX_HEADS * IDX_DIM // 4
    qi_b0 = COL_QI // qi_rows

    def qi_spec(t):
        return pl.BlockSpec((qi_rows, Q_BLOCK), lambda b, i: (qi_b0 + t, b * nb + i))

    return pl.pallas_call(
        _prompt_attn_kernel,
        grid=(batch, nb),
        in_specs=[
            pl.BlockSpec((ATTN_WIDTH, Q_BLOCK), lambda b, i: (COL_Q // ATTN_WIDTH, b * nb + i)),
            qi_spec(0), qi_spec(1), qi_spec(2), qi_spec(3),
            pl.BlockSpec((KV_WIDTH, seq), lambda b, i: (COL_V // KV_WIDTH, b)),
            pl.BlockSpec((IDX_HEADS, Q_BLOCK), lambda b, i: (0, b * nb + i)),
            pl.BlockSpec((seq, 2 * KV_WIDTH), lambda b, i: (b, 0)),
            pl.BlockSpec((seq, IDX_DIM), lambda b, i: (b, 0)),
        ],
        out_specs=pl.BlockSpec((Q_BLOCK, ATTN_WIDTH), lambda b, i: (b * nb + i, 0)),
        out_shape=jax.ShapeDtypeStruct((batch * seq, ATTN_WIDTH), F32),
        scratch_shapes=[
            pltpu.VMEM((seq, KV_WIDTH), BF16),
            pltpu.VMEM((seq, IDX_DIM), BF16),
            pltpu.VMEM((seq // KEY_CHUNK, N_KV_HEADS, HEAD_DIM + ONES_ROWS, KEY_CHUNK), BF16),
            pltpu.VMEM((seq // KEY_CHUNK, KEY_CHUNK, Q_BLOCK), F32),
            pltpu.VMEM((N_KV_HEADS, KEY_CHUNK, cols), F32),
            pltpu.VMEM((N_KV_HEADS, KEY_CHUNK, cols), F32),
            pltpu.VMEM((N_KV_HEADS, HEAD_DIM, cols), F32),
            pltpu.VMEM((2 * N_KV_HEADS, cols), F32),
            pltpu.VMEM((1, Q_BLOCK), I32),
            pltpu.VMEM((1, Q_BLOCK), F32),
        ],
        compiler_params=pltpu.CompilerParams(
            dimension_semantics=("arbitrary", "arbitrary"), vmem_limit_bytes=VMEM_LIMIT),
        name="prompt_attn",
    )(h_t, h_t, h_t, h_t, h_t, h_t, wi_t, kv, ki)


PAGES_PER_STEP = 32


def _sample_select_kernel(pt_ref, qi_ref, kvs_ref, *refs):
    page_refs = refs[:PAGES_PER_STEP]
    idx_ref, meta_ref, sc_ref, snew_ref = refs[PAGES_PER_STEP:]
    b = pl.program_id(0)
    p = pl.program_id(1)
    nb = pl.num_programs(0)
    n_steps = pl.num_programs(1)
    n_pages, db, _ = sc_ref.shape

    qi = qi_ref[0]
    w_col = kvs_ref[0, :, KVI_WI:KVI_WI + 1] * IDX_W_SCALE

    def key_scores(keys_bf16):
        s = _dot_nt(qi, keys_bf16)
        return jnp.sum(w_col * jnp.maximum(s, 0.0), axis=0, keepdims=True)

    keys = jnp.concatenate([page_refs[r][0, 0] for r in range(PAGES_PER_STEP)], axis=0).astype(BF16)
    row = key_scores(keys)
    for r in range(PAGES_PER_STEP):
        sc_ref[p * PAGES_PER_STEP + r, pl.ds(b, 1), :] = row[:, r * PAGE_SIZE:(r + 1) * PAGE_SIZE]

    @pl.when(p == n_steps - 1)
    def _():
        ki_new = kvs_ref[0, :, KVI_KI:KVI_KI + IDX_DIM].astype(BF16)
        s_new = key_scores(ki_new)[:, 0:1]
        snew_ref[pl.ds(b, 1), :] = jnp.broadcast_to(s_new, (1, PAGE_SIZE))

    @pl.when((b == nb - 1) & (p == n_steps - 1))
    def _():
        sc = sc_ref[...]
        s_new = snew_ref[:, 0:1]

        def total(x):
            return jnp.sum(jnp.sum(x, axis=0), axis=1, keepdims=True)

        def bit_body(j, u):
            cand = u | lax.shift_left(jnp.int32(1), 31 - j)
            thr = _ordered_bits_to_f32(cand)
            cnt = total(jnp.where(sc >= thr[None], 1.0, 0.0)) + jnp.where(s_new >= thr, 1.0, 0.0)
            return jnp.where(cnt >= float(TOPK_MAX), cand, u)

        u = lax.fori_loop(0, 32, bit_body, jnp.zeros((db, 1), I32))
        thr = _ordered_bits_to_f32(u)
        n_gt = total(jnp.where(sc > thr[None], 1.0, 0.0)) + jnp.where(s_new > thr, 1.0, 0.0)
        need = float(TOPK_MAX) - n_gt
        flat = (lax.broadcasted_iota(I32, sc.shape, 0) * PAGE_SIZE
                + lax.broadcasted_iota(I32, sc.shape, 2))
        tie = sc == thr[None]

        def jbit_body(j, jl):
            cand = jl | lax.shift_left(jnp.int32(1), 14 - j)
            cnt = total(jnp.where(tie & (flat < cand[None]), 1.0, 0.0))
            return jnp.where(cnt <= need, cand, jl)

        jlim = lax.fori_loop(0, 15, jbit_body, jnp.zeros((db, 1), I32))
        sel = (sc > thr[None]) | (tie & (flat < jlim[None]))
        self32 = jnp.where(sel, 1.0, 0.0)
        n_pool = total(self32)

        n_rows = n_pages * db
        selb = self32.astype(BF16).reshape(n_rows, PAGE_SIZE)
        ri = lax.broadcasted_iota(I32, (PAGE_SIZE, PAGE_SIZE), 0)
        ci = lax.broadcasted_iota(I32, (PAGE_SIZE, PAGE_SIZE), 1)
        incl = jnp.dot(selb, jnp.where(ri <= ci, 1.0, 0.0).astype(BF16), preferred_element_type=F32)
        tot = jnp.broadcast_to(incl[:, PAGE_SIZE - 1:PAGE_SIZE], (n_rows, PAGE_SIZE)).astype(BF16)
        rr = lax.broadcasted_iota(I32, (n_rows, n_rows), 0)
        cc = lax.broadcasted_iota(I32, (n_rows, n_rows), 1)
        earlier_page_same_row = ((rr & (db - 1)) == (cc & (db - 1))) & (cc < (rr & -db))
        offs = jnp.dot(jnp.where(earlier_page_same_row, 1.0, 0.0).astype(BF16), tot,
                       preferred_element_type=F32)
        ends_rows = offs + tot.astype(F32)
        r_i = lax.broadcasted_iota(I32, (n_rows, PAGE_SIZE), 0)
        l_i = lax.broadcasted_iota(I32, (n_rows, PAGE_SIZE), 1)
        log2_db = db.bit_length() - 1
        diag = jnp.where(l_i == lax.shift_right_logical(r_i, log2_db), ends_rows, 0.0).astype(BF16)
        pick = jnp.where((lax.broadcasted_iota(I32, (db, n_rows), 1) & (db - 1))
                         == lax.broadcasted_iota(I32, (db, n_rows), 0), 1.0, 0.0).astype(BF16)
        ends = jnp.dot(pick, diag, preferred_element_type=F32)

        table = jnp.concatenate([incl.astype(BF16), offs.astype(BF16)], axis=1)
        slot = lax.broadcasted_iota(I32, (TOPK_MAX, PAGE_SIZE), 0).astype(F32)
        rows_lane = lax.broadcasted_iota(I32, (TOPK_MAX, n_rows), 1)
        for row_i in range(db):
            page_s = jnp.sum(jnp.where(ends[row_i:row_i + 1, :] <= slot, 1.0, 0.0), axis=1, keepdims=True)
            mrow = page_s.astype(I32) * db + row_i
            onehot = jnp.where(rows_lane == mrow, 1.0, 0.0).astype(BF16)
            got = jnp.dot(onehot, table, preferred_element_type=F32)
            k = slot - got[:, PAGE_SIZE:]
            off_s = jnp.sum(jnp.where(got[:, :PAGE_SIZE] <= k, 1.0, 0.0), axis=1, keepdims=True)
            pos = page_s * float(PAGE_SIZE) + off_s
            valid = slot[:, 0:1] < n_pool[row_i:row_i + 1, :]
            idx_ref[row_i] = jnp.where(valid, pos, 0.0).astype(I32)
        meta_ref[...] = jnp.broadcast_to(n_pool, (db, PAGE_SIZE)).astype(I32)


def _sample_select(page_table, qi_s, kvi_s16, pool_ki):
    db, n_pages = page_table.shape
    assert db & (db - 1) == 0, "row index arithmetic in the rank matrix assumes a power-of-two row count"
    assert n_pages == PAGE_SIZE, "page ends are laid out one page per lane"
    n_steps = n_pages // PAGES_PER_STEP

    def page_spec(r):
        return pl.BlockSpec(
            (1, 1, PAGE_SIZE, IDX_DIM),
            lambda b, p, pt: (0, pt[b, p * PAGES_PER_STEP + r], 0, 0))

    grid_spec = pltpu.PrefetchScalarGridSpec(
        num_scalar_prefetch=1,
        grid=(db, n_steps),
        in_specs=[
            pl.BlockSpec((1, IDX_HEADS, IDX_DIM), lambda b, p, pt: (b, 0, 0)),
            pl.BlockSpec((1, IDX_HEADS, KVI_WIDTH), lambda b, p, pt: (b, 0, 0)),
        ] + [page_spec(r) for r in range(PAGES_PER_STEP)],
        out_specs=[
            pl.BlockSpec((db, TOPK_MAX, 1), lambda b, p, pt: (0, 0, 0)),
            pl.BlockSpec((db, PAGE_SIZE), lambda b, p, pt: (0, 0)),
        ],
        scratch_shapes=[
            pltpu.VMEM((n_pages, db, PAGE_SIZE), F32),
            pltpu.VMEM((db, PAGE_SIZE), F32),
        ],
    )
    return pl.pallas_call(
        _sample_select_kernel,
        grid_spec=grid_spec,
        out_shape=[
            jax.ShapeDtypeStruct((db, TOPK_MAX, 1), I32),
            jax.ShapeDtypeStruct((db, PAGE_SIZE), I32),
        ],
        compiler_params=pltpu.CompilerParams(
            dimension_semantics=("arbitrary", "arbitrary"), vmem_limit_bytes=VMEM_LIMIT),
        name="sample_select",
    )(page_table, qi_s, kvi_s16, *([pool_ki] * PAGES_PER_STEP))


def _sample_attend_kernel(idx_ref, npool_ref, pt_ref, q_ref, kvs_ref, pk_ref, pv_ref, o_ref,
                          kbuf, vbuf, kflat, vflat, sem):
    b = pl.program_id(0)

    def row_copies(r):
        ix = idx_ref[b, r]
        page = pt_ref[b, lax.shift_right_logical(ix, 7)]
        off = ix & (PAGE_SIZE - 1)
        return (pltpu.make_async_copy(pk_ref.at[0, page, off], kbuf.at[r], sem.at[0]),
                pltpu.make_async_copy(pv_ref.at[0, page, off], vbuf.at[r], sem.at[1]))

    def start_body(r, carry):
        ck, cv = row_copies(r)
        ck.start()
        cv.start()
        return carry

    def wait_body(r, carry):
        ck, cv = row_copies(r)
        ck.wait()
        cv.wait()
        return carry

    lax.fori_loop(0, TOPK_MAX, start_body, 0)
    lax.fori_loop(0, TOPK_MAX, wait_body, 0)

    def head_copies(g):
        return (pltpu.make_async_copy(kbuf.at[:, g, :], kflat.at[g], sem.at[2]),
                pltpu.make_async_copy(vbuf.at[:, g, :], vflat.at[g], sem.at[3]))

    for g in range(N_KV_HEADS):
        for cp in head_copies(g):
            cp.start()
    for g in range(N_KV_HEADS):
        for cp in head_copies(g):
            cp.wait()

    n_pool = npool_ref[b]
    slot = lax.broadcasted_iota(I32, (1, TOPK_MAX), 1)
    slot_ok = slot < n_pool
    new_ok = n_pool < TOPK_MAX
    q_all = q_ref[0].astype(F32)
    for g in range(N_KV_HEADS):
        qg = q_all[g * Q_PER_KV:(g + 1) * Q_PER_KV, :].astype(BF16)
        ks = kflat[g].astype(BF16)
        vs = vflat[g].astype(BF16)
        k_new = kvs_ref[0, 0:1, KVI_K + g * HEAD_DIM:KVI_K + (g + 1) * HEAD_DIM].astype(BF16)
        v_new = kvs_ref[0, 0:1, KVI_V + g * HEAD_DIM:KVI_V + (g + 1) * HEAD_DIM].astype(BF16)
        s = jnp.where(slot_ok, _dot_nt(qg, ks) * ATTN_SCALE, -jnp.inf)
        s_new = jnp.sum(qg.astype(F32) * k_new.astype(F32), axis=-1, keepdims=True) * ATTN_SCALE
        s_new = jnp.where(new_ok, s_new, -jnp.inf)
        m = jnp.maximum(jnp.max(s, axis=-1, keepdims=True), s_new)
        p = jnp.exp(s - m)
        p_new = jnp.exp(s_new - m)
        denom = jnp.sum(p, axis=-1, keepdims=True) + p_new
        pn = (p / denom).astype(BF16)
        pn_new = (p_new / denom).astype(BF16).astype(F32)
        o = jnp.dot(pn, vs, preferred_element_type=F32) + pn_new * v_new.astype(F32)
        o_ref[0, g * Q_PER_KV:(g + 1) * Q_PER_KV, :] = o


def _sample_attend(idx, n_pool, page_table, q_s, kvi_s16, pool_k, pool_v):
    db = page_table.shape[0]
    grid_spec = pltpu.PrefetchScalarGridSpec(
        num_scalar_prefetch=3,
        grid=(db,),
        in_specs=[
            pl.BlockSpec((1, N_HEADS, HEAD_DIM), lambda b, *_: (b, 0, 0)),
            pl.BlockSpec((1, IDX_HEADS, KVI_WIDTH), lambda b, *_: (b, 0, 0)),
            pl.BlockSpec(memory_space=pl.ANY),
            pl.BlockSpec(memory_space=pl.ANY),
        ],
        out_specs=pl.BlockSpec((1, N_HEADS, HEAD_DIM), lambda b, *_: (b, 0, 0)),
        scratch_shapes=[
            pltpu.VMEM((TOPK_MAX, N_KV_HEADS, HEAD_DIM), F32),
            pltpu.VMEM((TOPK_MAX, N_KV_HEADS, HEAD_DIM), F32),
            pltpu.VMEM((N_KV_HEADS, TOPK_MAX, HEAD_DIM), F32),
            pltpu.VMEM((N_KV_HEADS, TOPK_MAX, HEAD_DIM), F32),
            pltpu.SemaphoreType.DMA((4,)),
        ],
    )
    return pl.pallas_call(
        _sample_attend_kernel,
        grid_spec=grid_spec,
        out_shape=jax.ShapeDtypeStruct((db, N_HEADS, HEAD_DIM), F32),
        compiler_params=pltpu.CompilerParams(dimension_semantics=("arbitrary",)),
        name="sample_attend",
    )(idx, n_pool, page_table, q_s, kvi_s16, pool_k, pool_v)


def _merge_kernel(chunked, oa_ref, za_ref, u_ref, v_ref, zb_ref, ga_ref, gb_ref, x_ref,
                  wpa_ref, wpb_ref, wout_ref, lng_ref, lnb_ref, ws_ref, bs_ref, gf_ref, y_ref, *rest):
    vn_ref, ob_ref = (None, rest[0]) if chunked else rest
    tm = x_ref.shape[0]
    u = _gelu(u_ref[...])
    v = _gelu(v_ref[...])
    mu = jnp.mean(v, axis=-1, keepdims=True)
    vc = v - mu
    vn = (vc * lax.rsqrt(jnp.mean(vc * vc, axis=-1, keepdims=True) + LN_EPS)) * lng_ref[...] + lnb_ref[...]
    if vn_ref is not None:
        vn_ref[...] = vn

    if chunked:
        vnb = vn.astype(BF16)
        ri = lax.broadcasted_iota(I32, (CHUNK, CHUNK), 0)
        ci = lax.broadcasted_iota(I32, (CHUNK, CHUNK), 1)
        for g in range(GMLP_GROUPS):
            wm = jnp.where(ri >= ci, ws_ref[g], 0.0).astype(BF16)
            lo = g * GMLP_GROUP_DIM
            for c in range(tm // CHUNK):
                mixed = jnp.dot(wm, vnb[c * CHUNK:(c + 1) * CHUNK, lo:lo + GMLP_GROUP_DIM],
                                preferred_element_type=F32) + bs_ref[:, g:g + 1]
                ob_ref[c * CHUNK:(c + 1) * CHUNK, lo:lo + GMLP_GROUP_DIM] = (
                    u[c * CHUNK:(c + 1) * CHUNK, lo:lo + GMLP_GROUP_DIM] * mixed)
    else:
        ob_ref[...] = u * (ws_ref[...] * vn + bs_ref[...])

    ha = (oa_ref[...] * _silu(za_ref[...])).astype(BF16)
    hb = (ob_ref[...] * _silu(zb_ref[...])).astype(BF16)
    ya = jnp.dot(ha, wpa_ref[...], preferred_element_type=F32)
    yb = jnp.dot(hb, wpb_ref[...], preferred_element_type=F32)
    mix = jax.nn.sigmoid(ga_ref[...]) * ya + jax.nn.sigmoid(gb_ref[...]) * yb
    out = x_ref[...] + jnp.dot(mix.astype(BF16), wout_ref[...], preferred_element_type=F32)
    ms = jnp.mean(out * out, axis=-1, keepdims=True)
    y_ref[...] = (out * lax.rsqrt(ms + RMS_EPS)) * gf_ref[...]


def _merge(chunked, o_a, gate, x2d, wpa, wpb, wout, lng, lnb, ws, bs, gf, tm):
    rows = x2d.shape[0]
    gw = GMLP_WIDTH

    def const(shape):
        return pl.BlockSpec(shape, lambda i: (0,) * len(shape), pipeline_mode=pl.Buffered(1))

    return pl.pallas_call(
        functools.partial(_merge_kernel, chunked),
        grid=(rows // tm,),
        in_specs=[
            pl.BlockSpec((tm, ATTN_WIDTH), lambda i: (i, 0)),
            pl.BlockSpec((tm, gw), lambda i: (i, 0)),
            pl.BlockSpec((tm, gw), lambda i: (i, 1)),
            pl.BlockSpec((tm, gw), lambda i: (i, 2)),
            pl.BlockSpec((tm, gw), lambda i: (i, 3)),
            pl.BlockSpec((tm, D_MODEL), lambda i: (i, 2)),
            pl.BlockSpec((tm, D_MODEL), lambda i: (i, 3)),
            pl.BlockSpec((tm, D_MODEL), lambda i: (i, 0)),
            const(wpa.shape), const(wpb.shape), const(wout.shape),
            const(lng.shape), const(lnb.shape), const(ws.shape), const(bs.shape), const(gf.shape),
        ],
        out_specs=[pl.BlockSpec((tm, D_MODEL), lambda i: (i, 0))]
        + ([] if chunked else [pl.BlockSpec((tm, gw), lambda i: (i, 0))]),
        out_shape=[jax.ShapeDtypeStruct((rows, D_MODEL), F32)]
        + ([] if chunked else [jax.ShapeDtypeStruct((rows, gw), F32)]),
        scratch_shapes=[pltpu.VMEM((tm, gw), F32)],
        compiler_params=pltpu.CompilerParams(
            dimension_semantics=("arbitrary",), vmem_limit_bytes=VMEM_LIMIT),
        name="merge_prompt" if chunked else "merge_sample",
    )(o_a, gate, gate, gate, gate, gate, gate, x2d, wpa, wpb, wout, lng, lnb, ws, bs, gf)


def kernel(x_prompt, x_sample, cache_k, cache_v, cache_k_idx, page_table, norm_in_g, w_in,
           w_proj_a, w_proj_b, w_out, ln_g, ln_b, w_spatial, b_spatial, norm_f_g):
    depth = w_in.shape[0]
    assert depth == 1, "single trunk layer"
    batch, seq, _ = x_prompt.shape
    db, dseq, _ = x_sample.shape
    assert dseq == 1
    l = 0
    wpa = w_proj_a[l].astype(BF16)
    wpb = w_proj_b[l].astype(BF16)
    wout = w_out[l].astype(BF16)
    g_in = norm_in_g[l].reshape(1, D_MODEL)
    lng = ln_g[l].reshape(1, GMLP_WIDTH)
    lnb = ln_b[l].reshape(1, GMLP_WIDTH)
    gf = norm_f_g.reshape(1, D_MODEL)

    xp = x_prompt.reshape(batch * seq, D_MODEL)
    wt = w_in[l].T
    xn_p = _rmsnorm_bf16(xp, g_in, 1024)
    h_t = _inproj(xn_p, wt, COL_Q, COL_WI, 512, 1024, BF16, True)
    kv_p, ki_p2, wi_t = _inproj_kvi(xn_p, wt, 1024)
    gate_p = _inproj(xn_p, wt, GATE_COL0, GATE_WIDTH, 1024, 1024, F32, False)
    oa_p = _prompt_attn(h_t, wi_t, kv_p, ki_p2, batch, seq)
    bs_t = b_spatial[l].T
    (y_p,) = _merge(True, oa_p, gate_p, xp, wpa, wpb, wout, lng, lnb, w_spatial[l], bs_t, gf, 256)

    xs = x_sample.reshape(db, D_MODEL)
    xn_s = _rmsnorm_bf16(xs, g_in, db)
    h_s = _inproj(xn_s, wt, COL_Q, COL_WI, 512, db, F32, False)
    kv_s, ki_s2, wi_t_s = _inproj_kvi(xn_s, wt, db)
    gate_s = _inproj(xn_s, wt, GATE_COL0, GATE_WIDTH, 1024, db, F32, False)
    kvi_s = jnp.concatenate(
        [kv_s, ki_s2, wi_t_s.T, jnp.zeros((db, KVI_WIDTH - KVI_WI - IDX_HEADS), F32)], axis=1)
    q_s = h_s[:, COL_Q:COL_K].astype(BF16).reshape(db, N_HEADS, HEAD_DIM)
    qi_s = h_s[:, COL_QI:COL_WI].astype(BF16).reshape(db, IDX_HEADS, IDX_DIM)
    wi_col = kvi_s[:, KVI_WI:KVI_WI + IDX_HEADS].reshape(db, IDX_HEADS, 1)
    kvi_s16 = jnp.broadcast_to(kvi_s[:, None, :], (db, IDX_HEADS, KVI_WIDTH))
    kvi_s16 = lax.dynamic_update_slice(kvi_s16, wi_col, (0, 0, KVI_WI))
    idx, meta = _sample_select(page_table, qi_s, kvi_s16, cache_k_idx)
    oa_s = _sample_attend(idx.reshape(db, TOPK_MAX), meta[:, 0], page_table, q_s, kvi_s16,
                          cache_k, cache_v).reshape(db, ATTN_WIDTH)
    ws0 = jnp.repeat(w_spatial[l][:, 0, 0], GMLP_GROUP_DIM).reshape(1, GMLP_WIDTH)
    bs0 = jnp.repeat(b_spatial[l][:, 0], GMLP_GROUP_DIM).reshape(1, GMLP_WIDTH)
    y_s, vn_s = _merge(False, oa_s, gate_s, xs, wpa, wpb, wout, lng, lnb, ws0, bs0, gf, db)

    def kv_out(kvi, lead):
        k = kvi[:, KVI_K:KVI_K + KV_WIDTH].reshape((1,) + lead + (N_KV_HEADS, HEAD_DIM))
        v = kvi[:, KVI_V:KVI_V + KV_WIDTH].reshape((1,) + lead + (N_KV_HEADS, HEAD_DIM))
        ki = kvi[:, KVI_KI:KVI_KI + IDX_DIM].reshape((1,) + lead + (IDX_DIM,))
        return k, v, ki

    k_p = kv_p[:, :KV_WIDTH].reshape(1, batch, seq, N_KV_HEADS, HEAD_DIM)
    v_p = kv_p[:, KV_WIDTH:].reshape(1, batch, seq, N_KV_HEADS, HEAD_DIM)
    ki_p = ki_p2.reshape(1, batch, seq, IDX_DIM)
    k_s, v_s, ki_s = kv_out(kvi_s, (db, dseq))
    return (y_p.reshape(batch, seq, D_MODEL), y_s.reshape(db, dseq, D_MODEL),
            k_p, v_p, ki_p, k_s, v_s, ki_s, vn_s.reshape(1, db, dseq, GMLP_WIDTH))
```

```python
import functools

import jax
import jax.numpy as jnp
import numpy as np
from jax import lax
from jax.experimental import pallas as pl
from jax.experimental.pallas import tpu as pltpu

F32 = jnp.float32
BF16 = jnp.bfloat16
I32 = jnp.int32

D_MODEL = 2048
N_HEADS = 8
N_KV_HEADS = 2
HEAD_DIM = 128
Q_PER_KV = N_HEADS // N_KV_HEADS
ATTN_WIDTH = N_HEADS * HEAD_DIM
KV_WIDTH = N_KV_HEADS * HEAD_DIM
IDX_HEADS = 16
IDX_DIM = 128
TOPK_MAX = 256
PAGE_SIZE = 128
GMLP_WIDTH = 1024
GMLP_GROUPS = 8
GMLP_GROUP_DIM = GMLP_WIDTH // GMLP_GROUPS
CHUNK = 128
RMS_EPS = 1e-6
LN_EPS = 1e-5

SUBLANES = 8
COL_Q = 0
COL_K = COL_Q + ATTN_WIDTH
COL_V = COL_K + KV_WIDTH
COL_QI = COL_V + KV_WIDTH
COL_WI = COL_QI + IDX_HEADS * IDX_DIM
COL_KI = COL_WI + IDX_HEADS
GATE_COL0 = COL_KI + IDX_DIM
GATE_WIDTH = 4 * GMLP_WIDTH + 2 * D_MODEL
KVI_WIDTH = 768
KVI_K, KVI_V, KVI_KI, KVI_WI = 0, 256, 512, 640

Q_BLOCK = 128
KEY_CHUNK = 512
COUNT_ROWS = 64
ONES_ROWS = 16
SIGN_EXP_BITS = 9
MANTISSA_BITS = 32 - SIGN_EXP_BITS
MANTISSA_CLEAR = -(1 << MANTISSA_BITS)
PREFIX_TRIES = 3
SEARCH_ALWAYS_BITS = 16
SEARCH_GROUP_BITS = 4
NEG_SENTINEL = float(np.finfo(np.float32).min)
NEG_ABOVE_SENTINEL = float(np.nextafter(np.float32(NEG_SENTINEL), np.float32(0.0)))
INT_MIN = -(2 ** 31)
IDX_W_SCALE = float(IDX_HEADS ** -0.5 * IDX_DIM ** -0.5)
ATTN_SCALE = float(HEAD_DIM ** -0.5)
ATTN_SCALE_LOG2E = float(HEAD_DIM ** -0.5 * np.log2(np.e))
VMEM_LIMIT = 56 * 1024 * 1024


def _dot_nt(a, b):
    return lax.dot_general(a, b, (((1,), (1,)), ((), ())), preferred_element_type=F32)


def _ordered_bits_to_f32(u):
    signed_pow, zero_exp = _sign_exp_parts(u)
    return _with_mantissa(u, signed_pow, zero_exp)


def _sign_exp_parts(u):
    bits = jnp.where(u < 0, u ^ INT_MIN, ~u)
    exp = lax.shift_right_logical(bits, 23) & 0xFF
    e1 = exp - 1
    p = jnp.full(u.shape, float(2.0 ** -126), F32)
    for b in range(7):
        p = jnp.where(((e1 >> b) & 1) == 1, p * float(2.0 ** (2 ** b)), p)
    top = ((e1 >> 7) & 1) == 1
    p = jnp.where(top, p * float(2.0 ** 64), p)
    p = jnp.where(top, p * float(2.0 ** 64), p)
    return jnp.where(bits < 0, -p, p), exp == 0


def _with_mantissa(u, signed_pow, zero_exp):
    low = jnp.where(u < 0, u, ~u) & 0x7FFFFF
    mant = 1.0 + low.astype(F32) * float(2.0 ** -23)
    return jnp.where(zero_exp, 0.0, signed_pow * mant)


def _gelu(x):
    return 0.5 * x * (1.0 + lax.erf(x * float(np.sqrt(0.5))))


def _silu(x):
    return x * jax.nn.sigmoid(x)


def _rmsnorm_kernel(x_ref, g_ref, o_ref):
    x = x_ref[...]
    ms = jnp.mean(x * x, axis=-1, keepdims=True)
    o_ref[...] = ((x * lax.rsqrt(ms + RMS_EPS)) * g_ref[...]).astype(o_ref.dtype)


def _rmsnorm_bf16(x2d, g, tm):
    rows = x2d.shape[0]
    return pl.pallas_call(
        _rmsnorm_kernel,
        grid=(rows // tm,),
        in_specs=[pl.BlockSpec((tm, D_MODEL), lambda i: (i, 0)),
                  pl.BlockSpec((1, D_MODEL), lambda i: (0, 0))],
        out_specs=pl.BlockSpec((tm, D_MODEL), lambda i: (i, 0)),
        out_shape=jax.ShapeDtypeStruct((rows, D_MODEL), BF16),
        compiler_params=pltpu.CompilerParams(
            dimension_semantics=("arbitrary",), vmem_limit_bytes=VMEM_LIMIT),
        name="rmsnorm_in",
    )(x2d, g)


def _inproj_kernel(feature_major, xn_ref, w_ref, o_ref, wb_ref):
    @pl.when(pl.program_id(1) == 0)
    def _():
        wb_ref[...] = w_ref[...].astype(BF16)

    if feature_major:
        o_ref[...] = _dot_nt(wb_ref[...], xn_ref[...]).astype(o_ref.dtype)
    else:
        o_ref[...] = _dot_nt(xn_ref[...], wb_ref[...]).astype(o_ref.dtype)


def _inproj(xn, wt, row_start, n_rows, tn, tm, out_dtype, feature_major):
    rows = xn.shape[0]
    assert n_rows % tn == 0 and rows % tm == 0 and row_start % SUBLANES == 0
    if row_start % tn == 0:
        j0 = row_start // tn
        w_spec = pl.BlockSpec((tn, D_MODEL), lambda j, i: (j0 + j, 0))
    else:
        w_spec = pl.BlockSpec((pl.Element(tn), pl.Element(D_MODEL)), lambda j, i: ((row_start // SUBLANES + j * (tn // SUBLANES)) * SUBLANES, 0))
    if feature_major:
        out_spec = pl.BlockSpec((tn, tm), lambda j, i: (j, i))
        out_shape = (n_rows, rows)
    else:
        out_spec = pl.BlockSpec((tm, tn), lambda j, i: (i, j))
        out_shape = (rows, n_rows)
    return pl.pallas_call(
        functools.partial(_inproj_kernel, feature_major),
        grid=(n_rows // tn, rows // tm),
        in_specs=[pl.BlockSpec((tm, D_MODEL), lambda j, i: (i, 0)), w_spec],
        out_specs=out_spec,
        out_shape=jax.ShapeDtypeStruct(out_shape, out_dtype),
        scratch_shapes=[pltpu.VMEM((tn, D_MODEL), BF16)],
        compiler_params=pltpu.CompilerParams(
            dimension_semantics=("arbitrary", "arbitrary"), vmem_limit_bytes=VMEM_LIMIT),
        name="inproj_t" if feature_major else "inproj",
    )(xn, wt)


def _inproj_kvi_kernel(xn_ref, wkv_ref, wwi_ref, wki_ref, kv_ref, ki_ref, wit_ref, k3_ref, v3_ref,
                       wkvb_ref, wwib_ref, wkib_ref):
    @pl.when(pl.program_id(0) == 0)
    def _():
        wkvb_ref[...] = wkv_ref[...].astype(BF16)
        wwib_ref[...] = wwi_ref[...].astype(BF16)
        wkib_ref[...] = wki_ref[...].astype(BF16)

    xn = xn_ref[...]
    kv = _dot_nt(xn, wkvb_ref[...])
    kv_ref[...] = kv
    for g in range(N_KV_HEADS):
        k3_ref[:, g, :] = kv[:, g * HEAD_DIM:(g + 1) * HEAD_DIM]
        v3_ref[:, g, :] = kv[:, KV_WIDTH + g * HEAD_DIM:KV_WIDTH + (g + 1) * HEAD_DIM]
    ki_ref[...] = _dot_nt(xn, wkib_ref[...])
    wit_ref[...] = _dot_nt(wwib_ref[...], xn)


def _inproj_kvi(xn, wt, tm):
    rows = xn.shape[0]
    assert rows % tm == 0 and COL_K % (2 * KV_WIDTH) == 0 and COL_WI % IDX_HEADS == 0
    return pl.pallas_call(
        _inproj_kvi_kernel,
        grid=(rows // tm,),
        in_specs=[
            pl.BlockSpec((tm, D_MODEL), lambda i: (i, 0)),
            pl.BlockSpec((2 * KV_WIDTH, D_MODEL), lambda i: (COL_K // (2 * KV_WIDTH), 0)),
            pl.BlockSpec((IDX_HEADS, D_MODEL), lambda i: (COL_WI // IDX_HEADS, 0)),
            pl.BlockSpec((pl.Element(IDX_DIM), pl.Element(D_MODEL)), lambda i: (COL_KI, 0)),
        ],
        out_specs=[
            pl.BlockSpec((tm, 2 * KV_WIDTH), lambda i: (i, 0)),
            pl.BlockSpec((tm, IDX_DIM), lambda i: (i, 0)),
            pl.BlockSpec((IDX_HEADS, tm), lambda i: (0, i)),
            pl.BlockSpec((tm, N_KV_HEADS, HEAD_DIM), lambda i: (i, 0, 0)),
            pl.BlockSpec((tm, N_KV_HEADS, HEAD_DIM), lambda i: (i, 0, 0)),
        ],
        out_shape=[
            jax.ShapeDtypeStruct((rows, 2 * KV_WIDTH), F32),
            jax.ShapeDtypeStruct((rows, IDX_DIM), F32),
            jax.ShapeDtypeStruct((IDX_HEADS, rows), F32),
            jax.ShapeDtypeStruct((rows, N_KV_HEADS, HEAD_DIM), F32),
            jax.ShapeDtypeStruct((rows, N_KV_HEADS, HEAD_DIM), F32),
        ],
        scratch_shapes=[
            pltpu.VMEM((2 * KV_WIDTH, D_MODEL), BF16),
            pltpu.VMEM((IDX_HEADS, D_MODEL), BF16),
            pltpu.VMEM((IDX_DIM, D_MODEL), BF16),
        ],
        compiler_params=pltpu.CompilerParams(
            dimension_semantics=("arbitrary",), vmem_limit_bytes=VMEM_LIMIT),
        name="inproj_kvi",
    )(xn, wt, wt, wt)


def _prompt_attn_kernel(qt_ref, qit0_ref, qit1_ref, qit2_ref, qit3_ref, vt_ref, wit_ref, kvf_ref, kif_ref,
                        o_ref, kb_ref, kib_ref, vtc_ref, sc_ref, sa_ref, sb_ref, acc_ref, ml_ref, u_ref, cnt_ref):
    i = pl.program_id(1)
    seq = kvf_ref.shape[0]
    qit_refs = (qit0_ref, qit1_ref, qit2_ref, qit3_ref)
    heads_per_ref = IDX_HEADS // len(qit_refs)

    @pl.when(i == 0)
    def _():
        kb_ref[...] = kvf_ref[:, 0:KV_WIDTH].astype(BF16)
        kib_ref[...] = kif_ref[...].astype(BF16)
        for c in range(seq // KEY_CHUNK):
            for g in range(N_KV_HEADS):
                vtc_ref[c, g, :HEAD_DIM, :] = vt_ref[g * HEAD_DIM:(g + 1) * HEAD_DIM,
                                                     c * KEY_CHUNK:(c + 1) * KEY_CHUNK]
                vtc_ref[c, g, HEAD_DIM:, :] = jnp.ones((ONES_ROWS, KEY_CHUNK), BF16)

    n_chunks = (i * Q_BLOCK) // KEY_CHUNK + 1
    q_pos = i * Q_BLOCK + lax.broadcasted_iota(I32, (1, Q_BLOCK), 1)
    key0 = lax.broadcasted_iota(I32, (KEY_CHUNK, 1), 0)

    w_t = wit_ref[...] * IDX_W_SCALE
    qit_all = jnp.concatenate(
        [qit_refs[h // heads_per_ref][(h % heads_per_ref) * IDX_DIM:(h % heads_per_ref + 1) * IDX_DIM, :]
         for h in range(IDX_HEADS)], axis=1)

    def score_body(c, carry):
        k0 = pl.multiple_of(c * KEY_CHUNK, KEY_CHUNK)
        res = jnp.dot(kib_ref[pl.ds(k0, KEY_CHUNK), :], qit_all, preferred_element_type=F32)
        acc = jnp.zeros((KEY_CHUNK, Q_BLOCK), F32)
        for h in range(IDX_HEADS):
            acc = acc + w_t[h:h + 1, :] * jnp.maximum(res[:, h * Q_BLOCK:(h + 1) * Q_BLOCK], 0.0)
        sc_ref[c] = jnp.where(k0 + key0 <= q_pos, acc, NEG_SENTINEL)
        return carry

    lax.fori_loop(0, n_chunks, score_body, 0)

    def count(pred):
        def body(c, cnt):
            ones = jnp.where(pred(sc_ref[c], c), 1.0, 0.0)
            return cnt + ones.reshape(KEY_CHUNK // COUNT_ROWS, COUNT_ROWS, Q_BLOCK).sum(axis=0)
        part = lax.fori_loop(0, n_chunks, body, jnp.zeros((COUNT_ROWS, Q_BLOCK), F32))
        return jnp.sum(part, axis=0, keepdims=True)

    def search(j0, n, u, cnt_u, decode):
        def body(j, carry):
            u, cnt_u = carry
            cand = u | lax.shift_left(jnp.int32(1), 31 - j)
            thr_c = decode(cand)
            cnt = count(lambda blk, c: blk >= thr_c)
            ok = cnt >= float(TOPK_MAX)
            return jnp.where(ok, cand, u), jnp.where(ok, cnt, cnt_u)
        return lax.fori_loop(j0, j0 + n, body, (u, cnt_u))

    def max_body(c, m):
        return jnp.maximum(m, sc_ref[c].reshape(KEY_CHUNK // COUNT_ROWS, COUNT_ROWS, Q_BLOCK).max(axis=0))
    row_max = jnp.max(lax.fori_loop(0, n_chunks, max_body, jnp.full((COUNT_ROWS, Q_BLOCK), -jnp.inf, F32)),
                      axis=0, keepdims=True)
    max_bits = lax.bitcast_convert_type(row_max, I32)
    prefix = jnp.where(max_bits < 0, ~max_bits, max_bits ^ INT_MIN) & MANTISSA_CLEAR
    u_ref[...] = jnp.zeros((1, Q_BLOCK), I32)
    cnt_ref[...] = jnp.zeros((1, Q_BLOCK), F32)
    for _ in range(PREFIX_TRIES):
        thr_p = _ordered_bits_to_f32(prefix)
        cnt_p = count(lambda blk, c: blk >= thr_p)
        take = (cnt_ref[...] == 0.0) & (cnt_p >= float(TOPK_MAX))
        u_ref[...] = jnp.where(take, prefix, u_ref[...])
        cnt_ref[...] = jnp.where(take, cnt_p, cnt_ref[...])
        prefix = prefix - (1 << MANTISSA_BITS)

    @pl.when(jnp.min(cnt_ref[...]) == 0.0)
    def _():
        u0 = jnp.zeros((1, Q_BLOCK), I32)
        u9, cnt9 = search(0, SIGN_EXP_BITS, u0, (u0 + n_chunks * KEY_CHUNK).astype(F32), _ordered_bits_to_f32)
        u_ref[...] = u9
        cnt_ref[...] = cnt9

    u = u_ref[...]
    cnt_u = cnt_ref[...]
    signed_pow, zero_exp = _sign_exp_parts(u)

    def decode_mantissa(cand):
        return _with_mantissa(cand, signed_pow, zero_exp)

    u, cnt_u = search(SIGN_EXP_BITS, SEARCH_ALWAYS_BITS - SIGN_EXP_BITS, u, cnt_u, decode_mantissa)
    u_ref[...] = u
    cnt_ref[...] = cnt_u
    for j0 in range(SEARCH_ALWAYS_BITS, 32, SEARCH_GROUP_BITS):
        @pl.when(jnp.max(jnp.where(cnt_ref[...] == float(TOPK_MAX), 0.0, 1.0)) > 0.0)
        def _():
            u_g, cnt_g = search(j0, SEARCH_GROUP_BITS, u_ref[...], cnt_ref[...], decode_mantissa)
            u_ref[...] = u_g
            cnt_ref[...] = cnt_g
    u = u_ref[...]
    n_ge = cnt_ref[...]
    thr = decode_mantissa(u)
    thr_adm = jnp.maximum(thr, NEG_ABOVE_SENTINEL)
    excess = jnp.max(jnp.where((n_ge > float(TOPK_MAX)) & (thr > NEG_SENTINEL), 1.0, 0.0)) > 0.0

    @pl.when(excess)
    def _():
        need = float(TOPK_MAX) - count(lambda blk, c: blk > thr)

        def jbit_body(j, jl):
            cand = jl | lax.shift_left(jnp.int32(1), 12 - j)
            cnt = count(lambda blk, c: (blk == thr) & (c * KEY_CHUNK + key0 < cand))
            return jnp.where(cnt <= need, cand, jl)
        jlim = lax.fori_loop(0, 13, jbit_body, jnp.zeros((1, Q_BLOCK), I32))

        def demote_body(c, carry):
            blk = sc_ref[c]
            dropped = (blk == thr) & (thr > NEG_SENTINEL) & (c * KEY_CHUNK + key0 >= jlim)
            sc_ref[c] = jnp.where(dropped, NEG_SENTINEL, blk)
            return carry
        lax.fori_loop(0, n_chunks, demote_body, 0)

    acc_ref[...] = jnp.zeros(acc_ref.shape, F32)
    cols = Q_PER_KV * Q_BLOCK
    qt_groups = [
        jnp.concatenate([qt_ref[(g * Q_PER_KV + h) * HEAD_DIM:(g * Q_PER_KV + h + 1) * HEAD_DIM, :]
                         for h in range(Q_PER_KV)], axis=1)
        for g in range(N_KV_HEADS)
    ]

    last_chunk = seq // KEY_CHUNK - 1

    def qk_scores(c, dst_ref):
        k0 = pl.multiple_of(jnp.minimum(c, last_chunk) * KEY_CHUNK, KEY_CHUNK)
        for g in range(N_KV_HEADS):
            kc = kb_ref[pl.ds(k0, KEY_CHUNK), g * HEAD_DIM:(g + 1) * HEAD_DIM]
            dst_ref[g] = jnp.dot(kc, qt_groups[g], preferred_element_type=F32)

    def softmax_pv(c, src_ref):
        bias1 = jnp.where(sc_ref[c] >= thr_adm, 0.0, -jnp.inf)
        bias = jnp.concatenate([bias1] * Q_PER_KV, axis=1)
        for g in range(N_KV_HEADS):
            m_old = ml_ref[2 * g:2 * g + 1, :]
            l_old = ml_ref[2 * g + 1:2 * g + 2, :]
            s = src_ref[g] * ATTN_SCALE_LOG2E + bias
            m_new = jnp.maximum(m_old, jnp.max(s, axis=0, keepdims=True))
            m_safe = jnp.where(m_new == -jnp.inf, 0.0, m_new)
            alpha = jnp.exp2(m_old - m_safe)
            p = jnp.exp2(s - m_safe).astype(BF16)
            pv = jnp.dot(vtc_ref[c, g], p, preferred_element_type=F32)
            acc_ref[g] = alpha * acc_ref[g] + pv[:HEAD_DIM]
            ml_ref[2 * g:2 * g + 1, :] = m_new
            ml_ref[2 * g + 1:2 * g + 2, :] = alpha * l_old + pv[HEAD_DIM:HEAD_DIM + 1]

    def pair_body(t, carry):
        qk_scores(2 * t + 1, sb_ref)
        softmax_pv(2 * t, sa_ref)
        qk_scores(2 * t + 2, sa_ref)
        softmax_pv(2 * t + 1, sb_ref)
        return carry

    for g in range(N_KV_HEADS):
        ml_ref[2 * g:2 * g + 1, :] = jnp.full((1, cols), -jnp.inf, F32)
        ml_ref[2 * g + 1:2 * g + 2, :] = jnp.zeros((1, cols), F32)
    qk_scores(0, sa_ref)
    lax.fori_loop(0, n_chunks // 2, pair_body, 0)

    @pl.when(n_chunks % 2 == 1)
    def _():
        softmax_pv(n_chunks - 1, sa_ref)

    for g in range(N_KV_HEADS):
        o = acc_ref[g] / ml_ref[2 * g + 1:2 * g + 2, :]
        for h in range(Q_PER_KV):
            hd = (g * Q_PER_KV + h) * HEAD_DIM
            o_ref[:, hd:hd + HEAD_DIM] = o[:, h * Q_BLOCK:(h + 1) * Q_BLOCK].T


def _prompt_attn(h_t, wi_t, kv, ki, batch, seq):
    nb = seq // Q_BLOCK
    cols = Q_PER_KV * Q_BLOCK
    qi_rows = IDX_HEADS * IDX_DIM // 4
    qi_b0 = COL_QI // qi_rows

    def qi_spec(t):
        return pl.BlockSpec((qi_rows, Q_BLOCK), lambda b, i: (qi_b0 + t, b * nb + i))

    return pl.pallas_call(
        _prompt_attn_kernel,
        grid=(batch, nb),
        in_specs=[
            pl.BlockSpec((ATTN_WIDTH, Q_BLOCK), lambda b, i: (COL_Q // ATTN_WIDTH, b * nb + i)),
            qi_spec(0), qi_spec(1), qi_spec(2), qi_spec(3),
            pl.BlockSpec((KV_WIDTH, seq), lambda b, i: (COL_V // KV_WIDTH, b)),
            pl.BlockSpec((IDX_HEADS, Q_BLOCK), lambda b, i: (0, b * nb + i)),
            pl.BlockSpec((seq, 2 * KV_WIDTH), lambda b, i: (b, 0)),
            pl.BlockSpec((seq, IDX_DIM), lambda b, i: (b, 0)),
        ],
        out_specs=pl.BlockSpec((Q_BLOCK, ATTN_WIDTH), lambda b, i: (b * nb + i, 0)),
        out_shape=jax.ShapeDtypeStruct((batch * seq, ATTN_WIDTH), F32),
        scratch_shapes=[
            pltpu.VMEM((seq, KV_WIDTH), BF16),
            pltpu.VMEM((seq, IDX_DIM), BF16),
            pltpu.VMEM((seq // KEY_CHUNK, N_KV_HEADS, HEAD_DIM + ONES_ROWS, KEY_CHUNK), BF16),
            pltpu.VMEM((seq // KEY_CHUNK, KEY_CHUNK, Q_BLOCK), F32),
            pltpu.VMEM((N_KV_HEADS, KEY_CHUNK, cols), F32),
            pltpu.VMEM((N_KV_HEADS, KEY_CHUNK, cols), F32),
            pltpu.VMEM((N_KV_HEADS, HEAD_DIM, cols), F32),
            pltpu.VMEM((2 * N_KV_HEADS, cols), F32),
            pltpu.VMEM((1, Q_BLOCK), I32),
            pltpu.VMEM((1, Q_BLOCK), F32),
        ],
        compiler_params=pltpu.CompilerParams(
            dimension_semantics=("arbitrary", "arbitrary"), vmem_limit_bytes=VMEM_LIMIT),
        name="prompt_attn",
    )(h_t, h_t, h_t, h_t, h_t, h_t, wi_t, kv, ki)


PAGES_PER_STEP = 32


def _sample_select_kernel(pt_ref, qi_ref, kvs_ref, *refs):
    page_refs = refs[:PAGES_PER_STEP]
    idx_ref, meta_ref, sc_ref, snew_ref = refs[PAGES_PER_STEP:]
    b = pl.program_id(0)
    p = pl.program_id(1)
    nb = pl.num_programs(0)
    n_steps = pl.num_programs(1)
    n_pages, db, _ = sc_ref.shape

    qi = qi_ref[0]
    w_col = kvs_ref[0, :, KVI_WI:KVI_WI + 1] * IDX_W_SCALE

    def key_scores(keys_bf16):
        s = _dot_nt(qi, keys_bf16)
        return jnp.sum(w_col * jnp.maximum(s, 0.0), axis=0, keepdims=True)

    keys = jnp.concatenate([page_refs[r][0, 0] for r in range(PAGES_PER_STEP)], axis=0).astype(BF16)
    row = key_scores(keys)
    for r in range(PAGES_PER_STEP):
        sc_ref[p * PAGES_PER_STEP + r, pl.ds(b, 1), :] = row[:, r * PAGE_SIZE:(r + 1) * PAGE_SIZE]

    @pl.when(p == n_steps - 1)
    def _():
        ki_new = kvs_ref[0, :, KVI_KI:KVI_KI + IDX_DIM].astype(BF16)
        s_new = key_scores(ki_new)[:, 0:1]
        snew_ref[pl.ds(b, 1), :] = jnp.broadcast_to(s_new, (1, PAGE_SIZE))

    @pl.when((b == nb - 1) & (p == n_steps - 1))
    def _():
        sc = sc_ref[...]
        s_new = snew_ref[:, 0:1]

        def total(x):
            return jnp.sum(jnp.sum(x, axis=0), axis=1, keepdims=True)

        def bit_body(j, u):
            cand = u | lax.shift_left(jnp.int32(1), 31 - j)
            thr = _ordered_bits_to_f32(cand)
            cnt = total(jnp.where(sc >= thr[None], 1.0, 0.0)) + jnp.where(s_new >= thr, 1.0, 0.0)
            return jnp.where(cnt >= float(TOPK_MAX), cand, u)

        u = lax.fori_loop(0, 32, bit_body, jnp.zeros((db, 1), I32))
        thr = _ordered_bits_to_f32(u)
        n_gt = total(jnp.where(sc > thr[None], 1.0, 0.0)) + jnp.where(s_new > thr, 1.0, 0.0)
        need = float(TOPK_MAX) - n_gt
        flat = (lax.broadcasted_iota(I32, sc.shape, 0) * PAGE_SIZE
                + lax.broadcasted_iota(I32, sc.shape, 2))
        tie = sc == thr[None]

        def jbit_body(j, jl):
            cand = jl | lax.shift_left(jnp.int32(1), 14 - j)
            cnt = total(jnp.where(tie & (flat < cand[None]), 1.0, 0.0))
            return jnp.where(cnt <= need, cand, jl)

        jlim = lax.fori_loop(0, 15, jbit_body, jnp.zeros((db, 1), I32))
        sel = (sc > thr[None]) | (tie & (flat < jlim[None]))
        self32 = jnp.where(sel, 1.0, 0.0)
        n_pool = total(self32)

        n_rows = n_pages * db
        selb = self32.astype(BF16).reshape(n_rows, PAGE_SIZE)
        ri = lax.broadcasted_iota(I32, (PAGE_SIZE, PAGE_SIZE), 0)
        ci = lax.broadcasted_iota(I32, (PAGE_SIZE, PAGE_SIZE), 1)
        incl = jnp.dot(selb, jnp.where(ri <= ci, 1.0, 0.0).astype(BF16), preferred_element_type=F32)
        tot = jnp.broadcast_to(incl[:, PAGE_SIZE - 1:PAGE_SIZE], (n_rows, PAGE_SIZE)).astype(BF16)
        rr = lax.broadcasted_iota(I32, (n_rows, n_rows), 0)
        cc = lax.broadcasted_iota(I32, (n_rows, n_rows), 1)
        earlier_page_same_row = ((rr & (db - 1)) == (cc & (db - 1))) & (cc < (rr & -db))
        offs = jnp.dot(jnp.where(earlier_page_same_row, 1.0, 0.0).astype(BF16), tot,
                       preferred_element_type=F32)
        ends_rows = offs + tot.astype(F32)
        r_i = lax.broadcasted_iota(I32, (n_rows, PAGE_SIZE), 0)
        l_i = lax.broadcasted_iota(I32, (n_rows, PAGE_SIZE), 1)
        log2_db = db.bit_length() - 1
        diag = jnp.where(l_i == lax.shift_right_logical(r_i, log2_db), ends_rows, 0.0).astype(BF16)
        pick = jnp.where((lax.broadcasted_iota(I32, (db, n_rows), 1) & (db - 1))
                         == lax.broadcasted_iota(I32, (db, n_rows), 0), 1.0, 0.0).astype(BF16)
        ends = jnp.dot(pick, diag, preferred_element_type=F32)

        table = jnp.concatenate([incl.astype(BF16), offs.astype(BF16)], axis=1)
        slot = lax.broadcasted_iota(I32, (TOPK_MAX, PAGE_SIZE), 0).astype(F32)
        rows_lane = lax.broadcasted_iota(I32, (TOPK_MAX, n_rows), 1)
        for row_i in range(db):
            page_s = jnp.sum(jnp.where(ends[row_i:row_i + 1, :] <= slot, 1.0, 0.0), axis=1, keepdims=True)
            mrow = page_s.astype(I32) * db + row_i
            onehot = jnp.where(rows_lane == mrow, 1.0, 0.0).astype(BF16)
            got = jnp.dot(onehot, table, preferred_element_type=F32)
            k = slot - got[:, PAGE_SIZE:]
            off_s = jnp.sum(jnp.where(got[:, :PAGE_SIZE] <= k, 1.0, 0.0), axis=1, keepdims=True)
            pos = page_s * float(PAGE_SIZE) + off_s
            valid = slot[:, 0:1] < n_pool[row_i:row_i + 1, :]
            idx_ref[row_i] = jnp.where(valid, pos, 0.0).astype(I32)
        meta_ref[...] = jnp.broadcast_to(n_pool, (db, PAGE_SIZE)).astype(I32)


def _sample_select(page_table, qi_s, kvi_s16, pool_ki):
    db, n_pages = page_table.shape
    assert db & (db - 1) == 0, "row index arithmetic in the rank matrix assumes a power-of-two row count"
    assert n_pages == PAGE_SIZE, "page ends are laid out one page per lane"
    n_steps = n_pages // PAGES_PER_STEP

    def page_spec(r):
        return pl.BlockSpec(
            (1, 1, PAGE_SIZE, IDX_DIM),
            lambda b, p, pt: (0, pt[b, p * PAGES_PER_STEP + r], 0, 0))

    grid_spec = pltpu.PrefetchScalarGridSpec(
        num_scalar_prefetch=1,
        grid=(db, n_steps),
        in_specs=[
            pl.BlockSpec((1, IDX_HEADS, IDX_DIM), lambda b, p, pt: (b, 0, 0)),
            pl.BlockSpec((1, IDX_HEADS, KVI_WIDTH), lambda b, p, pt: (b, 0, 0)),
        ] + [page_spec(r) for r in range(PAGES_PER_STEP)],
        out_specs=[
            pl.BlockSpec((db, TOPK_MAX, 1), lambda b, p, pt: (0, 0, 0)),
            pl.BlockSpec((db, PAGE_SIZE), lambda b, p, pt: (0, 0)),
        ],
        scratch_shapes=[
            pltpu.VMEM((n_pages, db, PAGE_SIZE), F32),
            pltpu.VMEM((db, PAGE_SIZE), F32),
        ],
    )
    return pl.pallas_call(
        _sample_select_kernel,
        grid_spec=grid_spec,
        out_shape=[
            jax.ShapeDtypeStruct((db, TOPK_MAX, 1), I32),
            jax.ShapeDtypeStruct((db, PAGE_SIZE), I32),
        ],
        compiler_params=pltpu.CompilerParams(
            dimension_semantics=("arbitrary", "arbitrary"), vmem_limit_bytes=VMEM_LIMIT),
        name="sample_select",
    )(page_table, qi_s, kvi_s16, *([pool_ki] * PAGES_PER_STEP))


def _sample_attend_kernel(idx_ref, npool_ref, pt_ref, q_ref, kvs_ref, pk_ref, pv_ref, o_ref,
                          kbuf, vbuf, kflat, vflat, sem):
    b = pl.program_id(0)

    def row_copies(r):
        ix = idx_ref[b, r]
        page = pt_ref[b, lax.shift_right_logical(ix, 7)]
        off = ix & (PAGE_SIZE - 1)
        return (pltpu.make_async_copy(pk_ref.at[0, page, off], kbuf.at[r], sem.at[0]),
                pltpu.make_async_copy(pv_ref.at[0, page, off], vbuf.at[r], sem.at[1]))

    def start_body(r, carry):
        ck, cv = row_copies(r)
        ck.start()
        cv.start()
        return carry

    def wait_body(r, carry):
        ck, cv = row_copies(r)
        ck.wait()
        cv.wait()
        return carry

    lax.fori_loop(0, TOPK_MAX, start_body, 0, unroll=8)
    lax.fori_loop(0, TOPK_MAX, wait_body, 0, unroll=8)

    def head_copies(g):
        return (pltpu.make_async_copy(kbuf.at[:, g, :], kflat.at[g], sem.at[2]),
                pltpu.make_async_copy(vbuf.at[:, g, :], vflat.at[g], sem.at[3]))

    for g in range(N_KV_HEADS):
        for cp in head_copies(g):
            cp.start()
    for g in range(N_KV_HEADS):
        for cp in head_copies(g):
            cp.wait()

    n_pool = npool_ref[b]
    slot = lax.broadcasted_iota(I32, (1, TOPK_MAX), 1)
    slot_ok = slot < n_pool
    new_ok = n_pool < TOPK_MAX
    q_all = q_ref[0].astype(F32)
    for g in range(N_KV_HEADS):
        qg = q_all[g * Q_PER_KV:(g + 1) * Q_PER_KV, :].astype(BF16)
        ks = kflat[g].astype(BF16)
        vs = vflat[g].astype(BF16)
        k_new = kvs_ref[0, 0:1, KVI_K + g * HEAD_DIM:KVI_K + (g + 1) * HEAD_DIM].astype(BF16)
        v_new = kvs_ref[0, 0:1, KVI_V + g * HEAD_DIM:KVI_V + (g + 1) * HEAD_DIM].astype(BF16)
        s = jnp.where(slot_ok, _dot_nt(qg, ks) * ATTN_SCALE, -jnp.inf)
        s_new = jnp.sum(qg.astype(F32) * k_new.astype(F32), axis=-1, keepdims=True) * ATTN_SCALE
        s_new = jnp.where(new_ok, s_new, -jnp.inf)
        m = jnp.maximum(jnp.max(s, axis=-1, keepdims=True), s_new)
        p = jnp.exp(s - m)
        p_new = jnp.exp(s_new - m)
        denom = jnp.sum(p, axis=-1, keepdims=True) + p_new
        pn = (p / denom).astype(BF16)
        pn_new = (p_new / denom).astype(BF16).astype(F32)
        o = jnp.dot(pn, vs, preferred_element_type=F32) + pn_new * v_new.astype(F32)
        o_ref[0, g * Q_PER_KV:(g + 1) * Q_PER_KV, :] = o


def _sample_attend(idx, n_pool, page_table, q_s, kvi_s16, pool_k, pool_v):
    db = page_table.shape[0]
    grid_spec = pltpu.PrefetchScalarGridSpec(
        num_scalar_prefetch=3,
        grid=(db,),
        in_specs=[
            pl.BlockSpec((1, N_HEADS, HEAD_DIM), lambda b, *_: (b, 0, 0)),
            pl.BlockSpec((1, IDX_HEADS, KVI_WIDTH), lambda b, *_: (b, 0, 0)),
            pl.BlockSpec(memory_space=pl.ANY),
            pl.BlockSpec(memory_space=pl.ANY),
        ],
        out_specs=pl.BlockSpec((1, N_HEADS, HEAD_DIM), lambda b, *_: (b, 0, 0)),
        scratch_shapes=[
            pltpu.VMEM((TOPK_MAX, N_KV_HEADS, HEAD_DIM), F32),
            pltpu.VMEM((TOPK_MAX, N_KV_HEADS, HEAD_DIM), F32),
            pltpu.VMEM((N_KV_HEADS, TOPK_MAX, HEAD_DIM), F32),
            pltpu.VMEM((N_KV_HEADS, TOPK_MAX, HEAD_DIM), F32),
            pltpu.SemaphoreType.DMA((4,)),
        ],
    )
    return pl.pallas_call(
        _sample_attend_kernel,
        grid_spec=grid_spec,
        out_shape=jax.ShapeDtypeStruct((db, N_HEADS, HEAD_DIM), F32),
        compiler_params=pltpu.CompilerParams(dimension_semantics=("arbitrary",)),
        name="sample_attend",
    )(idx, n_pool, page_table, q_s, kvi_s16, pool_k, pool_v)


def _merge_kernel(chunked, oa_ref, za_ref, u_ref, v_ref, zb_ref, ga_ref, gb_ref, x_ref,
                  wpa_ref, wpb_ref, wout_ref, lng_ref, lnb_ref, ws_ref, bs_ref, gf_ref, y_ref, *rest):
    vn_ref, ob_ref = (None, rest[0]) if chunked else rest
    tm = x_ref.shape[0]
    u = _gelu(u_ref[...])
    v = _gelu(v_ref[...])
    mu = jnp.mean(v, axis=-1, keepdims=True)
    vc = v - mu
    vn = (vc * lax.rsqrt(jnp.mean(vc * vc, axis=-1, keepdims=True) + LN_EPS)) * lng_ref[...] + lnb_ref[...]
    if vn_ref is not None:
        vn_ref[...] = vn

    if chunked:
        vnb = vn.astype(BF16)
        ri = lax.broadcasted_iota(I32, (CHUNK, CHUNK), 0)
        ci = lax.broadcasted_iota(I32, (CHUNK, CHUNK), 1)
        for g in range(GMLP_GROUPS):
            wm = jnp.where(ri >= ci, ws_ref[g], 0.0).astype(BF16)
            lo = g * GMLP_GROUP_DIM
            for c in range(tm // CHUNK):
                mixed = jnp.dot(wm, vnb[c * CHUNK:(c + 1) * CHUNK, lo:lo + GMLP_GROUP_DIM],
                                preferred_element_type=F32) + bs_ref[:, g:g + 1]
                ob_ref[c * CHUNK:(c + 1) * CHUNK, lo:lo + GMLP_GROUP_DIM] = (
                    u[c * CHUNK:(c + 1) * CHUNK, lo:lo + GMLP_GROUP_DIM] * mixed)
    else:
        ob_ref[...] = u * (ws_ref[...] * vn + bs_ref[...])

    ha = (oa_ref[...] * _silu(za_ref[...])).astype(BF16)
    hb = (ob_ref[...] * _silu(zb_ref[...])).astype(BF16)
    ya = jnp.dot(ha, wpa_ref[...], preferred_element_type=F32)
    yb = jnp.dot(hb, wpb_ref[...], preferred_element_type=F32)
    mix = jax.nn.sigmoid(ga_ref[...]) * ya + jax.nn.sigmoid(gb_ref[...]) * yb
    out = x_ref[...] + jnp.dot(mix.astype(BF16), wout_ref[...], preferred_element_type=F32)
    ms = jnp.mean(out * out, axis=-1, keepdims=True)
    y_ref[...] = (out * lax.rsqrt(ms + RMS_EPS)) * gf_ref[...]


def _merge(chunked, o_a, gate, x2d, wpa, wpb, wout, lng, lnb, ws, bs, gf, tm):
    rows = x2d.shape[0]
    gw = GMLP_WIDTH

    def const(shape):
        return pl.BlockSpec(shape, lambda i: (0,) * len(shape), pipeline_mode=pl.Buffered(1))

    return pl.pallas_call(
        functools.partial(_merge_kernel, chunked),
        grid=(rows // tm,),
        in_specs=[
            pl.BlockSpec((tm, ATTN_WIDTH), lambda i: (i, 0)),
            pl.BlockSpec((tm, gw), lambda i: (i, 0)),
            pl.BlockSpec((tm, gw), lambda i: (i, 1)),
            pl.BlockSpec((tm, gw), lambda i: (i, 2)),
            pl.BlockSpec((tm, gw), lambda i: (i, 3)),
            pl.BlockSpec((tm, D_MODEL), lambda i: (i, 2)),
            pl.BlockSpec((tm, D_MODEL), lambda i: (i, 3)),
            pl.BlockSpec((tm, D_MODEL), lambda i: (i, 0)),
            const(wpa.shape), const(wpb.shape), const(wout.shape),
            const(lng.shape), const(lnb.shape), const(ws.shape), const(bs.shape), const(gf.shape),
        ],
        out_specs=[pl.BlockSpec((tm, D_MODEL), lambda i: (i, 0))]
        + ([] if chunked else [pl.BlockSpec((tm, gw), lambda i: (i, 0))]),
        out_shape=[jax.ShapeDtypeStruct((rows, D_MODEL), F32)]
        + ([] if chunked else [jax.ShapeDtypeStruct((rows, gw), F32)]),
        scratch_shapes=[pltpu.VMEM((tm, gw), F32)],
        compiler_params=pltpu.CompilerParams(
            dimension_semantics=("arbitrary",), vmem_limit_bytes=VMEM_LIMIT),
        name="merge_prompt" if chunked else "merge_sample",
    )(o_a, gate, gate, gate, gate, gate, gate, x2d, wpa, wpb, wout, lng, lnb, ws, bs, gf)


def kernel(x_prompt, x_sample, cache_k, cache_v, cache_k_idx, page_table, norm_in_g, w_in,
           w_proj_a, w_proj_b, w_out, ln_g, ln_b, w_spatial, b_spatial, norm_f_g):
    depth = w_in.shape[0]
    assert depth == 1, "single trunk layer"
    batch, seq, _ = x_prompt.shape
    db, dseq, _ = x_sample.shape
    assert dseq == 1
    l = 0
    wpa = w_proj_a[l].astype(BF16)
    wpb = w_proj_b[l].astype(BF16)
    wout = w_out[l].astype(BF16)
    g_in = norm_in_g[l].reshape(1, D_MODEL)
    lng = ln_g[l].reshape(1, GMLP_WIDTH)
    lnb = ln_b[l].reshape(1, GMLP_WIDTH)
    gf = norm_f_g.reshape(1, D_MODEL)

    xp = x_prompt.reshape(batch * seq, D_MODEL)
    wt = w_in[l].T
    xn_p = _rmsnorm_bf16(xp, g_in, 1024)
    h_t = _inproj(xn_p, wt, COL_Q, COL_WI, 512, 1024, BF16, True)
    kv_p, ki_p2, wi_t, k3_p, v3_p = _inproj_kvi(xn_p, wt, 1024)
    gate_p = _inproj(xn_p, wt, GATE_COL0, GATE_WIDTH, 1024, 1024, F32, False)
    oa_p = _prompt_attn(h_t, wi_t, kv_p, ki_p2, batch, seq)
    bs_t = b_spatial[l].T
    (y_p,) = _merge(True, oa_p, gate_p, xp, wpa, wpb, wout, lng, lnb, w_spatial[l], bs_t, gf, 256)

    xs = x_sample.reshape(db, D_MODEL)
    xn_s = _rmsnorm_bf16(xs, g_in, db)
    h_s = _inproj(xn_s, wt, COL_Q, COL_WI, 512, db, F32, False)
    kv_s, ki_s2, wi_t_s, _, _ = _inproj_kvi(xn_s, wt, db)
    gate_s = _inproj(xn_s, wt, GATE_COL0, GATE_WIDTH, 1024, db, F32, False)
    kvi_s = jnp.concatenate(
        [kv_s, ki_s2, wi_t_s.T, jnp.zeros((db, KVI_WIDTH - KVI_WI - IDX_HEADS), F32)], axis=1)
    q_s = h_s[:, COL_Q:COL_K].astype(BF16).reshape(db, N_HEADS, HEAD_DIM)
    qi_s = h_s[:, COL_QI:COL_WI].astype(BF16).reshape(db, IDX_HEADS, IDX_DIM)
    wi_col = kvi_s[:, KVI_WI:KVI_WI + IDX_HEADS].reshape(db, IDX_HEADS, 1)
    kvi_s16 = jnp.broadcast_to(kvi_s[:, None, :], (db, IDX_HEADS, KVI_WIDTH))
    kvi_s16 = lax.dynamic_update_slice(kvi_s16, wi_col, (0, 0, KVI_WI))
    idx, meta = _sample_select(page_table, qi_s, kvi_s16, cache_k_idx)
    oa_s = _sample_attend(idx.reshape(db, TOPK_MAX), meta[:, 0], page_table, q_s, kvi_s16,
                          cache_k, cache_v).reshape(db, ATTN_WIDTH)
    ws0 = jnp.repeat(w_spatial[l][:, 0, 0], GMLP_GROUP_DIM).reshape(1, GMLP_WIDTH)
    bs0 = jnp.repeat(b_spatial[l][:, 0], GMLP_GROUP_DIM).reshape(1, GMLP_WIDTH)
    y_s, vn_s = _merge(False, oa_s, gate_s, xs, wpa, wpb, wout, lng, lnb, ws0, bs0, gf, db)

    def kv_out(kvi, lead):
        k = kvi[:, KVI_K:KVI_K + KV_WIDTH].reshape((1,) + lead + (N_KV_HEADS, HEAD_DIM))
        v = kvi[:, KVI_V:KVI_V + KV_WIDTH].reshape((1,) + lead + (N_KV_HEADS, HEAD_DIM))
        ki = kvi[:, KVI_KI:KVI_KI + IDX_DIM].reshape((1,) + lead + (IDX_DIM,))
        return k, v, ki

    k_p = k3_p.reshape(1, batch, seq, N_KV_HEADS, HEAD_DIM)
    v_p = v3_p.reshape(1, batch, seq, N_KV_HEADS, HEAD_DIM)
    ki_p = ki_p2.reshape(1, batch, seq, IDX_DIM)
    k_s, v_s, ki_s = kv_out(kvi_s, (db, dseq))
    return (y_p.reshape(batch, seq, D_MODEL), y_s.reshape(db, dseq, D_MODEL),
            k_p, v_p, ki_p, k_s, v_s, ki_s, vn_s.reshape(1, db, dseq, GMLP_WIDTH))
```

```python
import functools

import jax
import jax.numpy as jnp
import numpy as np
from jax import lax
from jax.experimental import pallas as pl
from jax.experimental.pallas import tpu as pltpu

F32 = jnp.float32
BF16 = jnp.bfloat16
I32 = jnp.int32

D_MODEL = 2048
N_HEADS = 8
N_KV_HEADS = 2
HEAD_DIM = 128
Q_PER_KV = N_HEADS // N_KV_HEADS
ATTN_WIDTH = N_HEADS * HEAD_DIM
KV_WIDTH = N_KV_HEADS * HEAD_DIM
IDX_HEADS = 16
IDX_DIM = 128
TOPK_MAX = 256
PAGE_SIZE = 128
GMLP_WIDTH = 1024
GMLP_GROUPS = 8
GMLP_GROUP_DIM = GMLP_WIDTH // GMLP_GROUPS
CHUNK = 128
RMS_EPS = 1e-6
LN_EPS = 1e-5

SUBLANES = 8
COL_Q = 0
COL_K = COL_Q + ATTN_WIDTH
COL_V = COL_K + KV_WIDTH
COL_QI = COL_V + KV_WIDTH
COL_WI = COL_QI + IDX_HEADS * IDX_DIM
COL_KI = COL_WI + IDX_HEADS
GATE_COL0 = COL_KI + IDX_DIM
GATE_WIDTH = 4 * GMLP_WIDTH + 2 * D_MODEL
KVI_WIDTH = 768
KVI_K, KVI_V, KVI_KI, KVI_WI = 0, 256, 512, 640

Q_BLOCK = 128
KEY_CHUNK = 512
COUNT_ROWS = 64
ONES_ROWS = 16
SIGN_EXP_BITS = 9
SEARCH_ALWAYS_BITS = 16
SEARCH_GROUP_BITS = 4
NEG_SENTINEL = float(np.finfo(np.float32).min)
NEG_ABOVE_SENTINEL = float(np.nextafter(np.float32(NEG_SENTINEL), np.float32(0.0)))
INT_MIN = -(2 ** 31)
IDX_W_SCALE = float(IDX_HEADS ** -0.5 * IDX_DIM ** -0.5)
ATTN_SCALE = float(HEAD_DIM ** -0.5)
ATTN_SCALE_LOG2E = float(HEAD_DIM ** -0.5 * np.log2(np.e))
VMEM_LIMIT = 56 * 1024 * 1024


def _dot_nt(a, b):
    return lax.dot_general(a, b, (((1,), (1,)), ((), ())), preferred_element_type=F32)


def _ordered_bits_to_f32(u):
    signed_pow, zero_exp = _sign_exp_parts(u)
    return _with_mantissa(u, signed_pow, zero_exp)


def _sign_exp_parts(u):
    bits = jnp.where(u < 0, u ^ INT_MIN, ~u)
    exp = lax.shift_right_logical(bits, 23) & 0xFF
    e1 = exp - 1
    p = jnp.full(u.shape, float(2.0 ** -126), F32)
    for b in range(7):
        p = jnp.where(((e1 >> b) & 1) == 1, p * float(2.0 ** (2 ** b)), p)
    top = ((e1 >> 7) & 1) == 1
    p = jnp.where(top, p * float(2.0 ** 64), p)
    p = jnp.where(top, p * float(2.0 ** 64), p)
    return jnp.where(bits < 0, -p, p), exp == 0


def _with_mantissa(u, signed_pow, zero_exp):
    low = jnp.where(u < 0, u, ~u) & 0x7FFFFF
    mant = 1.0 + low.astype(F32) * float(2.0 ** -23)
    return jnp.where(zero_exp, 0.0, signed_pow * mant)


def _gelu(x):
    return 0.5 * x * (1.0 + lax.erf(x * float(np.sqrt(0.5))))


def _silu(x):
    return x * jax.nn.sigmoid(x)


def _inproj_kernel(feature_major, xn_ref, w_ref, o_ref, wb_ref):
    @pl.when(pl.program_id(1) == 0)
    def _():
        wb_ref[...] = w_ref[...].astype(BF16)

    if feature_major:
        o_ref[...] = _dot_nt(wb_ref[...], xn_ref[...]).astype(o_ref.dtype)
    else:
        o_ref[...] = _dot_nt(xn_ref[...], wb_ref[...]).astype(o_ref.dtype)


def _inproj(xn, wt, row_start, n_rows, tn, tm, out_dtype, feature_major):
    rows = xn.shape[0]
    assert n_rows % tn == 0 and rows % tm == 0 and row_start % SUBLANES == 0
    if row_start % tn == 0:
        j0 = row_start // tn
        w_spec = pl.BlockSpec((tn, D_MODEL), lambda j, i: (j0 + j, 0))
    else:
        w_spec = pl.BlockSpec((pl.Element(tn), pl.Element(D_MODEL)), lambda j, i: ((row_start // SUBLANES + j * (tn // SUBLANES)) * SUBLANES, 0))
    if feature_major:
        out_spec = pl.BlockSpec((tn, tm), lambda j, i: (j, i))
        out_shape = (n_rows, rows)
    else:
        out_spec = pl.BlockSpec((tm, tn), lambda j, i: (i, j))
        out_shape = (rows, n_rows)
    return pl.pallas_call(
        functools.partial(_inproj_kernel, feature_major),
        grid=(n_rows // tn, rows // tm),
        in_specs=[pl.BlockSpec((tm, D_MODEL), lambda j, i: (i, 0)), w_spec],
        out_specs=out_spec,
        out_shape=jax.ShapeDtypeStruct(out_shape, out_dtype),
        scratch_shapes=[pltpu.VMEM((tn, D_MODEL), BF16)],
        compiler_params=pltpu.CompilerParams(
            dimension_semantics=("arbitrary", "arbitrary"), vmem_limit_bytes=VMEM_LIMIT),
        name="inproj_t" if feature_major else "inproj",
    )(xn, wt)


def _inproj_kvi_kernel(x_ref, g_ref, wkv_ref, wwi_ref, wki_ref, xn_ref, kv_ref, ki_ref, wit_ref, k3_ref, v3_ref,
                       wkvb_ref, wwib_ref, wkib_ref):
    @pl.when(pl.program_id(0) == 0)
    def _():
        wkvb_ref[...] = wkv_ref[...].astype(BF16)
        wwib_ref[...] = wwi_ref[...].astype(BF16)
        wkib_ref[...] = wki_ref[...].astype(BF16)

    x = x_ref[...]
    ms = jnp.mean(x * x, axis=-1, keepdims=True)
    xn = ((x * lax.rsqrt(ms + RMS_EPS)) * g_ref[...]).astype(BF16)
    xn_ref[...] = xn
    kv = _dot_nt(xn, wkvb_ref[...])
    kv_ref[...] = kv
    for g in range(N_KV_HEADS):
        k3_ref[:, g, :] = kv[:, g * HEAD_DIM:(g + 1) * HEAD_DIM]
        v3_ref[:, g, :] = kv[:, KV_WIDTH + g * HEAD_DIM:KV_WIDTH + (g + 1) * HEAD_DIM]
    ki_ref[...] = _dot_nt(xn, wkib_ref[...])
    wit_ref[...] = _dot_nt(wwib_ref[...], xn)


def _inproj_kvi(x2d, g, wt, tm):
    rows = x2d.shape[0]
    assert rows % tm == 0 and COL_K % (2 * KV_WIDTH) == 0 and COL_WI % IDX_HEADS == 0
    return pl.pallas_call(
        _inproj_kvi_kernel,
        grid=(rows // tm,),
        in_specs=[
            pl.BlockSpec((tm, D_MODEL), lambda i: (i, 0)),
            pl.BlockSpec((1, D_MODEL), lambda i: (0, 0)),
            pl.BlockSpec((2 * KV_WIDTH, D_MODEL), lambda i: (COL_K // (2 * KV_WIDTH), 0)),
            pl.BlockSpec((IDX_HEADS, D_MODEL), lambda i: (COL_WI // IDX_HEADS, 0)),
            pl.BlockSpec((pl.Element(IDX_DIM), pl.Element(D_MODEL)), lambda i: (COL_KI, 0)),
        ],
        out_specs=[
            pl.BlockSpec((tm, D_MODEL), lambda i: (i, 0)),
            pl.BlockSpec((tm, 2 * KV_WIDTH), lambda i: (i, 0)),
            pl.BlockSpec((tm, IDX_DIM), lambda i: (i, 0)),
            pl.BlockSpec((IDX_HEADS, tm), lambda i: (0, i)),
            pl.BlockSpec((tm, N_KV_HEADS, HEAD_DIM), lambda i: (i, 0, 0)),
            pl.BlockSpec((tm, N_KV_HEADS, HEAD_DIM), lambda i: (i, 0, 0)),
        ],
        out_shape=[
            jax.ShapeDtypeStruct((rows, D_MODEL), BF16),
            jax.ShapeDtypeStruct((rows, 2 * KV_WIDTH), F32),
            jax.ShapeDtypeStruct((rows, IDX_DIM), F32),
            jax.ShapeDtypeStruct((IDX_HEADS, rows), F32),
            jax.ShapeDtypeStruct((rows, N_KV_HEADS, HEAD_DIM), F32),
            jax.ShapeDtypeStruct((rows, N_KV_HEADS, HEAD_DIM), F32),
        ],
        scratch_shapes=[
            pltpu.VMEM((2 * KV_WIDTH, D_MODEL), BF16),
            pltpu.VMEM((IDX_HEADS, D_MODEL), BF16),
            pltpu.VMEM((IDX_DIM, D_MODEL), BF16),
        ],
        compiler_params=pltpu.CompilerParams(
            dimension_semantics=("arbitrary",), vmem_limit_bytes=VMEM_LIMIT),
        name="inproj_kvi",
    )(x2d, g, wt, wt, wt)


def _prompt_attn_kernel(qt_ref, qit0_ref, qit1_ref, qit2_ref, qit3_ref, vt_ref, wit_ref, kvf_ref, kif_ref,
                        o_ref, kb_ref, kib_ref, vtc_ref, sc_ref, sa_ref, sb_ref, acc_ref, ml_ref, u_ref, cnt_ref):
    i = pl.program_id(1)
    seq = kvf_ref.shape[0]
    qit_refs = (qit0_ref, qit1_ref, qit2_ref, qit3_ref)
    heads_per_ref = IDX_HEADS // len(qit_refs)

    @pl.when(i == 0)
    def _():
        kb_ref[...] = kvf_ref[:, 0:KV_WIDTH].astype(BF16)
        kib_ref[...] = kif_ref[...].astype(BF16)
        for c in range(seq // KEY_CHUNK):
            for g in range(N_KV_HEADS):
                vtc_ref[c, g, :HEAD_DIM, :] = vt_ref[g * HEAD_DIM:(g + 1) * HEAD_DIM,
                                                     c * KEY_CHUNK:(c + 1) * KEY_CHUNK]
                vtc_ref[c, g, HEAD_DIM:, :] = jnp.ones((ONES_ROWS, KEY_CHUNK), BF16)

    n_chunks = (i * Q_BLOCK) // KEY_CHUNK + 1
    q_pos = i * Q_BLOCK + lax.broadcasted_iota(I32, (1, Q_BLOCK), 1)
    key0 = lax.broadcasted_iota(I32, (KEY_CHUNK, 1), 0)

    w_t = wit_ref[...] * IDX_W_SCALE
    qit_all = jnp.concatenate(
        [qit_refs[h // heads_per_ref][(h % heads_per_ref) * IDX_DIM:(h % heads_per_ref + 1) * IDX_DIM, :]
         for h in range(IDX_HEADS)], axis=1)

    def score_chunk(c):
        k0 = pl.multiple_of(c * KEY_CHUNK, KEY_CHUNK)
        res = jnp.dot(kib_ref[pl.ds(k0, KEY_CHUNK), :], qit_all, preferred_element_type=F32)
        acc = jnp.zeros((KEY_CHUNK, Q_BLOCK), F32)
        for h in range(IDX_HEADS):
            acc = acc + w_t[h:h + 1, :] * jnp.maximum(res[:, h * Q_BLOCK:(h + 1) * Q_BLOCK], 0.0)
        sc_ref[c] = jnp.where(k0 + key0 <= q_pos, acc, NEG_SENTINEL)

    def score_pair(t, carry):
        score_chunk(2 * t)
        score_chunk(2 * t + 1)
        return carry

    lax.fori_loop(0, n_chunks // 2, score_pair, 0)

    @pl.when(n_chunks % 2 == 1)
    def _():
        score_chunk(n_chunks - 1)

    def count(pred):
        def body(c, cnt):
            ones = jnp.where(pred(sc_ref[c], c), 1.0, 0.0)
            return cnt + ones.reshape(KEY_CHUNK // COUNT_ROWS, COUNT_ROWS, Q_BLOCK).sum(axis=0)
        part = lax.fori_loop(0, n_chunks, body, jnp.zeros((COUNT_ROWS, Q_BLOCK), F32))
        return jnp.sum(part, axis=0, keepdims=True)

    def search(j0, n, u, cnt_u, decode):
        def body(j, carry):
            u, cnt_u = carry
            cand = u | lax.shift_left(jnp.int32(1), 31 - j)
            thr_c = decode(cand)
            cnt = count(lambda blk, c: blk >= thr_c)
            ok = cnt >= float(TOPK_MAX)
            return jnp.where(ok, cand, u), jnp.where(ok, cnt, cnt_u)
        return lax.fori_loop(j0, j0 + n, body, (u, cnt_u))

    u = jnp.zeros((1, Q_BLOCK), I32)
    cnt_u = (u + n_chunks * KEY_CHUNK).astype(F32)
    u, cnt_u = search(0, SIGN_EXP_BITS, u, cnt_u, _ordered_bits_to_f32)
    signed_pow, zero_exp = _sign_exp_parts(u)

    def decode_mantissa(cand):
        return _with_mantissa(cand, signed_pow, zero_exp)

    u, cnt_u = search(SIGN_EXP_BITS, SEARCH_ALWAYS_BITS - SIGN_EXP_BITS, u, cnt_u, decode_mantissa)
    u_ref[...] = u
    cnt_ref[...] = cnt_u
    for j0 in range(SEARCH_ALWAYS_BITS, 32, SEARCH_GROUP_BITS):
        @pl.when(jnp.max(jnp.where(cnt_ref[...] == float(TOPK_MAX), 0.0, 1.0)) > 0.0)
        def _():
            u_g, cnt_g = search(j0, SEARCH_GROUP_BITS, u_ref[...], cnt_ref[...], decode_mantissa)
            u_ref[...] = u_g
            cnt_ref[...] = cnt_g
    u = u_ref[...]
    n_ge = cnt_ref[...]
    thr = decode_mantissa(u)
    thr_adm = jnp.maximum(thr, NEG_ABOVE_SENTINEL)
    excess = jnp.max(jnp.where((n_ge > float(TOPK_MAX)) & (thr > NEG_SENTINEL), 1.0, 0.0)) > 0.0

    @pl.when(excess)
    def _():
        need = float(TOPK_MAX) - count(lambda blk, c: blk > thr)

        def jbit_body(j, jl):
            cand = jl | lax.shift_left(jnp.int32(1), 12 - j)
            cnt = count(lambda blk, c: (blk == thr) & (c * KEY_CHUNK + key0 < cand))
            return jnp.where(cnt <= need, cand, jl)
        jlim = lax.fori_loop(0, 13, jbit_body, jnp.zeros((1, Q_BLOCK), I32))

        def demote_body(c, carry):
            blk = sc_ref[c]
            dropped = (blk == thr) & (thr > NEG_SENTINEL) & (c * KEY_CHUNK + key0 >= jlim)
            sc_ref[c] = jnp.where(dropped, NEG_SENTINEL, blk)
            return carry
        lax.fori_loop(0, n_chunks, demote_body, 0)

    acc_ref[...] = jnp.zeros(acc_ref.shape, F32)
    cols = Q_PER_KV * Q_BLOCK
    qt_groups = [
        jnp.concatenate([qt_ref[(g * Q_PER_KV + h) * HEAD_DIM:(g * Q_PER_KV + h + 1) * HEAD_DIM, :]
                         for h in range(Q_PER_KV)], axis=1)
        for g in range(N_KV_HEADS)
    ]

    last_chunk = seq // KEY_CHUNK - 1

    def qk_scores(c, dst_ref):
        k0 = pl.multiple_of(jnp.minimum(c, last_chunk) * KEY_CHUNK, KEY_CHUNK)
        for g in range(N_KV_HEADS):
            kc = kb_ref[pl.ds(k0, KEY_CHUNK), g * HEAD_DIM:(g + 1) * HEAD_DIM]
            dst_ref[g] = jnp.dot(kc, qt_groups[g], preferred_element_type=F32)

    def softmax_pv(c, src_ref):
        bias1 = jnp.where(sc_ref[c] >= thr_adm, 0.0, -jnp.inf)
        bias = jnp.concatenate([bias1] * Q_PER_KV, axis=1)
        for g in range(N_KV_HEADS):
            m_old = ml_ref[2 * g:2 * g + 1, :]
            l_old = ml_ref[2 * g + 1:2 * g + 2, :]
            s = src_ref[g] * ATTN_SCALE_LOG2E + bias
            m_new = jnp.maximum(m_old, jnp.max(s, axis=0, keepdims=True))
            m_safe = jnp.where(m_new == -jnp.inf, 0.0, m_new)
            alpha = jnp.exp2(m_old - m_safe)
            p = jnp.exp2(s - m_safe).astype(BF16)
            pv = jnp.dot(vtc_ref[c, g], p, preferred_element_type=F32)
            acc_ref[g] = alpha * acc_ref[g] + pv[:HEAD_DIM]
            ml_ref[2 * g:2 * g + 1, :] = m_new
            ml_ref[2 * g + 1:2 * g + 2, :] = alpha * l_old + pv[HEAD_DIM:HEAD_DIM + 1]

    def pair_body(t, carry):
        qk_scores(2 * t + 1, sb_ref)
        softmax_pv(2 * t, sa_ref)
        qk_scores(2 * t + 2, sa_ref)
        softmax_pv(2 * t + 1, sb_ref)
        return carry

    for g in range(N_KV_HEADS):
        ml_ref[2 * g:2 * g + 1, :] = jnp.full((1, cols), -jnp.inf, F32)
        ml_ref[2 * g + 1:2 * g + 2, :] = jnp.zeros((1, cols), F32)
    qk_scores(0, sa_ref)
    lax.fori_loop(0, n_chunks // 2, pair_body, 0)

    @pl.when(n_chunks % 2 == 1)
    def _():
        softmax_pv(n_chunks - 1, sa_ref)

    for g in range(N_KV_HEADS):
        o = acc_ref[g] / ml_ref[2 * g + 1:2 * g + 2, :]
        for h in range(Q_PER_KV):
            hd = (g * Q_PER_KV + h) * HEAD_DIM
            o_ref[:, hd:hd + HEAD_DIM] = o[:, h * Q_BLOCK:(h + 1) * Q_BLOCK].T


def _prompt_attn(h_t, wi_t, kv, ki, batch, seq):
    nb = seq // Q_BLOCK
    cols = Q_PER_KV * Q_BLOCK
    qi_rows = IDX_HEADS * IDX_DIM // 4
    qi_b0 = COL_QI // qi_rows

    def qi_spec(t):
        return pl.BlockSpec((qi_rows, Q_BLOCK), lambda b, i: (qi_b0 + t, b * nb + i))

    return pl.pallas_call(
        _prompt_attn_kernel,
        grid=(batch, nb),
        in_specs=[
            pl.BlockSpec((ATTN_WIDTH, Q_BLOCK), lambda b, i: (COL_Q // ATTN_WIDTH, b * nb + i)),
            qi_spec(0), qi_spec(1), qi_spec(2), qi_spec(3),
            pl.BlockSpec((KV_WIDTH, seq), lambda b, i: (COL_V // KV_WIDTH, b)),
            pl.BlockSpec((IDX_HEADS, Q_BLOCK), lambda b, i: (0, b * nb + i)),
            pl.BlockSpec((seq, 2 * KV_WIDTH), lambda b, i: (b, 0)),
            pl.BlockSpec((seq, IDX_DIM), lambda b, i: (b, 0)),
        ],
        out_specs=pl.BlockSpec((Q_BLOCK, ATTN_WIDTH), lambda b, i: (b * nb + i, 0)),
        out_shape=jax.ShapeDtypeStruct((batch * seq, ATTN_WIDTH), F32),
        scratch_shapes=[
            pltpu.VMEM((seq, KV_WIDTH), BF16),
            pltpu.VMEM((seq, IDX_DIM), BF16),
            pltpu.VMEM((seq // KEY_CHUNK, N_KV_HEADS, HEAD_DIM + ONES_ROWS, KEY_CHUNK), BF16),
            pltpu.VMEM((seq // KEY_CHUNK, KEY_CHUNK, Q_BLOCK), F32),
            pltpu.VMEM((N_KV_HEADS, KEY_CHUNK, cols), F32),
            pltpu.VMEM((N_KV_HEADS, KEY_CHUNK, cols), F32),
            pltpu.VMEM((N_KV_HEADS, HEAD_DIM, cols), F32),
            pltpu.VMEM((2 * N_KV_HEADS, cols), F32),
            pltpu.VMEM((1, Q_BLOCK), I32),
            pltpu.VMEM((1, Q_BLOCK), F32),
        ],
        compiler_params=pltpu.CompilerParams(
            dimension_semantics=("arbitrary", "arbitrary"), vmem_limit_bytes=VMEM_LIMIT),
        name="prompt_attn",
    )(h_t, h_t, h_t, h_t, h_t, h_t, wi_t, kv, ki)


PAGES_PER_STEP = 32


def _sample_select_kernel(pt_ref, qi_ref, kvs_ref, *refs):
    page_refs = refs[:PAGES_PER_STEP]
    idx_ref, meta_ref, sc_ref, snew_ref = refs[PAGES_PER_STEP:]
    b = pl.program_id(0)
    p = pl.program_id(1)
    nb = pl.num_programs(0)
    n_steps = pl.num_programs(1)
    n_pages, db, _ = sc_ref.shape

    qi = qi_ref[0]
    w_col = kvs_ref[0, :, KVI_WI:KVI_WI + 1] * IDX_W_SCALE

    def key_scores(keys_bf16):
        s = _dot_nt(qi, keys_bf16)
        return jnp.sum(w_col * jnp.maximum(s, 0.0), axis=0, keepdims=True)

    keys = jnp.concatenate([page_refs[r][0, 0] for r in range(PAGES_PER_STEP)], axis=0).astype(BF16)
    row = key_scores(keys)
    for r in range(PAGES_PER_STEP):
        sc_ref[p * PAGES_PER_STEP + r, pl.ds(b, 1), :] = row[:, r * PAGE_SIZE:(r + 1) * PAGE_SIZE]

    @pl.when(p == n_steps - 1)
    def _():
        ki_new = kvs_ref[0, :, KVI_KI:KVI_KI + IDX_DIM].astype(BF16)
        s_new = key_scores(ki_new)[:, 0:1]
        snew_ref[pl.ds(b, 1), :] = jnp.broadcast_to(s_new, (1, PAGE_SIZE))

    @pl.when((b == nb - 1) & (p == n_steps - 1))
    def _():
        sc = sc_ref[...]
        s_new = snew_ref[:, 0:1]

        def total(x):
            return jnp.sum(jnp.sum(x, axis=0), axis=1, keepdims=True)

        def bit_body(j, u):
            cand = u | lax.shift_left(jnp.int32(1), 31 - j)
            thr = _ordered_bits_to_f32(cand)
            cnt = total(jnp.where(sc >= thr[None], 1.0, 0.0)) + jnp.where(s_new >= thr, 1.0, 0.0)
            return jnp.where(cnt >= float(TOPK_MAX), cand, u)

        u = lax.fori_loop(0, 32, bit_body, jnp.zeros((db, 1), I32))
        thr = _ordered_bits_to_f32(u)
        n_gt = total(jnp.where(sc > thr[None], 1.0, 0.0)) + jnp.where(s_new > thr, 1.0, 0.0)
        need = float(TOPK_MAX) - n_gt
        flat = (lax.broadcasted_iota(I32, sc.shape, 0) * PAGE_SIZE
                + lax.broadcasted_iota(I32, sc.shape, 2))
        tie = sc == thr[None]

        def jbit_body(j, jl):
            cand = jl | lax.shift_left(jnp.int32(1), 14 - j)
            cnt = total(jnp.where(tie & (flat < cand[None]), 1.0, 0.0))
            return jnp.where(cnt <= need, cand, jl)

        jlim = lax.fori_loop(0, 15, jbit_body, jnp.zeros((db, 1), I32))
        sel = (sc > thr[None]) | (tie & (flat < jlim[None]))
        self32 = jnp.where(sel, 1.0, 0.0)
        n_pool = total(self32)

        n_rows = n_pages * db
        selb = self32.astype(BF16).reshape(n_rows, PAGE_SIZE)
        ri = lax.broadcasted_iota(I32, (PAGE_SIZE, PAGE_SIZE), 0)
        ci = lax.broadcasted_iota(I32, (PAGE_SIZE, PAGE_SIZE), 1)
        incl = jnp.dot(selb, jnp.where(ri <= ci, 1.0, 0.0).astype(BF16), preferred_element_type=F32)
        tot = jnp.broadcast_to(incl[:, PAGE_SIZE - 1:PAGE_SIZE], (n_rows, PAGE_SIZE)).astype(BF16)
        rr = lax.broadcasted_iota(I32, (n_rows, n_rows), 0)
        cc = lax.broadcasted_iota(I32, (n_rows, n_rows), 1)
        earlier_page_same_row = ((rr & (db - 1)) == (cc & (db - 1))) & (cc < (rr & -db))
        offs = jnp.dot(jnp.where(earlier_page_same_row, 1.0, 0.0).astype(BF16), tot,
                       preferred_element_type=F32)
        ends_rows = offs + tot.astype(F32)
        r_i = lax.broadcasted_iota(I32, (n_rows, PAGE_SIZE), 0)
        l_i = lax.broadcasted_iota(I32, (n_rows, PAGE_SIZE), 1)
        log2_db = db.bit_length() - 1
        diag = jnp.where(l_i == lax.shift_right_logical(r_i, log2_db), ends_rows, 0.0).astype(BF16)
        pick = jnp.where((lax.broadcasted_iota(I32, (db, n_rows), 1) & (db - 1))
                         == lax.broadcasted_iota(I32, (db, n_rows), 0), 1.0, 0.0).astype(BF16)
        ends = jnp.dot(pick, diag, preferred_element_type=F32)

        table = jnp.concatenate([incl.astype(BF16), offs.astype(BF16)], axis=1)
        slot = lax.broadcasted_iota(I32, (TOPK_MAX, PAGE_SIZE), 0).astype(F32)
        rows_lane = lax.broadcasted_iota(I32, (TOPK_MAX, n_rows), 1)
        for row_i in range(db):
            page_s = jnp.sum(jnp.where(ends[row_i:row_i + 1, :] <= slot, 1.0, 0.0), axis=1, keepdims=True)
            mrow = page_s.astype(I32) * db + row_i
            onehot = jnp.where(rows_lane == mrow, 1.0, 0.0).astype(BF16)
            got = jnp.dot(onehot, table, preferred_element_type=F32)
            k = slot - got[:, PAGE_SIZE:]
            off_s = jnp.sum(jnp.where(got[:, :PAGE_SIZE] <= k, 1.0, 0.0), axis=1, keepdims=True)
            pos = page_s * float(PAGE_SIZE) + off_s
            valid = slot[:, 0:1] < n_pool[row_i:row_i + 1, :]
            idx_ref[row_i] = jnp.where(valid, pos, 0.0).astype(I32)
        meta_ref[...] = jnp.broadcast_to(n_pool, (db, PAGE_SIZE)).astype(I32)


def _sample_select(page_table, qi_s, kvi_s16, pool_ki):
    db, n_pages = page_table.shape
    assert db & (db - 1) == 0, "row index arithmetic in the rank matrix assumes a power-of-two row count"
    assert n_pages == PAGE_SIZE, "page ends are laid out one page per lane"
    n_steps = n_pages // PAGES_PER_STEP

    def page_spec(r):
        return pl.BlockSpec(
            (1, 1, PAGE_SIZE, IDX_DIM),
            lambda b, p, pt: (0, pt[b, p * PAGES_PER_STEP + r], 0, 0))

    grid_spec = pltpu.PrefetchScalarGridSpec(
        num_scalar_prefetch=1,
        grid=(db, n_steps),
        in_specs=[
            pl.BlockSpec((1, IDX_HEADS, IDX_DIM), lambda b, p, pt: (b, 0, 0)),
            pl.BlockSpec((1, IDX_HEADS, KVI_WIDTH), lambda b, p, pt: (b, 0, 0)),
        ] + [page_spec(r) for r in range(PAGES_PER_STEP)],
        out_specs=[
            pl.BlockSpec((db, TOPK_MAX, 1), lambda b, p, pt: (0, 0, 0)),
            pl.BlockSpec((db, PAGE_SIZE), lambda b, p, pt: (0, 0)),
        ],
        scratch_shapes=[
            pltpu.VMEM((n_pages, db, PAGE_SIZE), F32),
            pltpu.VMEM((db, PAGE_SIZE), F32),
        ],
    )
    return pl.pallas_call(
        _sample_select_kernel,
        grid_spec=grid_spec,
        out_shape=[
            jax.ShapeDtypeStruct((db, TOPK_MAX, 1), I32),
            jax.ShapeDtypeStruct((db, PAGE_SIZE), I32),
        ],
        compiler_params=pltpu.CompilerParams(
            dimension_semantics=("arbitrary", "arbitrary"), vmem_limit_bytes=VMEM_LIMIT),
        name="sample_select",
    )(page_table, qi_s, kvi_s16, *([pool_ki] * PAGES_PER_STEP))


def _sample_attend_kernel(idx_ref, npool_ref, pt_ref, q_ref, kvs_ref, pk_ref, pv_ref, o_ref,
                          kbuf, vbuf, kflat, vflat, sem):
    b = pl.program_id(0)

    def row_copies(r):
        ix = idx_ref[b, r]
        page = pt_ref[b, lax.shift_right_logical(ix, 7)]
        off = ix & (PAGE_SIZE - 1)
        return (pltpu.make_async_copy(pk_ref.at[0, page, off], kbuf.at[r], sem.at[0]),
                pltpu.make_async_copy(pv_ref.at[0, page, off], vbuf.at[r], sem.at[1]))

    def start_body(r, carry):
        ck, cv = row_copies(r)
        ck.start()
        cv.start()
        return carry

    def wait_body(r, carry):
        ck, cv = row_copies(r)
        ck.wait()
        cv.wait()
        return carry

    lax.fori_loop(0, TOPK_MAX, start_body, 0, unroll=8)
    lax.fori_loop(0, TOPK_MAX, wait_body, 0, unroll=8)

    def head_copies(g):
        return (pltpu.make_async_copy(kbuf.at[:, g, :], kflat.at[g], sem.at[2]),
                pltpu.make_async_copy(vbuf.at[:, g, :], vflat.at[g], sem.at[3]))

    for g in range(N_KV_HEADS):
        for cp in head_copies(g):
            cp.start()
    for g in range(N_KV_HEADS):
        for cp in head_copies(g):
            cp.wait()

    n_pool = npool_ref[b]
    slot = lax.broadcasted_iota(I32, (1, TOPK_MAX), 1)
    slot_ok = slot < n_pool
    new_ok = n_pool < TOPK_MAX
    q_all = q_ref[0].astype(F32)
    for g in range(N_KV_HEADS):
        qg = q_all[g * Q_PER_KV:(g + 1) * Q_PER_KV, :].astype(BF16)
        ks = kflat[g].astype(BF16)
        vs = vflat[g].astype(BF16)
        k_new = kvs_ref[0, 0:1, KVI_K + g * HEAD_DIM:KVI_K + (g + 1) * HEAD_DIM].astype(BF16)
        v_new = kvs_ref[0, 0:1, KVI_V + g * HEAD_DIM:KVI_V + (g + 1) * HEAD_DIM].astype(BF16)
        s = jnp.where(slot_ok, _dot_nt(qg, ks) * ATTN_SCALE, -jnp.inf)
        s_new = jnp.sum(qg.astype(F32) * k_new.astype(F32), axis=-1, keepdims=True) * ATTN_SCALE
        s_new = jnp.where(new_ok, s_new, -jnp.inf)
        m = jnp.maximum(jnp.max(s, axis=-1, keepdims=True), s_new)
        p = jnp.exp(s - m)
        p_new = jnp.exp(s_new - m)
        denom = jnp.sum(p, axis=-1, keepdims=True) + p_new
        pn = (p / denom).astype(BF16)
        pn_new = (p_new / denom).astype(BF16).astype(F32)
        o = jnp.dot(pn, vs, preferred_element_type=F32) + pn_new * v_new.astype(F32)
        o_ref[0, g * Q_PER_KV:(g + 1) * Q_PER_KV, :] = o


def _sample_attend(idx, n_pool, page_table, q_s, kvi_s16, pool_k, pool_v):
    db = page_table.shape[0]
    grid_spec = pltpu.PrefetchScalarGridSpec(
        num_scalar_prefetch=3,
        grid=(db,),
        in_specs=[
            pl.BlockSpec((1, N_HEADS, HEAD_DIM), lambda b, *_: (b, 0, 0)),
            pl.BlockSpec((1, IDX_HEADS, KVI_WIDTH), lambda b, *_: (b, 0, 0)),
            pl.BlockSpec(memory_space=pl.ANY),
            pl.BlockSpec(memory_space=pl.ANY),
        ],
        out_specs=pl.BlockSpec((1, N_HEADS, HEAD_DIM), lambda b, *_: (b, 0, 0)),
        scratch_shapes=[
            pltpu.VMEM((TOPK_MAX, N_KV_HEADS, HEAD_DIM), F32),
            pltpu.VMEM((TOPK_MAX, N_KV_HEADS, HEAD_DIM), F32),
            pltpu.VMEM((N_KV_HEADS, TOPK_MAX, HEAD_DIM), F32),
            pltpu.VMEM((N_KV_HEADS, TOPK_MAX, HEAD_DIM), F32),
            pltpu.SemaphoreType.DMA((4,)),
        ],
    )
    return pl.pallas_call(
        _sample_attend_kernel,
        grid_spec=grid_spec,
        out_shape=jax.ShapeDtypeStruct((db, N_HEADS, HEAD_DIM), F32),
        compiler_params=pltpu.CompilerParams(dimension_semantics=("arbitrary",)),
        name="sample_attend",
    )(idx, n_pool, page_table, q_s, kvi_s16, pool_k, pool_v)


def _merge_kernel(chunked, oa_ref, za_ref, u_ref, v_ref, zb_ref, ga_ref, gb_ref, x_ref,
                  wpa_ref, wpb_ref, wout_ref, lng_ref, lnb_ref, ws_ref, bs_ref, gf_ref, y_ref, *rest):
    vn_ref, ob_ref = (None, rest[0]) if chunked else rest
    tm = x_ref.shape[0]
    u = _gelu(u_ref[...])
    v = _gelu(v_ref[...])
    mu = jnp.mean(v, axis=-1, keepdims=True)
    vc = v - mu
    vn = (vc * lax.rsqrt(jnp.mean(vc * vc, axis=-1, keepdims=True) + LN_EPS)) * lng_ref[...] + lnb_ref[...]
    if vn_ref is not None:
        vn_ref[...] = vn

    if chunked:
        vnb = vn.astype(BF16)
        ri = lax.broadcasted_iota(I32, (CHUNK, CHUNK), 0)
        ci = lax.broadcasted_iota(I32, (CHUNK, CHUNK), 1)
        for g in range(GMLP_GROUPS):
            wm = jnp.where(ri >= ci, ws_ref[g], 0.0).astype(BF16)
            lo = g * GMLP_GROUP_DIM
            for c in range(tm // CHUNK):
                mixed = jnp.dot(wm, vnb[c * CHUNK:(c + 1) * CHUNK, lo:lo + GMLP_GROUP_DIM],
                                preferred_element_type=F32) + bs_ref[:, g:g + 1]
                ob_ref[c * CHUNK:(c + 1) * CHUNK, lo:lo + GMLP_GROUP_DIM] = (
                    u[c * CHUNK:(c + 1) * CHUNK, lo:lo + GMLP_GROUP_DIM] * mixed)
    else:
        ob_ref[...] = u * (ws_ref[...] * vn + bs_ref[...])

    ha = (oa_ref[...] * _silu(za_ref[...])).astype(BF16)
    hb = (ob_ref[...] * _silu(zb_ref[...])).astype(BF16)
    ya = jnp.dot(ha, wpa_ref[...], preferred_element_type=F32)
    yb = jnp.dot(hb, wpb_ref[...], preferred_element_type=F32)
    mix = jax.nn.sigmoid(ga_ref[...]) * ya + jax.nn.sigmoid(gb_ref[...]) * yb
    out = x_ref[...] + jnp.dot(mix.astype(BF16), wout_ref[...], preferred_element_type=F32)
    ms = jnp.mean(out * out, axis=-1, keepdims=True)
    y_ref[...] = (out * lax.rsqrt(ms + RMS_EPS)) * gf_ref[...]


def _merge(chunked, o_a, gate, x2d, wpa, wpb, wout, lng, lnb, ws, bs, gf, tm):
    rows = x2d.shape[0]
    gw = GMLP_WIDTH

    def const(shape):
        return pl.BlockSpec(shape, lambda i: (0,) * len(shape), pipeline_mode=pl.Buffered(1))

    return pl.pallas_call(
        functools.partial(_merge_kernel, chunked),
        grid=(rows // tm,),
        in_specs=[
            pl.BlockSpec((tm, ATTN_WIDTH), lambda i: (i, 0)),
            pl.BlockSpec((tm, gw), lambda i: (i, 0)),
            pl.BlockSpec((tm, gw), lambda i: (i, 1)),
            pl.BlockSpec((tm, gw), lambda i: (i, 2)),
            pl.BlockSpec((tm, gw), lambda i: (i, 3)),
            pl.BlockSpec((tm, D_MODEL), lambda i: (i, 2)),
            pl.BlockSpec((tm, D_MODEL), lambda i: (i, 3)),
            pl.BlockSpec((tm, D_MODEL), lambda i: (i, 0)),
            const(wpa.shape), const(wpb.shape), const(wout.shape),
            const(lng.shape), const(lnb.shape), const(ws.shape), const(bs.shape), const(gf.shape),
        ],
        out_specs=[pl.BlockSpec((tm, D_MODEL), lambda i: (i, 0))]
        + ([] if chunked else [pl.BlockSpec((tm, gw), lambda i: (i, 0))]),
        out_shape=[jax.ShapeDtypeStruct((rows, D_MODEL), F32)]
        + ([] if chunked else [jax.ShapeDtypeStruct((rows, gw), F32)]),
        scratch_shapes=[pltpu.VMEM((tm, gw), F32)],
        compiler_params=pltpu.CompilerParams(
            dimension_semantics=("arbitrary",), vmem_limit_bytes=VMEM_LIMIT),
        name="merge_prompt" if chunked else "merge_sample",
    )(o_a, gate, gate, gate, gate, gate, gate, x2d, wpa, wpb, wout, lng, lnb, ws, bs, gf)


def kernel(x_prompt, x_sample, cache_k, cache_v, cache_k_idx, page_table, norm_in_g, w_in,
           w_proj_a, w_proj_b, w_out, ln_g, ln_b, w_spatial, b_spatial, norm_f_g):
    depth = w_in.shape[0]
    assert depth == 1, "single trunk layer"
    batch, seq, _ = x_prompt.shape
    db, dseq, _ = x_sample.shape
    assert dseq == 1
    l = 0
    wpa = w_proj_a[l].astype(BF16)
    wpb = w_proj_b[l].astype(BF16)
    wout = w_out[l].astype(BF16)
    g_in = norm_in_g[l].reshape(1, D_MODEL)
    lng = ln_g[l].reshape(1, GMLP_WIDTH)
    lnb = ln_b[l].reshape(1, GMLP_WIDTH)
    gf = norm_f_g.reshape(1, D_MODEL)

    xp = x_prompt.reshape(batch * seq, D_MODEL)
    wt = w_in[l].T
    xn_p, kv_p, ki_p2, wi_t, k3_p, v3_p = _inproj_kvi(xp, g_in, wt, 1024)
    h_t = _inproj(xn_p, wt, COL_Q, COL_WI, COL_WI // 4, 1024, BF16, True)
    gate_p = _inproj(xn_p, wt, GATE_COL0, GATE_WIDTH, 1024, 1024, F32, False)
    oa_p = _prompt_attn(h_t, wi_t, kv_p, ki_p2, batch, seq)
    bs_t = b_spatial[l].T
    (y_p,) = _merge(True, oa_p, gate_p, xp, wpa, wpb, wout, lng, lnb, w_spatial[l], bs_t, gf, 256)

    xs = x_sample.reshape(db, D_MODEL)
    xn_s, kv_s, ki_s2, wi_t_s, _, _ = _inproj_kvi(xs, g_in, wt, db)
    h_s = _inproj(xn_s, wt, COL_Q, COL_WI, 512, db, F32, False)
    gate_s = _inproj(xn_s, wt, GATE_COL0, GATE_WIDTH, 1024, db, F32, False)
    kvi_s = jnp.concatenate(
        [kv_s, ki_s2, wi_t_s.T, jnp.zeros((db, KVI_WIDTH - KVI_WI - IDX_HEADS), F32)], axis=1)
    q_s = h_s[:, COL_Q:COL_K].astype(BF16).reshape(db, N_HEADS, HEAD_DIM)
    qi_s = h_s[:, COL_QI:COL_WI].astype(BF16).reshape(db, IDX_HEADS, IDX_DIM)
    wi_col = kvi_s[:, KVI_WI:KVI_WI + IDX_HEADS].reshape(db, IDX_HEADS, 1)
    kvi_s16 = jnp.broadcast_to(kvi_s[:, None, :], (db, IDX_HEADS, KVI_WIDTH))
    kvi_s16 = lax.dynamic_update_slice(kvi_s16, wi_col, (0, 0, KVI_WI))
    idx, meta = _sample_select(page_table, qi_s, kvi_s16, cache_k_idx)
    oa_s = _sample_attend(idx.reshape(db, TOPK_MAX), meta[:, 0], page_table, q_s, kvi_s16,
                          cache_k, cache_v).reshape(db, ATTN_WIDTH)
    ws0 = jnp.repeat(w_spatial[l][:, 0, 0], GMLP_GROUP_DIM).reshape(1, GMLP_WIDTH)
    bs0 = jnp.repeat(b_spatial[l][:, 0], GMLP_GROUP_DIM).reshape(1, GMLP_WIDTH)
    y_s, vn_s = _merge(False, oa_s, gate_s, xs, wpa, wpb, wout, lng, lnb, ws0, bs0, gf, db)

    def kv_out(kvi, lead):
        k = kvi[:, KVI_K:KVI_K + KV_WIDTH].reshape((1,) + lead + (N_KV_HEADS, HEAD_DIM))
        v = kvi[:, KVI_V:KVI_V + KV_WIDTH].reshape((1,) + lead + (N_KV_HEADS, HEAD_DIM))
        ki = kvi[:, KVI_KI:KVI_KI + IDX_DIM].reshape((1,) + lead + (IDX_DIM,))
        return k, v, ki

    k_p = k3_p.reshape(1, batch, seq, N_KV_HEADS, HEAD_DIM)
    v_p = v3_p.reshape(1, batch, seq, N_KV_HEADS, HEAD_DIM)
    ki_p = ki_p2.reshape(1, batch, seq, IDX_DIM)
    k_s, v_s, ki_s = kv_out(kvi_s, (db, dseq))
    return (y_p.reshape(batch, seq, D_MODEL), y_s.reshape(db, dseq, D_MODEL),
            k_p, v_p, ki_p, k_s, v_s, ki_s, vn_s.reshape(1, db, dseq, GMLP_WIDTH))
```

```python
import functools

import jax
import jax.numpy as jnp
import numpy as np
from jax import lax
from jax.experimental import pallas as pl
from jax.experimental.pallas import tpu as pltpu

F32 = jnp.float32
BF16 = jnp.bfloat16
I32 = jnp.int32

D_MODEL = 2048
N_HEADS = 8
N_KV_HEADS = 2
HEAD_DIM = 128
Q_PER_KV = N_HEADS // N_KV_HEADS
ATTN_WIDTH = N_HEADS * HEAD_DIM
KV_WIDTH = N_KV_HEADS * HEAD_DIM
IDX_HEADS = 16
IDX_DIM = 128
TOPK_MAX = 256
PAGE_SIZE = 128
GMLP_WIDTH = 1024
GMLP_GROUPS = 8
GMLP_GROUP_DIM = GMLP_WIDTH // GMLP_GROUPS
CHUNK = 128
RMS_EPS = 1e-6
LN_EPS = 1e-5

SUBLANES = 8
COL_Q = 0
COL_K = COL_Q + ATTN_WIDTH
COL_V = COL_K + KV_WIDTH
COL_QI = COL_V + KV_WIDTH
COL_WI = COL_QI + IDX_HEADS * IDX_DIM
COL_KI = COL_WI + IDX_HEADS
GATE_COL0 = COL_KI + IDX_DIM
GATE_WIDTH = 4 * GMLP_WIDTH + 2 * D_MODEL
KVI_K = 0
KVI_V = KVI_K + KV_WIDTH
KVI_KI = KVI_V + KV_WIDTH
KVI_WI = KVI_KI + IDX_DIM
KVI_WIDTH = KVI_WI + 128

Q_BLOCK = 128
KEY_CHUNK = 512
COUNT_ROWS = 64
ONES_ROWS = 16
SIGN_EXP_BITS = 9
SEARCH_ALWAYS_BITS = 20
SEARCH_GROUP_BITS = 2
NEG_SENTINEL = float(np.finfo(np.float32).min)
NEG_ABOVE_SENTINEL = float(np.nextafter(np.float32(NEG_SENTINEL), np.float32(0.0)))
INT_MIN = -(2 ** 31)
F32_MANTISSA_BITS = 23
F32_MANTISSA_MASK = (1 << F32_MANTISSA_BITS) - 1
F32_EXP_MASK = 0xFF
F32_MIN_NORMAL = float(2.0 ** -126)
IDX_W_SCALE = float(IDX_HEADS ** -0.5 * IDX_DIM ** -0.5)
ATTN_SCALE = float(HEAD_DIM ** -0.5)
ATTN_SCALE_LOG2E = float(HEAD_DIM ** -0.5 * np.log2(np.e))
V7X_VMEM_BYTES = 64 * 1024 * 1024
VMEM_LIMIT = V7X_VMEM_BYTES - 8 * 1024 * 1024
PROJ_TOKEN_TILE = 1024
GATE_FEATURE_TILE = 1024
MERGE_TOKEN_TILE = 256


def _dot_nt(a, b):
    return lax.dot_general(a, b, (((1,), (1,)), ((), ())), preferred_element_type=F32)


def _ordered_bits_to_f32(u):
    signed_pow, zero_exp = _sign_exp_parts(u)
    return _with_mantissa(u, signed_pow, zero_exp)


def _sign_exp_parts(u):
    bits = jnp.where(u < 0, u ^ INT_MIN, ~u)
    exp = lax.shift_right_logical(bits, F32_MANTISSA_BITS) & F32_EXP_MASK
    e1 = exp - 1
    p = jnp.full(u.shape, F32_MIN_NORMAL, F32)
    for b in range(7):
        p = jnp.where(((e1 >> b) & 1) == 1, p * float(2.0 ** (2 ** b)), p)
    top = ((e1 >> 7) & 1) == 1
    p = jnp.where(top, p * float(2.0 ** 64), p)
    p = jnp.where(top, p * float(2.0 ** 64), p)
    return jnp.where(bits < 0, -p, p), exp == 0


def _with_mantissa(u, signed_pow, zero_exp):
    low = jnp.where(u < 0, u, ~u) & F32_MANTISSA_MASK
    mant = 1.0 + low.astype(F32) * float(2.0 ** -F32_MANTISSA_BITS)
    return jnp.where(zero_exp, 0.0, signed_pow * mant)


def _gelu(x):
    return 0.5 * x * (1.0 + lax.erf(x * float(np.sqrt(0.5))))


def _silu(x):
    return x * jax.nn.sigmoid(x)


def _inproj_kernel(feature_major, xn_ref, w_ref, o_ref, wb_ref):
    @pl.when(pl.program_id(1) == 0)
    def _():
        wb_ref[...] = w_ref[...].astype(BF16)

    if feature_major:
        o_ref[...] = _dot_nt(wb_ref[...], xn_ref[...]).astype(o_ref.dtype)
    else:
        o_ref[...] = _dot_nt(xn_ref[...], wb_ref[...]).astype(o_ref.dtype)


def _inproj(xn, wt, row_start, n_rows, tn, tm, out_dtype, feature_major):
    rows = xn.shape[0]
    assert n_rows % tn == 0 and rows % tm == 0 and row_start % SUBLANES == 0
    if row_start % tn == 0:
        j0 = row_start // tn
        w_spec = pl.BlockSpec((tn, D_MODEL), lambda j, i: (j0 + j, 0))
    else:
        w_spec = pl.BlockSpec((pl.Element(tn), pl.Element(D_MODEL)), lambda j, i: ((row_start // SUBLANES + j * (tn // SUBLANES)) * SUBLANES, 0))
    if feature_major:
        out_spec = pl.BlockSpec((tn, tm), lambda j, i: (j, i))
        out_shape = (n_rows, rows)
    else:
        out_spec = pl.BlockSpec((tm, tn), lambda j, i: (i, j))
        out_shape = (rows, n_rows)
    return pl.pallas_call(
        functools.partial(_inproj_kernel, feature_major),
        grid=(n_rows // tn, rows // tm),
        in_specs=[pl.BlockSpec((tm, D_MODEL), lambda j, i: (i, 0)), w_spec],
        out_specs=out_spec,
        out_shape=jax.ShapeDtypeStruct(out_shape, out_dtype),
        scratch_shapes=[pltpu.VMEM((tn, D_MODEL), BF16)],
        compiler_params=pltpu.CompilerParams(
            dimension_semantics=("arbitrary", "arbitrary"), vmem_limit_bytes=VMEM_LIMIT),
        name="inproj_t" if feature_major else "inproj",
    )(xn, wt)


def _inproj_kvi_kernel(x_ref, g_ref, wkv_ref, wwi_ref, wki_ref, xn_ref, kv_ref, ki_ref, wit_ref, k3_ref, v3_ref,
                       wkvb_ref, wwib_ref, wkib_ref):
    @pl.when(pl.program_id(0) == 0)
    def _():
        wkvb_ref[...] = wkv_ref[...].astype(BF16)
        wwib_ref[...] = wwi_ref[...].astype(BF16)
        wkib_ref[...] = wki_ref[...].astype(BF16)

    x = x_ref[...]
    ms = jnp.mean(x * x, axis=-1, keepdims=True)
    xn = ((x * lax.rsqrt(ms + RMS_EPS)) * g_ref[...]).astype(BF16)
    xn_ref[...] = xn
    kv = _dot_nt(xn, wkvb_ref[...])
    kv_ref[...] = kv
    for g in range(N_KV_HEADS):
        k3_ref[:, g, :] = kv[:, g * HEAD_DIM:(g + 1) * HEAD_DIM]
        v3_ref[:, g, :] = kv[:, KV_WIDTH + g * HEAD_DIM:KV_WIDTH + (g + 1) * HEAD_DIM]
    ki_ref[...] = _dot_nt(xn, wkib_ref[...])
    wit_ref[...] = _dot_nt(wwib_ref[...], xn)


def _inproj_kvi(x2d, g, wt, tm):
    rows = x2d.shape[0]
    assert rows % tm == 0 and COL_K % (2 * KV_WIDTH) == 0 and COL_WI % IDX_HEADS == 0
    return pl.pallas_call(
        _inproj_kvi_kernel,
        grid=(rows // tm,),
        in_specs=[
            pl.BlockSpec((tm, D_MODEL), lambda i: (i, 0)),
            pl.BlockSpec((1, D_MODEL), lambda i: (0, 0)),
            pl.BlockSpec((2 * KV_WIDTH, D_MODEL), lambda i: (COL_K // (2 * KV_WIDTH), 0)),
            pl.BlockSpec((IDX_HEADS, D_MODEL), lambda i: (COL_WI // IDX_HEADS, 0)),
            pl.BlockSpec((pl.Element(IDX_DIM), pl.Element(D_MODEL)), lambda i: (COL_KI, 0)),
        ],
        out_specs=[
            pl.BlockSpec((tm, D_MODEL), lambda i: (i, 0)),
            pl.BlockSpec((tm, 2 * KV_WIDTH), lambda i: (i, 0)),
            pl.BlockSpec((tm, IDX_DIM), lambda i: (i, 0)),
            pl.BlockSpec((IDX_HEADS, tm), lambda i: (0, i)),
            pl.BlockSpec((tm, N_KV_HEADS, HEAD_DIM), lambda i: (i, 0, 0)),
            pl.BlockSpec((tm, N_KV_HEADS, HEAD_DIM), lambda i: (i, 0, 0)),
        ],
        out_shape=[
            jax.ShapeDtypeStruct((rows, D_MODEL), BF16),
            jax.ShapeDtypeStruct((rows, 2 * KV_WIDTH), F32),
            jax.ShapeDtypeStruct((rows, IDX_DIM), F32),
            jax.ShapeDtypeStruct((IDX_HEADS, rows), F32),
            jax.ShapeDtypeStruct((rows, N_KV_HEADS, HEAD_DIM), F32),
            jax.ShapeDtypeStruct((rows, N_KV_HEADS, HEAD_DIM), F32),
        ],
        scratch_shapes=[
            pltpu.VMEM((2 * KV_WIDTH, D_MODEL), BF16),
            pltpu.VMEM((IDX_HEADS, D_MODEL), BF16),
            pltpu.VMEM((IDX_DIM, D_MODEL), BF16),
        ],
        compiler_params=pltpu.CompilerParams(
            dimension_semantics=("arbitrary",), vmem_limit_bytes=VMEM_LIMIT),
        name="inproj_kvi",
    )(x2d, g, wt, wt, wt)


def _prompt_attn_kernel(qt_ref, qit0_ref, qit1_ref, qit2_ref, qit3_ref, vt_ref, wit_ref, kvf_ref, kif_ref,
                        o_ref, kb_ref, kib_ref, vtc_ref, sc_ref, sa_ref, sb_ref, acc_ref, ml_ref, u_ref, cnt_ref):
    i = pl.program_id(1)
    seq = kvf_ref.shape[0]
    pos_bits = seq.bit_length()
    qit_refs = (qit0_ref, qit1_ref, qit2_ref, qit3_ref)
    heads_per_ref = IDX_HEADS // len(qit_refs)

    @pl.when(i == 0)
    def _():
        kb_ref[...] = kvf_ref[:, 0:KV_WIDTH].astype(BF16)
        kib_ref[...] = kif_ref[...].astype(BF16)
        for c in range(seq // KEY_CHUNK):
            for g in range(N_KV_HEADS):
                vtc_ref[c, g, :HEAD_DIM, :] = vt_ref[g * HEAD_DIM:(g + 1) * HEAD_DIM,
                                                     c * KEY_CHUNK:(c + 1) * KEY_CHUNK]
                vtc_ref[c, g, HEAD_DIM:, :] = jnp.ones((ONES_ROWS, KEY_CHUNK), BF16)

    n_chunks = (i * Q_BLOCK) // KEY_CHUNK + 1
    q_pos = i * Q_BLOCK + lax.broadcasted_iota(I32, (1, Q_BLOCK), 1)
    key0 = lax.broadcasted_iota(I32, (KEY_CHUNK, 1), 0)

    w_t = wit_ref[...] * IDX_W_SCALE
    qit_all = jnp.concatenate(
        [qit_refs[h // heads_per_ref][(h % heads_per_ref) * IDX_DIM:(h % heads_per_ref + 1) * IDX_DIM, :]
         for h in range(IDX_HEADS)], axis=1)

    def score_chunk(c):
        k0 = pl.multiple_of(c * KEY_CHUNK, KEY_CHUNK)
        res = jnp.dot(kib_ref[pl.ds(k0, KEY_CHUNK), :], qit_all, preferred_element_type=F32)
        acc = jnp.zeros((KEY_CHUNK, Q_BLOCK), F32)
        for h in range(IDX_HEADS):
            acc = acc + w_t[h:h + 1, :] * jnp.maximum(res[:, h * Q_BLOCK:(h + 1) * Q_BLOCK], 0.0)
        sc_ref[c] = jnp.where(k0 + key0 <= q_pos, acc, NEG_SENTINEL)

    def score_pair(t, carry):
        score_chunk(2 * t)
        score_chunk(2 * t + 1)
        return carry

    lax.fori_loop(0, n_chunks // 2, score_pair, 0)

    @pl.when(n_chunks % 2 == 1)
    def _():
        score_chunk(n_chunks - 1)

    def count(pred):
        def body(c, cnt):
            ones = jnp.where(pred(sc_ref[c], c), 1.0, 0.0)
            return cnt + ones.reshape(KEY_CHUNK // COUNT_ROWS, COUNT_ROWS, Q_BLOCK).sum(axis=0)
        part = lax.fori_loop(0, n_chunks, body, jnp.zeros((COUNT_ROWS, Q_BLOCK), F32))
        return jnp.sum(part, axis=0, keepdims=True)

    def search(j0, n, u, cnt_u, decode):
        def body(j, carry):
            u, cnt_u = carry
            cand = u | lax.shift_left(jnp.int32(1), 31 - j)
            thr_c = decode(cand)
            cnt = count(lambda blk, c: blk >= thr_c)
            ok = cnt >= float(TOPK_MAX)
            return jnp.where(ok, cand, u), jnp.where(ok, cnt, cnt_u)
        return lax.fori_loop(j0, j0 + n, body, (u, cnt_u))

    u = jnp.zeros((1, Q_BLOCK), I32)
    cnt_u = (u + n_chunks * KEY_CHUNK).astype(F32)
    u, cnt_u = search(0, SIGN_EXP_BITS, u, cnt_u, _ordered_bits_to_f32)
    signed_pow, zero_exp = _sign_exp_parts(u)

    def decode_mantissa(cand):
        return _with_mantissa(cand, signed_pow, zero_exp)

    u, cnt_u = search(SIGN_EXP_BITS, SEARCH_ALWAYS_BITS - SIGN_EXP_BITS, u, cnt_u, decode_mantissa)
    u_ref[...] = u
    cnt_ref[...] = cnt_u
    for j0 in range(SEARCH_ALWAYS_BITS, 32, SEARCH_GROUP_BITS):
        @pl.when(jnp.max(jnp.where(cnt_ref[...] == float(TOPK_MAX), 0.0, 1.0)) > 0.0)
        def _():
            u_g, cnt_g = search(j0, SEARCH_GROUP_BITS, u_ref[...], cnt_ref[...], decode_mantissa)
            u_ref[...] = u_g
            cnt_ref[...] = cnt_g
    u = u_ref[...]
    n_ge = cnt_ref[...]
    thr = decode_mantissa(u)
    thr_adm = jnp.maximum(thr, NEG_ABOVE_SENTINEL)
    excess = jnp.max(jnp.where((n_ge > float(TOPK_MAX)) & (thr > NEG_SENTINEL), 1.0, 0.0)) > 0.0

    @pl.when(excess)
    def _():
        need = float(TOPK_MAX) - count(lambda blk, c: blk > thr)

        def jbit_body(j, jl):
            cand = jl | lax.shift_left(jnp.int32(1), pos_bits - 1 - j)
            cnt = count(lambda blk, c: (blk == thr) & (c * KEY_CHUNK + key0 < cand))
            return jnp.where(cnt <= need, cand, jl)
        jlim = lax.fori_loop(0, pos_bits, jbit_body, jnp.zeros((1, Q_BLOCK), I32))

        def demote_body(c, carry):
            blk = sc_ref[c]
            dropped = (blk == thr) & (thr > NEG_SENTINEL) & (c * KEY_CHUNK + key0 >= jlim)
            sc_ref[c] = jnp.where(dropped, NEG_SENTINEL, blk)
            return carry
        lax.fori_loop(0, n_chunks, demote_body, 0)

    acc_ref[...] = jnp.zeros(acc_ref.shape, F32)
    cols = Q_PER_KV * Q_BLOCK
    qt_groups = [
        jnp.concatenate([qt_ref[(g * Q_PER_KV + h) * HEAD_DIM:(g * Q_PER_KV + h + 1) * HEAD_DIM, :]
                         for h in range(Q_PER_KV)], axis=1)
        for g in range(N_KV_HEADS)
    ]

    last_chunk = seq // KEY_CHUNK - 1

    def qk_scores(c, dst_ref):
        k0 = pl.multiple_of(jnp.minimum(c, last_chunk) * KEY_CHUNK, KEY_CHUNK)
        for g in range(N_KV_HEADS):
            kc = kb_ref[pl.ds(k0, KEY_CHUNK), g * HEAD_DIM:(g + 1) * HEAD_DIM]
            dst_ref[g] = jnp.dot(kc, qt_groups[g], preferred_element_type=F32)

    def softmax_pv(c, src_ref):
        bias1 = jnp.where(sc_ref[c] >= thr_adm, 0.0, -jnp.inf)
        bias = jnp.concatenate([bias1] * Q_PER_KV, axis=1)
        for g in range(N_KV_HEADS):
            m_old = ml_ref[2 * g:2 * g + 1, :]
            l_old = ml_ref[2 * g + 1:2 * g + 2, :]
            s = src_ref[g] * ATTN_SCALE_LOG2E + bias
            m_new = jnp.maximum(m_old, jnp.max(s, axis=0, keepdims=True))
            m_safe = jnp.where(m_new == -jnp.inf, 0.0, m_new)
            alpha = jnp.exp2(m_old - m_safe)
            p = jnp.exp2(s - m_safe).astype(BF16)
            pv = jnp.dot(vtc_ref[c, g], p, preferred_element_type=F32)
            acc_ref[g] = alpha * acc_ref[g] + pv[:HEAD_DIM]
            ml_ref[2 * g:2 * g + 1, :] = m_new
            ml_ref[2 * g + 1:2 * g + 2, :] = alpha * l_old + pv[HEAD_DIM:HEAD_DIM + 1]

    def pair_body(t, carry):
        qk_scores(2 * t + 1, sb_ref)
        softmax_pv(2 * t, sa_ref)
        qk_scores(2 * t + 2, sa_ref)
        softmax_pv(2 * t + 1, sb_ref)
        return carry

    for g in range(N_KV_HEADS):
        ml_ref[2 * g:2 * g + 1, :] = jnp.full((1, cols), -jnp.inf, F32)
        ml_ref[2 * g + 1:2 * g + 2, :] = jnp.zeros((1, cols), F32)
    qk_scores(0, sa_ref)
    lax.fori_loop(0, n_chunks // 2, pair_body, 0)

    @pl.when(n_chunks % 2 == 1)
    def _():
        softmax_pv(n_chunks - 1, sa_ref)

    for g in range(N_KV_HEADS):
        o = acc_ref[g] / ml_ref[2 * g + 1:2 * g + 2, :]
        for h in range(Q_PER_KV):
            hd = (g * Q_PER_KV + h) * HEAD_DIM
            o_ref[:, hd:hd + HEAD_DIM] = o[:, h * Q_BLOCK:(h + 1) * Q_BLOCK].T


def _prompt_attn(h_t, wi_t, kv, ki, batch, seq):
    nb = seq // Q_BLOCK
    cols = Q_PER_KV * Q_BLOCK
    qi_rows = IDX_HEADS * IDX_DIM // 4
    qi_b0 = COL_QI // qi_rows

    def qi_spec(t):
        return pl.BlockSpec((qi_rows, Q_BLOCK), lambda b, i: (qi_b0 + t, b * nb + i))

    return pl.pallas_call(
        _prompt_attn_kernel,
        grid=(batch, nb),
        in_specs=[
            pl.BlockSpec((ATTN_WIDTH, Q_BLOCK), lambda b, i: (COL_Q // ATTN_WIDTH, b * nb + i)),
            qi_spec(0), qi_spec(1), qi_spec(2), qi_spec(3),
            pl.BlockSpec((KV_WIDTH, seq), lambda b, i: (COL_V // KV_WIDTH, b)),
            pl.BlockSpec((IDX_HEADS, Q_BLOCK), lambda b, i: (0, b * nb + i)),
            pl.BlockSpec((seq, 2 * KV_WIDTH), lambda b, i: (b, 0)),
            pl.BlockSpec((seq, IDX_DIM), lambda b, i: (b, 0)),
        ],
        out_specs=pl.BlockSpec((Q_BLOCK, ATTN_WIDTH), lambda b, i: (b * nb + i, 0)),
        out_shape=jax.ShapeDtypeStruct((batch * seq, ATTN_WIDTH), F32),
        scratch_shapes=[
            pltpu.VMEM((seq, KV_WIDTH), BF16),
            pltpu.VMEM((seq, IDX_DIM), BF16),
            pltpu.VMEM((seq // KEY_CHUNK, N_KV_HEADS, HEAD_DIM + ONES_ROWS, KEY_CHUNK), BF16),
            pltpu.VMEM((seq // KEY_CHUNK, KEY_CHUNK, Q_BLOCK), F32),
            pltpu.VMEM((N_KV_HEADS, KEY_CHUNK, cols), F32),
            pltpu.VMEM((N_KV_HEADS, KEY_CHUNK, cols), F32),
            pltpu.VMEM((N_KV_HEADS, HEAD_DIM, cols), F32),
            pltpu.VMEM((2 * N_KV_HEADS, cols), F32),
            pltpu.VMEM((1, Q_BLOCK), I32),
            pltpu.VMEM((1, Q_BLOCK), F32),
        ],
        compiler_params=pltpu.CompilerParams(
            dimension_semantics=("arbitrary", "arbitrary"), vmem_limit_bytes=VMEM_LIMIT),
        name="prompt_attn",
    )(h_t, h_t, h_t, h_t, h_t, h_t, wi_t, kv, ki)


PAGES_PER_STEP = 32


def _sample_select_kernel(pt_ref, qi_ref, kvs_ref, *refs):
    page_refs = refs[:PAGES_PER_STEP]
    idx_ref, meta_ref, sc_ref, snew_ref = refs[PAGES_PER_STEP:]
    b = pl.program_id(0)
    p = pl.program_id(1)
    nb = pl.num_programs(0)
    n_steps = pl.num_programs(1)
    n_pages, db, _ = sc_ref.shape
    pos_bits = (n_pages * PAGE_SIZE).bit_length()

    qi = qi_ref[0]
    w_col = kvs_ref[0, :, KVI_WI:KVI_WI + 1] * IDX_W_SCALE

    def key_scores(keys_bf16):
        s = _dot_nt(qi, keys_bf16)
        return jnp.sum(w_col * jnp.maximum(s, 0.0), axis=0, keepdims=True)

    keys = jnp.concatenate([page_refs[r][0, 0] for r in range(PAGES_PER_STEP)], axis=0).astype(BF16)
    row = key_scores(keys)
    for r in range(PAGES_PER_STEP):
        sc_ref[p * PAGES_PER_STEP + r, pl.ds(b, 1), :] = row[:, r * PAGE_SIZE:(r + 1) * PAGE_SIZE]

    @pl.when(p == n_steps - 1)
    def _():
        ki_new = kvs_ref[0, :, KVI_KI:KVI_KI + IDX_DIM].astype(BF16)
        s_new = key_scores(ki_new)[:, 0:1]
        snew_ref[pl.ds(b, 1), :] = jnp.broadcast_to(s_new, (1, PAGE_SIZE))

    @pl.when((b == nb - 1) & (p == n_steps - 1))
    def _():
        sc = sc_ref[...]
        s_new = snew_ref[:, 0:1]

        def total(x):
            return jnp.sum(jnp.sum(x, axis=0), axis=1, keepdims=True)

        def bit_body(j, u):
            cand = u | lax.shift_left(jnp.int32(1), 31 - j)
            thr = _ordered_bits_to_f32(cand)
            cnt = total(jnp.where(sc >= thr[None], 1.0, 0.0)) + jnp.where(s_new >= thr, 1.0, 0.0)
            return jnp.where(cnt >= float(TOPK_MAX), cand, u)

        u = lax.fori_loop(0, 32, bit_body, jnp.zeros((db, 1), I32))
        thr = _ordered_bits_to_f32(u)
        n_gt = total(jnp.where(sc > thr[None], 1.0, 0.0)) + jnp.where(s_new > thr, 1.0, 0.0)
        need = float(TOPK_MAX) - n_gt
        flat = (lax.broadcasted_iota(I32, sc.shape, 0) * PAGE_SIZE
                + lax.broadcasted_iota(I32, sc.shape, 2))
        tie = sc == thr[None]

        def jbit_body(j, jl):
            cand = jl | lax.shift_left(jnp.int32(1), pos_bits - 1 - j)
            cnt = total(jnp.where(tie & (flat < cand[None]), 1.0, 0.0))
            return jnp.where(cnt <= need, cand, jl)

        jlim = lax.fori_loop(0, pos_bits, jbit_body, jnp.zeros((db, 1), I32))
        sel = (sc > thr[None]) | (tie & (flat < jlim[None]))
        self32 = jnp.where(sel, 1.0, 0.0)
        n_pool = total(self32)

        n_rows = n_pages * db
        selb = self32.astype(BF16).reshape(n_rows, PAGE_SIZE)
        ri = lax.broadcasted_iota(I32, (PAGE_SIZE, PAGE_SIZE), 0)
        ci = lax.broadcasted_iota(I32, (PAGE_SIZE, PAGE_SIZE), 1)
        incl = jnp.dot(selb, jnp.where(ri <= ci, 1.0, 0.0).astype(BF16), preferred_element_type=F32)
        tot = jnp.broadcast_to(incl[:, PAGE_SIZE - 1:PAGE_SIZE], (n_rows, PAGE_SIZE)).astype(BF16)
        rr = lax.broadcasted_iota(I32, (n_rows, n_rows), 0)
        cc = lax.broadcasted_iota(I32, (n_rows, n_rows), 1)
        earlier_page_same_row = ((rr & (db - 1)) == (cc & (db - 1))) & (cc < (rr & -db))
        offs = jnp.dot(jnp.where(earlier_page_same_row, 1.0, 0.0).astype(BF16), tot,
                       preferred_element_type=F32)
        ends_rows = offs + tot.astype(F32)
        r_i = lax.broadcasted_iota(I32, (n_rows, PAGE_SIZE), 0)
        l_i = lax.broadcasted_iota(I32, (n_rows, PAGE_SIZE), 1)
        log2_db = db.bit_length() - 1
        diag = jnp.where(l_i == lax.shift_right_logical(r_i, log2_db), ends_rows, 0.0).astype(BF16)
        pick = jnp.where((lax.broadcasted_iota(I32, (db, n_rows), 1) & (db - 1))
                         == lax.broadcasted_iota(I32, (db, n_rows), 0), 1.0, 0.0).astype(BF16)
        ends = jnp.dot(pick, diag, preferred_element_type=F32)

        table = jnp.concatenate([incl.astype(BF16), offs.astype(BF16)], axis=1)
        slot = lax.broadcasted_iota(I32, (TOPK_MAX, PAGE_SIZE), 0).astype(F32)
        rows_lane = lax.broadcasted_iota(I32, (TOPK_MAX, n_rows), 1)
        for row_i in range(db):
            page_s = jnp.sum(jnp.where(ends[row_i:row_i + 1, :] <= slot, 1.0, 0.0), axis=1, keepdims=True)
            mrow = page_s.astype(I32) * db + row_i
            onehot = jnp.where(rows_lane == mrow, 1.0, 0.0).astype(BF16)
            got = jnp.dot(onehot, table, preferred_element_type=F32)
            k = slot - got[:, PAGE_SIZE:]
            off_s = jnp.sum(jnp.where(got[:, :PAGE_SIZE] <= k, 1.0, 0.0), axis=1, keepdims=True)
            pos = page_s * float(PAGE_SIZE) + off_s
            valid = slot[:, 0:1] < n_pool[row_i:row_i + 1, :]
            idx_ref[row_i] = jnp.where(valid, pos, 0.0).astype(I32)
        meta_ref[...] = jnp.broadcast_to(n_pool, (db, PAGE_SIZE)).astype(I32)


def _sample_select(page_table, qi_s, kvi_s16, pool_ki):
    db, n_pages = page_table.shape
    assert db & (db - 1) == 0, "row index arithmetic in the rank matrix assumes a power-of-two row count"
    assert n_pages == PAGE_SIZE, "page ends are laid out one page per lane"
    n_steps = n_pages // PAGES_PER_STEP

    def page_spec(r):
        return pl.BlockSpec(
            (1, 1, PAGE_SIZE, IDX_DIM),
            lambda b, p, pt: (0, pt[b, p * PAGES_PER_STEP + r], 0, 0))

    grid_spec = pltpu.PrefetchScalarGridSpec(
        num_scalar_prefetch=1,
        grid=(db, n_steps),
        in_specs=[
            pl.BlockSpec((1, IDX_HEADS, IDX_DIM), lambda b, p, pt: (b, 0, 0)),
            pl.BlockSpec((1, IDX_HEADS, KVI_WIDTH), lambda b, p, pt: (b, 0, 0)),
        ] + [page_spec(r) for r in range(PAGES_PER_STEP)],
        out_specs=[
            pl.BlockSpec((db, TOPK_MAX, 1), lambda b, p, pt: (0, 0, 0)),
            pl.BlockSpec((db, PAGE_SIZE), lambda b, p, pt: (0, 0)),
        ],
        scratch_shapes=[
            pltpu.VMEM((n_pages, db, PAGE_SIZE), F32),
            pltpu.VMEM((db, PAGE_SIZE), F32),
        ],
    )
    return pl.pallas_call(
        _sample_select_kernel,
        grid_spec=grid_spec,
        out_shape=[
            jax.ShapeDtypeStruct((db, TOPK_MAX, 1), I32),
            jax.ShapeDtypeStruct((db, PAGE_SIZE), I32),
        ],
        compiler_params=pltpu.CompilerParams(
            dimension_semantics=("arbitrary", "arbitrary"), vmem_limit_bytes=VMEM_LIMIT),
        name="sample_select",
    )(page_table, qi_s, kvi_s16, *([pool_ki] * PAGES_PER_STEP))


def _sample_attend_kernel(idx_ref, npool_ref, pt_ref, q_ref, kvs_ref, pk_ref, pv_ref, o_ref,
                          kbuf, vbuf, kflat, vflat, sem):
    b = pl.program_id(0)

    def row_copies(r):
        ix = idx_ref[b, r]
        page = pt_ref[b, lax.shift_right_logical(ix, 7)]
        off = ix & (PAGE_SIZE - 1)
        return (pltpu.make_async_copy(pk_ref.at[0, page, off], kbuf.at[r], sem.at[0]),
                pltpu.make_async_copy(pv_ref.at[0, page, off], vbuf.at[r], sem.at[1]))

    def start_body(r, carry):
        ck, cv = row_copies(r)
        ck.start()
        cv.start()
        return carry

    def wait_body(r, carry):
        ck, cv = row_copies(r)
        ck.wait()
        cv.wait()
        return carry

    lax.fori_loop(0, TOPK_MAX, start_body, 0, unroll=8)
    lax.fori_loop(0, TOPK_MAX, wait_body, 0, unroll=8)

    def head_copies(g):
        return (pltpu.make_async_copy(kbuf.at[:, g, :], kflat.at[g], sem.at[2]),
                pltpu.make_async_copy(vbuf.at[:, g, :], vflat.at[g], sem.at[3]))

    for g in range(N_KV_HEADS):
        for cp in head_copies(g):
            cp.start()
    for g in range(N_KV_HEADS):
        for cp in head_copies(g):
            cp.wait()

    n_pool = npool_ref[b]
    slot = lax.broadcasted_iota(I32, (1, TOPK_MAX), 1)
    slot_ok = slot < n_pool
    new_ok = n_pool < TOPK_MAX
    q_all = q_ref[0].astype(F32)
    for g in range(N_KV_HEADS):
        qg = q_all[g * Q_PER_KV:(g + 1) * Q_PER_KV, :].astype(BF16)
        ks = kflat[g].astype(BF16)
        vs = vflat[g].astype(BF16)
        k_new = kvs_ref[0, 0:1, KVI_K + g * HEAD_DIM:KVI_K + (g + 1) * HEAD_DIM].astype(BF16)
        v_new = kvs_ref[0, 0:1, KVI_V + g * HEAD_DIM:KVI_V + (g + 1) * HEAD_DIM].astype(BF16)
        s = jnp.where(slot_ok, _dot_nt(qg, ks) * ATTN_SCALE, -jnp.inf)
        s_new = jnp.sum(qg.astype(F32) * k_new.astype(F32), axis=-1, keepdims=True) * ATTN_SCALE
        s_new = jnp.where(new_ok, s_new, -jnp.inf)
        m = jnp.maximum(jnp.max(s, axis=-1, keepdims=True), s_new)
        p = jnp.exp(s - m)
        p_new = jnp.exp(s_new - m)
        denom = jnp.sum(p, axis=-1, keepdims=True) + p_new
        pn = (p / denom).astype(BF16)
        pn_new = (p_new / denom).astype(BF16).astype(F32)
        o = jnp.dot(pn, vs, preferred_element_type=F32) + pn_new * v_new.astype(F32)
        o_ref[0, g * Q_PER_KV:(g + 1) * Q_PER_KV, :] = o


def _sample_attend(idx, n_pool, page_table, q_s, kvi_s16, pool_k, pool_v):
    db = page_table.shape[0]
    grid_spec = pltpu.PrefetchScalarGridSpec(
        num_scalar_prefetch=3,
        grid=(db,),
        in_specs=[
            pl.BlockSpec((1, N_HEADS, HEAD_DIM), lambda b, *_: (b, 0, 0)),
            pl.BlockSpec((1, IDX_HEADS, KVI_WIDTH), lambda b, *_: (b, 0, 0)),
            pl.BlockSpec(memory_space=pl.ANY),
            pl.BlockSpec(memory_space=pl.ANY),
        ],
        out_specs=pl.BlockSpec((1, N_HEADS, HEAD_DIM), lambda b, *_: (b, 0, 0)),
        scratch_shapes=[
            pltpu.VMEM((TOPK_MAX, N_KV_HEADS, HEAD_DIM), F32),
            pltpu.VMEM((TOPK_MAX, N_KV_HEADS, HEAD_DIM), F32),
            pltpu.VMEM((N_KV_HEADS, TOPK_MAX, HEAD_DIM), F32),
            pltpu.VMEM((N_KV_HEADS, TOPK_MAX, HEAD_DIM), F32),
            pltpu.SemaphoreType.DMA((4,)),
        ],
    )
    return pl.pallas_call(
        _sample_attend_kernel,
        grid_spec=grid_spec,
        out_shape=jax.ShapeDtypeStruct((db, N_HEADS, HEAD_DIM), F32),
        compiler_params=pltpu.CompilerParams(dimension_semantics=("arbitrary",)),
        name="sample_attend",
    )(idx, n_pool, page_table, q_s, kvi_s16, pool_k, pool_v)


def _merge_kernel(chunked, oa_ref, za_ref, u_ref, v_ref, zb_ref, ga_ref, gb_ref, x_ref,
                  wpa_ref, wpb_ref, wout_ref, lng_ref, lnb_ref, ws_ref, bs_ref, gf_ref, y_ref, *rest):
    vn_ref, ob_ref = (None, rest[0]) if chunked else rest
    tm = x_ref.shape[0]
    u = _gelu(u_ref[...])
    v = _gelu(v_ref[...])
    mu = jnp.mean(v, axis=-1, keepdims=True)
    vc = v - mu
    vn = (vc * lax.rsqrt(jnp.mean(vc * vc, axis=-1, keepdims=True) + LN_EPS)) * lng_ref[...] + lnb_ref[...]
    if vn_ref is not None:
        vn_ref[...] = vn

    if chunked:
        vnb = vn.astype(BF16)
        ri = lax.broadcasted_iota(I32, (CHUNK, CHUNK), 0)
        ci = lax.broadcasted_iota(I32, (CHUNK, CHUNK), 1)
        for g in range(GMLP_GROUPS):
            wm = jnp.where(ri >= ci, ws_ref[g], 0.0).astype(BF16)
            lo = g * GMLP_GROUP_DIM
            for c in range(tm // CHUNK):
                mixed = jnp.dot(wm, vnb[c * CHUNK:(c + 1) * CHUNK, lo:lo + GMLP_GROUP_DIM],
                                preferred_element_type=F32) + bs_ref[:, g:g + 1]
                ob_ref[c * CHUNK:(c + 1) * CHUNK, lo:lo + GMLP_GROUP_DIM] = (
                    u[c * CHUNK:(c + 1) * CHUNK, lo:lo + GMLP_GROUP_DIM] * mixed)
    else:
        ob_ref[...] = u * (ws_ref[...] * vn + bs_ref[...])

    ha = (oa_ref[...] * _silu(za_ref[...])).astype(BF16)
    hb = (ob_ref[...] * _silu(zb_ref[...])).astype(BF16)
    ya = jnp.dot(ha, wpa_ref[...], preferred_element_type=F32)
    yb = jnp.dot(hb, wpb_ref[...], preferred_element_type=F32)
    mix = jax.nn.sigmoid(ga_ref[...]) * ya + jax.nn.sigmoid(gb_ref[...]) * yb
    out = x_ref[...] + jnp.dot(mix.astype(BF16), wout_ref[...], preferred_element_type=F32)
    ms = jnp.mean(out * out, axis=-1, keepdims=True)
    y_ref[...] = (out * lax.rsqrt(ms + RMS_EPS)) * gf_ref[...]


def _merge(chunked, o_a, gate, x2d, wpa, wpb, wout, lng, lnb, ws, bs, gf, tm):
    rows = x2d.shape[0]
    gw = GMLP_WIDTH

    def const(shape):
        return pl.BlockSpec(shape, lambda i: (0,) * len(shape), pipeline_mode=pl.Buffered(1))

    return pl.pallas_call(
        functools.partial(_merge_kernel, chunked),
        grid=(rows // tm,),
        in_specs=[
            pl.BlockSpec((tm, ATTN_WIDTH), lambda i: (i, 0)),
            pl.BlockSpec((tm, gw), lambda i: (i, 0)),
            pl.BlockSpec((tm, gw), lambda i: (i, 1)),
            pl.BlockSpec((tm, gw), lambda i: (i, 2)),
            pl.BlockSpec((tm, gw), lambda i: (i, 3)),
            pl.BlockSpec((tm, D_MODEL), lambda i: (i, 2)),
            pl.BlockSpec((tm, D_MODEL), lambda i: (i, 3)),
            pl.BlockSpec((tm, D_MODEL), lambda i: (i, 0)),
            const(wpa.shape), const(wpb.shape), const(wout.shape),
            const(lng.shape), const(lnb.shape), const(ws.shape), const(bs.shape), const(gf.shape),
        ],
        out_specs=[pl.BlockSpec((tm, D_MODEL), lambda i: (i, 0))]
        + ([] if chunked else [pl.BlockSpec((tm, gw), lambda i: (i, 0))]),
        out_shape=[jax.ShapeDtypeStruct((rows, D_MODEL), F32)]
        + ([] if chunked else [jax.ShapeDtypeStruct((rows, gw), F32)]),
        scratch_shapes=[pltpu.VMEM((tm, gw), F32)],
        compiler_params=pltpu.CompilerParams(
            dimension_semantics=("arbitrary",), vmem_limit_bytes=VMEM_LIMIT),
        name="merge_prompt" if chunked else "merge_sample",
    )(o_a, gate, gate, gate, gate, gate, gate, x2d, wpa, wpb, wout, lng, lnb, ws, bs, gf)


def kernel(x_prompt, x_sample, cache_k, cache_v, cache_k_idx, page_table, norm_in_g, w_in,
           w_proj_a, w_proj_b, w_out, ln_g, ln_b, w_spatial, b_spatial, norm_f_g):
    depth = w_in.shape[0]
    assert depth == 1, "single trunk layer"
    batch, seq, _ = x_prompt.shape
    db, dseq, _ = x_sample.shape
    assert dseq == 1
    l = 0
    wpa = w_proj_a[l].astype(BF16)
    wpb = w_proj_b[l].astype(BF16)
    wout = w_out[l].astype(BF16)
    g_in = norm_in_g[l].reshape(1, D_MODEL)
    lng = ln_g[l].reshape(1, GMLP_WIDTH)
    lnb = ln_b[l].reshape(1, GMLP_WIDTH)
    gf = norm_f_g.reshape(1, D_MODEL)

    xp = x_prompt.reshape(batch * seq, D_MODEL)
    wt = w_in[l].T
    xn_p, kv_p, ki_p2, wi_t, k3_p, v3_p = _inproj_kvi(xp, g_in, wt, PROJ_TOKEN_TILE)
    h_t = _inproj(xn_p, wt, COL_Q, COL_WI, COL_WI // 4, PROJ_TOKEN_TILE, BF16, True)
    gate_p = _inproj(xn_p, wt, GATE_COL0, GATE_WIDTH, GATE_FEATURE_TILE, PROJ_TOKEN_TILE, F32, False)
    oa_p = _prompt_attn(h_t, wi_t, kv_p, ki_p2, batch, seq)
    bs_t = b_spatial[l].T
    (y_p,) = _merge(True, oa_p, gate_p, xp, wpa, wpb, wout, lng, lnb, w_spatial[l], bs_t, gf, MERGE_TOKEN_TILE)

    xs = x_sample.reshape(db, D_MODEL)
    xn_s, kv_s, ki_s2, wi_t_s, _, _ = _inproj_kvi(xs, g_in, wt, db)
    h_s = _inproj(xn_s, wt, COL_Q, COL_WI, 512, db, F32, False)
    gate_s = _inproj(xn_s, wt, GATE_COL0, GATE_WIDTH, GATE_FEATURE_TILE, db, F32, False)
    kvi_s = jnp.concatenate(
        [kv_s, ki_s2, wi_t_s.T, jnp.zeros((db, KVI_WIDTH - KVI_WI - IDX_HEADS), F32)], axis=1)
    q_s = h_s[:, COL_Q:COL_K].astype(BF16).reshape(db, N_HEADS, HEAD_DIM)
    qi_s = h_s[:, COL_QI:COL_WI].astype(BF16).reshape(db, IDX_HEADS, IDX_DIM)
    wi_col = kvi_s[:, KVI_WI:KVI_WI + IDX_HEADS].reshape(db, IDX_HEADS, 1)
    kvi_s16 = jnp.broadcast_to(kvi_s[:, None, :], (db, IDX_HEADS, KVI_WIDTH))
    kvi_s16 = lax.dynamic_update_slice(kvi_s16, wi_col, (0, 0, KVI_WI))
    idx, meta = _sample_select(page_table, qi_s, kvi_s16, cache_k_idx)
    oa_s = _sample_attend(idx.reshape(db, TOPK_MAX), meta[:, 0], page_table, q_s, kvi_s16,
                          cache_k, cache_v).reshape(db, ATTN_WIDTH)
    ws0 = jnp.repeat(w_spatial[l][:, 0, 0], GMLP_GROUP_DIM).reshape(1, GMLP_WIDTH)
    bs0 = jnp.repeat(b_spatial[l][:, 0], GMLP_GROUP_DIM).reshape(1, GMLP_WIDTH)
    y_s, vn_s = _merge(False, oa_s, gate_s, xs, wpa, wpb, wout, lng, lnb, ws0, bs0, gf, db)

    def kv_out(kvi, lead):
        k = kvi[:, KVI_K:KVI_K + KV_WIDTH].reshape((1,) + lead + (N_KV_HEADS, HEAD_DIM))
        v = kvi[:, KVI_V:KVI_V + KV_WIDTH].reshape((1,) + lead + (N_KV_HEADS, HEAD_DIM))
        ki = kvi[:, KVI_KI:KVI_KI + IDX_DIM].reshape((1,) + lead + (IDX_DIM,))
        return k, v, ki

    k_p = k3_p.reshape(1, batch, seq, N_KV_HEADS, HEAD_DIM)
    v_p = v3_p.reshape(1, batch, seq, N_KV_HEADS, HEAD_DIM)
    ki_p = ki_p2.reshape(1, batch, seq, IDX_DIM)
    k_s, v_s, ki_s = kv_out(kvi_s, (db, dseq))
    return (y_p.reshape(batch, seq, D_MODEL), y_s.reshape(db, dseq, D_MODEL),
            k_p, v_p, ki_p, k_s, v_s, ki_s, vn_s.reshape(1, db, dseq, GMLP_WIDTH))
```

```python
import functools

import jax
import jax.numpy as jnp
import numpy as np
from jax import lax
from jax.experimental import pallas as pl
from jax.experimental.pallas import tpu as pltpu

F32 = jnp.float32
BF16 = jnp.bfloat16
I32 = jnp.int32

D_MODEL = 2048
N_HEADS = 8
N_KV_HEADS = 2
HEAD_DIM = 128
Q_PER_KV = N_HEADS // N_KV_HEADS
ATTN_WIDTH = N_HEADS * HEAD_DIM
KV_WIDTH = N_KV_HEADS * HEAD_DIM
IDX_HEADS = 16
IDX_DIM = 128
TOPK_MAX = 256
PAGE_SIZE = 128
GMLP_WIDTH = 1024
GMLP_GROUPS = 8
GMLP_GROUP_DIM = GMLP_WIDTH // GMLP_GROUPS
CHUNK = 128
RMS_EPS = 1e-6
LN_EPS = 1e-5

SUBLANES = 8
COL_Q = 0
COL_K = COL_Q + ATTN_WIDTH
COL_V = COL_K + KV_WIDTH
COL_QI = COL_V + KV_WIDTH
COL_WI = COL_QI + IDX_HEADS * IDX_DIM
COL_KI = COL_WI + IDX_HEADS
GATE_COL0 = COL_KI + IDX_DIM
GATE_WIDTH = 4 * GMLP_WIDTH + 2 * D_MODEL
KVI_K = 0
KVI_V = KVI_K + KV_WIDTH
KVI_KI = KVI_V + KV_WIDTH
KVI_WI = KVI_KI + IDX_DIM
KVI_WIDTH = KVI_WI + 128

Q_BLOCK = 128
KEY_CHUNK = 512
COUNT_ROWS = 64
ONES_ROWS = 16
SIGN_EXP_BITS = 9
SEARCH_ALWAYS_BITS = 20
SEARCH_GROUP_BITS = 4
NEG_SENTINEL = float(np.finfo(np.float32).min)
NEG_ABOVE_SENTINEL = float(np.nextafter(np.float32(NEG_SENTINEL), np.float32(0.0)))
INT_MIN = -(2 ** 31)
F32_MANTISSA_BITS = 23
F32_MANTISSA_MASK = (1 << F32_MANTISSA_BITS) - 1
F32_EXP_MASK = 0xFF
F32_MIN_NORMAL = float(2.0 ** -126)
IDX_W_SCALE = float(IDX_HEADS ** -0.5 * IDX_DIM ** -0.5)
ATTN_SCALE = float(HEAD_DIM ** -0.5)
ATTN_SCALE_LOG2E = float(HEAD_DIM ** -0.5 * np.log2(np.e))
V7X_VMEM_BYTES = 64 * 1024 * 1024
VMEM_LIMIT = V7X_VMEM_BYTES - 8 * 1024 * 1024
PROJ_TOKEN_TILE = 1024
GATE_FEATURE_TILE = 1024
MERGE_TOKEN_TILE = 256


def _dot_nt(a, b):
    return lax.dot_general(a, b, (((1,), (1,)), ((), ())), preferred_element_type=F32)


def _ordered_bits_to_f32(u):
    signed_pow, zero_exp = _sign_exp_parts(u)
    return _with_mantissa(u, signed_pow, zero_exp)


def _sign_exp_parts(u):
    bits = jnp.where(u < 0, u ^ INT_MIN, ~u)
    exp = lax.shift_right_logical(bits, F32_MANTISSA_BITS) & F32_EXP_MASK
    e1 = exp - 1
    p = jnp.full(u.shape, F32_MIN_NORMAL, F32)
    for b in range(7):
        p = jnp.where(((e1 >> b) & 1) == 1, p * float(2.0 ** (2 ** b)), p)
    top = ((e1 >> 7) & 1) == 1
    p = jnp.where(top, p * float(2.0 ** 64), p)
    p = jnp.where(top, p * float(2.0 ** 64), p)
    return jnp.where(bits < 0, -p, p), exp == 0


def _with_mantissa(u, signed_pow, zero_exp):
    low = jnp.where(u < 0, u, ~u) & F32_MANTISSA_MASK
    mant = 1.0 + low.astype(F32) * float(2.0 ** -F32_MANTISSA_BITS)
    return jnp.where(zero_exp, 0.0, signed_pow * mant)


def _gelu(x):
    return 0.5 * x * (1.0 + lax.erf(x * float(np.sqrt(0.5))))


def _silu(x):
    return x * jax.nn.sigmoid(x)


def _inproj_kernel(feature_major, xn_ref, w_ref, o_ref, wb_ref):
    @pl.when(pl.program_id(1) == 0)
    def _():
        wb_ref[...] = w_ref[...].astype(BF16)

    if feature_major:
        o_ref[...] = _dot_nt(wb_ref[...], xn_ref[...]).astype(o_ref.dtype)
    else:
        o_ref[...] = _dot_nt(xn_ref[...], wb_ref[...]).astype(o_ref.dtype)


def _inproj(xn, wt, row_start, n_rows, tn, tm, out_dtype, feature_major):
    rows = xn.shape[0]
    assert n_rows % tn == 0 and rows % tm == 0 and row_start % SUBLANES == 0
    if row_start % tn == 0:
        j0 = row_start // tn
        w_spec = pl.BlockSpec((tn, D_MODEL), lambda j, i: (j0 + j, 0))
    else:
        w_spec = pl.BlockSpec((pl.Element(tn), pl.Element(D_MODEL)), lambda j, i: ((row_start // SUBLANES + j * (tn // SUBLANES)) * SUBLANES, 0))
    if feature_major:
        out_spec = pl.BlockSpec((tn, tm), lambda j, i: (j, i))
        out_shape = (n_rows, rows)
    else:
        out_spec = pl.BlockSpec((tm, tn), lambda j, i: (i, j))
        out_shape = (rows, n_rows)
    return pl.pallas_call(
        functools.partial(_inproj_kernel, feature_major),
        grid=(n_rows // tn, rows // tm),
        in_specs=[pl.BlockSpec((tm, D_MODEL), lambda j, i: (i, 0)), w_spec],
        out_specs=out_spec,
        out_shape=jax.ShapeDtypeStruct(out_shape, out_dtype),
        scratch_shapes=[pltpu.VMEM((tn, D_MODEL), BF16)],
        compiler_params=pltpu.CompilerParams(
            dimension_semantics=("arbitrary", "arbitrary"), vmem_limit_bytes=VMEM_LIMIT),
        name="inproj_t" if feature_major else "inproj",
    )(xn, wt)


def _inproj_kvi_kernel(x_ref, g_ref, wkv_ref, wwi_ref, wki_ref, xn_ref, kv_ref, ki_ref, wit_ref, k3_ref, v3_ref,
                       wkvb_ref, wwib_ref, wkib_ref):
    @pl.when(pl.program_id(0) == 0)
    def _():
        wkvb_ref[...] = wkv_ref[...].astype(BF16)
        wwib_ref[...] = wwi_ref[...].astype(BF16)
        wkib_ref[...] = wki_ref[...].astype(BF16)

    x = x_ref[...]
    ms = jnp.mean(x * x, axis=-1, keepdims=True)
    xn = ((x * lax.rsqrt(ms + RMS_EPS)) * g_ref[...]).astype(BF16)
    xn_ref[...] = xn
    kv = _dot_nt(xn, wkvb_ref[...])
    kv_ref[...] = kv
    for g in range(N_KV_HEADS):
        k3_ref[:, g, :] = kv[:, g * HEAD_DIM:(g + 1) * HEAD_DIM]
        v3_ref[:, g, :] = kv[:, KV_WIDTH + g * HEAD_DIM:KV_WIDTH + (g + 1) * HEAD_DIM]
    ki_ref[...] = _dot_nt(xn, wkib_ref[...])
    wit_ref[...] = _dot_nt(wwib_ref[...], xn)


def _inproj_kvi(x2d, g, wt, tm):
    rows = x2d.shape[0]
    assert rows % tm == 0 and COL_K % (2 * KV_WIDTH) == 0 and COL_WI % IDX_HEADS == 0
    return pl.pallas_call(
        _inproj_kvi_kernel,
        grid=(rows // tm,),
        in_specs=[
            pl.BlockSpec((tm, D_MODEL), lambda i: (i, 0)),
            pl.BlockSpec((1, D_MODEL), lambda i: (0, 0)),
            pl.BlockSpec((2 * KV_WIDTH, D_MODEL), lambda i: (COL_K // (2 * KV_WIDTH), 0)),
            pl.BlockSpec((IDX_HEADS, D_MODEL), lambda i: (COL_WI // IDX_HEADS, 0)),
            pl.BlockSpec((pl.Element(IDX_DIM), pl.Element(D_MODEL)), lambda i: (COL_KI, 0)),
        ],
        out_specs=[
            pl.BlockSpec((tm, D_MODEL), lambda i: (i, 0)),
            pl.BlockSpec((tm, 2 * KV_WIDTH), lambda i: (i, 0)),
            pl.BlockSpec((tm, IDX_DIM), lambda i: (i, 0)),
            pl.BlockSpec((IDX_HEADS, tm), lambda i: (0, i)),
            pl.BlockSpec((tm, N_KV_HEADS, HEAD_DIM), lambda i: (i, 0, 0)),
            pl.BlockSpec((tm, N_KV_HEADS, HEAD_DIM), lambda i: (i, 0, 0)),
        ],
        out_shape=[
            jax.ShapeDtypeStruct((rows, D_MODEL), BF16),
            jax.ShapeDtypeStruct((rows, 2 * KV_WIDTH), F32),
            jax.ShapeDtypeStruct((rows, IDX_DIM), F32),
            jax.ShapeDtypeStruct((IDX_HEADS, rows), F32),
            jax.ShapeDtypeStruct((rows, N_KV_HEADS, HEAD_DIM), F32),
            jax.ShapeDtypeStruct((rows, N_KV_HEADS, HEAD_DIM), F32),
        ],
        scratch_shapes=[
            pltpu.VMEM((2 * KV_WIDTH, D_MODEL), BF16),
            pltpu.VMEM((IDX_HEADS, D_MODEL), BF16),
            pltpu.VMEM((IDX_DIM, D_MODEL), BF16),
        ],
        compiler_params=pltpu.CompilerParams(
            dimension_semantics=("arbitrary",), vmem_limit_bytes=VMEM_LIMIT),
        name="inproj_kvi",
    )(x2d, g, wt, wt, wt)


def _prompt_attn_kernel(qt_ref, qit0_ref, qit1_ref, qit2_ref, qit3_ref, vt_ref, wit_ref, kvf_ref, kif_ref,
                        o_ref, kb_ref, kib_ref, vtc_ref, sc_ref, sa_ref, sb_ref, acc_ref, ml_ref, u_ref, cnt_ref):
    i = pl.program_id(1)
    seq = kvf_ref.shape[0]
    pos_bits = seq.bit_length()
    qit_refs = (qit0_ref, qit1_ref, qit2_ref, qit3_ref)
    heads_per_ref = IDX_HEADS // len(qit_refs)

    @pl.when(i == 0)
    def _():
        kb_ref[...] = kvf_ref[:, 0:KV_WIDTH].astype(BF16)
        kib_ref[...] = kif_ref[...].astype(BF16)
        for c in range(seq // KEY_CHUNK):
            for g in range(N_KV_HEADS):
                vtc_ref[c, g, :HEAD_DIM, :] = vt_ref[g * HEAD_DIM:(g + 1) * HEAD_DIM,
                                                     c * KEY_CHUNK:(c + 1) * KEY_CHUNK]
                vtc_ref[c, g, HEAD_DIM:, :] = jnp.ones((ONES_ROWS, KEY_CHUNK), BF16)

    n_chunks = (i * Q_BLOCK) // KEY_CHUNK + 1
    q_pos = i * Q_BLOCK + lax.broadcasted_iota(I32, (1, Q_BLOCK), 1)
    key0 = lax.broadcasted_iota(I32, (KEY_CHUNK, 1), 0)

    w_t = wit_ref[...] * IDX_W_SCALE
    qit_all = jnp.concatenate(
        [qit_refs[h // heads_per_ref][(h % heads_per_ref) * IDX_DIM:(h % heads_per_ref + 1) * IDX_DIM, :]
         for h in range(IDX_HEADS)], axis=1)

    def score_chunk(c):
        k0 = pl.multiple_of(c * KEY_CHUNK, KEY_CHUNK)
        res = jnp.dot(kib_ref[pl.ds(k0, KEY_CHUNK), :], qit_all, preferred_element_type=F32)
        acc = jnp.zeros((KEY_CHUNK, Q_BLOCK), F32)
        for h in range(IDX_HEADS):
            acc = acc + w_t[h:h + 1, :] * jnp.maximum(res[:, h * Q_BLOCK:(h + 1) * Q_BLOCK], 0.0)
        sc_ref[c] = jnp.where(k0 + key0 <= q_pos, acc, NEG_SENTINEL)

    def score_pair(t, carry):
        score_chunk(2 * t)
        score_chunk(2 * t + 1)
        return carry

    lax.fori_loop(0, n_chunks // 2, score_pair, 0)

    @pl.when(n_chunks % 2 == 1)
    def _():
        score_chunk(n_chunks - 1)

    def count(pred):
        def body(c, cnt):
            ones = jnp.where(pred(sc_ref[c], c), 1.0, 0.0)
            return cnt + ones.reshape(KEY_CHUNK // COUNT_ROWS, COUNT_ROWS, Q_BLOCK).sum(axis=0)
        part = lax.fori_loop(0, n_chunks, body, jnp.zeros((COUNT_ROWS, Q_BLOCK), F32))
        return jnp.sum(part, axis=0, keepdims=True)

    def search(j0, n, u, cnt_u, decode):
        def body(j, carry):
            u, cnt_u = carry
            cand = u | lax.shift_left(jnp.int32(1), 31 - j)
            thr_c = decode(cand)
            cnt = count(lambda blk, c: blk >= thr_c)
            ok = cnt >= float(TOPK_MAX)
            return jnp.where(ok, cand, u), jnp.where(ok, cnt, cnt_u)
        return lax.fori_loop(j0, j0 + n, body, (u, cnt_u))

    u = jnp.zeros((1, Q_BLOCK), I32)
    cnt_u = (u + n_chunks * KEY_CHUNK).astype(F32)
    u, cnt_u = search(0, SIGN_EXP_BITS, u, cnt_u, _ordered_bits_to_f32)
    signed_pow, zero_exp = _sign_exp_parts(u)

    def decode_mantissa(cand):
        return _with_mantissa(cand, signed_pow, zero_exp)

    u, cnt_u = search(SIGN_EXP_BITS, SEARCH_ALWAYS_BITS - SIGN_EXP_BITS, u, cnt_u, decode_mantissa)
    u_ref[...] = u
    cnt_ref[...] = cnt_u
    for j0 in range(SEARCH_ALWAYS_BITS, 32, SEARCH_GROUP_BITS):
        @pl.when(jnp.max(jnp.where(cnt_ref[...] == float(TOPK_MAX), 0.0, 1.0)) > 0.0)
        def _():
            u_g, cnt_g = search(j0, SEARCH_GROUP_BITS, u_ref[...], cnt_ref[...], decode_mantissa)
            u_ref[...] = u_g
            cnt_ref[...] = cnt_g
    u = u_ref[...]
    n_ge = cnt_ref[...]
    thr = decode_mantissa(u)
    thr_adm = jnp.maximum(thr, NEG_ABOVE_SENTINEL)
    excess = jnp.max(jnp.where((n_ge > float(TOPK_MAX)) & (thr > NEG_SENTINEL), 1.0, 0.0)) > 0.0

    @pl.when(excess)
    def _():
        need = float(TOPK_MAX) - count(lambda blk, c: blk > thr)

        def jbit_body(j, jl):
            cand = jl | lax.shift_left(jnp.int32(1), pos_bits - 1 - j)
            cnt = count(lambda blk, c: (blk == thr) & (c * KEY_CHUNK + key0 < cand))
            return jnp.where(cnt <= need, cand, jl)
        jlim = lax.fori_loop(0, pos_bits, jbit_body, jnp.zeros((1, Q_BLOCK), I32))

        def demote_body(c, carry):
            blk = sc_ref[c]
            dropped = (blk == thr) & (thr > NEG_SENTINEL) & (c * KEY_CHUNK + key0 >= jlim)
            sc_ref[c] = jnp.where(dropped, NEG_SENTINEL, blk)
            return carry
        lax.fori_loop(0, n_chunks, demote_body, 0)

    acc_ref[...] = jnp.zeros(acc_ref.shape, F32)
    cols = Q_PER_KV * Q_BLOCK
    qt_groups = [
        jnp.concatenate([qt_ref[(g * Q_PER_KV + h) * HEAD_DIM:(g * Q_PER_KV + h + 1) * HEAD_DIM, :]
                         for h in range(Q_PER_KV)], axis=1)
        for g in range(N_KV_HEADS)
    ]

    last_chunk = seq // KEY_CHUNK - 1

    def qk_scores(c, dst_ref):
        k0 = pl.multiple_of(jnp.minimum(c, last_chunk) * KEY_CHUNK, KEY_CHUNK)
        for g in range(N_KV_HEADS):
            kc = kb_ref[pl.ds(k0, KEY_CHUNK), g * HEAD_DIM:(g + 1) * HEAD_DIM]
            dst_ref[g] = jnp.dot(kc, qt_groups[g], preferred_element_type=F32)

    def softmax_pv(c, src_ref):
        bias1 = jnp.where(sc_ref[c] >= thr_adm, 0.0, -jnp.inf)
        bias = jnp.concatenate([bias1] * Q_PER_KV, axis=1)
        for g in range(N_KV_HEADS):
            m_old = ml_ref[2 * g:2 * g + 1, :]
            l_old = ml_ref[2 * g + 1:2 * g + 2, :]
            s = src_ref[g] * ATTN_SCALE_LOG2E + bias
            m_new = jnp.maximum(m_old, jnp.max(s, axis=0, keepdims=True))
            m_safe = jnp.where(m_new == -jnp.inf, 0.0, m_new)
            alpha = jnp.exp2(m_old - m_safe)
            p = jnp.exp2(s - m_safe).astype(BF16)
            pv = jnp.dot(vtc_ref[c, g], p, preferred_element_type=F32)
            acc_ref[g] = alpha * acc_ref[g] + pv[:HEAD_DIM]
            ml_ref[2 * g:2 * g + 1, :] = m_new
            ml_ref[2 * g + 1:2 * g + 2, :] = alpha * l_old + pv[HEAD_DIM:HEAD_DIM + 1]

    def pair_body(t, carry):
        qk_scores(2 * t + 1, sb_ref)
        softmax_pv(2 * t, sa_ref)
        qk_scores(2 * t + 2, sa_ref)
        softmax_pv(2 * t + 1, sb_ref)
        return carry

    for g in range(N_KV_HEADS):
        ml_ref[2 * g:2 * g + 1, :] = jnp.full((1, cols), -jnp.inf, F32)
        ml_ref[2 * g + 1:2 * g + 2, :] = jnp.zeros((1, cols), F32)
    qk_scores(0, sa_ref)
    lax.fori_loop(0, n_chunks // 2, pair_body, 0)

    @pl.when(n_chunks % 2 == 1)
    def _():
        softmax_pv(n_chunks - 1, sa_ref)

    for g in range(N_KV_HEADS):
        o = acc_ref[g] / ml_ref[2 * g + 1:2 * g + 2, :]
        for h in range(Q_PER_KV):
            hd = (g * Q_PER_KV + h) * HEAD_DIM
            o_ref[:, hd:hd + HEAD_DIM] = o[:, h * Q_BLOCK:(h + 1) * Q_BLOCK].T


def _prompt_attn(h_t, wi_t, kv, ki, batch, seq):
    nb = seq // Q_BLOCK
    cols = Q_PER_KV * Q_BLOCK
    qi_rows = IDX_HEADS * IDX_DIM // 4
    qi_b0 = COL_QI // qi_rows

    def qi_spec(t):
        return pl.BlockSpec((qi_rows, Q_BLOCK), lambda b, i: (qi_b0 + t, b * nb + i))

    return pl.pallas_call(
        _prompt_attn_kernel,
        grid=(batch, nb),
        in_specs=[
            pl.BlockSpec((ATTN_WIDTH, Q_BLOCK), lambda b, i: (COL_Q // ATTN_WIDTH, b * nb + i)),
            qi_spec(0), qi_spec(1), qi_spec(2), qi_spec(3),
            pl.BlockSpec((KV_WIDTH, seq), lambda b, i: (COL_V // KV_WIDTH, b)),
            pl.BlockSpec((IDX_HEADS, Q_BLOCK), lambda b, i: (0, b * nb + i)),
            pl.BlockSpec((seq, 2 * KV_WIDTH), lambda b, i: (b, 0)),
            pl.BlockSpec((seq, IDX_DIM), lambda b, i: (b, 0)),
        ],
        out_specs=pl.BlockSpec((Q_BLOCK, ATTN_WIDTH), lambda b, i: (b * nb + i, 0)),
        out_shape=jax.ShapeDtypeStruct((batch * seq, ATTN_WIDTH), F32),
        scratch_shapes=[
            pltpu.VMEM((seq, KV_WIDTH), BF16),
            pltpu.VMEM((seq, IDX_DIM), BF16),
            pltpu.VMEM((seq // KEY_CHUNK, N_KV_HEADS, HEAD_DIM + ONES_ROWS, KEY_CHUNK), BF16),
            pltpu.VMEM((seq // KEY_CHUNK, KEY_CHUNK, Q_BLOCK), F32),
            pltpu.VMEM((N_KV_HEADS, KEY_CHUNK, cols), F32),
            pltpu.VMEM((N_KV_HEADS, KEY_CHUNK, cols), F32),
            pltpu.VMEM((N_KV_HEADS, HEAD_DIM, cols), F32),
            pltpu.VMEM((2 * N_KV_HEADS, cols), F32),
            pltpu.VMEM((1, Q_BLOCK), I32),
            pltpu.VMEM((1, Q_BLOCK), F32),
        ],
        compiler_params=pltpu.CompilerParams(
            dimension_semantics=("arbitrary", "arbitrary"), vmem_limit_bytes=VMEM_LIMIT),
        name="prompt_attn",
    )(h_t, h_t, h_t, h_t, h_t, h_t, wi_t, kv, ki)


PAGES_PER_STEP = 64


def _sample_select_kernel(pt_ref, qi_ref, kvs_ref, *refs):
    page_refs = refs[:PAGES_PER_STEP]
    idx_ref, meta_ref, sc_ref, snew_ref = refs[PAGES_PER_STEP:]
    b = pl.program_id(0)
    p = pl.program_id(1)
    nb = pl.num_programs(0)
    n_steps = pl.num_programs(1)
    n_pages, db, _ = sc_ref.shape
    pos_bits = (n_pages * PAGE_SIZE).bit_length()

    qi = qi_ref[0]
    w_col = kvs_ref[0, :, KVI_WI:KVI_WI + 1] * IDX_W_SCALE

    def key_scores(keys_bf16):
        s = _dot_nt(qi, keys_bf16)
        return jnp.sum(w_col * jnp.maximum(s, 0.0), axis=0, keepdims=True)

    keys = jnp.concatenate([page_refs[r][0, 0] for r in range(PAGES_PER_STEP)], axis=0).astype(BF16)
    row = key_scores(keys)
    for r in range(PAGES_PER_STEP):
        sc_ref[p * PAGES_PER_STEP + r, pl.ds(b, 1), :] = row[:, r * PAGE_SIZE:(r + 1) * PAGE_SIZE]

    @pl.when(p == n_steps - 1)
    def _():
        ki_new = kvs_ref[0, :, KVI_KI:KVI_KI + IDX_DIM].astype(BF16)
        s_new = key_scores(ki_new)[:, 0:1]
        snew_ref[pl.ds(b, 1), :] = jnp.broadcast_to(s_new, (1, PAGE_SIZE))

    @pl.when((b == nb - 1) & (p == n_steps - 1))
    def _():
        sc = sc_ref[...]
        s_new = snew_ref[:, 0:1]

        def total(x):
            return jnp.sum(jnp.sum(x, axis=0), axis=1, keepdims=True)

        def bit_body(j, u):
            cand = u | lax.shift_left(jnp.int32(1), 31 - j)
            thr = _ordered_bits_to_f32(cand)
            cnt = total(jnp.where(sc >= thr[None], 1.0, 0.0)) + jnp.where(s_new >= thr, 1.0, 0.0)
            return jnp.where(cnt >= float(TOPK_MAX), cand, u)

        u = lax.fori_loop(0, 32, bit_body, jnp.zeros((db, 1), I32))
        thr = _ordered_bits_to_f32(u)
        n_gt = total(jnp.where(sc > thr[None], 1.0, 0.0)) + jnp.where(s_new > thr, 1.0, 0.0)
        need = float(TOPK_MAX) - n_gt
        flat = (lax.broadcasted_iota(I32, sc.shape, 0) * PAGE_SIZE
                + lax.broadcasted_iota(I32, sc.shape, 2))
        tie = sc == thr[None]

        def jbit_body(j, jl):
            cand = jl | lax.shift_left(jnp.int32(1), pos_bits - 1 - j)
            cnt = total(jnp.where(tie & (flat < cand[None]), 1.0, 0.0))
            return jnp.where(cnt <= need, cand, jl)

        jlim = lax.fori_loop(0, pos_bits, jbit_body, jnp.zeros((db, 1), I32))
        sel = (sc > thr[None]) | (tie & (flat < jlim[None]))
        self32 = jnp.where(sel, 1.0, 0.0)
        n_pool = total(self32)

        n_rows = n_pages * db
        selb = self32.astype(BF16).reshape(n_rows, PAGE_SIZE)
        ri = lax.broadcasted_iota(I32, (PAGE_SIZE, PAGE_SIZE), 0)
        ci = lax.broadcasted_iota(I32, (PAGE_SIZE, PAGE_SIZE), 1)
        incl = jnp.dot(selb, jnp.where(ri <= ci, 1.0, 0.0).astype(BF16), preferred_element_type=F32)
        tot = jnp.broadcast_to(incl[:, PAGE_SIZE - 1:PAGE_SIZE], (n_rows, PAGE_SIZE)).astype(BF16)
        rr = lax.broadcasted_iota(I32, (n_rows, n_rows), 0)
        cc = lax.broadcasted_iota(I32, (n_rows, n_rows), 1)
        earlier_page_same_row = ((rr & (db - 1)) == (cc & (db - 1))) & (cc < (rr & -db))
        offs = jnp.dot(jnp.where(earlier_page_same_row, 1.0, 0.0).astype(BF16), tot,
                       preferred_element_type=F32)
        ends_rows = offs + tot.astype(F32)
        r_i = lax.broadcasted_iota(I32, (n_rows, PAGE_SIZE), 0)
        l_i = lax.broadcasted_iota(I32, (n_rows, PAGE_SIZE), 1)
        log2_db = db.bit_length() - 1
        diag = jnp.where(l_i == lax.shift_right_logical(r_i, log2_db), ends_rows, 0.0).astype(BF16)
        pick = jnp.where((lax.broadcasted_iota(I32, (db, n_rows), 1) & (db - 1))
                         == lax.broadcasted_iota(I32, (db, n_rows), 0), 1.0, 0.0).astype(BF16)
        ends = jnp.dot(pick, diag, preferred_element_type=F32)

        table = jnp.concatenate([incl.astype(BF16), offs.astype(BF16)], axis=1)
        slot = lax.broadcasted_iota(I32, (TOPK_MAX, PAGE_SIZE), 0).astype(F32)
        rows_lane = lax.broadcasted_iota(I32, (TOPK_MAX, n_rows), 1)
        for row_i in range(db):
            page_s = jnp.sum(jnp.where(ends[row_i:row_i + 1, :] <= slot, 1.0, 0.0), axis=1, keepdims=True)
            mrow = page_s.astype(I32) * db + row_i
            onehot = jnp.where(rows_lane == mrow, 1.0, 0.0).astype(BF16)
            got = jnp.dot(onehot, table, preferred_element_type=F32)
            k = slot - got[:, PAGE_SIZE:]
            off_s = jnp.sum(jnp.where(got[:, :PAGE_SIZE] <= k, 1.0, 0.0), axis=1, keepdims=True)
            pos = page_s * float(PAGE_SIZE) + off_s
            valid = slot[:, 0:1] < n_pool[row_i:row_i + 1, :]
            idx_ref[row_i] = jnp.where(valid, pos, 0.0).astype(I32)
        meta_ref[...] = jnp.broadcast_to(n_pool, (db, PAGE_SIZE)).astype(I32)


def _sample_select(page_table, qi_s, kvi_s16, pool_ki):
    db, n_pages = page_table.shape
    assert db & (db - 1) == 0, "row index arithmetic in the rank matrix assumes a power-of-two row count"
    assert n_pages == PAGE_SIZE, "page ends are laid out one page per lane"
    n_steps = n_pages // PAGES_PER_STEP

    def page_spec(r):
        return pl.BlockSpec(
            (1, 1, PAGE_SIZE, IDX_DIM),
            lambda b, p, pt: (0, pt[b, p * PAGES_PER_STEP + r], 0, 0))

    grid_spec = pltpu.PrefetchScalarGridSpec(
        num_scalar_prefetch=1,
        grid=(db, n_steps),
        in_specs=[
            pl.BlockSpec((1, IDX_HEADS, IDX_DIM), lambda b, p, pt: (b, 0, 0)),
            pl.BlockSpec((1, IDX_HEADS, KVI_WIDTH), lambda b, p, pt: (b, 0, 0)),
        ] + [page_spec(r) for r in range(PAGES_PER_STEP)],
        out_specs=[
            pl.BlockSpec((db, TOPK_MAX, 1), lambda b, p, pt: (0, 0, 0)),
            pl.BlockSpec((db, PAGE_SIZE), lambda b, p, pt: (0, 0)),
        ],
        scratch_shapes=[
            pltpu.VMEM((n_pages, db, PAGE_SIZE), F32),
            pltpu.VMEM((db, PAGE_SIZE), F32),
        ],
    )
    return pl.pallas_call(
        _sample_select_kernel,
        grid_spec=grid_spec,
        out_shape=[
            jax.ShapeDtypeStruct((db, TOPK_MAX, 1), I32),
            jax.ShapeDtypeStruct((db, PAGE_SIZE), I32),
        ],
        compiler_params=pltpu.CompilerParams(
            dimension_semantics=("arbitrary", "arbitrary"), vmem_limit_bytes=VMEM_LIMIT),
        name="sample_select",
    )(page_table, qi_s, kvi_s16, *([pool_ki] * PAGES_PER_STEP))


def _sample_attend_kernel(idx_ref, npool_ref, pt_ref, q_ref, kvs_ref, pk_ref, pv_ref, o_ref,
                          kbuf, vbuf, kflat, vflat, sem):
    b = pl.program_id(0)

    def row_copies(r):
        ix = idx_ref[b, r]
        page = pt_ref[b, lax.shift_right_logical(ix, 7)]
        off = ix & (PAGE_SIZE - 1)
        return (pltpu.make_async_copy(pk_ref.at[0, page, off], kbuf.at[r], sem.at[0]),
                pltpu.make_async_copy(pv_ref.at[0, page, off], vbuf.at[r], sem.at[1]))

    def start_body(r, carry):
        ck, cv = row_copies(r)
        ck.start()
        cv.start()
        return carry

    def wait_body(r, carry):
        ck, cv = row_copies(r)
        ck.wait()
        cv.wait()
        return carry

    lax.fori_loop(0, TOPK_MAX, start_body, 0, unroll=8)
    lax.fori_loop(0, TOPK_MAX, wait_body, 0, unroll=8)

    def head_copies(g):
        return (pltpu.make_async_copy(kbuf.at[:, g, :], kflat.at[g], sem.at[2]),
                pltpu.make_async_copy(vbuf.at[:, g, :], vflat.at[g], sem.at[3]))

    for g in range(N_KV_HEADS):
        for cp in head_copies(g):
            cp.start()
    for g in range(N_KV_HEADS):
        for cp in head_copies(g):
            cp.wait()

    n_pool = npool_ref[b]
    slot = lax.broadcasted_iota(I32, (1, TOPK_MAX), 1)
    slot_ok = slot < n_pool
    new_ok = n_pool < TOPK_MAX
    q_all = q_ref[0].astype(F32)
    for g in range(N_KV_HEADS):
        qg = q_all[g * Q_PER_KV:(g + 1) * Q_PER_KV, :].astype(BF16)
        ks = kflat[g].astype(BF16)
        vs = vflat[g].astype(BF16)
        k_new = kvs_ref[0, 0:1, KVI_K + g * HEAD_DIM:KVI_K + (g + 1) * HEAD_DIM].astype(BF16)
        v_new = kvs_ref[0, 0:1, KVI_V + g * HEAD_DIM:KVI_V + (g + 1) * HEAD_DIM].astype(BF16)
        s = jnp.where(slot_ok, _dot_nt(qg, ks) * ATTN_SCALE, -jnp.inf)
        s_new = jnp.sum(qg.astype(F32) * k_new.astype(F32), axis=-1, keepdims=True) * ATTN_SCALE
        s_new = jnp.where(new_ok, s_new, -jnp.inf)
        m = jnp.maximum(jnp.max(s, axis=-1, keepdims=True), s_new)
        p = jnp.exp(s - m)
        p_new = jnp.exp(s_new - m)
        denom = jnp.sum(p, axis=-1, keepdims=True) + p_new
        pn = (p / denom).astype(BF16)
        pn_new = (p_new / denom).astype(BF16).astype(F32)
        o = jnp.dot(pn, vs, preferred_element_type=F32) + pn_new * v_new.astype(F32)
        o_ref[0, g * Q_PER_KV:(g + 1) * Q_PER_KV, :] = o


def _sample_attend(idx, n_pool, page_table, q_s, kvi_s16, pool_k, pool_v):
    db = page_table.shape[0]
    grid_spec = pltpu.PrefetchScalarGridSpec(
        num_scalar_prefetch=3,
        grid=(db,),
        in_specs=[
            pl.BlockSpec((1, N_HEADS, HEAD_DIM), lambda b, *_: (b, 0, 0)),
            pl.BlockSpec((1, IDX_HEADS, KVI_WIDTH), lambda b, *_: (b, 0, 0)),
            pl.BlockSpec(memory_space=pl.ANY),
            pl.BlockSpec(memory_space=pl.ANY),
        ],
        out_specs=pl.BlockSpec((1, N_HEADS, HEAD_DIM), lambda b, *_: (b, 0, 0)),
        scratch_shapes=[
            pltpu.VMEM((TOPK_MAX, N_KV_HEADS, HEAD_DIM), F32),
            pltpu.VMEM((TOPK_MAX, N_KV_HEADS, HEAD_DIM), F32),
            pltpu.VMEM((N_KV_HEADS, TOPK_MAX, HEAD_DIM), F32),
            pltpu.VMEM((N_KV_HEADS, TOPK_MAX, HEAD_DIM), F32),
            pltpu.SemaphoreType.DMA((4,)),
        ],
    )
    return pl.pallas_call(
        _sample_attend_kernel,
        grid_spec=grid_spec,
        out_shape=jax.ShapeDtypeStruct((db, N_HEADS, HEAD_DIM), F32),
        compiler_params=pltpu.CompilerParams(dimension_semantics=("arbitrary",)),
        name="sample_attend",
    )(idx, n_pool, page_table, q_s, kvi_s16, pool_k, pool_v)


def _merge_kernel(chunked, oa_ref, za_ref, u_ref, v_ref, zb_ref, ga_ref, gb_ref, x_ref,
                  wpa_ref, wpb_ref, wout_ref, lng_ref, lnb_ref, ws_ref, bs_ref, gf_ref, y_ref, *rest):
    vn_ref, ob_ref = (None, rest[0]) if chunked else rest
    tm = x_ref.shape[0]
    u = _gelu(u_ref[...])
    v = _gelu(v_ref[...])
    mu = jnp.mean(v, axis=-1, keepdims=True)
    vc = v - mu
    vn = (vc * lax.rsqrt(jnp.mean(vc * vc, axis=-1, keepdims=True) + LN_EPS)) * lng_ref[...] + lnb_ref[...]
    if vn_ref is not None:
        vn_ref[...] = vn

    if chunked:
        vnb = vn.astype(BF16)
        ri = lax.broadcasted_iota(I32, (CHUNK, CHUNK), 0)
        ci = lax.broadcasted_iota(I32, (CHUNK, CHUNK), 1)
        for g in range(GMLP_GROUPS):
            wm = jnp.where(ri >= ci, ws_ref[g], 0.0).astype(BF16)
            lo = g * GMLP_GROUP_DIM
            for c in range(tm // CHUNK):
                mixed = jnp.dot(wm, vnb[c * CHUNK:(c + 1) * CHUNK, lo:lo + GMLP_GROUP_DIM],
                                preferred_element_type=F32) + bs_ref[:, g:g + 1]
                ob_ref[c * CHUNK:(c + 1) * CHUNK, lo:lo + GMLP_GROUP_DIM] = (
                    u[c * CHUNK:(c + 1) * CHUNK, lo:lo + GMLP_GROUP_DIM] * mixed)
    else:
        ob_ref[...] = u * (ws_ref[...] * vn + bs_ref[...])

    ha = (oa_ref[...] * _silu(za_ref[...])).astype(BF16)
    hb = (ob_ref[...] * _silu(zb_ref[...])).astype(BF16)
    ya = jnp.dot(ha, wpa_ref[...], preferred_element_type=F32)
    yb = jnp.dot(hb, wpb_ref[...], preferred_element_type=F32)
    mix = jax.nn.sigmoid(ga_ref[...]) * ya + jax.nn.sigmoid(gb_ref[...]) * yb
    out = x_ref[...] + jnp.dot(mix.astype(BF16), wout_ref[...], preferred_element_type=F32)
    ms = jnp.mean(out * out, axis=-1, keepdims=True)
    y_ref[...] = (out * lax.rsqrt(ms + RMS_EPS)) * gf_ref[...]


def _merge(chunked, o_a, gate, x2d, wpa, wpb, wout, lng, lnb, ws, bs, gf, tm):
    rows = x2d.shape[0]
    gw = GMLP_WIDTH

    def const(shape):
        return pl.BlockSpec(shape, lambda i: (0,) * len(shape), pipeline_mode=pl.Buffered(1))

    return pl.pallas_call(
        functools.partial(_merge_kernel, chunked),
        grid=(rows // tm,),
        in_specs=[
            pl.BlockSpec((tm, ATTN_WIDTH), lambda i: (i, 0)),
            pl.BlockSpec((tm, gw), lambda i: (i, 0)),
            pl.BlockSpec((tm, gw), lambda i: (i, 1)),
            pl.BlockSpec((tm, gw), lambda i: (i, 2)),
            pl.BlockSpec((tm, gw), lambda i: (i, 3)),
            pl.BlockSpec((tm, D_MODEL), lambda i: (i, 2)),
            pl.BlockSpec((tm, D_MODEL), lambda i: (i, 3)),
            pl.BlockSpec((tm, D_MODEL), lambda i: (i, 0)),
            const(wpa.shape), const(wpb.shape), const(wout.shape),
            const(lng.shape), const(lnb.shape), const(ws.shape), const(bs.shape), const(gf.shape),
        ],
        out_specs=[pl.BlockSpec((tm, D_MODEL), lambda i: (i, 0))]
        + ([] if chunked else [pl.BlockSpec((tm, gw), lambda i: (i, 0))]),
        out_shape=[jax.ShapeDtypeStruct((rows, D_MODEL), F32)]
        + ([] if chunked else [jax.ShapeDtypeStruct((rows, gw), F32)]),
        scratch_shapes=[pltpu.VMEM((tm, gw), F32)],
        compiler_params=pltpu.CompilerParams(
            dimension_semantics=("arbitrary",), vmem_limit_bytes=VMEM_LIMIT),
        name="merge_prompt" if chunked else "merge_sample",
    )(o_a, gate, gate, gate, gate, gate, gate, x2d, wpa, wpb, wout, lng, lnb, ws, bs, gf)


def kernel(x_prompt, x_sample, cache_k, cache_v, cache_k_idx, page_table, norm_in_g, w_in,
           w_proj_a, w_proj_b, w_out, ln_g, ln_b, w_spatial, b_spatial, norm_f_g):
    depth = w_in.shape[0]
    assert depth == 1, "single trunk layer"
    batch, seq, _ = x_prompt.shape
    db, dseq, _ = x_sample.shape
    assert dseq == 1
    l = 0
    wpa = w_proj_a[l].astype(BF16)
    wpb = w_proj_b[l].astype(BF16)
    wout = w_out[l].astype(BF16)
    g_in = norm_in_g[l].reshape(1, D_MODEL)
    lng = ln_g[l].reshape(1, GMLP_WIDTH)
    lnb = ln_b[l].reshape(1, GMLP_WIDTH)
    gf = norm_f_g.reshape(1, D_MODEL)

    xp = x_prompt.reshape(batch * seq, D_MODEL)
    wt = w_in[l].T
    xn_p, kv_p, ki_p2, wi_t, k3_p, v3_p = _inproj_kvi(xp, g_in, wt, PROJ_TOKEN_TILE)
    h_t = _inproj(xn_p, wt, COL_Q, COL_WI, COL_WI // 4, PROJ_TOKEN_TILE, BF16, True)
    gate_p = _inproj(xn_p, wt, GATE_COL0, GATE_WIDTH, GATE_FEATURE_TILE, PROJ_TOKEN_TILE, F32, False)
    oa_p = _prompt_attn(h_t, wi_t, kv_p, ki_p2, batch, seq)
    bs_t = b_spatial[l].T
    (y_p,) = _merge(True, oa_p, gate_p, xp, wpa, wpb, wout, lng, lnb, w_spatial[l], bs_t, gf, MERGE_TOKEN_TILE)

    xs = x_sample.reshape(db, D_MODEL)
    xn_s, kv_s, ki_s2, wi_t_s, _, _ = _inproj_kvi(xs, g_in, wt, db)
    h_s = _inproj(xn_s, wt, COL_Q, COL_WI, 512, db, F32, False)
    gate_s = _inproj(xn_s, wt, GATE_COL0, GATE_WIDTH, GATE_FEATURE_TILE, db, F32, False)
    kvi_s = jnp.concatenate(
        [kv_s, ki_s2, wi_t_s.T, jnp.zeros((db, KVI_WIDTH - KVI_WI - IDX_HEADS), F32)], axis=1)
    q_s = h_s[:, COL_Q:COL_K].astype(BF16).reshape(db, N_HEADS, HEAD_DIM)
    qi_s = h_s[:, COL_QI:COL_WI].astype(BF16).reshape(db, IDX_HEADS, IDX_DIM)
    wi_col = kvi_s[:, KVI_WI:KVI_WI + IDX_HEADS].reshape(db, IDX_HEADS, 1)
    kvi_s16 = jnp.broadcast_to(kvi_s[:, None, :], (db, IDX_HEADS, KVI_WIDTH))
    kvi_s16 = lax.dynamic_update_slice(kvi_s16, wi_col, (0, 0, KVI_WI))
    idx, meta = _sample_select(page_table, qi_s, kvi_s16, cache_k_idx)
    oa_s = _sample_attend(idx.reshape(db, TOPK_MAX), meta[:, 0], page_table, q_s, kvi_s16,
                          cache_k, cache_v).reshape(db, ATTN_WIDTH)
    ws0 = jnp.repeat(w_spatial[l][:, 0, 0], GMLP_GROUP_DIM).reshape(1, GMLP_WIDTH)
    bs0 = jnp.repeat(b_spatial[l][:, 0], GMLP_GROUP_DIM).reshape(1, GMLP_WIDTH)
    y_s, vn_s = _merge(False, oa_s, gate_s, xs, wpa, wpb, wout, lng, lnb, ws0, bs0, gf, db)

    def kv_out(kvi, lead):
        k = kvi[:, KVI_K:KVI_K + KV_WIDTH].reshape((1,) + lead + (N_KV_HEADS, HEAD_DIM))
        v = kvi[:, KVI_V:KVI_V + KV_WIDTH].reshape((1,) + lead + (N_KV_HEADS, HEAD_DIM))
        ki = kvi[:, KVI_KI:KVI_KI + IDX_DIM].reshape((1,) + lead + (IDX_DIM,))
        return k, v, ki

    k_p = k3_p.reshape(1, batch, seq, N_KV_HEADS, HEAD_DIM)
    v_p = v3_p.reshape(1, batch, seq, N_KV_HEADS, HEAD_DIM)
    ki_p = ki_p2.reshape(1, batch, seq, IDX_DIM)
    k_s, v_s, ki_s = kv_out(kvi_s, (db, dseq))
    return (y_p.reshape(batch, seq, D_MODEL), y_s.reshape(db, dseq, D_MODEL),
            k_p, v_p, ki_p, k_s, v_s, ki_s, vn_s.reshape(1, db, dseq, GMLP_WIDTH))
```

```python
import functools

import jax
import jax.numpy as jnp
import numpy as np
from jax import lax
from jax.experimental import pallas as pl
from jax.experimental.pallas import tpu as pltpu

F32 = jnp.float32
BF16 = jnp.bfloat16
I32 = jnp.int32

D_MODEL = 2048
N_HEADS = 8
N_KV_HEADS = 2
HEAD_DIM = 128
Q_PER_KV = N_HEADS // N_KV_HEADS
ATTN_WIDTH = N_HEADS * HEAD_DIM
KV_WIDTH = N_KV_HEADS * HEAD_DIM
IDX_HEADS = 16
IDX_DIM = 128
TOPK_MAX = 256
PAGE_SIZE = 128
GMLP_WIDTH = 1024
GMLP_GROUPS = 8
GMLP_GROUP_DIM = GMLP_WIDTH // GMLP_GROUPS
CHUNK = 128
RMS_EPS = 1e-6
LN_EPS = 1e-5

SUBLANES = 8
COL_Q = 0
COL_K = COL_Q + ATTN_WIDTH
COL_V = COL_K + KV_WIDTH
COL_QI = COL_V + KV_WIDTH
COL_WI = COL_QI + IDX_HEADS * IDX_DIM
COL_KI = COL_WI + IDX_HEADS
GATE_COL0 = COL_KI + IDX_DIM
GATE_WIDTH = 4 * GMLP_WIDTH + 2 * D_MODEL
KVI_K = 0
KVI_V = KVI_K + KV_WIDTH
KVI_KI = KVI_V + KV_WIDTH
KVI_WI = KVI_KI + IDX_DIM
KVI_WIDTH = KVI_WI + 128

Q_BLOCK = 128
KEY_CHUNK = 512
COUNT_ROWS = 64
ONES_ROWS = 16
SIGN_EXP_BITS = 9
SEARCH_ALWAYS_BITS = 20
SEARCH_GROUP_BITS = 4
NEG_SENTINEL = float(np.finfo(np.float32).min)
NEG_ABOVE_SENTINEL = float(np.nextafter(np.float32(NEG_SENTINEL), np.float32(0.0)))
INT_MIN = -(2 ** 31)
F32_MANTISSA_BITS = 23
F32_MANTISSA_MASK = (1 << F32_MANTISSA_BITS) - 1
F32_EXP_MASK = 0xFF
F32_MIN_NORMAL = float(2.0 ** -126)
IDX_W_SCALE = float(IDX_HEADS ** -0.5 * IDX_DIM ** -0.5)
ATTN_SCALE = float(HEAD_DIM ** -0.5)
ATTN_SCALE_LOG2E = float(HEAD_DIM ** -0.5 * np.log2(np.e))
MASKED_SCORE = float(jnp.finfo(jnp.bfloat16).min)
MASKED_LEVEL = -1e30
V7X_VMEM_BYTES = 64 * 1024 * 1024
VMEM_LIMIT = V7X_VMEM_BYTES - 8 * 1024 * 1024
PROJ_TOKEN_TILE = 1024
GATE_FEATURE_TILE = 1024
MERGE_TOKEN_TILE = 256


def _dot_nt(a, b):
    return lax.dot_general(a, b, (((1,), (1,)), ((), ())), preferred_element_type=F32)


def _ordered_bits_to_f32(u):
    signed_pow, zero_exp = _sign_exp_parts(u)
    return _with_mantissa(u, signed_pow, zero_exp)


def _sign_exp_parts(u):
    bits = jnp.where(u < 0, u ^ INT_MIN, ~u)
    exp = lax.shift_right_logical(bits, F32_MANTISSA_BITS) & F32_EXP_MASK
    e1 = exp - 1
    p = jnp.full(u.shape, F32_MIN_NORMAL, F32)
    for b in range(7):
        p = jnp.where(((e1 >> b) & 1) == 1, p * float(2.0 ** (2 ** b)), p)
    top = ((e1 >> 7) & 1) == 1
    p = jnp.where(top, p * float(2.0 ** 64), p)
    p = jnp.where(top, p * float(2.0 ** 64), p)
    return jnp.where(bits < 0, -p, p), exp == 0


def _with_mantissa(u, signed_pow, zero_exp):
    low = jnp.where(u < 0, u, ~u) & F32_MANTISSA_MASK
    mant = 1.0 + low.astype(F32) * float(2.0 ** -F32_MANTISSA_BITS)
    return jnp.where(zero_exp, 0.0, signed_pow * mant)


def _gelu(x):
    return 0.5 * x * (1.0 + lax.erf(x * float(np.sqrt(0.5))))


def _silu(x):
    return x * jax.nn.sigmoid(x)


def _inproj_kernel(feature_major, xn_ref, w_ref, o_ref, wb_ref):
    @pl.when(pl.program_id(1) == 0)
    def _():
        wb_ref[...] = w_ref[...].astype(BF16)

    if feature_major:
        o_ref[...] = _dot_nt(wb_ref[...], xn_ref[...]).astype(o_ref.dtype)
    else:
        o_ref[...] = _dot_nt(xn_ref[...], wb_ref[...]).astype(o_ref.dtype)


def _inproj(xn, wt, row_start, n_rows, tn, tm, out_dtype, feature_major):
    rows = xn.shape[0]
    assert n_rows % tn == 0 and rows % tm == 0 and row_start % SUBLANES == 0
    if row_start % tn == 0:
        j0 = row_start // tn
        w_spec = pl.BlockSpec((tn, D_MODEL), lambda j, i: (j0 + j, 0))
    else:
        w_spec = pl.BlockSpec((pl.Element(tn), pl.Element(D_MODEL)), lambda j, i: ((row_start // SUBLANES + j * (tn // SUBLANES)) * SUBLANES, 0))
    if feature_major:
        out_spec = pl.BlockSpec((tn, tm), lambda j, i: (j, i))
        out_shape = (n_rows, rows)
    else:
        out_spec = pl.BlockSpec((tm, tn), lambda j, i: (i, j))
        out_shape = (rows, n_rows)
    return pl.pallas_call(
        functools.partial(_inproj_kernel, feature_major),
        grid=(n_rows // tn, rows // tm),
        in_specs=[pl.BlockSpec((tm, D_MODEL), lambda j, i: (i, 0)), w_spec],
        out_specs=out_spec,
        out_shape=jax.ShapeDtypeStruct(out_shape, out_dtype),
        scratch_shapes=[pltpu.VMEM((tn, D_MODEL), BF16)],
        compiler_params=pltpu.CompilerParams(
            dimension_semantics=("arbitrary", "arbitrary"), vmem_limit_bytes=VMEM_LIMIT),
        name="inproj_t" if feature_major else "inproj",
    )(xn, wt)


def _inproj_kvi_kernel(x_ref, g_ref, wkv_ref, wwi_ref, wki_ref, xn_ref, kv_ref, ki_ref, wit_ref, k3_ref, v3_ref,
                       wkvb_ref, wwib_ref, wkib_ref):
    @pl.when(pl.program_id(0) == 0)
    def _():
        wkvb_ref[...] = wkv_ref[...].astype(BF16)
        wwib_ref[...] = wwi_ref[...].astype(BF16)
        wkib_ref[...] = wki_ref[...].astype(BF16)

    x = x_ref[...]
    ms = jnp.mean(x * x, axis=-1, keepdims=True)
    xn = ((x * lax.rsqrt(ms + RMS_EPS)) * g_ref[...]).astype(BF16)
    xn_ref[...] = xn
    kv = _dot_nt(xn, wkvb_ref[...])
    kv_ref[...] = kv
    for g in range(N_KV_HEADS):
        k3_ref[:, g, :] = kv[:, g * HEAD_DIM:(g + 1) * HEAD_DIM]
        v3_ref[:, g, :] = kv[:, KV_WIDTH + g * HEAD_DIM:KV_WIDTH + (g + 1) * HEAD_DIM]
    ki_ref[...] = _dot_nt(xn, wkib_ref[...])
    wit_ref[...] = _dot_nt(wwib_ref[...], xn)


def _inproj_kvi(x2d, g, wt, tm):
    rows = x2d.shape[0]
    assert rows % tm == 0 and COL_K % (2 * KV_WIDTH) == 0 and COL_WI % IDX_HEADS == 0
    return pl.pallas_call(
        _inproj_kvi_kernel,
        grid=(rows // tm,),
        in_specs=[
            pl.BlockSpec((tm, D_MODEL), lambda i: (i, 0)),
            pl.BlockSpec((1, D_MODEL), lambda i: (0, 0)),
            pl.BlockSpec((2 * KV_WIDTH, D_MODEL), lambda i: (COL_K // (2 * KV_WIDTH), 0)),
            pl.BlockSpec((IDX_HEADS, D_MODEL), lambda i: (COL_WI // IDX_HEADS, 0)),
            pl.BlockSpec((pl.Element(IDX_DIM), pl.Element(D_MODEL)), lambda i: (COL_KI, 0)),
        ],
        out_specs=[
            pl.BlockSpec((tm, D_MODEL), lambda i: (i, 0)),
            pl.BlockSpec((tm, 2 * KV_WIDTH), lambda i: (i, 0)),
            pl.BlockSpec((tm, IDX_DIM), lambda i: (i, 0)),
            pl.BlockSpec((IDX_HEADS, tm), lambda i: (0, i)),
            pl.BlockSpec((tm, N_KV_HEADS, HEAD_DIM), lambda i: (i, 0, 0)),
            pl.BlockSpec((tm, N_KV_HEADS, HEAD_DIM), lambda i: (i, 0, 0)),
        ],
        out_shape=[
            jax.ShapeDtypeStruct((rows, D_MODEL), BF16),
            jax.ShapeDtypeStruct((rows, 2 * KV_WIDTH), F32),
            jax.ShapeDtypeStruct((rows, IDX_DIM), F32),
            jax.ShapeDtypeStruct((IDX_HEADS, rows), F32),
            jax.ShapeDtypeStruct((rows, N_KV_HEADS, HEAD_DIM), F32),
            jax.ShapeDtypeStruct((rows, N_KV_HEADS, HEAD_DIM), F32),
        ],
        scratch_shapes=[
            pltpu.VMEM((2 * KV_WIDTH, D_MODEL), BF16),
            pltpu.VMEM((IDX_HEADS, D_MODEL), BF16),
            pltpu.VMEM((IDX_DIM, D_MODEL), BF16),
        ],
        compiler_params=pltpu.CompilerParams(
            dimension_semantics=("arbitrary",), vmem_limit_bytes=VMEM_LIMIT),
        name="inproj_kvi",
    )(x2d, g, wt, wt, wt)


def _prompt_attn_kernel(qt_ref, qit0_ref, qit1_ref, qit2_ref, qit3_ref, vt_ref, wit_ref, kvf_ref, kif_ref,
                        o_ref, kb_ref, kib_ref, vtc_ref, sc_ref, sa_ref, sb_ref, acc_ref, ml_ref, u_ref, cnt_ref):
    i = pl.program_id(1)
    seq = kvf_ref.shape[0]
    pos_bits = seq.bit_length()
    qit_refs = (qit0_ref, qit1_ref, qit2_ref, qit3_ref)
    heads_per_ref = IDX_HEADS // len(qit_refs)

    @pl.when(i == 0)
    def _():
        kb_ref[...] = kvf_ref[:, 0:KV_WIDTH].astype(BF16)
        kib_ref[...] = kif_ref[...].astype(BF16)
        for c in range(seq // KEY_CHUNK):
            for g in range(N_KV_HEADS):
                vtc_ref[c, g, :HEAD_DIM, :] = vt_ref[g * HEAD_DIM:(g + 1) * HEAD_DIM,
                                                     c * KEY_CHUNK:(c + 1) * KEY_CHUNK]
                vtc_ref[c, g, HEAD_DIM:, :] = jnp.ones((ONES_ROWS, KEY_CHUNK), BF16)

    n_chunks = (i * Q_BLOCK) // KEY_CHUNK + 1
    q_pos = i * Q_BLOCK + lax.broadcasted_iota(I32, (1, Q_BLOCK), 1)
    key0 = lax.broadcasted_iota(I32, (KEY_CHUNK, 1), 0)

    w_t = wit_ref[...] * IDX_W_SCALE
    qit_all = jnp.concatenate(
        [qit_refs[h // heads_per_ref][(h % heads_per_ref) * IDX_DIM:(h % heads_per_ref + 1) * IDX_DIM, :]
         for h in range(IDX_HEADS)], axis=1)

    def score_chunk(c):
        k0 = pl.multiple_of(c * KEY_CHUNK, KEY_CHUNK)
        res = jnp.dot(kib_ref[pl.ds(k0, KEY_CHUNK), :], qit_all, preferred_element_type=F32)
        acc = jnp.zeros((KEY_CHUNK, Q_BLOCK), F32)
        for h in range(IDX_HEADS):
            acc = acc + w_t[h:h + 1, :] * jnp.maximum(res[:, h * Q_BLOCK:(h + 1) * Q_BLOCK], 0.0)
        sc_ref[c] = jnp.where(k0 + key0 <= q_pos, acc, NEG_SENTINEL)

    def score_pair(t, carry):
        score_chunk(2 * t)
        score_chunk(2 * t + 1)
        return carry

    lax.fori_loop(0, n_chunks // 2, score_pair, 0)

    @pl.when(n_chunks % 2 == 1)
    def _():
        score_chunk(n_chunks - 1)

    def count(pred):
        def body(c, cnt):
            ones = jnp.where(pred(sc_ref[c], c), 1.0, 0.0)
            return cnt + ones.reshape(KEY_CHUNK // COUNT_ROWS, COUNT_ROWS, Q_BLOCK).sum(axis=0)
        part = lax.fori_loop(0, n_chunks, body, jnp.zeros((COUNT_ROWS, Q_BLOCK), F32))
        return jnp.sum(part, axis=0, keepdims=True)

    def search(j0, n, u, cnt_u, decode):
        def body(j, carry):
            u, cnt_u = carry
            cand = u | lax.shift_left(jnp.int32(1), 31 - j)
            thr_c = decode(cand)
            cnt = count(lambda blk, c: blk >= thr_c)
            ok = cnt >= float(TOPK_MAX)
            return jnp.where(ok, cand, u), jnp.where(ok, cnt, cnt_u)
        return lax.fori_loop(j0, j0 + n, body, (u, cnt_u))

    u = jnp.zeros((1, Q_BLOCK), I32)
    cnt_u = (u + n_chunks * KEY_CHUNK).astype(F32)
    u, cnt_u = search(0, SIGN_EXP_BITS, u, cnt_u, _ordered_bits_to_f32)
    signed_pow, zero_exp = _sign_exp_parts(u)

    def decode_mantissa(cand):
        return _with_mantissa(cand, signed_pow, zero_exp)

    u, cnt_u = search(SIGN_EXP_BITS, SEARCH_ALWAYS_BITS - SIGN_EXP_BITS, u, cnt_u, decode_mantissa)
    u_ref[...] = u
    cnt_ref[...] = cnt_u
    for j0 in range(SEARCH_ALWAYS_BITS, 32, SEARCH_GROUP_BITS):
        @pl.when(jnp.max(jnp.where(cnt_ref[...] == float(TOPK_MAX), 0.0, 1.0)) > 0.0)
        def _():
            u_g, cnt_g = search(j0, SEARCH_GROUP_BITS, u_ref[...], cnt_ref[...], decode_mantissa)
            u_ref[...] = u_g
            cnt_ref[...] = cnt_g
    u = u_ref[...]
    n_ge = cnt_ref[...]
    thr = decode_mantissa(u)
    thr_adm = jnp.maximum(thr, NEG_ABOVE_SENTINEL)
    excess = jnp.max(jnp.where((n_ge > float(TOPK_MAX)) & (thr > NEG_SENTINEL), 1.0, 0.0)) > 0.0

    @pl.when(excess)
    def _():
        need = float(TOPK_MAX) - count(lambda blk, c: blk > thr)

        def jbit_body(j, jl):
            cand = jl | lax.shift_left(jnp.int32(1), pos_bits - 1 - j)
            cnt = count(lambda blk, c: (blk == thr) & (c * KEY_CHUNK + key0 < cand))
            return jnp.where(cnt <= need, cand, jl)
        jlim = lax.fori_loop(0, pos_bits, jbit_body, jnp.zeros((1, Q_BLOCK), I32))

        def demote_body(c, carry):
            blk = sc_ref[c]
            dropped = (blk == thr) & (thr > NEG_SENTINEL) & (c * KEY_CHUNK + key0 >= jlim)
            sc_ref[c] = jnp.where(dropped, NEG_SENTINEL, blk)
            return carry
        lax.fori_loop(0, n_chunks, demote_body, 0)

    acc_ref[...] = jnp.zeros(acc_ref.shape, F32)
    cols = Q_PER_KV * Q_BLOCK
    qt_groups = [
        jnp.concatenate([qt_ref[(g * Q_PER_KV + h) * HEAD_DIM:(g * Q_PER_KV + h + 1) * HEAD_DIM, :]
                         for h in range(Q_PER_KV)], axis=1)
        for g in range(N_KV_HEADS)
    ]

    last_chunk = seq // KEY_CHUNK - 1
    eye = (lax.broadcasted_iota(I32, (Q_BLOCK, Q_BLOCK), 0)
           == lax.broadcasted_iota(I32, (Q_BLOCK, Q_BLOCK), 1))
    eye_heads = jnp.concatenate([jnp.where(eye, 1.0, 0.0).astype(BF16)] * Q_PER_KV, axis=1)
    qt_aug = [jnp.concatenate([qt_groups[g], eye_heads], axis=0) for g in range(N_KV_HEADS)]

    def qk_scores(c, dst_ref):
        cc = jnp.minimum(c, last_chunk)
        k0 = pl.multiple_of(cc * KEY_CHUNK, KEY_CHUNK)
        masked = jnp.where(sc_ref[cc] >= thr_adm, 0.0, MASKED_SCORE).astype(BF16)
        for g in range(N_KV_HEADS):
            kc = kb_ref[pl.ds(k0, KEY_CHUNK), g * HEAD_DIM:(g + 1) * HEAD_DIM]
            dst_ref[g] = jnp.dot(jnp.concatenate([kc, masked], axis=1), qt_aug[g], preferred_element_type=F32)

    def softmax_pv(c, src_ref):
        for g in range(N_KV_HEADS):
            m_old = ml_ref[2 * g:2 * g + 1, :]
            l_old = ml_ref[2 * g + 1:2 * g + 2, :]
            s = src_ref[g] * ATTN_SCALE_LOG2E
            m_new = jnp.maximum(m_old, jnp.max(s, axis=0, keepdims=True))
            m_safe = jnp.where(m_new < MASKED_LEVEL, 0.0, m_new)
            alpha = jnp.exp2(m_old - m_safe)
            p = jnp.exp2(s - m_safe).astype(BF16)
            pv = jnp.dot(vtc_ref[c, g], p, preferred_element_type=F32)
            acc_ref[g] = alpha * acc_ref[g] + pv[:HEAD_DIM]
            ml_ref[2 * g:2 * g + 1, :] = m_new
            ml_ref[2 * g + 1:2 * g + 2, :] = alpha * l_old + pv[HEAD_DIM:HEAD_DIM + 1]

    def pair_body(t, carry):
        qk_scores(2 * t + 1, sb_ref)
        softmax_pv(2 * t, sa_ref)
        qk_scores(2 * t + 2, sa_ref)
        softmax_pv(2 * t + 1, sb_ref)
        return carry

    for g in range(N_KV_HEADS):
        ml_ref[2 * g:2 * g + 1, :] = jnp.full((1, cols), -jnp.inf, F32)
        ml_ref[2 * g + 1:2 * g + 2, :] = jnp.zeros((1, cols), F32)
    qk_scores(0, sa_ref)
    lax.fori_loop(0, n_chunks // 2, pair_body, 0)

    @pl.when(n_chunks % 2 == 1)
    def _():
        softmax_pv(n_chunks - 1, sa_ref)

    for g in range(N_KV_HEADS):
        o = acc_ref[g] / ml_ref[2 * g + 1:2 * g + 2, :]
        for h in range(Q_PER_KV):
            hd = (g * Q_PER_KV + h) * HEAD_DIM
            o_ref[:, hd:hd + HEAD_DIM] = o[:, h * Q_BLOCK:(h + 1) * Q_BLOCK].T


def _prompt_attn(h_t, wi_t, kv, ki, batch, seq):
    nb = seq // Q_BLOCK
    cols = Q_PER_KV * Q_BLOCK
    qi_rows = IDX_HEADS * IDX_DIM // 4
    qi_b0 = COL_QI // qi_rows

    def qi_spec(t):
        return pl.BlockSpec((qi_rows, Q_BLOCK), lambda b, i: (qi_b0 + t, b * nb + i))

    return pl.pallas_call(
        _prompt_attn_kernel,
        grid=(batch, nb),
        in_specs=[
            pl.BlockSpec((ATTN_WIDTH, Q_BLOCK), lambda b, i: (COL_Q // ATTN_WIDTH, b * nb + i)),
            qi_spec(0), qi_spec(1), qi_spec(2), qi_spec(3),
            pl.BlockSpec((KV_WIDTH, seq), lambda b, i: (COL_V // KV_WIDTH, b)),
            pl.BlockSpec((IDX_HEADS, Q_BLOCK), lambda b, i: (0, b * nb + i)),
            pl.BlockSpec((seq, 2 * KV_WIDTH), lambda b, i: (b, 0)),
            pl.BlockSpec((seq, IDX_DIM), lambda b, i: (b, 0)),
        ],
        out_specs=pl.BlockSpec((Q_BLOCK, ATTN_WIDTH), lambda b, i: (b * nb + i, 0)),
        out_shape=jax.ShapeDtypeStruct((batch * seq, ATTN_WIDTH), F32),
        scratch_shapes=[
            pltpu.VMEM((seq, KV_WIDTH), BF16),
            pltpu.VMEM((seq, IDX_DIM), BF16),
            pltpu.VMEM((seq // KEY_CHUNK, N_KV_HEADS, HEAD_DIM + ONES_ROWS, KEY_CHUNK), BF16),
            pltpu.VMEM((seq // KEY_CHUNK, KEY_CHUNK, Q_BLOCK), F32),
            pltpu.VMEM((N_KV_HEADS, KEY_CHUNK, cols), F32),
            pltpu.VMEM((N_KV_HEADS, KEY_CHUNK, cols), F32),
            pltpu.VMEM((N_KV_HEADS, HEAD_DIM, cols), F32),
            pltpu.VMEM((2 * N_KV_HEADS, cols), F32),
            pltpu.VMEM((1, Q_BLOCK), I32),
            pltpu.VMEM((1, Q_BLOCK), F32),
        ],
        compiler_params=pltpu.CompilerParams(
            dimension_semantics=("arbitrary", "arbitrary"), vmem_limit_bytes=VMEM_LIMIT),
        name="prompt_attn",
    )(h_t, h_t, h_t, h_t, h_t, h_t, wi_t, kv, ki)


PAGES_PER_STEP = 64


def _sample_select_kernel(pt_ref, qi_ref, kvs_ref, *refs):
    page_refs = refs[:PAGES_PER_STEP]
    idx_ref, meta_ref, sc_ref, snew_ref = refs[PAGES_PER_STEP:]
    b = pl.program_id(0)
    p = pl.program_id(1)
    nb = pl.num_programs(0)
    n_steps = pl.num_programs(1)
    n_pages, db, _ = sc_ref.shape
    pos_bits = (n_pages * PAGE_SIZE).bit_length()

    qi = qi_ref[0]
    w_col = kvs_ref[0, :, KVI_WI:KVI_WI + 1] * IDX_W_SCALE

    def key_scores(keys_bf16):
        s = _dot_nt(qi, keys_bf16)
        return jnp.sum(w_col * jnp.maximum(s, 0.0), axis=0, keepdims=True)

    keys = jnp.concatenate([page_refs[r][0, 0] for r in range(PAGES_PER_STEP)], axis=0).astype(BF16)
    row = key_scores(keys)
    for r in range(PAGES_PER_STEP):
        sc_ref[p * PAGES_PER_STEP + r, pl.ds(b, 1), :] = row[:, r * PAGE_SIZE:(r + 1) * PAGE_SIZE]

    @pl.when(p == n_steps - 1)
    def _():
        ki_new = kvs_ref[0, :, KVI_KI:KVI_KI + IDX_DIM].astype(BF16)
        s_new = key_scores(ki_new)[:, 0:1]
        snew_ref[pl.ds(b, 1), :] = jnp.broadcast_to(s_new, (1, PAGE_SIZE))

    @pl.when((b == nb - 1) & (p == n_steps - 1))
    def _():
        sc = sc_ref[...]
        s_new = snew_ref[:, 0:1]

        def total(x):
            return jnp.sum(jnp.sum(x, axis=0), axis=1, keepdims=True)

        def bit_body(j, u):
            cand = u | lax.shift_left(jnp.int32(1), 31 - j)
            thr = _ordered_bits_to_f32(cand)
            cnt = total(jnp.where(sc >= thr[None], 1.0, 0.0)) + jnp.where(s_new >= thr, 1.0, 0.0)
            return jnp.where(cnt >= float(TOPK_MAX), cand, u)

        u = lax.fori_loop(0, 32, bit_body, jnp.zeros((db, 1), I32))
        thr = _ordered_bits_to_f32(u)
        n_gt = total(jnp.where(sc > thr[None], 1.0, 0.0)) + jnp.where(s_new > thr, 1.0, 0.0)
        need = float(TOPK_MAX) - n_gt
        flat = (lax.broadcasted_iota(I32, sc.shape, 0) * PAGE_SIZE
                + lax.broadcasted_iota(I32, sc.shape, 2))
        tie = sc == thr[None]

        def jbit_body(j, jl):
            cand = jl | lax.shift_left(jnp.int32(1), pos_bits - 1 - j)
            cnt = total(jnp.where(tie & (flat < cand[None]), 1.0, 0.0))
            return jnp.where(cnt <= need, cand, jl)

        jlim = lax.fori_loop(0, pos_bits, jbit_body, jnp.zeros((db, 1), I32))
        sel = (sc > thr[None]) | (tie & (flat < jlim[None]))
        self32 = jnp.where(sel, 1.0, 0.0)
        n_pool = total(self32)

        n_rows = n_pages * db
        selb = self32.astype(BF16).reshape(n_rows, PAGE_SIZE)
        ri = lax.broadcasted_iota(I32, (PAGE_SIZE, PAGE_SIZE), 0)
        ci = lax.broadcasted_iota(I32, (PAGE_SIZE, PAGE_SIZE), 1)
        incl = jnp.dot(selb, jnp.where(ri <= ci, 1.0, 0.0).astype(BF16), preferred_element_type=F32)
        tot = jnp.broadcast_to(incl[:, PAGE_SIZE - 1:PAGE_SIZE], (n_rows, PAGE_SIZE)).astype(BF16)
        rr = lax.broadcasted_iota(I32, (n_rows, n_rows), 0)
        cc = lax.broadcasted_iota(I32, (n_rows, n_rows), 1)
        earlier_page_same_row = ((rr & (db - 1)) == (cc & (db - 1))) & (cc < (rr & -db))
        offs = jnp.dot(jnp.where(earlier_page_same_row, 1.0, 0.0).astype(BF16), tot,
                       preferred_element_type=F32)
        ends_rows = offs + tot.astype(F32)
        r_i = lax.broadcasted_iota(I32, (n_rows, PAGE_SIZE), 0)
        l_i = lax.broadcasted_iota(I32, (n_rows, PAGE_SIZE), 1)
        log2_db = db.bit_length() - 1
        diag = jnp.where(l_i == lax.shift_right_logical(r_i, log2_db), ends_rows, 0.0).astype(BF16)
        pick = jnp.where((lax.broadcasted_iota(I32, (db, n_rows), 1) & (db - 1))
                         == lax.broadcasted_iota(I32, (db, n_rows), 0), 1.0, 0.0).astype(BF16)
        ends = jnp.dot(pick, diag, preferred_element_type=F32)

        table = jnp.concatenate([incl.astype(BF16), offs.astype(BF16)], axis=1)
        slot = lax.broadcasted_iota(I32, (TOPK_MAX, PAGE_SIZE), 0).astype(F32)
        rows_lane = lax.broadcasted_iota(I32, (TOPK_MAX, n_rows), 1)
        for row_i in range(db):
            page_s = jnp.sum(jnp.where(ends[row_i:row_i + 1, :] <= slot, 1.0, 0.0), axis=1, keepdims=True)
            mrow = page_s.astype(I32) * db + row_i
            onehot = jnp.where(rows_lane == mrow, 1.0, 0.0).astype(BF16)
            got = jnp.dot(onehot, table, preferred_element_type=F32)
            k = slot - got[:, PAGE_SIZE:]
            off_s = jnp.sum(jnp.where(got[:, :PAGE_SIZE] <= k, 1.0, 0.0), axis=1, keepdims=True)
            pos = page_s * float(PAGE_SIZE) + off_s
            valid = slot[:, 0:1] < n_pool[row_i:row_i + 1, :]
            idx_ref[row_i] = jnp.where(valid, pos, 0.0).astype(I32)
        meta_ref[...] = jnp.broadcast_to(n_pool, (db, PAGE_SIZE)).astype(I32)


def _sample_select(page_table, qi_s, kvi_s16, pool_ki):
    db, n_pages = page_table.shape
    assert db & (db - 1) == 0, "row index arithmetic in the rank matrix assumes a power-of-two row count"
    assert n_pages == PAGE_SIZE, "page ends are laid out one page per lane"
    n_steps = n_pages // PAGES_PER_STEP

    def page_spec(r):
        return pl.BlockSpec(
            (1, 1, PAGE_SIZE, IDX_DIM),
            lambda b, p, pt: (0, pt[b, p * PAGES_PER_STEP + r], 0, 0))

    grid_spec = pltpu.PrefetchScalarGridSpec(
        num_scalar_prefetch=1,
        grid=(db, n_steps),
        in_specs=[
            pl.BlockSpec((1, IDX_HEADS, IDX_DIM), lambda b, p, pt: (b, 0, 0)),
            pl.BlockSpec((1, IDX_HEADS, KVI_WIDTH), lambda b, p, pt: (b, 0, 0)),
        ] + [page_spec(r) for r in range(PAGES_PER_STEP)],
        out_specs=[
            pl.BlockSpec((db, TOPK_MAX, 1), lambda b, p, pt: (0, 0, 0)),
            pl.BlockSpec((db, PAGE_SIZE), lambda b, p, pt: (0, 0)),
        ],
        scratch_shapes=[
            pltpu.VMEM((n_pages, db, PAGE_SIZE), F32),
            pltpu.VMEM((db, PAGE_SIZE), F32),
        ],
    )
    return pl.pallas_call(
        _sample_select_kernel,
        grid_spec=grid_spec,
        out_shape=[
            jax.ShapeDtypeStruct((db, TOPK_MAX, 1), I32),
            jax.ShapeDtypeStruct((db, PAGE_SIZE), I32),
        ],
        compiler_params=pltpu.CompilerParams(
            dimension_semantics=("arbitrary", "arbitrary"), vmem_limit_bytes=VMEM_LIMIT),
        name="sample_select",
    )(page_table, qi_s, kvi_s16, *([pool_ki] * PAGES_PER_STEP))


def _sample_attend_kernel(idx_ref, npool_ref, pt_ref, q_ref, kvs_ref, pk_ref, pv_ref, o_ref,
                          kbuf, vbuf, kflat, vflat, sem):
    b = pl.program_id(0)

    def row_copies(r):
        ix = idx_ref[b, r]
        page = pt_ref[b, lax.shift_right_logical(ix, 7)]
        off = ix & (PAGE_SIZE - 1)
        return (pltpu.make_async_copy(pk_ref.at[0, page, off], kbuf.at[r], sem.at[0]),
                pltpu.make_async_copy(pv_ref.at[0, page, off], vbuf.at[r], sem.at[1]))

    def start_body(r, carry):
        ck, cv = row_copies(r)
        ck.start()
        cv.start()
        return carry

    def wait_body(r, carry):
        ck, cv = row_copies(r)
        ck.wait()
        cv.wait()
        return carry

    lax.fori_loop(0, TOPK_MAX, start_body, 0, unroll=8)
    lax.fori_loop(0, TOPK_MAX, wait_body, 0, unroll=8)

    def head_copies(g):
        return (pltpu.make_async_copy(kbuf.at[:, g, :], kflat.at[g], sem.at[2]),
                pltpu.make_async_copy(vbuf.at[:, g, :], vflat.at[g], sem.at[3]))

    for g in range(N_KV_HEADS):
        for cp in head_copies(g):
            cp.start()
    for g in range(N_KV_HEADS):
        for cp in head_copies(g):
            cp.wait()

    n_pool = npool_ref[b]
    slot = lax.broadcasted_iota(I32, (1, TOPK_MAX), 1)
    slot_ok = slot < n_pool
    new_ok = n_pool < TOPK_MAX
    q_all = q_ref[0].astype(F32)
    for g in range(N_KV_HEADS):
        qg = q_all[g * Q_PER_KV:(g + 1) * Q_PER_KV, :].astype(BF16)
        ks = kflat[g].astype(BF16)
        vs = vflat[g].astype(BF16)
        k_new = kvs_ref[0, 0:1, KVI_K + g * HEAD_DIM:KVI_K + (g + 1) * HEAD_DIM].astype(BF16)
        v_new = kvs_ref[0, 0:1, KVI_V + g * HEAD_DIM:KVI_V + (g + 1) * HEAD_DIM].astype(BF16)
        s = jnp.where(slot_ok, _dot_nt(qg, ks) * ATTN_SCALE, -jnp.inf)
        s_new = jnp.sum(qg.astype(F32) * k_new.astype(F32), axis=-1, keepdims=True) * ATTN_SCALE
        s_new = jnp.where(new_ok, s_new, -jnp.inf)
        m = jnp.maximum(jnp.max(s, axis=-1, keepdims=True), s_new)
        p = jnp.exp(s - m)
        p_new = jnp.exp(s_new - m)
        denom = jnp.sum(p, axis=-1, keepdims=True) + p_new
        pn = (p / denom).astype(BF16)
        pn_new = (p_new / denom).astype(BF16).astype(F32)
        o = jnp.dot(pn, vs, preferred_element_type=F32) + pn_new * v_new.astype(F32)
        o_ref[0, g * Q_PER_KV:(g + 1) * Q_PER_KV, :] = o


def _sample_attend(idx, n_pool, page_table, q_s, kvi_s16, pool_k, pool_v):
    db = page_table.shape[0]
    grid_spec = pltpu.PrefetchScalarGridSpec(
        num_scalar_prefetch=3,
        grid=(db,),
        in_specs=[
            pl.BlockSpec((1, N_HEADS, HEAD_DIM), lambda b, *_: (b, 0, 0)),
            pl.BlockSpec((1, IDX_HEADS, KVI_WIDTH), lambda b, *_: (b, 0, 0)),
            pl.BlockSpec(memory_space=pl.ANY),
            pl.BlockSpec(memory_space=pl.ANY),
        ],
        out_specs=pl.BlockSpec((1, N_HEADS, HEAD_DIM), lambda b, *_: (b, 0, 0)),
        scratch_shapes=[
            pltpu.VMEM((TOPK_MAX, N_KV_HEADS, HEAD_DIM), F32),
            pltpu.VMEM((TOPK_MAX, N_KV_HEADS, HEAD_DIM), F32),
            pltpu.VMEM((N_KV_HEADS, TOPK_MAX, HEAD_DIM), F32),
            pltpu.VMEM((N_KV_HEADS, TOPK_MAX, HEAD_DIM), F32),
            pltpu.SemaphoreType.DMA((4,)),
        ],
    )
    return pl.pallas_call(
        _sample_attend_kernel,
        grid_spec=grid_spec,
        out_shape=jax.ShapeDtypeStruct((db, N_HEADS, HEAD_DIM), F32),
        compiler_params=pltpu.CompilerParams(dimension_semantics=("arbitrary",)),
        name="sample_attend",
    )(idx, n_pool, page_table, q_s, kvi_s16, pool_k, pool_v)


def _merge_kernel(chunked, oa_ref, za_ref, u_ref, v_ref, zb_ref, ga_ref, gb_ref, x_ref,
                  wpa_ref, wpb_ref, wout_ref, lng_ref, lnb_ref, ws_ref, bs_ref, gf_ref, y_ref, *rest):
    vn_ref, ob_ref = (None, rest[0]) if chunked else rest
    tm = x_ref.shape[0]
    u = _gelu(u_ref[...])
    v = _gelu(v_ref[...])
    mu = jnp.mean(v, axis=-1, keepdims=True)
    vc = v - mu
    vn = (vc * lax.rsqrt(jnp.mean(vc * vc, axis=-1, keepdims=True) + LN_EPS)) * lng_ref[...] + lnb_ref[...]
    if vn_ref is not None:
        vn_ref[...] = vn

    if chunked:
        vnb = vn.astype(BF16)
        ri = lax.broadcasted_iota(I32, (CHUNK, CHUNK), 0)
        ci = lax.broadcasted_iota(I32, (CHUNK, CHUNK), 1)
        for g in range(GMLP_GROUPS):
            wm = jnp.where(ri >= ci, ws_ref[g], 0.0).astype(BF16)
            lo = g * GMLP_GROUP_DIM
            for c in range(tm // CHUNK):
                mixed = jnp.dot(wm, vnb[c * CHUNK:(c + 1) * CHUNK, lo:lo + GMLP_GROUP_DIM],
                                preferred_element_type=F32) + bs_ref[:, g:g + 1]
                ob_ref[c * CHUNK:(c + 1) * CHUNK, lo:lo + GMLP_GROUP_DIM] = (
                    u[c * CHUNK:(c + 1) * CHUNK, lo:lo + GMLP_GROUP_DIM] * mixed)
    else:
        ob_ref[...] = u * (ws_ref[...] * vn + bs_ref[...])

    ha = (oa_ref[...] * _silu(za_ref[...])).astype(BF16)
    hb = (ob_ref[...] * _silu(zb_ref[...])).astype(BF16)
    ya = jnp.dot(ha, wpa_ref[...], preferred_element_type=F32)
    yb = jnp.dot(hb, wpb_ref[...], preferred_element_type=F32)
    mix = jax.nn.sigmoid(ga_ref[...]) * ya + jax.nn.sigmoid(gb_ref[...]) * yb
    out = x_ref[...] + jnp.dot(mix.astype(BF16), wout_ref[...], preferred_element_type=F32)
    ms = jnp.mean(out * out, axis=-1, keepdims=True)
    y_ref[...] = (out * lax.rsqrt(ms + RMS_EPS)) * gf_ref[...]


def _merge(chunked, o_a, gate, x2d, wpa, wpb, wout, lng, lnb, ws, bs, gf, tm):
    rows = x2d.shape[0]
    gw = GMLP_WIDTH

    def const(shape):
        return pl.BlockSpec(shape, lambda i: (0,) * len(shape), pipeline_mode=pl.Buffered(1))

    return pl.pallas_call(
        functools.partial(_merge_kernel, chunked),
        grid=(rows // tm,),
        in_specs=[
            pl.BlockSpec((tm, ATTN_WIDTH), lambda i: (i, 0)),
            pl.BlockSpec((tm, gw), lambda i: (i, 0)),
            pl.BlockSpec((tm, gw), lambda i: (i, 1)),
            pl.BlockSpec((tm, gw), lambda i: (i, 2)),
            pl.BlockSpec((tm, gw), lambda i: (i, 3)),
            pl.BlockSpec((tm, D_MODEL), lambda i: (i, 2)),
            pl.BlockSpec((tm, D_MODEL), lambda i: (i, 3)),
            pl.BlockSpec((tm, D_MODEL), lambda i: (i, 0)),
            const(wpa.shape), const(wpb.shape), const(wout.shape),
            const(lng.shape), const(lnb.shape), const(ws.shape), const(bs.shape), const(gf.shape),
        ],
        out_specs=[pl.BlockSpec((tm, D_MODEL), lambda i: (i, 0))]
        + ([] if chunked else [pl.BlockSpec((tm, gw), lambda i: (i, 0))]),
        out_shape=[jax.ShapeDtypeStruct((rows, D_MODEL), F32)]
        + ([] if chunked else [jax.ShapeDtypeStruct((rows, gw), F32)]),
        scratch_shapes=[pltpu.VMEM((tm, gw), F32)],
        compiler_params=pltpu.CompilerParams(
            dimension_semantics=("arbitrary",), vmem_limit_bytes=VMEM_LIMIT),
        name="merge_prompt" if chunked else "merge_sample",
    )(o_a, gate, gate, gate, gate, gate, gate, x2d, wpa, wpb, wout, lng, lnb, ws, bs, gf)


def kernel(x_prompt, x_sample, cache_k, cache_v, cache_k_idx, page_table, norm_in_g, w_in,
           w_proj_a, w_proj_b, w_out, ln_g, ln_b, w_spatial, b_spatial, norm_f_g):
    depth = w_in.shape[0]
    assert depth == 1, "single trunk layer"
    batch, seq, _ = x_prompt.shape
    db, dseq, _ = x_sample.shape
    assert dseq == 1
    l = 0
    wpa = w_proj_a[l].astype(BF16)
    wpb = w_proj_b[l].astype(BF16)
    wout = w_out[l].astype(BF16)
    g_in = norm_in_g[l].reshape(1, D_MODEL)
    lng = ln_g[l].reshape(1, GMLP_WIDTH)
    lnb = ln_b[l].reshape(1, GMLP_WIDTH)
    gf = norm_f_g.reshape(1, D_MODEL)

    xp = x_prompt.reshape(batch * seq, D_MODEL)
    wt = w_in[l].T
    xn_p, kv_p, ki_p2, wi_t, k3_p, v3_p = _inproj_kvi(xp, g_in, wt, PROJ_TOKEN_TILE)
    h_t = _inproj(xn_p, wt, COL_Q, COL_WI, COL_WI // 4, PROJ_TOKEN_TILE, BF16, True)
    gate_p = _inproj(xn_p, wt, GATE_COL0, GATE_WIDTH, GATE_FEATURE_TILE, PROJ_TOKEN_TILE, F32, False)
    oa_p = _prompt_attn(h_t, wi_t, kv_p, ki_p2, batch, seq)
    bs_t = b_spatial[l].T
    (y_p,) = _merge(True, oa_p, gate_p, xp, wpa, wpb, wout, lng, lnb, w_spatial[l], bs_t, gf, MERGE_TOKEN_TILE)

    xs = x_sample.reshape(db, D_MODEL)
    xn_s, kv_s, ki_s2, wi_t_s, _, _ = _inproj_kvi(xs, g_in, wt, db)
    h_s = _inproj(xn_s, wt, COL_Q, COL_WI, 512, db, F32, False)
    gate_s = _inproj(xn_s, wt, GATE_COL0, GATE_WIDTH, GATE_FEATURE_TILE, db, F32, False)
    kvi_s = jnp.concatenate(
        [kv_s, ki_s2, wi_t_s.T, jnp.zeros((db, KVI_WIDTH - KVI_WI - IDX_HEADS), F32)], axis=1)
    q_s = h_s[:, COL_Q:COL_K].astype(BF16).reshape(db, N_HEADS, HEAD_DIM)
    qi_s = h_s[:, COL_QI:COL_WI].astype(BF16).reshape(db, IDX_HEADS, IDX_DIM)
    wi_col = kvi_s[:, KVI_WI:KVI_WI + IDX_HEADS].reshape(db, IDX_HEADS, 1)
    kvi_s16 = jnp.broadcast_to(kvi_s[:, None, :], (db, IDX_HEADS, KVI_WIDTH))
    kvi_s16 = lax.dynamic_update_slice(kvi_s16, wi_col, (0, 0, KVI_WI))
    idx, meta = _sample_select(page_table, qi_s, kvi_s16, cache_k_idx)
    oa_s = _sample_attend(idx.reshape(db, TOPK_MAX), meta[:, 0], page_table, q_s, kvi_s16,
                          cache_k, cache_v).reshape(db, ATTN_WIDTH)
    ws0 = jnp.repeat(w_spatial[l][:, 0, 0], GMLP_GROUP_DIM).reshape(1, GMLP_WIDTH)
    bs0 = jnp.repeat(b_spatial[l][:, 0], GMLP_GROUP_DIM).reshape(1, GMLP_WIDTH)
    y_s, vn_s = _merge(False, oa_s, gate_s, xs, wpa, wpb, wout, lng, lnb, ws0, bs0, gf, db)

    def kv_out(kvi, lead):
        k = kvi[:, KVI_K:KVI_K + KV_WIDTH].reshape((1,) + lead + (N_KV_HEADS, HEAD_DIM))
        v = kvi[:, KVI_V:KVI_V + KV_WIDTH].reshape((1,) + lead + (N_KV_HEADS, HEAD_DIM))
        ki = kvi[:, KVI_KI:KVI_KI + IDX_DIM].reshape((1,) + lead + (IDX_DIM,))
        return k, v, ki

    k_p = k3_p.reshape(1, batch, seq, N_KV_HEADS, HEAD_DIM)
    v_p = v3_p.reshape(1, batch, seq, N_KV_HEADS, HEAD_DIM)
    ki_p = ki_p2.reshape(1, batch, seq, IDX_DIM)
    k_s, v_s, ki_s = kv_out(kvi_s, (db, dseq))
    return (y_p.reshape(batch, seq, D_MODEL), y_s.reshape(db, dseq, D_MODEL),
            k_p, v_p, ki_p, k_s, v_s, ki_s, vn_s.reshape(1, db, dseq, GMLP_WIDTH))
```

```python
import functools

import jax
import jax.numpy as jnp
import numpy as np
from jax import lax
from jax.experimental import pallas as pl
from jax.experimental.pallas import tpu as pltpu

F32 = jnp.float32
BF16 = jnp.bfloat16
I32 = jnp.int32

D_MODEL = 2048
N_HEADS = 8
N_KV_HEADS = 2
HEAD_DIM = 128
Q_PER_KV = N_HEADS // N_KV_HEADS
ATTN_WIDTH = N_HEADS * HEAD_DIM
KV_WIDTH = N_KV_HEADS * HEAD_DIM
IDX_HEADS = 16
IDX_DIM = 128
TOPK_MAX = 256
PAGE_SIZE = 128
GMLP_WIDTH = 1024
GMLP_GROUPS = 8
GMLP_GROUP_DIM = GMLP_WIDTH // GMLP_GROUPS
CHUNK = 128
RMS_EPS = 1e-6
LN_EPS = 1e-5

SUBLANES = 8
COL_Q = 0
COL_K = COL_Q + ATTN_WIDTH
COL_V = COL_K + KV_WIDTH
COL_QI = COL_V + KV_WIDTH
COL_WI = COL_QI + IDX_HEADS * IDX_DIM
COL_KI = COL_WI + IDX_HEADS
GATE_COL0 = COL_KI + IDX_DIM
GATE_WIDTH = 4 * GMLP_WIDTH + 2 * D_MODEL
KVI_K = 0
KVI_V = KVI_K + KV_WIDTH
KVI_KI = KVI_V + KV_WIDTH
KVI_WI = KVI_KI + IDX_DIM
KVI_WIDTH = KVI_WI + 128

Q_BLOCK = 128
KEY_CHUNK = 512
COUNT_ROWS = 32
ONES_ROWS = 16
SIGN_EXP_BITS = 9
SEARCH_ALWAYS_BITS = 20
SEARCH_GROUP_BITS = 4
NEG_SENTINEL = float(np.finfo(np.float32).min)
NEG_ABOVE_SENTINEL = float(np.nextafter(np.float32(NEG_SENTINEL), np.float32(0.0)))
INT_MIN = -(2 ** 31)
F32_MANTISSA_BITS = 23
F32_MANTISSA_MASK = (1 << F32_MANTISSA_BITS) - 1
F32_EXP_MASK = 0xFF
F32_MIN_NORMAL = float(2.0 ** -126)
IDX_W_SCALE = float(IDX_HEADS ** -0.5 * IDX_DIM ** -0.5)
ATTN_SCALE = float(HEAD_DIM ** -0.5)
ATTN_SCALE_LOG2E = float(HEAD_DIM ** -0.5 * np.log2(np.e))
MASKED_SCORE = float(jnp.finfo(jnp.bfloat16).min)
MASKED_LEVEL = -1e30
V7X_VMEM_BYTES = 64 * 1024 * 1024
VMEM_LIMIT = V7X_VMEM_BYTES - 8 * 1024 * 1024
PROJ_TOKEN_TILE = 1024
GATE_FEATURE_TILE = 1024
MERGE_TOKEN_TILE = 256


def _dot_nt(a, b):
    return lax.dot_general(a, b, (((1,), (1,)), ((), ())), preferred_element_type=F32)


def _ordered_bits_to_f32(u):
    signed_pow, zero_exp = _sign_exp_parts(u)
    return _with_mantissa(u, signed_pow, zero_exp)


def _sign_exp_parts(u):
    bits = jnp.where(u < 0, u ^ INT_MIN, ~u)
    exp = lax.shift_right_logical(bits, F32_MANTISSA_BITS) & F32_EXP_MASK
    e1 = exp - 1
    p = jnp.full(u.shape, F32_MIN_NORMAL, F32)
    for b in range(7):
        p = jnp.where(((e1 >> b) & 1) == 1, p * float(2.0 ** (2 ** b)), p)
    top = ((e1 >> 7) & 1) == 1
    p = jnp.where(top, p * float(2.0 ** 64), p)
    p = jnp.where(top, p * float(2.0 ** 64), p)
    return jnp.where(bits < 0, -p, p), exp == 0


def _with_mantissa(u, signed_pow, zero_exp):
    low = jnp.where(u < 0, u, ~u) & F32_MANTISSA_MASK
    mant = 1.0 + low.astype(F32) * float(2.0 ** -F32_MANTISSA_BITS)
    return jnp.where(zero_exp, 0.0, signed_pow * mant)


def _gelu(x):
    return 0.5 * x * (1.0 + lax.erf(x * float(np.sqrt(0.5))))


def _silu(x):
    return x * jax.nn.sigmoid(x)


def _inproj_kernel(feature_major, xn_ref, w_ref, o_ref, wb_ref):
    @pl.when(pl.program_id(1) == 0)
    def _():
        wb_ref[...] = w_ref[...].astype(BF16)

    if feature_major:
        o_ref[...] = _dot_nt(wb_ref[...], xn_ref[...]).astype(o_ref.dtype)
    else:
        o_ref[...] = _dot_nt(xn_ref[...], wb_ref[...]).astype(o_ref.dtype)


def _inproj(xn, wt, row_start, n_rows, tn, tm, out_dtype, feature_major):
    rows = xn.shape[0]
    assert n_rows % tn == 0 and rows % tm == 0 and row_start % SUBLANES == 0
    if row_start % tn == 0:
        j0 = row_start // tn
        w_spec = pl.BlockSpec((tn, D_MODEL), lambda j, i: (j0 + j, 0))
    else:
        w_spec = pl.BlockSpec((pl.Element(tn), pl.Element(D_MODEL)), lambda j, i: ((row_start // SUBLANES + j * (tn // SUBLANES)) * SUBLANES, 0))
    if feature_major:
        out_spec = pl.BlockSpec((tn, tm), lambda j, i: (j, i))
        out_shape = (n_rows, rows)
    else:
        out_spec = pl.BlockSpec((tm, tn), lambda j, i: (i, j))
        out_shape = (rows, n_rows)
    return pl.pallas_call(
        functools.partial(_inproj_kernel, feature_major),
        grid=(n_rows // tn, rows // tm),
        in_specs=[pl.BlockSpec((tm, D_MODEL), lambda j, i: (i, 0)), w_spec],
        out_specs=out_spec,
        out_shape=jax.ShapeDtypeStruct(out_shape, out_dtype),
        scratch_shapes=[pltpu.VMEM((tn, D_MODEL), BF16)],
        compiler_params=pltpu.CompilerParams(
            dimension_semantics=("arbitrary", "arbitrary"), vmem_limit_bytes=VMEM_LIMIT),
        name="inproj_t" if feature_major else "inproj",
    )(xn, wt)


def _inproj_kvi_kernel(x_ref, g_ref, wkv_ref, wwi_ref, wki_ref, xn_ref, kv_ref, ki_ref, wit_ref, k3_ref, v3_ref,
                       wkvb_ref, wwib_ref, wkib_ref):
    @pl.when(pl.program_id(0) == 0)
    def _():
        wkvb_ref[...] = wkv_ref[...].astype(BF16)
        wwib_ref[...] = wwi_ref[...].astype(BF16)
        wkib_ref[...] = wki_ref[...].astype(BF16)

    x = x_ref[...]
    ms = jnp.mean(x * x, axis=-1, keepdims=True)
    xn = ((x * lax.rsqrt(ms + RMS_EPS)) * g_ref[...]).astype(BF16)
    xn_ref[...] = xn
    kv = _dot_nt(xn, wkvb_ref[...])
    kv_ref[...] = kv
    for g in range(N_KV_HEADS):
        k3_ref[:, g, :] = kv[:, g * HEAD_DIM:(g + 1) * HEAD_DIM]
        v3_ref[:, g, :] = kv[:, KV_WIDTH + g * HEAD_DIM:KV_WIDTH + (g + 1) * HEAD_DIM]
    ki_ref[...] = _dot_nt(xn, wkib_ref[...])
    wit_ref[...] = _dot_nt(wwib_ref[...], xn)


def _inproj_kvi(x2d, g, wt, tm):
    rows = x2d.shape[0]
    assert rows % tm == 0 and COL_K % (2 * KV_WIDTH) == 0 and COL_WI % IDX_HEADS == 0
    return pl.pallas_call(
        _inproj_kvi_kernel,
        grid=(rows // tm,),
        in_specs=[
            pl.BlockSpec((tm, D_MODEL), lambda i: (i, 0)),
            pl.BlockSpec((1, D_MODEL), lambda i: (0, 0)),
            pl.BlockSpec((2 * KV_WIDTH, D_MODEL), lambda i: (COL_K // (2 * KV_WIDTH), 0)),
            pl.BlockSpec((IDX_HEADS, D_MODEL), lambda i: (COL_WI // IDX_HEADS, 0)),
            pl.BlockSpec((pl.Element(IDX_DIM), pl.Element(D_MODEL)), lambda i: (COL_KI, 0)),
        ],
        out_specs=[
            pl.BlockSpec((tm, D_MODEL), lambda i: (i, 0)),
            pl.BlockSpec((tm, 2 * KV_WIDTH), lambda i: (i, 0)),
            pl.BlockSpec((tm, IDX_DIM), lambda i: (i, 0)),
            pl.BlockSpec((IDX_HEADS, tm), lambda i: (0, i)),
            pl.BlockSpec((tm, N_KV_HEADS, HEAD_DIM), lambda i: (i, 0, 0)),
            pl.BlockSpec((tm, N_KV_HEADS, HEAD_DIM), lambda i: (i, 0, 0)),
        ],
        out_shape=[
            jax.ShapeDtypeStruct((rows, D_MODEL), BF16),
            jax.ShapeDtypeStruct((rows, 2 * KV_WIDTH), F32),
            jax.ShapeDtypeStruct((rows, IDX_DIM), F32),
            jax.ShapeDtypeStruct((IDX_HEADS, rows), F32),
            jax.ShapeDtypeStruct((rows, N_KV_HEADS, HEAD_DIM), F32),
            jax.ShapeDtypeStruct((rows, N_KV_HEADS, HEAD_DIM), F32),
        ],
        scratch_shapes=[
            pltpu.VMEM((2 * KV_WIDTH, D_MODEL), BF16),
            pltpu.VMEM((IDX_HEADS, D_MODEL), BF16),
            pltpu.VMEM((IDX_DIM, D_MODEL), BF16),
        ],
        compiler_params=pltpu.CompilerParams(
            dimension_semantics=("arbitrary",), vmem_limit_bytes=VMEM_LIMIT),
        name="inproj_kvi",
    )(x2d, g, wt, wt, wt)


def _prompt_attn_kernel(qt_ref, qit0_ref, qit1_ref, qit2_ref, qit3_ref, vt_ref, wit_ref, kvf_ref, kif_ref,
                        o_ref, kb_ref, kib_ref, vtc_ref, sc_ref, sa_ref, sb_ref, acc_ref, ml_ref, u_ref, cnt_ref):
    i = pl.program_id(1)
    seq = kvf_ref.shape[0]
    pos_bits = seq.bit_length()
    qit_refs = (qit0_ref, qit1_ref, qit2_ref, qit3_ref)
    heads_per_ref = IDX_HEADS // len(qit_refs)

    @pl.when(i == 0)
    def _():
        kb_ref[...] = kvf_ref[:, 0:KV_WIDTH].astype(BF16)
        kib_ref[...] = kif_ref[...].astype(BF16)
        for c in range(seq // KEY_CHUNK):
            for g in range(N_KV_HEADS):
                vtc_ref[c, g, :HEAD_DIM, :] = vt_ref[g * HEAD_DIM:(g + 1) * HEAD_DIM,
                                                     c * KEY_CHUNK:(c + 1) * KEY_CHUNK]
                vtc_ref[c, g, HEAD_DIM:, :] = jnp.ones((ONES_ROWS, KEY_CHUNK), BF16)

    n_chunks = (i * Q_BLOCK) // KEY_CHUNK + 1
    q_pos = i * Q_BLOCK + lax.broadcasted_iota(I32, (1, Q_BLOCK), 1)
    key0 = lax.broadcasted_iota(I32, (KEY_CHUNK, 1), 0)

    w_t = wit_ref[...] * IDX_W_SCALE
    qit_all = jnp.concatenate(
        [qit_refs[h // heads_per_ref][(h % heads_per_ref) * IDX_DIM:(h % heads_per_ref + 1) * IDX_DIM, :]
         for h in range(IDX_HEADS)], axis=1)

    def score_chunk(c):
        k0 = pl.multiple_of(c * KEY_CHUNK, KEY_CHUNK)
        res = jnp.dot(kib_ref[pl.ds(k0, KEY_CHUNK), :], qit_all, preferred_element_type=F32)
        acc = jnp.zeros((KEY_CHUNK, Q_BLOCK), F32)
        for h in range(IDX_HEADS):
            acc = acc + w_t[h:h + 1, :] * jnp.maximum(res[:, h * Q_BLOCK:(h + 1) * Q_BLOCK], 0.0)
        sc_ref[c] = jnp.where(k0 + key0 <= q_pos, acc, NEG_SENTINEL)

    def score_pair(t, carry):
        score_chunk(2 * t)
        score_chunk(2 * t + 1)
        return carry

    lax.fori_loop(0, n_chunks // 2, score_pair, 0)

    @pl.when(n_chunks % 2 == 1)
    def _():
        score_chunk(n_chunks - 1)

    def count(pred):
        def body(c, cnt):
            ones = jnp.where(pred(sc_ref[c], c), 1.0, 0.0)
            return cnt + ones.reshape(KEY_CHUNK // COUNT_ROWS, COUNT_ROWS, Q_BLOCK).sum(axis=0)
        part = lax.fori_loop(0, n_chunks, body, jnp.zeros((COUNT_ROWS, Q_BLOCK), F32))
        return jnp.sum(part, axis=0, keepdims=True)

    def search(j0, n, u, cnt_u, decode):
        def body(j, carry):
            u, cnt_u = carry
            cand = u | lax.shift_left(jnp.int32(1), 31 - j)
            thr_c = decode(cand)
            cnt = count(lambda blk, c: blk >= thr_c)
            ok = cnt >= float(TOPK_MAX)
            return jnp.where(ok, cand, u), jnp.where(ok, cnt, cnt_u)
        return lax.fori_loop(j0, j0 + n, body, (u, cnt_u))

    u = jnp.zeros((1, Q_BLOCK), I32)
    cnt_u = (u + n_chunks * KEY_CHUNK).astype(F32)
    u, cnt_u = search(0, SIGN_EXP_BITS, u, cnt_u, _ordered_bits_to_f32)
    signed_pow, zero_exp = _sign_exp_parts(u)

    def decode_mantissa(cand):
        return _with_mantissa(cand, signed_pow, zero_exp)

    u, cnt_u = search(SIGN_EXP_BITS, SEARCH_ALWAYS_BITS - SIGN_EXP_BITS, u, cnt_u, decode_mantissa)
    u_ref[...] = u
    cnt_ref[...] = cnt_u
    for j0 in range(SEARCH_ALWAYS_BITS, 32, SEARCH_GROUP_BITS):
        @pl.when(jnp.max(jnp.where(cnt_ref[...] == float(TOPK_MAX), 0.0, 1.0)) > 0.0)
        def _():
            u_g, cnt_g = search(j0, SEARCH_GROUP_BITS, u_ref[...], cnt_ref[...], decode_mantissa)
            u_ref[...] = u_g
            cnt_ref[...] = cnt_g
    u = u_ref[...]
    n_ge = cnt_ref[...]
    thr = decode_mantissa(u)
    thr_adm = jnp.maximum(thr, NEG_ABOVE_SENTINEL)
    excess = jnp.max(jnp.where((n_ge > float(TOPK_MAX)) & (thr > NEG_SENTINEL), 1.0, 0.0)) > 0.0

    @pl.when(excess)
    def _():
        need = float(TOPK_MAX) - count(lambda blk, c: blk > thr)

        def jbit_body(j, jl):
            cand = jl | lax.shift_left(jnp.int32(1), pos_bits - 1 - j)
            cnt = count(lambda blk, c: (blk == thr) & (c * KEY_CHUNK + key0 < cand))
            return jnp.where(cnt <= need, cand, jl)
        jlim = lax.fori_loop(0, pos_bits, jbit_body, jnp.zeros((1, Q_BLOCK), I32))

        def demote_body(c, carry):
            blk = sc_ref[c]
            dropped = (blk == thr) & (thr > NEG_SENTINEL) & (c * KEY_CHUNK + key0 >= jlim)
            sc_ref[c] = jnp.where(dropped, NEG_SENTINEL, blk)
            return carry
        lax.fori_loop(0, n_chunks, demote_body, 0)

    acc_ref[...] = jnp.zeros(acc_ref.shape, F32)
    cols = Q_PER_KV * Q_BLOCK
    qt_groups = [
        jnp.concatenate([qt_ref[(g * Q_PER_KV + h) * HEAD_DIM:(g * Q_PER_KV + h + 1) * HEAD_DIM, :]
                         for h in range(Q_PER_KV)], axis=1)
        for g in range(N_KV_HEADS)
    ]

    last_chunk = seq // KEY_CHUNK - 1
    eye = (lax.broadcasted_iota(I32, (Q_BLOCK, Q_BLOCK), 0)
           == lax.broadcasted_iota(I32, (Q_BLOCK, Q_BLOCK), 1))
    eye_heads = jnp.concatenate([jnp.where(eye, 1.0, 0.0).astype(BF16)] * Q_PER_KV, axis=1)
    qt_aug = [jnp.concatenate([qt_groups[g], eye_heads], axis=0) for g in range(N_KV_HEADS)]

    def qk_scores(c, dst_ref):
        cc = jnp.minimum(c, last_chunk)
        k0 = pl.multiple_of(cc * KEY_CHUNK, KEY_CHUNK)
        masked = jnp.where(sc_ref[cc] >= thr_adm, 0.0, MASKED_SCORE).astype(BF16)
        for g in range(N_KV_HEADS):
            kc = kb_ref[pl.ds(k0, KEY_CHUNK), g * HEAD_DIM:(g + 1) * HEAD_DIM]
            dst_ref[g] = jnp.dot(jnp.concatenate([kc, masked], axis=1), qt_aug[g], preferred_element_type=F32)

    def softmax_pv(c, src_ref):
        for g in range(N_KV_HEADS):
            m_old = ml_ref[2 * g:2 * g + 1, :]
            l_old = ml_ref[2 * g + 1:2 * g + 2, :]
            s = src_ref[g] * ATTN_SCALE_LOG2E
            m_new = jnp.maximum(m_old, jnp.max(s, axis=0, keepdims=True))
            m_safe = jnp.where(m_new < MASKED_LEVEL, 0.0, m_new)
            alpha = jnp.exp2(m_old - m_safe)
            p = jnp.exp2(s - m_safe).astype(BF16)
            pv = jnp.dot(vtc_ref[c, g], p, preferred_element_type=F32)
            acc_ref[g] = alpha * acc_ref[g] + pv[:HEAD_DIM]
            ml_ref[2 * g:2 * g + 1, :] = m_new
            ml_ref[2 * g + 1:2 * g + 2, :] = alpha * l_old + pv[HEAD_DIM:HEAD_DIM + 1]

    def pair_body(t, carry):
        qk_scores(2 * t + 1, sb_ref)
        softmax_pv(2 * t, sa_ref)
        qk_scores(2 * t + 2, sa_ref)
        softmax_pv(2 * t + 1, sb_ref)
        return carry

    for g in range(N_KV_HEADS):
        ml_ref[2 * g:2 * g + 1, :] = jnp.full((1, cols), -jnp.inf, F32)
        ml_ref[2 * g + 1:2 * g + 2, :] = jnp.zeros((1, cols), F32)
    qk_scores(0, sa_ref)
    lax.fori_loop(0, n_chunks // 2, pair_body, 0)

    @pl.when(n_chunks % 2 == 1)
    def _():
        softmax_pv(n_chunks - 1, sa_ref)

    for g in range(N_KV_HEADS):
        o = acc_ref[g] / ml_ref[2 * g + 1:2 * g + 2, :]
        for h in range(Q_PER_KV):
            hd = (g * Q_PER_KV + h) * HEAD_DIM
            o_ref[:, hd:hd + HEAD_DIM] = o[:, h * Q_BLOCK:(h + 1) * Q_BLOCK].T


def _prompt_attn(h_t, wi_t, kv, ki, batch, seq):
    nb = seq // Q_BLOCK
    cols = Q_PER_KV * Q_BLOCK
    qi_rows = IDX_HEADS * IDX_DIM // 4
    qi_b0 = COL_QI // qi_rows

    def qi_spec(t):
        return pl.BlockSpec((qi_rows, Q_BLOCK), lambda b, i: (qi_b0 + t, b * nb + i))

    return pl.pallas_call(
        _prompt_attn_kernel,
        grid=(batch, nb),
        in_specs=[
            pl.BlockSpec((ATTN_WIDTH, Q_BLOCK), lambda b, i: (COL_Q // ATTN_WIDTH, b * nb + i)),
            qi_spec(0), qi_spec(1), qi_spec(2), qi_spec(3),
            pl.BlockSpec((KV_WIDTH, seq), lambda b, i: (COL_V // KV_WIDTH, b)),
            pl.BlockSpec((IDX_HEADS, Q_BLOCK), lambda b, i: (0, b * nb + i)),
            pl.BlockSpec((seq, 2 * KV_WIDTH), lambda b, i: (b, 0)),
            pl.BlockSpec((seq, IDX_DIM), lambda b, i: (b, 0)),
        ],
        out_specs=pl.BlockSpec((Q_BLOCK, ATTN_WIDTH), lambda b, i: (b * nb + i, 0)),
        out_shape=jax.ShapeDtypeStruct((batch * seq, ATTN_WIDTH), F32),
        scratch_shapes=[
            pltpu.VMEM((seq, KV_WIDTH), BF16),
            pltpu.VMEM((seq, IDX_DIM), BF16),
            pltpu.VMEM((seq // KEY_CHUNK, N_KV_HEADS, HEAD_DIM + ONES_ROWS, KEY_CHUNK), BF16),
            pltpu.VMEM((seq // KEY_CHUNK, KEY_CHUNK, Q_BLOCK), F32),
            pltpu.VMEM((N_KV_HEADS, KEY_CHUNK, cols), F32),
            pltpu.VMEM((N_KV_HEADS, KEY_CHUNK, cols), F32),
            pltpu.VMEM((N_KV_HEADS, HEAD_DIM, cols), F32),
            pltpu.VMEM((2 * N_KV_HEADS, cols), F32),
            pltpu.VMEM((1, Q_BLOCK), I32),
            pltpu.VMEM((1, Q_BLOCK), F32),
        ],
        compiler_params=pltpu.CompilerParams(
            dimension_semantics=("arbitrary", "arbitrary"), vmem_limit_bytes=VMEM_LIMIT),
        name="prompt_attn",
    )(h_t, h_t, h_t, h_t, h_t, h_t, wi_t, kv, ki)


PAGES_PER_STEP = 64


def _sample_select_kernel(pt_ref, qi_ref, kvs_ref, *refs):
    page_refs = refs[:PAGES_PER_STEP]
    idx_ref, meta_ref, sc_ref, snew_ref = refs[PAGES_PER_STEP:]
    b = pl.program_id(0)
    p = pl.program_id(1)
    nb = pl.num_programs(0)
    n_steps = pl.num_programs(1)
    n_pages, db, _ = sc_ref.shape
    pos_bits = (n_pages * PAGE_SIZE).bit_length()

    qi = qi_ref[0]
    w_col = kvs_ref[0, :, KVI_WI:KVI_WI + 1] * IDX_W_SCALE

    def key_scores(keys_bf16):
        s = _dot_nt(qi, keys_bf16)
        return jnp.sum(w_col * jnp.maximum(s, 0.0), axis=0, keepdims=True)

    keys = jnp.concatenate([page_refs[r][0, 0] for r in range(PAGES_PER_STEP)], axis=0).astype(BF16)
    row = key_scores(keys)
    for r in range(PAGES_PER_STEP):
        sc_ref[p * PAGES_PER_STEP + r, pl.ds(b, 1), :] = row[:, r * PAGE_SIZE:(r + 1) * PAGE_SIZE]

    @pl.when(p == n_steps - 1)
    def _():
        ki_new = kvs_ref[0, :, KVI_KI:KVI_KI + IDX_DIM].astype(BF16)
        s_new = key_scores(ki_new)[:, 0:1]
        snew_ref[pl.ds(b, 1), :] = jnp.broadcast_to(s_new, (1, PAGE_SIZE))

    @pl.when((b == nb - 1) & (p == n_steps - 1))
    def _():
        sc = sc_ref[...]
        s_new = snew_ref[:, 0:1]

        def total(x):
            return jnp.sum(jnp.sum(x, axis=0), axis=1, keepdims=True)

        def bit_body(j, u):
            cand = u | lax.shift_left(jnp.int32(1), 31 - j)
            thr = _ordered_bits_to_f32(cand)
            cnt = total(jnp.where(sc >= thr[None], 1.0, 0.0)) + jnp.where(s_new >= thr, 1.0, 0.0)
            return jnp.where(cnt >= float(TOPK_MAX), cand, u)

        u = lax.fori_loop(0, 32, bit_body, jnp.zeros((db, 1), I32))
        thr = _ordered_bits_to_f32(u)
        n_gt = total(jnp.where(sc > thr[None], 1.0, 0.0)) + jnp.where(s_new > thr, 1.0, 0.0)
        need = float(TOPK_MAX) - n_gt
        flat = (lax.broadcasted_iota(I32, sc.shape, 0) * PAGE_SIZE
                + lax.broadcasted_iota(I32, sc.shape, 2))
        tie = sc == thr[None]

        def jbit_body(j, jl):
            cand = jl | lax.shift_left(jnp.int32(1), pos_bits - 1 - j)
            cnt = total(jnp.where(tie & (flat < cand[None]), 1.0, 0.0))
            return jnp.where(cnt <= need, cand, jl)

        jlim = lax.fori_loop(0, pos_bits, jbit_body, jnp.zeros((db, 1), I32))
        sel = (sc > thr[None]) | (tie & (flat < jlim[None]))
        self32 = jnp.where(sel, 1.0, 0.0)
        n_pool = total(self32)

        n_rows = n_pages * db
        selb = self32.astype(BF16).reshape(n_rows, PAGE_SIZE)
        ri = lax.broadcasted_iota(I32, (PAGE_SIZE, PAGE_SIZE), 0)
        ci = lax.broadcasted_iota(I32, (PAGE_SIZE, PAGE_SIZE), 1)
        incl = jnp.dot(selb, jnp.where(ri <= ci, 1.0, 0.0).astype(BF16), preferred_element_type=F32)
        tot = jnp.broadcast_to(incl[:, PAGE_SIZE - 1:PAGE_SIZE], (n_rows, PAGE_SIZE)).astype(BF16)
        rr = lax.broadcasted_iota(I32, (n_rows, n_rows), 0)
        cc = lax.broadcasted_iota(I32, (n_rows, n_rows), 1)
        earlier_page_same_row = ((rr & (db - 1)) == (cc & (db - 1))) & (cc < (rr & -db))
        offs = jnp.dot(jnp.where(earlier_page_same_row, 1.0, 0.0).astype(BF16), tot,
                       preferred_element_type=F32)
        ends_rows = offs + tot.astype(F32)
        r_i = lax.broadcasted_iota(I32, (n_rows, PAGE_SIZE), 0)
        l_i = lax.broadcasted_iota(I32, (n_rows, PAGE_SIZE), 1)
        log2_db = db.bit_length() - 1
        diag = jnp.where(l_i == lax.shift_right_logical(r_i, log2_db), ends_rows, 0.0).astype(BF16)
        pick = jnp.where((lax.broadcasted_iota(I32, (db, n_rows), 1) & (db - 1))
                         == lax.broadcasted_iota(I32, (db, n_rows), 0), 1.0, 0.0).astype(BF16)
        ends = jnp.dot(pick, diag, preferred_element_type=F32)

        table = jnp.concatenate([incl.astype(BF16), offs.astype(BF16)], axis=1)
        slot = lax.broadcasted_iota(I32, (TOPK_MAX, PAGE_SIZE), 0).astype(F32)
        rows_lane = lax.broadcasted_iota(I32, (TOPK_MAX, n_rows), 1)
        for row_i in range(db):
            page_s = jnp.sum(jnp.where(ends[row_i:row_i + 1, :] <= slot, 1.0, 0.0), axis=1, keepdims=True)
            mrow = page_s.astype(I32) * db + row_i
            onehot = jnp.where(rows_lane == mrow, 1.0, 0.0).astype(BF16)
            got = jnp.dot(onehot, table, preferred_element_type=F32)
            k = slot - got[:, PAGE_SIZE:]
            off_s = jnp.sum(jnp.where(got[:, :PAGE_SIZE] <= k, 1.0, 0.0), axis=1, keepdims=True)
            pos = page_s * float(PAGE_SIZE) + off_s
            valid = slot[:, 0:1] < n_pool[row_i:row_i + 1, :]
            idx_ref[row_i] = jnp.where(valid, pos, 0.0).astype(I32)
        meta_ref[...] = jnp.broadcast_to(n_pool, (db, PAGE_SIZE)).astype(I32)


def _sample_select(page_table, qi_s, kvi_s16, pool_ki):
    db, n_pages = page_table.shape
    assert db & (db - 1) == 0, "row index arithmetic in the rank matrix assumes a power-of-two row count"
    assert n_pages == PAGE_SIZE, "page ends are laid out one page per lane"
    n_steps = n_pages // PAGES_PER_STEP

    def page_spec(r):
        return pl.BlockSpec(
            (1, 1, PAGE_SIZE, IDX_DIM),
            lambda b, p, pt: (0, pt[b, p * PAGES_PER_STEP + r], 0, 0))

    grid_spec = pltpu.PrefetchScalarGridSpec(
        num_scalar_prefetch=1,
        grid=(db, n_steps),
        in_specs=[
            pl.BlockSpec((1, IDX_HEADS, IDX_DIM), lambda b, p, pt: (b, 0, 0)),
            pl.BlockSpec((1, IDX_HEADS, KVI_WIDTH), lambda b, p, pt: (b, 0, 0)),
        ] + [page_spec(r) for r in range(PAGES_PER_STEP)],
        out_specs=[
            pl.BlockSpec((db, TOPK_MAX, 1), lambda b, p, pt: (0, 0, 0)),
            pl.BlockSpec((db, PAGE_SIZE), lambda b, p, pt: (0, 0)),
        ],
        scratch_shapes=[
            pltpu.VMEM((n_pages, db, PAGE_SIZE), F32),
            pltpu.VMEM((db, PAGE_SIZE), F32),
        ],
    )
    return pl.pallas_call(
        _sample_select_kernel,
        grid_spec=grid_spec,
        out_shape=[
            jax.ShapeDtypeStruct((db, TOPK_MAX, 1), I32),
            jax.ShapeDtypeStruct((db, PAGE_SIZE), I32),
        ],
        compiler_params=pltpu.CompilerParams(
            dimension_semantics=("arbitrary", "arbitrary"), vmem_limit_bytes=VMEM_LIMIT),
        name="sample_select",
    )(page_table, qi_s, kvi_s16, *([pool_ki] * PAGES_PER_STEP))


def _sample_attend_kernel(idx_ref, npool_ref, pt_ref, q_ref, kvs_ref, pk_ref, pv_ref, o_ref,
                          kbuf, vbuf, kflat, vflat, sem):
    b = pl.program_id(0)

    def row_copies(r):
        ix = idx_ref[b, r]
        page = pt_ref[b, lax.shift_right_logical(ix, 7)]
        off = ix & (PAGE_SIZE - 1)
        return (pltpu.make_async_copy(pk_ref.at[0, page, off], kbuf.at[r], sem.at[0]),
                pltpu.make_async_copy(pv_ref.at[0, page, off], vbuf.at[r], sem.at[1]))

    def start_body(r, carry):
        ck, cv = row_copies(r)
        ck.start()
        cv.start()
        return carry

    def wait_body(r, carry):
        ck, cv = row_copies(r)
        ck.wait()
        cv.wait()
        return carry

    lax.fori_loop(0, TOPK_MAX, start_body, 0, unroll=8)
    lax.fori_loop(0, TOPK_MAX, wait_body, 0, unroll=8)

    def head_copies(g):
        return (pltpu.make_async_copy(kbuf.at[:, g, :], kflat.at[g], sem.at[2]),
                pltpu.make_async_copy(vbuf.at[:, g, :], vflat.at[g], sem.at[3]))

    for g in range(N_KV_HEADS):
        for cp in head_copies(g):
            cp.start()
    for g in range(N_KV_HEADS):
        for cp in head_copies(g):
            cp.wait()

    n_pool = npool_ref[b]
    slot = lax.broadcasted_iota(I32, (1, TOPK_MAX), 1)
    slot_ok = slot < n_pool
    new_ok = n_pool < TOPK_MAX
    q_all = q_ref[0].astype(F32)
    for g in range(N_KV_HEADS):
        qg = q_all[g * Q_PER_KV:(g + 1) * Q_PER_KV, :].astype(BF16)
        ks = kflat[g].astype(BF16)
        vs = vflat[g].astype(BF16)
        k_new = kvs_ref[0, 0:1, KVI_K + g * HEAD_DIM:KVI_K + (g + 1) * HEAD_DIM].astype(BF16)
        v_new = kvs_ref[0, 0:1, KVI_V + g * HEAD_DIM:KVI_V + (g + 1) * HEAD_DIM].astype(BF16)
        s = jnp.where(slot_ok, _dot_nt(qg, ks) * ATTN_SCALE, -jnp.inf)
        s_new = jnp.sum(qg.astype(F32) * k_new.astype(F32), axis=-1, keepdims=True) * ATTN_SCALE
        s_new = jnp.where(new_ok, s_new, -jnp.inf)
        m = jnp.maximum(jnp.max(s, axis=-1, keepdims=True), s_new)
        p = jnp.exp(s - m)
        p_new = jnp.exp(s_new - m)
        denom = jnp.sum(p, axis=-1, keepdims=True) + p_new
        pn = (p / denom).astype(BF16)
        pn_new = (p_new / denom).astype(BF16).astype(F32)
        o = jnp.dot(pn, vs, preferred_element_type=F32) + pn_new * v_new.astype(F32)
        o_ref[0, g * Q_PER_KV:(g + 1) * Q_PER_KV, :] = o


def _sample_attend(idx, n_pool, page_table, q_s, kvi_s16, pool_k, pool_v):
    db = page_table.shape[0]
    grid_spec = pltpu.PrefetchScalarGridSpec(
        num_scalar_prefetch=3,
        grid=(db,),
        in_specs=[
            pl.BlockSpec((1, N_HEADS, HEAD_DIM), lambda b, *_: (b, 0, 0)),
            pl.BlockSpec((1, IDX_HEADS, KVI_WIDTH), lambda b, *_: (b, 0, 0)),
            pl.BlockSpec(memory_space=pl.ANY),
            pl.BlockSpec(memory_space=pl.ANY),
        ],
        out_specs=pl.BlockSpec((1, N_HEADS, HEAD_DIM), lambda b, *_: (b, 0, 0)),
        scratch_shapes=[
            pltpu.VMEM((TOPK_MAX, N_KV_HEADS, HEAD_DIM), F32),
            pltpu.VMEM((TOPK_MAX, N_KV_HEADS, HEAD_DIM), F32),
            pltpu.VMEM((N_KV_HEADS, TOPK_MAX, HEAD_DIM), F32),
            pltpu.VMEM((N_KV_HEADS, TOPK_MAX, HEAD_DIM), F32),
            pltpu.SemaphoreType.DMA((4,)),
        ],
    )
    return pl.pallas_call(
        _sample_attend_kernel,
        grid_spec=grid_spec,
        out_shape=jax.ShapeDtypeStruct((db, N_HEADS, HEAD_DIM), F32),
        compiler_params=pltpu.CompilerParams(dimension_semantics=("arbitrary",)),
        name="sample_attend",
    )(idx, n_pool, page_table, q_s, kvi_s16, pool_k, pool_v)


def _merge_kernel(chunked, oa_ref, za_ref, u_ref, v_ref, zb_ref, ga_ref, gb_ref, x_ref,
                  wpa_ref, wpb_ref, wout_ref, lng_ref, lnb_ref, ws_ref, bs_ref, gf_ref, y_ref, *rest):
    vn_ref, ob_ref = (None, rest[0]) if chunked else rest
    tm = x_ref.shape[0]
    u = _gelu(u_ref[...])
    v = _gelu(v_ref[...])
    mu = jnp.mean(v, axis=-1, keepdims=True)
    vc = v - mu
    vn = (vc * lax.rsqrt(jnp.mean(vc * vc, axis=-1, keepdims=True) + LN_EPS)) * lng_ref[...] + lnb_ref[...]
    if vn_ref is not None:
        vn_ref[...] = vn

    if chunked:
        vnb = vn.astype(BF16)
        ri = lax.broadcasted_iota(I32, (CHUNK, CHUNK), 0)
        ci = lax.broadcasted_iota(I32, (CHUNK, CHUNK), 1)
        for g in range(GMLP_GROUPS):
            wm = jnp.where(ri >= ci, ws_ref[g], 0.0).astype(BF16)
            lo = g * GMLP_GROUP_DIM
            for c in range(tm // CHUNK):
                mixed = jnp.dot(wm, vnb[c * CHUNK:(c + 1) * CHUNK, lo:lo + GMLP_GROUP_DIM],
                                preferred_element_type=F32) + bs_ref[:, g:g + 1]
                ob_ref[c * CHUNK:(c + 1) * CHUNK, lo:lo + GMLP_GROUP_DIM] = (
                    u[c * CHUNK:(c + 1) * CHUNK, lo:lo + GMLP_GROUP_DIM] * mixed)
    else:
        ob_ref[...] = u * (ws_ref[...] * vn + bs_ref[...])

    ha = (oa_ref[...] * _silu(za_ref[...])).astype(BF16)
    hb = (ob_ref[...] * _silu(zb_ref[...])).astype(BF16)
    ya = jnp.dot(ha, wpa_ref[...], preferred_element_type=F32)
    yb = jnp.dot(hb, wpb_ref[...], preferred_element_type=F32)
    mix = jax.nn.sigmoid(ga_ref[...]) * ya + jax.nn.sigmoid(gb_ref[...]) * yb
    out = x_ref[...] + jnp.dot(mix.astype(BF16), wout_ref[...], preferred_element_type=F32)
    ms = jnp.mean(out * out, axis=-1, keepdims=True)
    y_ref[...] = (out * lax.rsqrt(ms + RMS_EPS)) * gf_ref[...]


def _merge(chunked, o_a, gate, x2d, wpa, wpb, wout, lng, lnb, ws, bs, gf, tm):
    rows = x2d.shape[0]
    gw = GMLP_WIDTH

    def const(shape):
        return pl.BlockSpec(shape, lambda i: (0,) * len(shape), pipeline_mode=pl.Buffered(1))

    return pl.pallas_call(
        functools.partial(_merge_kernel, chunked),
        grid=(rows // tm,),
        in_specs=[
            pl.BlockSpec((tm, ATTN_WIDTH), lambda i: (i, 0)),
            pl.BlockSpec((tm, gw), lambda i: (i, 0)),
            pl.BlockSpec((tm, gw), lambda i: (i, 1)),
            pl.BlockSpec((tm, gw), lambda i: (i, 2)),
            pl.BlockSpec((tm, gw), lambda i: (i, 3)),
            pl.BlockSpec((tm, D_MODEL), lambda i: (i, 2)),
            pl.BlockSpec((tm, D_MODEL), lambda i: (i, 3)),
            pl.BlockSpec((tm, D_MODEL), lambda i: (i, 0)),
            const(wpa.shape), const(wpb.shape), const(wout.shape),
            const(lng.shape), const(lnb.shape), const(ws.shape), const(bs.shape), const(gf.shape),
        ],
        out_specs=[pl.BlockSpec((tm, D_MODEL), lambda i: (i, 0))]
        + ([] if chunked else [pl.BlockSpec((tm, gw), lambda i: (i, 0))]),
        out_shape=[jax.ShapeDtypeStruct((rows, D_MODEL), F32)]
        + ([] if chunked else [jax.ShapeDtypeStruct((rows, gw), F32)]),
        scratch_shapes=[pltpu.VMEM((tm, gw), F32)],
        compiler_params=pltpu.CompilerParams(
            dimension_semantics=("arbitrary",), vmem_limit_bytes=VMEM_LIMIT),
        name="merge_prompt" if chunked else "merge_sample",
    )(o_a, gate, gate, gate, gate, gate, gate, x2d, wpa, wpb, wout, lng, lnb, ws, bs, gf)


def kernel(x_prompt, x_sample, cache_k, cache_v, cache_k_idx, page_table, norm_in_g, w_in,
           w_proj_a, w_proj_b, w_out, ln_g, ln_b, w_spatial, b_spatial, norm_f_g):
    depth = w_in.shape[0]
    assert depth == 1, "single trunk layer"
    batch, seq, _ = x_prompt.shape
    db, dseq, _ = x_sample.shape
    assert dseq == 1
    l = 0
    wpa = w_proj_a[l].astype(BF16)
    wpb = w_proj_b[l].astype(BF16)
    wout = w_out[l].astype(BF16)
    g_in = norm_in_g[l].reshape(1, D_MODEL)
    lng = ln_g[l].reshape(1, GMLP_WIDTH)
    lnb = ln_b[l].reshape(1, GMLP_WIDTH)
    gf = norm_f_g.reshape(1, D_MODEL)

    xp = x_prompt.reshape(batch * seq, D_MODEL)
    wt = w_in[l].T
    xn_p, kv_p, ki_p2, wi_t, k3_p, v3_p = _inproj_kvi(xp, g_in, wt, PROJ_TOKEN_TILE)
    h_t = _inproj(xn_p, wt, COL_Q, COL_WI, COL_WI // 4, PROJ_TOKEN_TILE, BF16, True)
    gate_p = _inproj(xn_p, wt, GATE_COL0, GATE_WIDTH, GATE_FEATURE_TILE, PROJ_TOKEN_TILE, F32, False)
    oa_p = _prompt_attn(h_t, wi_t, kv_p, ki_p2, batch, seq)
    bs_t = b_spatial[l].T
    (y_p,) = _merge(True, oa_p, gate_p, xp, wpa, wpb, wout, lng, lnb, w_spatial[l], bs_t, gf, MERGE_TOKEN_TILE)

    xs = x_sample.reshape(db, D_MODEL)
    xn_s, kv_s, ki_s2, wi_t_s, _, _ = _inproj_kvi(xs, g_in, wt, db)
    h_s = _inproj(xn_s, wt, COL_Q, COL_WI, 512, db, F32, False)
    gate_s = _inproj(xn_s, wt, GATE_COL0, GATE_WIDTH, GATE_FEATURE_TILE, db, F32, False)
    kvi_s = jnp.concatenate(
        [kv_s, ki_s2, wi_t_s.T, jnp.zeros((db, KVI_WIDTH - KVI_WI - IDX_HEADS), F32)], axis=1)
    q_s = h_s[:, COL_Q:COL_K].astype(BF16).reshape(db, N_HEADS, HEAD_DIM)
    qi_s = h_s[:, COL_QI:COL_WI].astype(BF16).reshape(db, IDX_HEADS, IDX_DIM)
    wi_col = kvi_s[:, KVI_WI:KVI_WI + IDX_HEADS].reshape(db, IDX_HEADS, 1)
    kvi_s16 = jnp.broadcast_to(kvi_s[:, None, :], (db, IDX_HEADS, KVI_WIDTH))
    kvi_s16 = lax.dynamic_update_slice(kvi_s16, wi_col, (0, 0, KVI_WI))
    idx, meta = _sample_select(page_table, qi_s, kvi_s16, cache_k_idx)
    oa_s = _sample_attend(idx.reshape(db, TOPK_MAX), meta[:, 0], page_table, q_s, kvi_s16,
                          cache_k, cache_v).reshape(db, ATTN_WIDTH)
    ws0 = jnp.repeat(w_spatial[l][:, 0, 0], GMLP_GROUP_DIM).reshape(1, GMLP_WIDTH)
    bs0 = jnp.repeat(b_spatial[l][:, 0], GMLP_GROUP_DIM).reshape(1, GMLP_WIDTH)
    y_s, vn_s = _merge(False, oa_s, gate_s, xs, wpa, wpb, wout, lng, lnb, ws0, bs0, gf, db)

    def kv_out(kvi, lead):
        k = kvi[:, KVI_K:KVI_K + KV_WIDTH].reshape((1,) + lead + (N_KV_HEADS, HEAD_DIM))
        v = kvi[:, KVI_V:KVI_V + KV_WIDTH].reshape((1,) + lead + (N_KV_HEADS, HEAD_DIM))
        ki = kvi[:, KVI_KI:KVI_KI + IDX_DIM].reshape((1,) + lead + (IDX_DIM,))
        return k, v, ki

    k_p = k3_p.reshape(1, batch, seq, N_KV_HEADS, HEAD_DIM)
    v_p = v3_p.reshape(1, batch, seq, N_KV_HEADS, HEAD_DIM)
    ki_p = ki_p2.reshape(1, batch, seq, IDX_DIM)
    k_s, v_s, ki_s = kv_out(kvi_s, (db, dseq))
    return (y_p.reshape(batch, seq, D_MODEL), y_s.reshape(db, dseq, D_MODEL),
            k_p, v_p, ki_p, k_s, v_s, ki_s, vn_s.reshape(1, db, dseq, GMLP_WIDTH))
```

```python
import functools

import jax
import jax.numpy as jnp
import numpy as np
from jax import lax
from jax.experimental import pallas as pl
from jax.experimental.pallas import tpu as pltpu

F32 = jnp.float32
BF16 = jnp.bfloat16
I32 = jnp.int32

D_MODEL = 2048
N_HEADS = 8
N_KV_HEADS = 2
HEAD_DIM = 128
Q_PER_KV = N_HEADS // N_KV_HEADS
ATTN_WIDTH = N_HEADS * HEAD_DIM
KV_WIDTH = N_KV_HEADS * HEAD_DIM
IDX_HEADS = 16
IDX_DIM = 128
TOPK_MAX = 256
PAGE_SIZE = 128
GMLP_WIDTH = 1024
GMLP_GROUPS = 8
GMLP_GROUP_DIM = GMLP_WIDTH // GMLP_GROUPS
CHUNK = 128
RMS_EPS = 1e-6
LN_EPS = 1e-5

SUBLANES = 8
COL_Q = 0
COL_K = COL_Q + ATTN_WIDTH
COL_V = COL_K + KV_WIDTH
COL_QI = COL_V + KV_WIDTH
COL_WI = COL_QI + IDX_HEADS * IDX_DIM
COL_KI = COL_WI + IDX_HEADS
GATE_COL0 = COL_KI + IDX_DIM
GATE_WIDTH = 4 * GMLP_WIDTH + 2 * D_MODEL
KVI_K = 0
KVI_V = KVI_K + KV_WIDTH
KVI_KI = KVI_V + KV_WIDTH
KVI_WI = KVI_KI + IDX_DIM
KVI_WIDTH = KVI_WI + 128

Q_BLOCK = 128
KEY_CHUNK = 512
COUNT_ROWS = 32
ONES_ROWS = 16
SIGN_EXP_BITS = 9
SEARCH_ALWAYS_BITS = 20
SEARCH_GROUP_BITS = 4
NEG_SENTINEL = float(np.finfo(np.float32).min)
NEG_ABOVE_SENTINEL = float(np.nextafter(np.float32(NEG_SENTINEL), np.float32(0.0)))
INT_MIN = -(2 ** 31)
F32_MANTISSA_BITS = 23
F32_MANTISSA_MASK = (1 << F32_MANTISSA_BITS) - 1
F32_EXP_MASK = 0xFF
F32_MIN_NORMAL = float(2.0 ** -126)
IDX_W_SCALE = float(IDX_HEADS ** -0.5 * IDX_DIM ** -0.5)
ATTN_SCALE = float(HEAD_DIM ** -0.5)
ATTN_SCALE_LOG2E = float(HEAD_DIM ** -0.5 * np.log2(np.e))
MASKED_SCORE = float(jnp.finfo(jnp.bfloat16).min)
MASKED_LEVEL = -1e30
V7X_VMEM_BYTES = 64 * 1024 * 1024
VMEM_LIMIT = V7X_VMEM_BYTES - 8 * 1024 * 1024
PROJ_TOKEN_TILE = 1024
GATE_FEATURE_TILE = 1024
MERGE_TOKEN_TILE = 256


def _dot_nt(a, b):
    return lax.dot_general(a, b, (((1,), (1,)), ((), ())), preferred_element_type=F32)


def _ordered_bits_to_f32(u):
    signed_pow, zero_exp = _sign_exp_parts(u)
    return _with_mantissa(u, signed_pow, zero_exp)


def _sign_exp_parts(u):
    bits = jnp.where(u < 0, u ^ INT_MIN, ~u)
    exp = lax.shift_right_logical(bits, F32_MANTISSA_BITS) & F32_EXP_MASK
    e1 = exp - 1
    p = jnp.full(u.shape, F32_MIN_NORMAL, F32)
    for b in range(7):
        p = jnp.where(((e1 >> b) & 1) == 1, p * float(2.0 ** (2 ** b)), p)
    top = ((e1 >> 7) & 1) == 1
    p = jnp.where(top, p * float(2.0 ** 64), p)
    p = jnp.where(top, p * float(2.0 ** 64), p)
    return jnp.where(bits < 0, -p, p), exp == 0


def _with_mantissa(u, signed_pow, zero_exp):
    low = jnp.where(u < 0, u, ~u) & F32_MANTISSA_MASK
    mant = 1.0 + low.astype(F32) * float(2.0 ** -F32_MANTISSA_BITS)
    return jnp.where(zero_exp, 0.0, signed_pow * mant)


def _gelu(x):
    return 0.5 * x * (1.0 + lax.erf(x * float(np.sqrt(0.5))))


def _silu(x):
    return x * jax.nn.sigmoid(x)


def _inproj_kernel(feature_major, xn_ref, w_ref, o_ref, wb_ref):
    @pl.when(pl.program_id(1) == 0)
    def _():
        wb_ref[...] = w_ref[...].astype(BF16)

    if feature_major:
        o_ref[...] = _dot_nt(wb_ref[...], xn_ref[...]).astype(o_ref.dtype)
    else:
        o_ref[...] = _dot_nt(xn_ref[...], wb_ref[...]).astype(o_ref.dtype)


def _inproj(xn, wt, row_start, n_rows, tn, tm, out_dtype, feature_major):
    rows = xn.shape[0]
    assert n_rows % tn == 0 and rows % tm == 0 and row_start % SUBLANES == 0
    if row_start % tn == 0:
        j0 = row_start // tn
        w_spec = pl.BlockSpec((tn, D_MODEL), lambda j, i: (j0 + j, 0))
    else:
        w_spec = pl.BlockSpec((pl.Element(tn), pl.Element(D_MODEL)), lambda j, i: ((row_start // SUBLANES + j * (tn // SUBLANES)) * SUBLANES, 0))
    if feature_major:
        out_spec = pl.BlockSpec((tn, tm), lambda j, i: (j, i))
        out_shape = (n_rows, rows)
    else:
        out_spec = pl.BlockSpec((tm, tn), lambda j, i: (i, j))
        out_shape = (rows, n_rows)
    return pl.pallas_call(
        functools.partial(_inproj_kernel, feature_major),
        grid=(n_rows // tn, rows // tm),
        in_specs=[pl.BlockSpec((tm, D_MODEL), lambda j, i: (i, 0)), w_spec],
        out_specs=out_spec,
        out_shape=jax.ShapeDtypeStruct(out_shape, out_dtype),
        scratch_shapes=[pltpu.VMEM((tn, D_MODEL), BF16)],
        compiler_params=pltpu.CompilerParams(
            dimension_semantics=("arbitrary", "arbitrary"), vmem_limit_bytes=VMEM_LIMIT),
        name="inproj_t" if feature_major else "inproj",
    )(xn, wt)


def _inproj_kvi_kernel(x_ref, g_ref, wkv_ref, wwi_ref, wki_ref, xn_ref, kv_ref, ki_ref, wit_ref, k3_ref, v3_ref,
                       wkvb_ref, wwib_ref, wkib_ref):
    @pl.when(pl.program_id(0) == 0)
    def _():
        wkvb_ref[...] = wkv_ref[...].astype(BF16)
        wwib_ref[...] = wwi_ref[...].astype(BF16)
        wkib_ref[...] = wki_ref[...].astype(BF16)

    x = x_ref[...]
    ms = jnp.mean(x * x, axis=-1, keepdims=True)
    xn = ((x * lax.rsqrt(ms + RMS_EPS)) * g_ref[...]).astype(BF16)
    xn_ref[...] = xn
    kv = _dot_nt(xn, wkvb_ref[...])
    kv_ref[...] = kv
    for g in range(N_KV_HEADS):
        k3_ref[:, g, :] = kv[:, g * HEAD_DIM:(g + 1) * HEAD_DIM]
        v3_ref[:, g, :] = kv[:, KV_WIDTH + g * HEAD_DIM:KV_WIDTH + (g + 1) * HEAD_DIM]
    ki_ref[...] = _dot_nt(xn, wkib_ref[...])
    wit_ref[...] = _dot_nt(wwib_ref[...], xn)


def _inproj_kvi(x2d, g, wt, tm):
    rows = x2d.shape[0]
    assert rows % tm == 0 and COL_K % (2 * KV_WIDTH) == 0 and COL_WI % IDX_HEADS == 0
    return pl.pallas_call(
        _inproj_kvi_kernel,
        grid=(rows // tm,),
        in_specs=[
            pl.BlockSpec((tm, D_MODEL), lambda i: (i, 0)),
            pl.BlockSpec((1, D_MODEL), lambda i: (0, 0)),
            pl.BlockSpec((2 * KV_WIDTH, D_MODEL), lambda i: (COL_K // (2 * KV_WIDTH), 0)),
            pl.BlockSpec((IDX_HEADS, D_MODEL), lambda i: (COL_WI // IDX_HEADS, 0)),
            pl.BlockSpec((pl.Element(IDX_DIM), pl.Element(D_MODEL)), lambda i: (COL_KI, 0)),
        ],
        out_specs=[
            pl.BlockSpec((tm, D_MODEL), lambda i: (i, 0)),
            pl.BlockSpec((tm, 2 * KV_WIDTH), lambda i: (i, 0)),
            pl.BlockSpec((tm, IDX_DIM), lambda i: (i, 0)),
            pl.BlockSpec((IDX_HEADS, tm), lambda i: (0, i)),
            pl.BlockSpec((tm, N_KV_HEADS, HEAD_DIM), lambda i: (i, 0, 0)),
            pl.BlockSpec((tm, N_KV_HEADS, HEAD_DIM), lambda i: (i, 0, 0)),
        ],
        out_shape=[
            jax.ShapeDtypeStruct((rows, D_MODEL), BF16),
            jax.ShapeDtypeStruct((rows, 2 * KV_WIDTH), F32),
            jax.ShapeDtypeStruct((rows, IDX_DIM), F32),
            jax.ShapeDtypeStruct((IDX_HEADS, rows), F32),
            jax.ShapeDtypeStruct((rows, N_KV_HEADS, HEAD_DIM), F32),
            jax.ShapeDtypeStruct((rows, N_KV_HEADS, HEAD_DIM), F32),
        ],
        scratch_shapes=[
            pltpu.VMEM((2 * KV_WIDTH, D_MODEL), BF16),
            pltpu.VMEM((IDX_HEADS, D_MODEL), BF16),
            pltpu.VMEM((IDX_DIM, D_MODEL), BF16),
        ],
        compiler_params=pltpu.CompilerParams(
            dimension_semantics=("arbitrary",), vmem_limit_bytes=VMEM_LIMIT),
        name="inproj_kvi",
    )(x2d, g, wt, wt, wt)


def _prompt_attn_kernel(qt_ref, qit0_ref, qit1_ref, qit2_ref, qit3_ref, vt_ref, wit_ref, kvf_ref, kif_ref,
                        o_ref, kb_ref, kib_ref, vtc_ref, sc_ref, sa_ref, sb_ref, acc_ref, ml_ref, u_ref, cnt_ref):
    i = pl.program_id(1)
    seq = kvf_ref.shape[0]
    pos_bits = seq.bit_length()
    qit_refs = (qit0_ref, qit1_ref, qit2_ref, qit3_ref)
    heads_per_ref = IDX_HEADS // len(qit_refs)

    @pl.when(i == 0)
    def _():
        kb_ref[...] = kvf_ref[:, 0:KV_WIDTH].astype(BF16)
        kib_ref[...] = kif_ref[...].astype(BF16)
        for c in range(seq // KEY_CHUNK):
            for g in range(N_KV_HEADS):
                vtc_ref[c, g, :HEAD_DIM, :] = vt_ref[g * HEAD_DIM:(g + 1) * HEAD_DIM,
                                                     c * KEY_CHUNK:(c + 1) * KEY_CHUNK]
                vtc_ref[c, g, HEAD_DIM:, :] = jnp.ones((ONES_ROWS, KEY_CHUNK), BF16)

    n_chunks = (i * Q_BLOCK) // KEY_CHUNK + 1
    q_pos = i * Q_BLOCK + lax.broadcasted_iota(I32, (1, Q_BLOCK), 1)
    key0 = lax.broadcasted_iota(I32, (KEY_CHUNK, 1), 0)

    w_t = wit_ref[...] * IDX_W_SCALE
    qit_all = jnp.concatenate(
        [qit_refs[h // heads_per_ref][(h % heads_per_ref) * IDX_DIM:(h % heads_per_ref + 1) * IDX_DIM, :]
         for h in range(IDX_HEADS)], axis=1)

    def score_chunk(c):
        k0 = pl.multiple_of(c * KEY_CHUNK, KEY_CHUNK)
        res = jnp.dot(kib_ref[pl.ds(k0, KEY_CHUNK), :], qit_all, preferred_element_type=F32)
        acc = jnp.zeros((KEY_CHUNK, Q_BLOCK), F32)
        for h in range(IDX_HEADS):
            acc = acc + w_t[h:h + 1, :] * jnp.maximum(res[:, h * Q_BLOCK:(h + 1) * Q_BLOCK], 0.0)
        sc_ref[c] = jnp.where(k0 + key0 <= q_pos, acc, NEG_SENTINEL)

    def score_pair(t, carry):
        score_chunk(2 * t)
        score_chunk(2 * t + 1)
        return carry

    lax.fori_loop(0, n_chunks // 2, score_pair, 0)

    @pl.when(n_chunks % 2 == 1)
    def _():
        score_chunk(n_chunks - 1)

    def count(pred):
        def body(c, cnt):
            ones = jnp.where(pred(sc_ref[c], c), 1.0, 0.0)
            return cnt + ones.reshape(KEY_CHUNK // COUNT_ROWS, COUNT_ROWS, Q_BLOCK).sum(axis=0)
        part = lax.fori_loop(0, n_chunks, body, jnp.zeros((COUNT_ROWS, Q_BLOCK), F32))
        return jnp.sum(part, axis=0, keepdims=True)

    def search(j0, n, u, cnt_u, decode):
        def body(j, carry):
            u, cnt_u = carry
            cand = u | lax.shift_left(jnp.int32(1), 31 - j)
            thr_c = decode(cand)
            cnt = count(lambda blk, c: blk >= thr_c)
            ok = cnt >= float(TOPK_MAX)
            return jnp.where(ok, cand, u), jnp.where(ok, cnt, cnt_u)
        return lax.fori_loop(j0, j0 + n, body, (u, cnt_u))

    u = jnp.zeros((1, Q_BLOCK), I32)
    cnt_u = (u + n_chunks * KEY_CHUNK).astype(F32)
    u, cnt_u = search(0, SIGN_EXP_BITS, u, cnt_u, _ordered_bits_to_f32)
    signed_pow, zero_exp = _sign_exp_parts(u)

    def decode_mantissa(cand):
        return _with_mantissa(cand, signed_pow, zero_exp)

    u, cnt_u = search(SIGN_EXP_BITS, SEARCH_ALWAYS_BITS - SIGN_EXP_BITS, u, cnt_u, decode_mantissa)
    u_ref[...] = u
    cnt_ref[...] = cnt_u
    for j0 in range(SEARCH_ALWAYS_BITS, 32, SEARCH_GROUP_BITS):
        @pl.when(jnp.max(jnp.where(cnt_ref[...] == float(TOPK_MAX), 0.0, 1.0)) > 0.0)
        def _():
            u_g, cnt_g = search(j0, SEARCH_GROUP_BITS, u_ref[...], cnt_ref[...], decode_mantissa)
            u_ref[...] = u_g
            cnt_ref[...] = cnt_g
    u = u_ref[...]
    n_ge = cnt_ref[...]
    thr = decode_mantissa(u)
    thr_adm = jnp.maximum(thr, NEG_ABOVE_SENTINEL)
    excess = jnp.max(jnp.where((n_ge > float(TOPK_MAX)) & (thr > NEG_SENTINEL), 1.0, 0.0)) > 0.0

    @pl.when(excess)
    def _():
        need = float(TOPK_MAX) - count(lambda blk, c: blk > thr)

        def jbit_body(j, jl):
            cand = jl | lax.shift_left(jnp.int32(1), pos_bits - 1 - j)
            cnt = count(lambda blk, c: (blk == thr) & (c * KEY_CHUNK + key0 < cand))
            return jnp.where(cnt <= need, cand, jl)
        jlim = lax.fori_loop(0, pos_bits, jbit_body, jnp.zeros((1, Q_BLOCK), I32))

        def demote_body(c, carry):
            blk = sc_ref[c]
            dropped = (blk == thr) & (thr > NEG_SENTINEL) & (c * KEY_CHUNK + key0 >= jlim)
            sc_ref[c] = jnp.where(dropped, NEG_SENTINEL, blk)
            return carry
        lax.fori_loop(0, n_chunks, demote_body, 0)

    acc_ref[...] = jnp.zeros(acc_ref.shape, F32)
    cols = Q_PER_KV * Q_BLOCK
    qt_groups = [
        jnp.concatenate([qt_ref[(g * Q_PER_KV + h) * HEAD_DIM:(g * Q_PER_KV + h + 1) * HEAD_DIM, :]
                         for h in range(Q_PER_KV)], axis=1)
        for g in range(N_KV_HEADS)
    ]

    last_chunk = seq // KEY_CHUNK - 1
    eye = (lax.broadcasted_iota(I32, (Q_BLOCK, Q_BLOCK), 0)
           == lax.broadcasted_iota(I32, (Q_BLOCK, Q_BLOCK), 1))
    eye_heads = jnp.concatenate([jnp.where(eye, 1.0, 0.0).astype(BF16)] * Q_PER_KV, axis=1)
    qt_aug = [jnp.concatenate([qt_groups[g], eye_heads], axis=0) for g in range(N_KV_HEADS)]

    def qk_scores(c, dst_ref):
        cc = jnp.minimum(c, last_chunk)
        k0 = pl.multiple_of(cc * KEY_CHUNK, KEY_CHUNK)
        masked = jnp.where(sc_ref[cc] >= thr_adm, 0.0, MASKED_SCORE).astype(BF16)
        for g in range(N_KV_HEADS):
            kc = kb_ref[pl.ds(k0, KEY_CHUNK), g * HEAD_DIM:(g + 1) * HEAD_DIM]
            dst_ref[g] = jnp.dot(jnp.concatenate([kc, masked], axis=1), qt_aug[g], preferred_element_type=F32)

    def softmax_pv(c, src_ref):
        for g in range(N_KV_HEADS):
            m_old = ml_ref[2 * g:2 * g + 1, :]
            l_old = ml_ref[2 * g + 1:2 * g + 2, :]
            s = src_ref[g] * ATTN_SCALE_LOG2E
            m_new = jnp.maximum(m_old, jnp.max(s, axis=0, keepdims=True))
            m_safe = jnp.where(m_new < MASKED_LEVEL, 0.0, m_new)
            alpha = jnp.exp2(m_old - m_safe)
            p = jnp.exp2(s - m_safe).astype(BF16)
            pv = jnp.dot(vtc_ref[c, g], p, preferred_element_type=F32)
            acc_ref[g] = alpha * acc_ref[g] + pv[:HEAD_DIM]
            ml_ref[2 * g:2 * g + 1, :] = m_new
            ml_ref[2 * g + 1:2 * g + 2, :] = alpha * l_old + pv[HEAD_DIM:HEAD_DIM + 1]

    def pair_body(t, carry):
        qk_scores(2 * t + 1, sb_ref)
        softmax_pv(2 * t, sa_ref)
        qk_scores(2 * t + 2, sa_ref)
        softmax_pv(2 * t + 1, sb_ref)
        return carry

    for g in range(N_KV_HEADS):
        ml_ref[2 * g:2 * g + 1, :] = jnp.full((1, cols), -jnp.inf, F32)
        ml_ref[2 * g + 1:2 * g + 2, :] = jnp.zeros((1, cols), F32)
    qk_scores(0, sa_ref)
    n_pairs = (n_chunks - 1) // 2
    lax.fori_loop(0, n_pairs, pair_body, 0)

    @pl.when(n_chunks % 2 == 1)
    def _():
        softmax_pv(n_chunks - 1, sa_ref)

    @pl.when(n_chunks % 2 == 0)
    def _():
        qk_scores(n_chunks - 1, sb_ref)
        softmax_pv(n_chunks - 2, sa_ref)
        softmax_pv(n_chunks - 1, sb_ref)

    for g in range(N_KV_HEADS):
        o = acc_ref[g] / ml_ref[2 * g + 1:2 * g + 2, :]
        for h in range(Q_PER_KV):
            hd = (g * Q_PER_KV + h) * HEAD_DIM
            o_ref[:, hd:hd + HEAD_DIM] = o[:, h * Q_BLOCK:(h + 1) * Q_BLOCK].T


def _prompt_attn(h_t, wi_t, kv, ki, batch, seq):
    nb = seq // Q_BLOCK
    cols = Q_PER_KV * Q_BLOCK
    qi_rows = IDX_HEADS * IDX_DIM // 4
    qi_b0 = COL_QI // qi_rows

    def qi_spec(t):
        return pl.BlockSpec((qi_rows, Q_BLOCK), lambda b, i: (qi_b0 + t, b * nb + i))

    return pl.pallas_call(
        _prompt_attn_kernel,
        grid=(batch, nb),
        in_specs=[
            pl.BlockSpec((ATTN_WIDTH, Q_BLOCK), lambda b, i: (COL_Q // ATTN_WIDTH, b * nb + i)),
            qi_spec(0), qi_spec(1), qi_spec(2), qi_spec(3),
            pl.BlockSpec((KV_WIDTH, seq), lambda b, i: (COL_V // KV_WIDTH, b)),
            pl.BlockSpec((IDX_HEADS, Q_BLOCK), lambda b, i: (0, b * nb + i)),
            pl.BlockSpec((seq, 2 * KV_WIDTH), lambda b, i: (b, 0)),
            pl.BlockSpec((seq, IDX_DIM), lambda b, i: (b, 0)),
        ],
        out_specs=pl.BlockSpec((Q_BLOCK, ATTN_WIDTH), lambda b, i: (b * nb + i, 0)),
        out_shape=jax.ShapeDtypeStruct((batch * seq, ATTN_WIDTH), F32),
        scratch_shapes=[
            pltpu.VMEM((seq, KV_WIDTH), BF16),
            pltpu.VMEM((seq, IDX_DIM), BF16),
            pltpu.VMEM((seq // KEY_CHUNK, N_KV_HEADS, HEAD_DIM + ONES_ROWS, KEY_CHUNK), BF16),
            pltpu.VMEM((seq // KEY_CHUNK, KEY_CHUNK, Q_BLOCK), F32),
            pltpu.VMEM((N_KV_HEADS, KEY_CHUNK, cols), F32),
            pltpu.VMEM((N_KV_HEADS, KEY_CHUNK, cols), F32),
            pltpu.VMEM((N_KV_HEADS, HEAD_DIM, cols), F32),
            pltpu.VMEM((2 * N_KV_HEADS, cols), F32),
            pltpu.VMEM((1, Q_BLOCK), I32),
            pltpu.VMEM((1, Q_BLOCK), F32),
        ],
        compiler_params=pltpu.CompilerParams(
            dimension_semantics=("arbitrary", "arbitrary"), vmem_limit_bytes=VMEM_LIMIT),
        name="prompt_attn",
    )(h_t, h_t, h_t, h_t, h_t, h_t, wi_t, kv, ki)


PAGES_PER_STEP = 64


def _sample_select_kernel(pt_ref, qi_ref, kvs_ref, *refs):
    page_refs = refs[:PAGES_PER_STEP]
    idx_ref, meta_ref, sc_ref, snew_ref = refs[PAGES_PER_STEP:]
    b = pl.program_id(0)
    p = pl.program_id(1)
    nb = pl.num_programs(0)
    n_steps = pl.num_programs(1)
    n_pages, db, _ = sc_ref.shape
    pos_bits = (n_pages * PAGE_SIZE).bit_length()

    qi = qi_ref[0]
    w_col = kvs_ref[0, :, KVI_WI:KVI_WI + 1] * IDX_W_SCALE

    def key_scores(keys_bf16):
        s = _dot_nt(qi, keys_bf16)
        return jnp.sum(w_col * jnp.maximum(s, 0.0), axis=0, keepdims=True)

    keys = jnp.concatenate([page_refs[r][0, 0] for r in range(PAGES_PER_STEP)], axis=0).astype(BF16)
    row = key_scores(keys)
    for r in range(PAGES_PER_STEP):
        sc_ref[p * PAGES_PER_STEP + r, pl.ds(b, 1), :] = row[:, r * PAGE_SIZE:(r + 1) * PAGE_SIZE]

    @pl.when(p == n_steps - 1)
    def _():
        ki_new = kvs_ref[0, :, KVI_KI:KVI_KI + IDX_DIM].astype(BF16)
        s_new = key_scores(ki_new)[:, 0:1]
        snew_ref[pl.ds(b, 1), :] = jnp.broadcast_to(s_new, (1, PAGE_SIZE))

    @pl.when((b == nb - 1) & (p == n_steps - 1))
    def _():
        sc = sc_ref[...]
        s_new = snew_ref[:, 0:1]

        def total(x):
            return jnp.sum(jnp.sum(x, axis=0), axis=1, keepdims=True)

        def bit_body(j, u):
            cand = u | lax.shift_left(jnp.int32(1), 31 - j)
            thr = _ordered_bits_to_f32(cand)
            cnt = total(jnp.where(sc >= thr[None], 1.0, 0.0)) + jnp.where(s_new >= thr, 1.0, 0.0)
            return jnp.where(cnt >= float(TOPK_MAX), cand, u)

        u = lax.fori_loop(0, 32, bit_body, jnp.zeros((db, 1), I32))
        thr = _ordered_bits_to_f32(u)
        n_gt = total(jnp.where(sc > thr[None], 1.0, 0.0)) + jnp.where(s_new > thr, 1.0, 0.0)
        need = float(TOPK_MAX) - n_gt
        flat = (lax.broadcasted_iota(I32, sc.shape, 0) * PAGE_SIZE
                + lax.broadcasted_iota(I32, sc.shape, 2))
        tie = sc == thr[None]

        def jbit_body(j, jl):
            cand = jl | lax.shift_left(jnp.int32(1), pos_bits - 1 - j)
            cnt = total(jnp.where(tie & (flat < cand[None]), 1.0, 0.0))
            return jnp.where(cnt <= need, cand, jl)

        jlim = lax.fori_loop(0, pos_bits, jbit_body, jnp.zeros((db, 1), I32))
        sel = (sc > thr[None]) | (tie & (flat < jlim[None]))
        self32 = jnp.where(sel, 1.0, 0.0)
        n_pool = total(self32)

        n_rows = n_pages * db
        selb = self32.astype(BF16).reshape(n_rows, PAGE_SIZE)
        ri = lax.broadcasted_iota(I32, (PAGE_SIZE, PAGE_SIZE), 0)
        ci = lax.broadcasted_iota(I32, (PAGE_SIZE, PAGE_SIZE), 1)
        incl = jnp.dot(selb, jnp.where(ri <= ci, 1.0, 0.0).astype(BF16), preferred_element_type=F32)
        tot = jnp.broadcast_to(incl[:, PAGE_SIZE - 1:PAGE_SIZE], (n_rows, PAGE_SIZE)).astype(BF16)
        rr = lax.broadcasted_iota(I32, (n_rows, n_rows), 0)
        cc = lax.broadcasted_iota(I32, (n_rows, n_rows), 1)
        earlier_page_same_row = ((rr & (db - 1)) == (cc & (db - 1))) & (cc < (rr & -db))
        offs = jnp.dot(jnp.where(earlier_page_same_row, 1.0, 0.0).astype(BF16), tot,
                       preferred_element_type=F32)
        ends_rows = offs + tot.astype(F32)
        r_i = lax.broadcasted_iota(I32, (n_rows, PAGE_SIZE), 0)
        l_i = lax.broadcasted_iota(I32, (n_rows, PAGE_SIZE), 1)
        log2_db = db.bit_length() - 1
        diag = jnp.where(l_i == lax.shift_right_logical(r_i, log2_db), ends_rows, 0.0).astype(BF16)
        pick = jnp.where((lax.broadcasted_iota(I32, (db, n_rows), 1) & (db - 1))
                         == lax.broadcasted_iota(I32, (db, n_rows), 0), 1.0, 0.0).astype(BF16)
        ends = jnp.dot(pick, diag, preferred_element_type=F32)

        table = jnp.concatenate([incl.astype(BF16), offs.astype(BF16)], axis=1)
        slot = lax.broadcasted_iota(I32, (TOPK_MAX, PAGE_SIZE), 0).astype(F32)
        rows_lane = lax.broadcasted_iota(I32, (TOPK_MAX, n_rows), 1)
        for row_i in range(db):
            page_s = jnp.sum(jnp.where(ends[row_i:row_i + 1, :] <= slot, 1.0, 0.0), axis=1, keepdims=True)
            mrow = page_s.astype(I32) * db + row_i
            onehot = jnp.where(rows_lane == mrow, 1.0, 0.0).astype(BF16)
            got = jnp.dot(onehot, table, preferred_element_type=F32)
            k = slot - got[:, PAGE_SIZE:]
            off_s = jnp.sum(jnp.where(got[:, :PAGE_SIZE] <= k, 1.0, 0.0), axis=1, keepdims=True)
            pos = page_s * float(PAGE_SIZE) + off_s
            valid = slot[:, 0:1] < n_pool[row_i:row_i + 1, :]
            idx_ref[row_i] = jnp.where(valid, pos, 0.0).astype(I32)
        meta_ref[...] = jnp.broadcast_to(n_pool, (db, PAGE_SIZE)).astype(I32)


def _sample_select(page_table, qi_s, kvi_s16, pool_ki):
    db, n_pages = page_table.shape
    assert db & (db - 1) == 0, "row index arithmetic in the rank matrix assumes a power-of-two row count"
    assert n_pages == PAGE_SIZE, "page ends are laid out one page per lane"
    n_steps = n_pages // PAGES_PER_STEP

    def page_spec(r):
        return pl.BlockSpec(
            (1, 1, PAGE_SIZE, IDX_DIM),
            lambda b, p, pt: (0, pt[b, p * PAGES_PER_STEP + r], 0, 0))

    grid_spec = pltpu.PrefetchScalarGridSpec(
        num_scalar_prefetch=1,
        grid=(db, n_steps),
        in_specs=[
            pl.BlockSpec((1, IDX_HEADS, IDX_DIM), lambda b, p, pt: (b, 0, 0)),
            pl.BlockSpec((1, IDX_HEADS, KVI_WIDTH), lambda b, p, pt: (b, 0, 0)),
        ] + [page_spec(r) for r in range(PAGES_PER_STEP)],
        out_specs=[
            pl.BlockSpec((db, TOPK_MAX, 1), lambda b, p, pt: (0, 0, 0)),
            pl.BlockSpec((db, PAGE_SIZE), lambda b, p, pt: (0, 0)),
        ],
        scratch_shapes=[
            pltpu.VMEM((n_pages, db, PAGE_SIZE), F32),
            pltpu.VMEM((db, PAGE_SIZE), F32),
        ],
    )
    return pl.pallas_call(
        _sample_select_kernel,
        grid_spec=grid_spec,
        out_shape=[
            jax.ShapeDtypeStruct((db, TOPK_MAX, 1), I32),
            jax.ShapeDtypeStruct((db, PAGE_SIZE), I32),
        ],
        compiler_params=pltpu.CompilerParams(
            dimension_semantics=("arbitrary", "arbitrary"), vmem_limit_bytes=VMEM_LIMIT),
        name="sample_select",
    )(page_table, qi_s, kvi_s16, *([pool_ki] * PAGES_PER_STEP))


def _sample_attend_kernel(idx_ref, npool_ref, pt_ref, q_ref, kvs_ref, pk_ref, pv_ref, o_ref,
                          kbuf, vbuf, kflat, vflat, sem):
    b = pl.program_id(0)

    def row_copies(r):
        ix = idx_ref[b, r]
        page = pt_ref[b, lax.shift_right_logical(ix, 7)]
        off = ix & (PAGE_SIZE - 1)
        return (pltpu.make_async_copy(pk_ref.at[0, page, off], kbuf.at[r], sem.at[0]),
                pltpu.make_async_copy(pv_ref.at[0, page, off], vbuf.at[r], sem.at[1]))

    def start_body(r, carry):
        ck, cv = row_copies(r)
        ck.start()
        cv.start()
        return carry

    def wait_body(r, carry):
        ck, cv = row_copies(r)
        ck.wait()
        cv.wait()
        return carry

    lax.fori_loop(0, TOPK_MAX, start_body, 0, unroll=8)
    lax.fori_loop(0, TOPK_MAX, wait_body, 0, unroll=8)

    def head_copies(g):
        return (pltpu.make_async_copy(kbuf.at[:, g, :], kflat.at[g], sem.at[2]),
                pltpu.make_async_copy(vbuf.at[:, g, :], vflat.at[g], sem.at[3]))

    for g in range(N_KV_HEADS):
        for cp in head_copies(g):
            cp.start()
    for g in range(N_KV_HEADS):
        for cp in head_copies(g):
            cp.wait()

    n_pool = npool_ref[b]
    slot = lax.broadcasted_iota(I32, (1, TOPK_MAX), 1)
    slot_ok = slot < n_pool
    new_ok = n_pool < TOPK_MAX
    q_all = q_ref[0].astype(F32)
    for g in range(N_KV_HEADS):
        qg = q_all[g * Q_PER_KV:(g + 1) * Q_PER_KV, :].astype(BF16)
        ks = kflat[g].astype(BF16)
        vs = vflat[g].astype(BF16)
        k_new = kvs_ref[0, 0:1, KVI_K + g * HEAD_DIM:KVI_K + (g + 1) * HEAD_DIM].astype(BF16)
        v_new = kvs_ref[0, 0:1, KVI_V + g * HEAD_DIM:KVI_V + (g + 1) * HEAD_DIM].astype(BF16)
        s = jnp.where(slot_ok, _dot_nt(qg, ks) * ATTN_SCALE, -jnp.inf)
        s_new = jnp.sum(qg.astype(F32) * k_new.astype(F32), axis=-1, keepdims=True) * ATTN_SCALE
        s_new = jnp.where(new_ok, s_new, -jnp.inf)
        m = jnp.maximum(jnp.max(s, axis=-1, keepdims=True), s_new)
        p = jnp.exp(s - m)
        p_new = jnp.exp(s_new - m)
        denom = jnp.sum(p, axis=-1, keepdims=True) + p_new
        pn = (p / denom).astype(BF16)
        pn_new = (p_new / denom).astype(BF16).astype(F32)
        o = jnp.dot(pn, vs, preferred_element_type=F32) + pn_new * v_new.astype(F32)
        o_ref[0, g * Q_PER_KV:(g + 1) * Q_PER_KV, :] = o


def _sample_attend(idx, n_pool, page_table, q_s, kvi_s16, pool_k, pool_v):
    db = page_table.shape[0]
    grid_spec = pltpu.PrefetchScalarGridSpec(
        num_scalar_prefetch=3,
        grid=(db,),
        in_specs=[
            pl.BlockSpec((1, N_HEADS, HEAD_DIM), lambda b, *_: (b, 0, 0)),
            pl.BlockSpec((1, IDX_HEADS, KVI_WIDTH), lambda b, *_: (b, 0, 0)),
            pl.BlockSpec(memory_space=pl.ANY),
            pl.BlockSpec(memory_space=pl.ANY),
        ],
        out_specs=pl.BlockSpec((1, N_HEADS, HEAD_DIM), lambda b, *_: (b, 0, 0)),
        scratch_shapes=[
            pltpu.VMEM((TOPK_MAX, N_KV_HEADS, HEAD_DIM), F32),
            pltpu.VMEM((TOPK_MAX, N_KV_HEADS, HEAD_DIM), F32),
            pltpu.VMEM((N_KV_HEADS, TOPK_MAX, HEAD_DIM), F32),
            pltpu.VMEM((N_KV_HEADS, TOPK_MAX, HEAD_DIM), F32),
            pltpu.SemaphoreType.DMA((4,)),
        ],
    )
    return pl.pallas_call(
        _sample_attend_kernel,
        grid_spec=grid_spec,
        out_shape=jax.ShapeDtypeStruct((db, N_HEADS, HEAD_DIM), F32),
        compiler_params=pltpu.CompilerParams(dimension_semantics=("arbitrary",)),
        name="sample_attend",
    )(idx, n_pool, page_table, q_s, kvi_s16, pool_k, pool_v)


def _merge_kernel(chunked, oa_ref, za_ref, u_ref, v_ref, zb_ref, ga_ref, gb_ref, x_ref,
                  wpa_ref, wpb_ref, wout_ref, lng_ref, lnb_ref, ws_ref, bs_ref, gf_ref, y_ref, *rest):
    vn_ref, ob_ref = (None, rest[0]) if chunked else rest
    tm = x_ref.shape[0]
    u = _gelu(u_ref[...])
    v = _gelu(v_ref[...])
    mu = jnp.mean(v, axis=-1, keepdims=True)
    vc = v - mu
    vn = (vc * lax.rsqrt(jnp.mean(vc * vc, axis=-1, keepdims=True) + LN_EPS)) * lng_ref[...] + lnb_ref[...]
    if vn_ref is not None:
        vn_ref[...] = vn

    if chunked:
        vnb = vn.astype(BF16)
        ri = lax.broadcasted_iota(I32, (CHUNK, CHUNK), 0)
        ci = lax.broadcasted_iota(I32, (CHUNK, CHUNK), 1)
        for g in range(GMLP_GROUPS):
            wm = jnp.where(ri >= ci, ws_ref[g], 0.0).astype(BF16)
            lo = g * GMLP_GROUP_DIM
            for c in range(tm // CHUNK):
                mixed = jnp.dot(wm, vnb[c * CHUNK:(c + 1) * CHUNK, lo:lo + GMLP_GROUP_DIM],
                                preferred_element_type=F32) + bs_ref[:, g:g + 1]
                ob_ref[c * CHUNK:(c + 1) * CHUNK, lo:lo + GMLP_GROUP_DIM] = (
                    u[c * CHUNK:(c + 1) * CHUNK, lo:lo + GMLP_GROUP_DIM] * mixed)
    else:
        ob_ref[...] = u * (ws_ref[...] * vn + bs_ref[...])

    ha = (oa_ref[...] * _silu(za_ref[...])).astype(BF16)
    hb = (ob_ref[...] * _silu(zb_ref[...])).astype(BF16)
    ya = jnp.dot(ha, wpa_ref[...], preferred_element_type=F32)
    yb = jnp.dot(hb, wpb_ref[...], preferred_element_type=F32)
    mix = jax.nn.sigmoid(ga_ref[...]) * ya + jax.nn.sigmoid(gb_ref[...]) * yb
    out = x_ref[...] + jnp.dot(mix.astype(BF16), wout_ref[...], preferred_element_type=F32)
    ms = jnp.mean(out * out, axis=-1, keepdims=True)
    y_ref[...] = (out * lax.rsqrt(ms + RMS_EPS)) * gf_ref[...]


def _merge(chunked, o_a, gate, x2d, wpa, wpb, wout, lng, lnb, ws, bs, gf, tm):
    rows = x2d.shape[0]
    gw = GMLP_WIDTH

    def const(shape):
        return pl.BlockSpec(shape, lambda i: (0,) * len(shape), pipeline_mode=pl.Buffered(1))

    return pl.pallas_call(
        functools.partial(_merge_kernel, chunked),
        grid=(rows // tm,),
        in_specs=[
            pl.BlockSpec((tm, ATTN_WIDTH), lambda i: (i, 0)),
            pl.BlockSpec((tm, gw), lambda i: (i, 0)),
            pl.BlockSpec((tm, gw), lambda i: (i, 1)),
            pl.BlockSpec((tm, gw), lambda i: (i, 2)),
            pl.BlockSpec((tm, gw), lambda i: (i, 3)),
            pl.BlockSpec((tm, D_MODEL), lambda i: (i, 2)),
            pl.BlockSpec((tm, D_MODEL), lambda i: (i, 3)),
            pl.BlockSpec((tm, D_MODEL), lambda i: (i, 0)),
            const(wpa.shape), const(wpb.shape), const(wout.shape),
            const(lng.shape), const(lnb.shape), const(ws.shape), const(bs.shape), const(gf.shape),
        ],
        out_specs=[pl.BlockSpec((tm, D_MODEL), lambda i: (i, 0))]
        + ([] if chunked else [pl.BlockSpec((tm, gw), lambda i: (i, 0))]),
        out_shape=[jax.ShapeDtypeStruct((rows, D_MODEL), F32)]
        + ([] if chunked else [jax.ShapeDtypeStruct((rows, gw), F32)]),
        scratch_shapes=[pltpu.VMEM((tm, gw), F32)],
        compiler_params=pltpu.CompilerParams(
            dimension_semantics=("arbitrary",), vmem_limit_bytes=VMEM_LIMIT),
        name="merge_prompt" if chunked else "merge_sample",
    )(o_a, gate, gate, gate, gate, gate, gate, x2d, wpa, wpb, wout, lng, lnb, ws, bs, gf)


def kernel(x_prompt, x_sample, cache_k, cache_v, cache_k_idx, page_table, norm_in_g, w_in,
           w_proj_a, w_proj_b, w_out, ln_g, ln_b, w_spatial, b_spatial, norm_f_g):
    depth = w_in.shape[0]
    assert depth == 1, "single trunk layer"
    batch, seq, _ = x_prompt.shape
    db, dseq, _ = x_sample.shape
    assert dseq == 1
    l = 0
    wpa = w_proj_a[l].astype(BF16)
    wpb = w_proj_b[l].astype(BF16)
    wout = w_out[l].astype(BF16)
    g_in = norm_in_g[l].reshape(1, D_MODEL)
    lng = ln_g[l].reshape(1, GMLP_WIDTH)
    lnb = ln_b[l].reshape(1, GMLP_WIDTH)
    gf = norm_f_g.reshape(1, D_MODEL)

    xp = x_prompt.reshape(batch * seq, D_MODEL)
    wt = w_in[l].T
    xn_p, kv_p, ki_p2, wi_t, k3_p, v3_p = _inproj_kvi(xp, g_in, wt, PROJ_TOKEN_TILE)
    h_t = _inproj(xn_p, wt, COL_Q, COL_WI, COL_WI // 4, PROJ_TOKEN_TILE, BF16, True)
    gate_p = _inproj(xn_p, wt, GATE_COL0, GATE_WIDTH, GATE_FEATURE_TILE, PROJ_TOKEN_TILE, F32, False)
    oa_p = _prompt_attn(h_t, wi_t, kv_p, ki_p2, batch, seq)
    bs_t = b_spatial[l].T
    (y_p,) = _merge(True, oa_p, gate_p, xp, wpa, wpb, wout, lng, lnb, w_spatial[l], bs_t, gf, MERGE_TOKEN_TILE)

    xs = x_sample.reshape(db, D_MODEL)
    xn_s, kv_s, ki_s2, wi_t_s, _, _ = _inproj_kvi(xs, g_in, wt, db)
    h_s = _inproj(xn_s, wt, COL_Q, COL_WI, 512, db, F32, False)
    gate_s = _inproj(xn_s, wt, GATE_COL0, GATE_WIDTH, GATE_FEATURE_TILE, db, F32, False)
    kvi_s = jnp.concatenate(
        [kv_s, ki_s2, wi_t_s.T, jnp.zeros((db, KVI_WIDTH - KVI_WI - IDX_HEADS), F32)], axis=1)
    q_s = h_s[:, COL_Q:COL_K].astype(BF16).reshape(db, N_HEADS, HEAD_DIM)
    qi_s = h_s[:, COL_QI:COL_WI].astype(BF16).reshape(db, IDX_HEADS, IDX_DIM)
    wi_col = kvi_s[:, KVI_WI:KVI_WI + IDX_HEADS].reshape(db, IDX_HEADS, 1)
    kvi_s16 = jnp.broadcast_to(kvi_s[:, None, :], (db, IDX_HEADS, KVI_WIDTH))
    kvi_s16 = lax.dynamic_update_slice(kvi_s16, wi_col, (0, 0, KVI_WI))
    idx, meta = _sample_select(page_table, qi_s, kvi_s16, cache_k_idx)
    oa_s = _sample_attend(idx.reshape(db, TOPK_MAX), meta[:, 0], page_table, q_s, kvi_s16,
                          cache_k, cache_v).reshape(db, ATTN_WIDTH)
    ws0 = jnp.repeat(w_spatial[l][:, 0, 0], GMLP_GROUP_DIM).reshape(1, GMLP_WIDTH)
    bs0 = jnp.repeat(b_spatial[l][:, 0], GMLP_GROUP_DIM).reshape(1, GMLP_WIDTH)
    y_s, vn_s = _merge(False, oa_s, gate_s, xs, wpa, wpb, wout, lng, lnb, ws0, bs0, gf, db)

    def kv_out(kvi, lead):
        k = kvi[:, KVI_K:KVI_K + KV_WIDTH].reshape((1,) + lead + (N_KV_HEADS, HEAD_DIM))
        v = kvi[:, KVI_V:KVI_V + KV_WIDTH].reshape((1,) + lead + (N_KV_HEADS, HEAD_DIM))
        ki = kvi[:, KVI_KI:KVI_KI + IDX_DIM].reshape((1,) + lead + (IDX_DIM,))
        return k, v, ki

    k_p = k3_p.reshape(1, batch, seq, N_KV_HEADS, HEAD_DIM)
    v_p = v3_p.reshape(1, batch, seq, N_KV_HEADS, HEAD_DIM)
    ki_p = ki_p2.reshape(1, batch, seq, IDX_DIM)
    k_s, v_s, ki_s = kv_out(kvi_s, (db, dseq))
    return (y_p.reshape(batch, seq, D_MODEL), y_s.reshape(db, dseq, D_MODEL),
            k_p, v_p, ki_p, k_s, v_s, ki_s, vn_s.reshape(1, db, dseq, GMLP_WIDTH))
```

```python
import functools

import jax
import jax.numpy as jnp
import numpy as np
from jax import lax
from jax.experimental import pallas as pl
from jax.experimental.pallas import tpu as pltpu

F32 = jnp.float32
BF16 = jnp.bfloat16
I32 = jnp.int32

D_MODEL = 2048
N_HEADS = 8
N_KV_HEADS = 2
HEAD_DIM = 128
Q_PER_KV = N_HEADS // N_KV_HEADS
ATTN_WIDTH = N_HEADS * HEAD_DIM
KV_WIDTH = N_KV_HEADS * HEAD_DIM
IDX_HEADS = 16
IDX_DIM = 128
TOPK_MAX = 256
PAGE_SIZE = 128
GMLP_WIDTH = 1024
GMLP_GROUPS = 8
GMLP_GROUP_DIM = GMLP_WIDTH // GMLP_GROUPS
CHUNK = 128
RMS_EPS = 1e-6
LN_EPS = 1e-5

SUBLANES = 8
COL_Q = 0
COL_K = COL_Q + ATTN_WIDTH
COL_V = COL_K + KV_WIDTH
COL_QI = COL_V + KV_WIDTH
COL_WI = COL_QI + IDX_HEADS * IDX_DIM
COL_KI = COL_WI + IDX_HEADS
GATE_COL0 = COL_KI + IDX_DIM
GATE_WIDTH = 4 * GMLP_WIDTH + 2 * D_MODEL
KVI_K = 0
KVI_V = KVI_K + KV_WIDTH
KVI_KI = KVI_V + KV_WIDTH
KVI_WI = KVI_KI + IDX_DIM
KVI_WIDTH = KVI_WI + 128

Q_BLOCK = 128
KEY_CHUNK = 512
COUNT_ROWS = 32
ONES_ROWS = 16
SIGN_EXP_BITS = 9
SEARCH_ALWAYS_BITS = 20
SEARCH_GROUP_BITS = 4
NEG_SENTINEL = float(np.finfo(np.float32).min)
NEG_ABOVE_SENTINEL = float(np.nextafter(np.float32(NEG_SENTINEL), np.float32(0.0)))
INT_MIN = -(2 ** 31)
F32_MANTISSA_BITS = 23
F32_MANTISSA_MASK = (1 << F32_MANTISSA_BITS) - 1
F32_EXP_MASK = 0xFF
F32_MIN_NORMAL = float(2.0 ** -126)
IDX_W_SCALE = float(IDX_HEADS ** -0.5 * IDX_DIM ** -0.5)
ATTN_SCALE = float(HEAD_DIM ** -0.5)
ATTN_SCALE_LOG2E = float(HEAD_DIM ** -0.5 * np.log2(np.e))
MASKED_SCORE = float(jnp.finfo(jnp.bfloat16).min)
MASKED_LEVEL = -1e30
V7X_VMEM_BYTES = 64 * 1024 * 1024
VMEM_LIMIT = V7X_VMEM_BYTES - 8 * 1024 * 1024
PROJ_TOKEN_TILE = 1024
GATE_FEATURE_TILE = 1024
MERGE_TOKEN_TILE = 256


def _dot_nt(a, b):
    return lax.dot_general(a, b, (((1,), (1,)), ((), ())), preferred_element_type=F32)


def _ordered_bits_to_f32(u):
    signed_pow, zero_exp = _sign_exp_parts(u)
    return _with_mantissa(u, signed_pow, zero_exp)


def _sign_exp_parts(u):
    bits = jnp.where(u < 0, u ^ INT_MIN, ~u)
    exp = lax.shift_right_logical(bits, F32_MANTISSA_BITS) & F32_EXP_MASK
    e1 = exp - 1
    p = jnp.full(u.shape, F32_MIN_NORMAL, F32)
    for b in range(7):
        p = jnp.where(((e1 >> b) & 1) == 1, p * float(2.0 ** (2 ** b)), p)
    top = ((e1 >> 7) & 1) == 1
    p = jnp.where(top, p * float(2.0 ** 64), p)
    p = jnp.where(top, p * float(2.0 ** 64), p)
    return jnp.where(bits < 0, -p, p), exp == 0


def _with_mantissa(u, signed_pow, zero_exp):
    low = jnp.where(u < 0, u, ~u) & F32_MANTISSA_MASK
    mant = 1.0 + low.astype(F32) * float(2.0 ** -F32_MANTISSA_BITS)
    return jnp.where(zero_exp, 0.0, signed_pow * mant)


def _gelu(x):
    return 0.5 * x * (1.0 + lax.erf(x * float(np.sqrt(0.5))))


def _silu(x):
    return x * jax.nn.sigmoid(x)


def _inproj_kernel(feature_major, xn_ref, w_ref, o_ref, wb_ref):
    @pl.when(pl.program_id(1) == 0)
    def _():
        wb_ref[...] = w_ref[...].astype(BF16)

    if feature_major:
        o_ref[...] = _dot_nt(wb_ref[...], xn_ref[...]).astype(o_ref.dtype)
    else:
        o_ref[...] = _dot_nt(xn_ref[...], wb_ref[...]).astype(o_ref.dtype)


def _inproj(xn, wt, row_start, n_rows, tn, tm, out_dtype, feature_major):
    rows = xn.shape[0]
    assert n_rows % tn == 0 and rows % tm == 0 and row_start % SUBLANES == 0
    if row_start % tn == 0:
        j0 = row_start // tn
        w_spec = pl.BlockSpec((tn, D_MODEL), lambda j, i: (j0 + j, 0))
    else:
        w_spec = pl.BlockSpec((pl.Element(tn), pl.Element(D_MODEL)), lambda j, i: ((row_start // SUBLANES + j * (tn // SUBLANES)) * SUBLANES, 0))
    if feature_major:
        out_spec = pl.BlockSpec((tn, tm), lambda j, i: (j, i))
        out_shape = (n_rows, rows)
    else:
        out_spec = pl.BlockSpec((tm, tn), lambda j, i: (i, j))
        out_shape = (rows, n_rows)
    return pl.pallas_call(
        functools.partial(_inproj_kernel, feature_major),
        grid=(n_rows // tn, rows // tm),
        in_specs=[pl.BlockSpec((tm, D_MODEL), lambda j, i: (i, 0)), w_spec],
        out_specs=out_spec,
        out_shape=jax.ShapeDtypeStruct(out_shape, out_dtype),
        scratch_shapes=[pltpu.VMEM((tn, D_MODEL), BF16)],
        compiler_params=pltpu.CompilerParams(
            dimension_semantics=("arbitrary", "arbitrary"), vmem_limit_bytes=VMEM_LIMIT),
        name="inproj_t" if feature_major else "inproj",
    )(xn, wt)


def _inproj_kvi_kernel(x_ref, g_ref, wkv_ref, wwi_ref, wki_ref, xn_ref, kv_ref, ki_ref, wit_ref, k3_ref, v3_ref,
                       wkvb_ref, wwib_ref, wkib_ref):
    @pl.when(pl.program_id(0) == 0)
    def _():
        wkvb_ref[...] = wkv_ref[...].astype(BF16)
        wwib_ref[...] = wwi_ref[...].astype(BF16)
        wkib_ref[...] = wki_ref[...].astype(BF16)

    x = x_ref[...]
    ms = jnp.mean(x * x, axis=-1, keepdims=True)
    xn = ((x * lax.rsqrt(ms + RMS_EPS)) * g_ref[...]).astype(BF16)
    xn_ref[...] = xn
    kv = _dot_nt(xn, wkvb_ref[...])
    kv_ref[...] = kv
    for g in range(N_KV_HEADS):
        k3_ref[:, g, :] = kv[:, g * HEAD_DIM:(g + 1) * HEAD_DIM]
        v3_ref[:, g, :] = kv[:, KV_WIDTH + g * HEAD_DIM:KV_WIDTH + (g + 1) * HEAD_DIM]
    ki_ref[...] = _dot_nt(xn, wkib_ref[...])
    wit_ref[...] = _dot_nt(wwib_ref[...], xn)


def _inproj_kvi(x2d, g, wt, tm):
    rows = x2d.shape[0]
    assert rows % tm == 0 and COL_K % (2 * KV_WIDTH) == 0 and COL_WI % IDX_HEADS == 0
    return pl.pallas_call(
        _inproj_kvi_kernel,
        grid=(rows // tm,),
        in_specs=[
            pl.BlockSpec((tm, D_MODEL), lambda i: (i, 0)),
            pl.BlockSpec((1, D_MODEL), lambda i: (0, 0)),
            pl.BlockSpec((2 * KV_WIDTH, D_MODEL), lambda i: (COL_K // (2 * KV_WIDTH), 0)),
            pl.BlockSpec((IDX_HEADS, D_MODEL), lambda i: (COL_WI // IDX_HEADS, 0)),
            pl.BlockSpec((pl.Element(IDX_DIM), pl.Element(D_MODEL)), lambda i: (COL_KI, 0)),
        ],
        out_specs=[
            pl.BlockSpec((tm, D_MODEL), lambda i: (i, 0)),
            pl.BlockSpec((tm, 2 * KV_WIDTH), lambda i: (i, 0)),
            pl.BlockSpec((tm, IDX_DIM), lambda i: (i, 0)),
            pl.BlockSpec((IDX_HEADS, tm), lambda i: (0, i)),
            pl.BlockSpec((tm, N_KV_HEADS, HEAD_DIM), lambda i: (i, 0, 0)),
            pl.BlockSpec((tm, N_KV_HEADS, HEAD_DIM), lambda i: (i, 0, 0)),
        ],
        out_shape=[
            jax.ShapeDtypeStruct((rows, D_MODEL), BF16),
            jax.ShapeDtypeStruct((rows, 2 * KV_WIDTH), F32),
            jax.ShapeDtypeStruct((rows, IDX_DIM), F32),
            jax.ShapeDtypeStruct((IDX_HEADS, rows), F32),
            jax.ShapeDtypeStruct((rows, N_KV_HEADS, HEAD_DIM), F32),
            jax.ShapeDtypeStruct((rows, N_KV_HEADS, HEAD_DIM), F32),
        ],
        scratch_shapes=[
            pltpu.VMEM((2 * KV_WIDTH, D_MODEL), BF16),
            pltpu.VMEM((IDX_HEADS, D_MODEL), BF16),
            pltpu.VMEM((IDX_DIM, D_MODEL), BF16),
        ],
        compiler_params=pltpu.CompilerParams(
            dimension_semantics=("arbitrary",), vmem_limit_bytes=VMEM_LIMIT),
        name="inproj_kvi",
    )(x2d, g, wt, wt, wt)


def _prompt_attn_kernel(qt_ref, qit0_ref, qit1_ref, qit2_ref, qit3_ref, vt_ref, wit_ref, kvf_ref, kif_ref,
                        o_ref, kb_ref, kib_ref, vtc_ref, sc_ref, sa_ref, sb_ref, acc_ref, ml_ref, u_ref, cnt_ref):
    i = pl.program_id(1)
    seq = kvf_ref.shape[0]
    pos_bits = seq.bit_length()
    qit_refs = (qit0_ref, qit1_ref, qit2_ref, qit3_ref)
    heads_per_ref = IDX_HEADS // len(qit_refs)

    @pl.when(i == 0)
    def _():
        kb_ref[...] = kvf_ref[:, 0:KV_WIDTH].astype(BF16)
        kib_ref[...] = kif_ref[...].astype(BF16)
        for c in range(seq // KEY_CHUNK):
            for g in range(N_KV_HEADS):
                vtc_ref[c, g, :HEAD_DIM, :] = vt_ref[g * HEAD_DIM:(g + 1) * HEAD_DIM,
                                                     c * KEY_CHUNK:(c + 1) * KEY_CHUNK]
                vtc_ref[c, g, HEAD_DIM:, :] = jnp.ones((ONES_ROWS, KEY_CHUNK), BF16)

    n_chunks = (i * Q_BLOCK) // KEY_CHUNK + 1
    q_pos = i * Q_BLOCK + lax.broadcasted_iota(I32, (1, Q_BLOCK), 1)
    key0 = lax.broadcasted_iota(I32, (KEY_CHUNK, 1), 0)

    w_t = wit_ref[...] * IDX_W_SCALE
    qit_all = jnp.concatenate(
        [qit_refs[h // heads_per_ref][(h % heads_per_ref) * IDX_DIM:(h % heads_per_ref + 1) * IDX_DIM, :]
         for h in range(IDX_HEADS)], axis=1)

    def score_chunk(c):
        k0 = pl.multiple_of(c * KEY_CHUNK, KEY_CHUNK)
        res = jnp.dot(kib_ref[pl.ds(k0, KEY_CHUNK), :], qit_all, preferred_element_type=F32)
        acc = jnp.zeros((KEY_CHUNK, Q_BLOCK), F32)
        for h in range(IDX_HEADS):
            acc = acc + w_t[h:h + 1, :] * jnp.maximum(res[:, h * Q_BLOCK:(h + 1) * Q_BLOCK], 0.0)
        sc_ref[c] = jnp.where(k0 + key0 <= q_pos, acc, NEG_SENTINEL)

    def score_pair(t, carry):
        score_chunk(2 * t)
        score_chunk(2 * t + 1)
        return carry

    lax.fori_loop(0, n_chunks // 2, score_pair, 0)

    @pl.when(n_chunks % 2 == 1)
    def _():
        score_chunk(n_chunks - 1)

    def count(pred):
        def body(c, cnt):
            ones = jnp.where(pred(sc_ref[c], c), 1.0, 0.0)
            return cnt + ones.reshape(KEY_CHUNK // COUNT_ROWS, COUNT_ROWS, Q_BLOCK).sum(axis=0)
        part = lax.fori_loop(0, n_chunks, body, jnp.zeros((COUNT_ROWS, Q_BLOCK), F32))
        return jnp.sum(part, axis=0, keepdims=True)

    def search(j0, n, u, cnt_u, decode):
        def body(j, carry):
            u, cnt_u = carry
            cand = u | lax.shift_left(jnp.int32(1), 31 - j)
            thr_c = decode(cand)
            cnt = count(lambda blk, c: blk >= thr_c)
            ok = cnt >= float(TOPK_MAX)
            return jnp.where(ok, cand, u), jnp.where(ok, cnt, cnt_u)
        return lax.fori_loop(j0, j0 + n, body, (u, cnt_u))

    u = jnp.zeros((1, Q_BLOCK), I32)
    cnt_u = (u + n_chunks * KEY_CHUNK).astype(F32)
    u, cnt_u = search(0, SIGN_EXP_BITS, u, cnt_u, _ordered_bits_to_f32)
    signed_pow, zero_exp = _sign_exp_parts(u)

    def decode_mantissa(cand):
        return _with_mantissa(cand, signed_pow, zero_exp)

    u, cnt_u = search(SIGN_EXP_BITS, SEARCH_ALWAYS_BITS - SIGN_EXP_BITS, u, cnt_u, decode_mantissa)
    u_ref[...] = u
    cnt_ref[...] = cnt_u
    for j0 in range(SEARCH_ALWAYS_BITS, 32, SEARCH_GROUP_BITS):
        @pl.when(jnp.max(jnp.where(cnt_ref[...] == float(TOPK_MAX), 0.0, 1.0)) > 0.0)
        def _():
            u_g, cnt_g = search(j0, SEARCH_GROUP_BITS, u_ref[...], cnt_ref[...], decode_mantissa)
            u_ref[...] = u_g
            cnt_ref[...] = cnt_g
    u = u_ref[...]
    n_ge = cnt_ref[...]
    thr = decode_mantissa(u)
    thr_adm = jnp.maximum(thr, NEG_ABOVE_SENTINEL)
    excess = jnp.max(jnp.where((n_ge > float(TOPK_MAX)) & (thr > NEG_SENTINEL), 1.0, 0.0)) > 0.0

    @pl.when(excess)
    def _():
        need = float(TOPK_MAX) - count(lambda blk, c: blk > thr)

        def jbit_body(j, jl):
            cand = jl | lax.shift_left(jnp.int32(1), pos_bits - 1 - j)
            cnt = count(lambda blk, c: (blk == thr) & (c * KEY_CHUNK + key0 < cand))
            return jnp.where(cnt <= need, cand, jl)
        jlim = lax.fori_loop(0, pos_bits, jbit_body, jnp.zeros((1, Q_BLOCK), I32))

        def demote_body(c, carry):
            blk = sc_ref[c]
            dropped = (blk == thr) & (thr > NEG_SENTINEL) & (c * KEY_CHUNK + key0 >= jlim)
            sc_ref[c] = jnp.where(dropped, NEG_SENTINEL, blk)
            return carry
        lax.fori_loop(0, n_chunks, demote_body, 0)

    acc_ref[...] = jnp.zeros(acc_ref.shape, F32)
    cols = Q_PER_KV * Q_BLOCK
    qt_groups = [
        jnp.concatenate([qt_ref[(g * Q_PER_KV + h) * HEAD_DIM:(g * Q_PER_KV + h + 1) * HEAD_DIM, :]
                         for h in range(Q_PER_KV)], axis=1)
        for g in range(N_KV_HEADS)
    ]

    eye = (lax.broadcasted_iota(I32, (Q_BLOCK, Q_BLOCK), 0)
           == lax.broadcasted_iota(I32, (Q_BLOCK, Q_BLOCK), 1))
    eye_heads = jnp.concatenate([jnp.where(eye, 1.0, 0.0).astype(BF16)] * Q_PER_KV, axis=1)
    qt_aug = [jnp.concatenate([qt_groups[g], eye_heads], axis=0) for g in range(N_KV_HEADS)]

    def qk_scores(c, dst_ref):
        k0 = pl.multiple_of(c * KEY_CHUNK, KEY_CHUNK)
        masked = jnp.where(sc_ref[c] >= thr_adm, 0.0, MASKED_SCORE).astype(BF16)
        for g in range(N_KV_HEADS):
            kc = kb_ref[pl.ds(k0, KEY_CHUNK), g * HEAD_DIM:(g + 1) * HEAD_DIM]
            dst_ref[g] = jnp.dot(jnp.concatenate([kc, masked], axis=1), qt_aug[g], preferred_element_type=F32)

    def softmax_pv(c, src_ref):
        for g in range(N_KV_HEADS):
            m_old = ml_ref[2 * g:2 * g + 1, :]
            l_old = ml_ref[2 * g + 1:2 * g + 2, :]
            s = src_ref[g] * ATTN_SCALE_LOG2E
            m_new = jnp.maximum(m_old, jnp.max(s, axis=0, keepdims=True))
            m_safe = jnp.where(m_new < MASKED_LEVEL, 0.0, m_new)
            alpha = jnp.exp2(m_old - m_safe)
            p = jnp.exp2(s - m_safe).astype(BF16)
            pv = jnp.dot(vtc_ref[c, g], p, preferred_element_type=F32)
            acc_ref[g] = alpha * acc_ref[g] + pv[:HEAD_DIM]
            ml_ref[2 * g:2 * g + 1, :] = m_new
            ml_ref[2 * g + 1:2 * g + 2, :] = alpha * l_old + pv[HEAD_DIM:HEAD_DIM + 1]

    def pair_body(t, carry):
        qk_scores(2 * t + 1, sb_ref)
        softmax_pv(2 * t, sa_ref)
        qk_scores(2 * t + 2, sa_ref)
        softmax_pv(2 * t + 1, sb_ref)
        return carry

    for g in range(N_KV_HEADS):
        ml_ref[2 * g:2 * g + 1, :] = jnp.full((1, cols), -jnp.inf, F32)
        ml_ref[2 * g + 1:2 * g + 2, :] = jnp.zeros((1, cols), F32)
    qk_scores(0, sa_ref)
    n_pairs = (n_chunks - 1) // 2
    lax.fori_loop(0, n_pairs, pair_body, 0)

    @pl.when(n_chunks % 2 == 1)
    def _():
        softmax_pv(n_chunks - 1, sa_ref)

    @pl.when(n_chunks % 2 == 0)
    def _():
        qk_scores(n_chunks - 1, sb_ref)
        softmax_pv(n_chunks - 2, sa_ref)
        softmax_pv(n_chunks - 1, sb_ref)

    for g in range(N_KV_HEADS):
        o = acc_ref[g] / ml_ref[2 * g + 1:2 * g + 2, :]
        for h in range(Q_PER_KV):
            hd = (g * Q_PER_KV + h) * HEAD_DIM
            o_ref[:, hd:hd + HEAD_DIM] = o[:, h * Q_BLOCK:(h + 1) * Q_BLOCK].T


def _prompt_attn(h_t, wi_t, kv, ki, batch, seq):
    nb = seq // Q_BLOCK
    cols = Q_PER_KV * Q_BLOCK
    qi_rows = IDX_HEADS * IDX_DIM // 4
    qi_b0 = COL_QI // qi_rows

    def qi_spec(t):
        return pl.BlockSpec((qi_rows, Q_BLOCK), lambda b, i: (qi_b0 + t, b * nb + i))

    return pl.pallas_call(
        _prompt_attn_kernel,
        grid=(batch, nb),
        in_specs=[
            pl.BlockSpec((ATTN_WIDTH, Q_BLOCK), lambda b, i: (COL_Q // ATTN_WIDTH, b * nb + i)),
            qi_spec(0), qi_spec(1), qi_spec(2), qi_spec(3),
            pl.BlockSpec((KV_WIDTH, seq), lambda b, i: (COL_V // KV_WIDTH, b)),
            pl.BlockSpec((IDX_HEADS, Q_BLOCK), lambda b, i: (0, b * nb + i)),
            pl.BlockSpec((seq, 2 * KV_WIDTH), lambda b, i: (b, 0)),
            pl.BlockSpec((seq, IDX_DIM), lambda b, i: (b, 0)),
        ],
        out_specs=pl.BlockSpec((Q_BLOCK, ATTN_WIDTH), lambda b, i: (b * nb + i, 0)),
        out_shape=jax.ShapeDtypeStruct((batch * seq, ATTN_WIDTH), F32),
        scratch_shapes=[
            pltpu.VMEM((seq, KV_WIDTH), BF16),
            pltpu.VMEM((seq, IDX_DIM), BF16),
            pltpu.VMEM((seq // KEY_CHUNK, N_KV_HEADS, HEAD_DIM + ONES_ROWS, KEY_CHUNK), BF16),
            pltpu.VMEM((seq // KEY_CHUNK, KEY_CHUNK, Q_BLOCK), F32),
            pltpu.VMEM((N_KV_HEADS, KEY_CHUNK, cols), F32),
            pltpu.VMEM((N_KV_HEADS, KEY_CHUNK, cols), F32),
            pltpu.VMEM((N_KV_HEADS, HEAD_DIM, cols), F32),
            pltpu.VMEM((2 * N_KV_HEADS, cols), F32),
            pltpu.VMEM((1, Q_BLOCK), I32),
            pltpu.VMEM((1, Q_BLOCK), F32),
        ],
        compiler_params=pltpu.CompilerParams(
            dimension_semantics=("arbitrary", "arbitrary"), vmem_limit_bytes=VMEM_LIMIT),
        name="prompt_attn",
    )(h_t, h_t, h_t, h_t, h_t, h_t, wi_t, kv, ki)


PAGES_PER_STEP = 64


def _sample_select_kernel(pt_ref, qi_ref, kvs_ref, *refs):
    page_refs = refs[:PAGES_PER_STEP]
    idx_ref, meta_ref, sc_ref, snew_ref = refs[PAGES_PER_STEP:]
    b = pl.program_id(0)
    p = pl.program_id(1)
    nb = pl.num_programs(0)
    n_steps = pl.num_programs(1)
    n_pages, db, _ = sc_ref.shape
    pos_bits = (n_pages * PAGE_SIZE).bit_length()

    qi = qi_ref[0]
    w_col = kvs_ref[0, :, KVI_WI:KVI_WI + 1] * IDX_W_SCALE

    def key_scores(keys_bf16):
        s = _dot_nt(qi, keys_bf16)
        return jnp.sum(w_col * jnp.maximum(s, 0.0), axis=0, keepdims=True)

    keys = jnp.concatenate([page_refs[r][0, 0] for r in range(PAGES_PER_STEP)], axis=0).astype(BF16)
    row = key_scores(keys)
    for r in range(PAGES_PER_STEP):
        sc_ref[p * PAGES_PER_STEP + r, pl.ds(b, 1), :] = row[:, r * PAGE_SIZE:(r + 1) * PAGE_SIZE]

    @pl.when(p == n_steps - 1)
    def _():
        ki_new = kvs_ref[0, :, KVI_KI:KVI_KI + IDX_DIM].astype(BF16)
        s_new = key_scores(ki_new)[:, 0:1]
        snew_ref[pl.ds(b, 1), :] = jnp.broadcast_to(s_new, (1, PAGE_SIZE))

    @pl.when((b == nb - 1) & (p == n_steps - 1))
    def _():
        sc = sc_ref[...]
        s_new = snew_ref[:, 0:1]

        def total(x):
            return jnp.sum(jnp.sum(x, axis=0), axis=1, keepdims=True)

        def bit_body(j, u):
            cand = u | lax.shift_left(jnp.int32(1), 31 - j)
            thr = _ordered_bits_to_f32(cand)
            cnt = total(jnp.where(sc >= thr[None], 1.0, 0.0)) + jnp.where(s_new >= thr, 1.0, 0.0)
            return jnp.where(cnt >= float(TOPK_MAX), cand, u)

        u = lax.fori_loop(0, 32, bit_body, jnp.zeros((db, 1), I32))
        thr = _ordered_bits_to_f32(u)
        n_gt = total(jnp.where(sc > thr[None], 1.0, 0.0)) + jnp.where(s_new > thr, 1.0, 0.0)
        need = float(TOPK_MAX) - n_gt
        flat = (lax.broadcasted_iota(I32, sc.shape, 0) * PAGE_SIZE
                + lax.broadcasted_iota(I32, sc.shape, 2))
        tie = sc == thr[None]

        def jbit_body(j, jl):
            cand = jl | lax.shift_left(jnp.int32(1), pos_bits - 1 - j)
            cnt = total(jnp.where(tie & (flat < cand[None]), 1.0, 0.0))
            return jnp.where(cnt <= need, cand, jl)

        jlim = lax.fori_loop(0, pos_bits, jbit_body, jnp.zeros((db, 1), I32))
        sel = (sc > thr[None]) | (tie & (flat < jlim[None]))
        self32 = jnp.where(sel, 1.0, 0.0)
        n_pool = total(self32)

        n_rows = n_pages * db
        selb = self32.astype(BF16).reshape(n_rows, PAGE_SIZE)
        ri = lax.broadcasted_iota(I32, (PAGE_SIZE, PAGE_SIZE), 0)
        ci = lax.broadcasted_iota(I32, (PAGE_SIZE, PAGE_SIZE), 1)
        incl = jnp.dot(selb, jnp.where(ri <= ci, 1.0, 0.0).astype(BF16), preferred_element_type=F32)
        tot = jnp.broadcast_to(incl[:, PAGE_SIZE - 1:PAGE_SIZE], (n_rows, PAGE_SIZE)).astype(BF16)
        rr = lax.broadcasted_iota(I32, (n_rows, n_rows), 0)
        cc = lax.broadcasted_iota(I32, (n_rows, n_rows), 1)
        earlier_page_same_row = ((rr & (db - 1)) == (cc & (db - 1))) & (cc < (rr & -db))
        offs = jnp.dot(jnp.where(earlier_page_same_row, 1.0, 0.0).astype(BF16), tot,
                       preferred_element_type=F32)
        ends_rows = offs + tot.astype(F32)
        r_i = lax.broadcasted_iota(I32, (n_rows, PAGE_SIZE), 0)
        l_i = lax.broadcasted_iota(I32, (n_rows, PAGE_SIZE), 1)
        log2_db = db.bit_length() - 1
        diag = jnp.where(l_i == lax.shift_right_logical(r_i, log2_db), ends_rows, 0.0).astype(BF16)
        pick = jnp.where((lax.broadcasted_iota(I32, (db, n_rows), 1) & (db - 1))
                         == lax.broadcasted_iota(I32, (db, n_rows), 0), 1.0, 0.0).astype(BF16)
        ends = jnp.dot(pick, diag, preferred_element_type=F32)

        table = jnp.concatenate([incl.astype(BF16), offs.astype(BF16)], axis=1)
        slot = lax.broadcasted_iota(I32, (TOPK_MAX, PAGE_SIZE), 0).astype(F32)
        rows_lane = lax.broadcasted_iota(I32, (TOPK_MAX, n_rows), 1)
        for row_i in range(db):
            page_s = jnp.sum(jnp.where(ends[row_i:row_i + 1, :] <= slot, 1.0, 0.0), axis=1, keepdims=True)
            mrow = page_s.astype(I32) * db + row_i
            onehot = jnp.where(rows_lane == mrow, 1.0, 0.0).astype(BF16)
            got = jnp.dot(onehot, table, preferred_element_type=F32)
            k = slot - got[:, PAGE_SIZE:]
            off_s = jnp.sum(jnp.where(got[:, :PAGE_SIZE] <= k, 1.0, 0.0), axis=1, keepdims=True)
            pos = page_s * float(PAGE_SIZE) + off_s
            valid = slot[:, 0:1] < n_pool[row_i:row_i + 1, :]
            idx_ref[row_i] = jnp.where(valid, pos, 0.0).astype(I32)
        meta_ref[...] = jnp.broadcast_to(n_pool, (db, PAGE_SIZE)).astype(I32)


def _sample_select(page_table, qi_s, kvi_s16, pool_ki):
    db, n_pages = page_table.shape
    assert db & (db - 1) == 0, "row index arithmetic in the rank matrix assumes a power-of-two row count"
    assert n_pages == PAGE_SIZE, "page ends are laid out one page per lane"
    n_steps = n_pages // PAGES_PER_STEP

    def page_spec(r):
        return pl.BlockSpec(
            (1, 1, PAGE_SIZE, IDX_DIM),
            lambda b, p, pt: (0, pt[b, p * PAGES_PER_STEP + r], 0, 0))

    grid_spec = pltpu.PrefetchScalarGridSpec(
        num_scalar_prefetch=1,
        grid=(db, n_steps),
        in_specs=[
            pl.BlockSpec((1, IDX_HEADS, IDX_DIM), lambda b, p, pt: (b, 0, 0)),
            pl.BlockSpec((1, IDX_HEADS, KVI_WIDTH), lambda b, p, pt: (b, 0, 0)),
        ] + [page_spec(r) for r in range(PAGES_PER_STEP)],
        out_specs=[
            pl.BlockSpec((db, TOPK_MAX, 1), lambda b, p, pt: (0, 0, 0)),
            pl.BlockSpec((db, PAGE_SIZE), lambda b, p, pt: (0, 0)),
        ],
        scratch_shapes=[
            pltpu.VMEM((n_pages, db, PAGE_SIZE), F32),
            pltpu.VMEM((db, PAGE_SIZE), F32),
        ],
    )
    return pl.pallas_call(
        _sample_select_kernel,
        grid_spec=grid_spec,
        out_shape=[
            jax.ShapeDtypeStruct((db, TOPK_MAX, 1), I32),
            jax.ShapeDtypeStruct((db, PAGE_SIZE), I32),
        ],
        compiler_params=pltpu.CompilerParams(
            dimension_semantics=("arbitrary", "arbitrary"), vmem_limit_bytes=VMEM_LIMIT),
        name="sample_select",
    )(page_table, qi_s, kvi_s16, *([pool_ki] * PAGES_PER_STEP))


def _sample_attend_kernel(idx_ref, npool_ref, pt_ref, q_ref, kvs_ref, pk_ref, pv_ref, o_ref,
                          kbuf, vbuf, kflat, vflat, sem):
    b = pl.program_id(0)

    def row_copies(r):
        ix = idx_ref[b, r]
        page = pt_ref[b, lax.shift_right_logical(ix, 7)]
        off = ix & (PAGE_SIZE - 1)
        return (pltpu.make_async_copy(pk_ref.at[0, page, off], kbuf.at[r], sem.at[0]),
                pltpu.make_async_copy(pv_ref.at[0, page, off], vbuf.at[r], sem.at[1]))

    def start_body(r, carry):
        ck, cv = row_copies(r)
        ck.start()
        cv.start()
        return carry

    def wait_body(r, carry):
        ck, cv = row_copies(r)
        ck.wait()
        cv.wait()
        return carry

    lax.fori_loop(0, TOPK_MAX, start_body, 0, unroll=8)
    lax.fori_loop(0, TOPK_MAX, wait_body, 0, unroll=8)

    def head_copies(g):
        return (pltpu.make_async_copy(kbuf.at[:, g, :], kflat.at[g], sem.at[2]),
                pltpu.make_async_copy(vbuf.at[:, g, :], vflat.at[g], sem.at[3]))

    for g in range(N_KV_HEADS):
        for cp in head_copies(g):
            cp.start()
    for g in range(N_KV_HEADS):
        for cp in head_copies(g):
            cp.wait()

    n_pool = npool_ref[b]
    slot = lax.broadcasted_iota(I32, (1, TOPK_MAX), 1)
    slot_ok = slot < n_pool
    new_ok = n_pool < TOPK_MAX
    q_all = q_ref[0].astype(F32)
    for g in range(N_KV_HEADS):
        qg = q_all[g * Q_PER_KV:(g + 1) * Q_PER_KV, :].astype(BF16)
        ks = kflat[g].astype(BF16)
        vs = vflat[g].astype(BF16)
        k_new = kvs_ref[0, 0:1, KVI_K + g * HEAD_DIM:KVI_K + (g + 1) * HEAD_DIM].astype(BF16)
        v_new = kvs_ref[0, 0:1, KVI_V + g * HEAD_DIM:KVI_V + (g + 1) * HEAD_DIM].astype(BF16)
        s = jnp.where(slot_ok, _dot_nt(qg, ks) * ATTN_SCALE, -jnp.inf)
        s_new = jnp.sum(qg.astype(F32) * k_new.astype(F32), axis=-1, keepdims=True) * ATTN_SCALE
        s_new = jnp.where(new_ok, s_new, -jnp.inf)
        m = jnp.maximum(jnp.max(s, axis=-1, keepdims=True), s_new)
        p = jnp.exp(s - m)
        p_new = jnp.exp(s_new - m)
        denom = jnp.sum(p, axis=-1, keepdims=True) + p_new
        pn = (p / denom).astype(BF16)
        pn_new = (p_new / denom).astype(BF16).astype(F32)
        o = jnp.dot(pn, vs, preferred_element_type=F32) + pn_new * v_new.astype(F32)
        o_ref[0, g * Q_PER_KV:(g + 1) * Q_PER_KV, :] = o


def _sample_attend(idx, n_pool, page_table, q_s, kvi_s16, pool_k, pool_v):
    db = page_table.shape[0]
    grid_spec = pltpu.PrefetchScalarGridSpec(
        num_scalar_prefetch=3,
        grid=(db,),
        in_specs=[
            pl.BlockSpec((1, N_HEADS, HEAD_DIM), lambda b, *_: (b, 0, 0)),
            pl.BlockSpec((1, IDX_HEADS, KVI_WIDTH), lambda b, *_: (b, 0, 0)),
            pl.BlockSpec(memory_space=pl.ANY),
            pl.BlockSpec(memory_space=pl.ANY),
        ],
        out_specs=pl.BlockSpec((1, N_HEADS, HEAD_DIM), lambda b, *_: (b, 0, 0)),
        scratch_shapes=[
            pltpu.VMEM((TOPK_MAX, N_KV_HEADS, HEAD_DIM), F32),
            pltpu.VMEM((TOPK_MAX, N_KV_HEADS, HEAD_DIM), F32),
            pltpu.VMEM((N_KV_HEADS, TOPK_MAX, HEAD_DIM), F32),
            pltpu.VMEM((N_KV_HEADS, TOPK_MAX, HEAD_DIM), F32),
            pltpu.SemaphoreType.DMA((4,)),
        ],
    )
    return pl.pallas_call(
        _sample_attend_kernel,
        grid_spec=grid_spec,
        out_shape=jax.ShapeDtypeStruct((db, N_HEADS, HEAD_DIM), F32),
        compiler_params=pltpu.CompilerParams(dimension_semantics=("arbitrary",)),
        name="sample_attend",
    )(idx, n_pool, page_table, q_s, kvi_s16, pool_k, pool_v)


def _merge_kernel(chunked, oa_ref, za_ref, u_ref, v_ref, zb_ref, ga_ref, gb_ref, x_ref,
                  wpa_ref, wpb_ref, wout_ref, lng_ref, lnb_ref, ws_ref, bs_ref, gf_ref, y_ref, *rest):
    vn_ref, ob_ref = (None, rest[0]) if chunked else rest
    tm = x_ref.shape[0]
    u = _gelu(u_ref[...])
    v = _gelu(v_ref[...])
    mu = jnp.mean(v, axis=-1, keepdims=True)
    vc = v - mu
    vn = (vc * lax.rsqrt(jnp.mean(vc * vc, axis=-1, keepdims=True) + LN_EPS)) * lng_ref[...] + lnb_ref[...]
    if vn_ref is not None:
        vn_ref[...] = vn

    if chunked:
        vnb = vn.astype(BF16)
        ri = lax.broadcasted_iota(I32, (CHUNK, CHUNK), 0)
        ci = lax.broadcasted_iota(I32, (CHUNK, CHUNK), 1)
        for g in range(GMLP_GROUPS):
            wm = jnp.where(ri >= ci, ws_ref[g], 0.0).astype(BF16)
            lo = g * GMLP_GROUP_DIM
            for c in range(tm // CHUNK):
                mixed = jnp.dot(wm, vnb[c * CHUNK:(c + 1) * CHUNK, lo:lo + GMLP_GROUP_DIM],
                                preferred_element_type=F32) + bs_ref[:, g:g + 1]
                ob_ref[c * CHUNK:(c + 1) * CHUNK, lo:lo + GMLP_GROUP_DIM] = (
                    u[c * CHUNK:(c + 1) * CHUNK, lo:lo + GMLP_GROUP_DIM] * mixed)
    else:
        ob_ref[...] = u * (ws_ref[...] * vn + bs_ref[...])

    ha = (oa_ref[...] * _silu(za_ref[...])).astype(BF16)
    hb = (ob_ref[...] * _silu(zb_ref[...])).astype(BF16)
    ya = jnp.dot(ha, wpa_ref[...], preferred_element_type=F32)
    yb = jnp.dot(hb, wpb_ref[...], preferred_element_type=F32)
    mix = jax.nn.sigmoid(ga_ref[...]) * ya + jax.nn.sigmoid(gb_ref[...]) * yb
    out = x_ref[...] + jnp.dot(mix.astype(BF16), wout_ref[...], preferred_element_type=F32)
    ms = jnp.mean(out * out, axis=-1, keepdims=True)
    y_ref[...] = (out * lax.rsqrt(ms + RMS_EPS)) * gf_ref[...]


def _merge(chunked, o_a, gate, x2d, wpa, wpb, wout, lng, lnb, ws, bs, gf, tm):
    rows = x2d.shape[0]
    gw = GMLP_WIDTH

    def const(shape):
        return pl.BlockSpec(shape, lambda i: (0,) * len(shape), pipeline_mode=pl.Buffered(1))

    return pl.pallas_call(
        functools.partial(_merge_kernel, chunked),
        grid=(rows // tm,),
        in_specs=[
            pl.BlockSpec((tm, ATTN_WIDTH), lambda i: (i, 0)),
            pl.BlockSpec((tm, gw), lambda i: (i, 0)),
            pl.BlockSpec((tm, gw), lambda i: (i, 1)),
            pl.BlockSpec((tm, gw), lambda i: (i, 2)),
            pl.BlockSpec((tm, gw), lambda i: (i, 3)),
            pl.BlockSpec((tm, D_MODEL), lambda i: (i, 2)),
            pl.BlockSpec((tm, D_MODEL), lambda i: (i, 3)),
            pl.BlockSpec((tm, D_MODEL), lambda i: (i, 0)),
            const(wpa.shape), const(wpb.shape), const(wout.shape),
            const(lng.shape), const(lnb.shape), const(ws.shape), const(bs.shape), const(gf.shape),
        ],
        out_specs=[pl.BlockSpec((tm, D_MODEL), lambda i: (i, 0))]
        + ([] if chunked else [pl.BlockSpec((tm, gw), lambda i: (i, 0))]),
        out_shape=[jax.ShapeDtypeStruct((rows, D_MODEL), F32)]
        + ([] if chunked else [jax.ShapeDtypeStruct((rows, gw), F32)]),
        scratch_shapes=[pltpu.VMEM((tm, gw), F32)],
        compiler_params=pltpu.CompilerParams(
            dimension_semantics=("arbitrary",), vmem_limit_bytes=VMEM_LIMIT),
        name="merge_prompt" if chunked else "merge_sample",
    )(o_a, gate, gate, gate, gate, gate, gate, x2d, wpa, wpb, wout, lng, lnb, ws, bs, gf)


def kernel(x_prompt, x_sample, cache_k, cache_v, cache_k_idx, page_table, norm_in_g, w_in,
           w_proj_a, w_proj_b, w_out, ln_g, ln_b, w_spatial, b_spatial, norm_f_g):
    depth = w_in.shape[0]
    assert depth == 1, "single trunk layer"
    batch, seq, _ = x_prompt.shape
    db, dseq, _ = x_sample.shape
    assert dseq == 1
    l = 0
    wpa = w_proj_a[l].astype(BF16)
    wpb = w_proj_b[l].astype(BF16)
    wout = w_out[l].astype(BF16)
    g_in = norm_in_g[l].reshape(1, D_MODEL)
    lng = ln_g[l].reshape(1, GMLP_WIDTH)
    lnb = ln_b[l].reshape(1, GMLP_WIDTH)
    gf = norm_f_g.reshape(1, D_MODEL)

    xp = x_prompt.reshape(batch * seq, D_MODEL)
    wt = w_in[l].T
    xn_p, kv_p, ki_p2, wi_t, k3_p, v3_p = _inproj_kvi(xp, g_in, wt, PROJ_TOKEN_TILE)
    h_t = _inproj(xn_p, wt, COL_Q, COL_WI, COL_WI // 4, PROJ_TOKEN_TILE, BF16, True)
    gate_p = _inproj(xn_p, wt, GATE_COL0, GATE_WIDTH, GATE_FEATURE_TILE, PROJ_TOKEN_TILE, F32, False)
    oa_p = _prompt_attn(h_t, wi_t, kv_p, ki_p2, batch, seq)
    bs_t = b_spatial[l].T
    (y_p,) = _merge(True, oa_p, gate_p, xp, wpa, wpb, wout, lng, lnb, w_spatial[l], bs_t, gf, MERGE_TOKEN_TILE)

    xs = x_sample.reshape(db, D_MODEL)
    xn_s, kv_s, ki_s2, wi_t_s, _, _ = _inproj_kvi(xs, g_in, wt, db)
    h_s = _inproj(xn_s, wt, COL_Q, COL_WI, 512, db, F32, False)
    gate_s = _inproj(xn_s, wt, GATE_COL0, GATE_WIDTH, GATE_FEATURE_TILE, db, F32, False)
    kvi_s = jnp.concatenate(
        [kv_s, ki_s2, wi_t_s.T, jnp.zeros((db, KVI_WIDTH - KVI_WI - IDX_HEADS), F32)], axis=1)
    q_s = h_s[:, COL_Q:COL_K].astype(BF16).reshape(db, N_HEADS, HEAD_DIM)
    qi_s = h_s[:, COL_QI:COL_WI].astype(BF16).reshape(db, IDX_HEADS, IDX_DIM)
    wi_col = kvi_s[:, KVI_WI:KVI_WI + IDX_HEADS].reshape(db, IDX_HEADS, 1)
    kvi_s16 = jnp.broadcast_to(kvi_s[:, None, :], (db, IDX_HEADS, KVI_WIDTH))
    kvi_s16 = lax.dynamic_update_slice(kvi_s16, wi_col, (0, 0, KVI_WI))
    idx, meta = _sample_select(page_table, qi_s, kvi_s16, cache_k_idx)
    oa_s = _sample_attend(idx.reshape(db, TOPK_MAX), meta[:, 0], page_table, q_s, kvi_s16,
                          cache_k, cache_v).reshape(db, ATTN_WIDTH)
    ws0 = jnp.repeat(w_spatial[l][:, 0, 0], GMLP_GROUP_DIM).reshape(1, GMLP_WIDTH)
    bs0 = jnp.repeat(b_spatial[l][:, 0], GMLP_GROUP_DIM).reshape(1, GMLP_WIDTH)
    y_s, vn_s = _merge(False, oa_s, gate_s, xs, wpa, wpb, wout, lng, lnb, ws0, bs0, gf, db)

    def kv_out(kvi, lead):
        k = kvi[:, KVI_K:KVI_K + KV_WIDTH].reshape((1,) + lead + (N_KV_HEADS, HEAD_DIM))
        v = kvi[:, KVI_V:KVI_V + KV_WIDTH].reshape((1,) + lead + (N_KV_HEADS, HEAD_DIM))
        ki = kvi[:, KVI_KI:KVI_KI + IDX_DIM].reshape((1,) + lead + (IDX_DIM,))
        return k, v, ki

    k_p = k3_p.reshape(1, batch, seq, N_KV_HEADS, HEAD_DIM)
    v_p = v3_p.reshape(1, batch, seq, N_KV_HEADS, HEAD_DIM)
    ki_p = ki_p2.reshape(1, batch, seq, IDX_DIM)
    k_s, v_s, ki_s = kv_out(kvi_s, (db, dseq))
    return (y_p.reshape(batch, seq, D_MODEL), y_s.reshape(db, dseq, D_MODEL),
            k_p, v_p, ki_p, k_s, v_s, ki_s, vn_s.reshape(1, db, dseq, GMLP_WIDTH))
```

```python
import functools

import jax
import jax.numpy as jnp
import numpy as np
from jax import lax
from jax.experimental import pallas as pl
from jax.experimental.pallas import tpu as pltpu

F32 = jnp.float32
BF16 = jnp.bfloat16
I32 = jnp.int32

D_MODEL = 2048
N_HEADS = 8
N_KV_HEADS = 2
HEAD_DIM = 128
Q_PER_KV = N_HEADS // N_KV_HEADS
ATTN_WIDTH = N_HEADS * HEAD_DIM
KV_WIDTH = N_KV_HEADS * HEAD_DIM
IDX_HEADS = 16
IDX_DIM = 128
TOPK_MAX = 256
PAGE_SIZE = 128
GMLP_WIDTH = 1024
GMLP_GROUPS = 8
GMLP_GROUP_DIM = GMLP_WIDTH // GMLP_GROUPS
CHUNK = 128
RMS_EPS = 1e-6
LN_EPS = 1e-5

SUBLANES = 8
COL_Q = 0
COL_K = COL_Q + ATTN_WIDTH
COL_V = COL_K + KV_WIDTH
COL_QI = COL_V + KV_WIDTH
COL_WI = COL_QI + IDX_HEADS * IDX_DIM
COL_KI = COL_WI + IDX_HEADS
GATE_COL0 = COL_KI + IDX_DIM
GATE_WIDTH = 4 * GMLP_WIDTH + 2 * D_MODEL
KVI_K = 0
KVI_V = KVI_K + KV_WIDTH
KVI_KI = KVI_V + KV_WIDTH
KVI_WI = KVI_KI + IDX_DIM
KVI_WIDTH = KVI_WI + 128

Q_BLOCK = 128
KEY_CHUNK = 512
COUNT_ROWS = 32
ONES_ROWS = 16
SIGN_EXP_BITS = 9
SEARCH_ALWAYS_BITS = 20
SEARCH_GROUP_BITS = 4
NEG_SENTINEL = float(np.finfo(np.float32).min)
NEG_ABOVE_SENTINEL = float(np.nextafter(np.float32(NEG_SENTINEL), np.float32(0.0)))
INT_MIN = -(2 ** 31)
F32_MANTISSA_BITS = 23
F32_MANTISSA_MASK = (1 << F32_MANTISSA_BITS) - 1
F32_EXP_MASK = 0xFF
F32_MIN_NORMAL = float(2.0 ** -126)
IDX_W_SCALE = float(IDX_HEADS ** -0.5 * IDX_DIM ** -0.5)
ATTN_SCALE = float(HEAD_DIM ** -0.5)
ATTN_SCALE_LOG2E = float(HEAD_DIM ** -0.5 * np.log2(np.e))
MASKED_SCORE = float(jnp.finfo(jnp.bfloat16).min)
MASKED_LEVEL = -1e30
V7X_VMEM_BYTES = 64 * 1024 * 1024
VMEM_LIMIT = V7X_VMEM_BYTES - 8 * 1024 * 1024
PROJ_TOKEN_TILE = 1024
GATE_FEATURE_TILE = 1024
MERGE_TOKEN_TILE = 256


def _dot_nt(a, b):
    return lax.dot_general(a, b, (((1,), (1,)), ((), ())), preferred_element_type=F32)


def _ordered_bits_to_f32(u):
    signed_pow, zero_exp = _sign_exp_parts(u)
    return _with_mantissa(u, signed_pow, zero_exp)


def _sign_exp_parts(u):
    bits = jnp.where(u < 0, u ^ INT_MIN, ~u)
    exp = lax.shift_right_logical(bits, F32_MANTISSA_BITS) & F32_EXP_MASK
    e1 = exp - 1
    p = jnp.full(u.shape, F32_MIN_NORMAL, F32)
    for b in range(7):
        p = jnp.where(((e1 >> b) & 1) == 1, p * float(2.0 ** (2 ** b)), p)
    top = ((e1 >> 7) & 1) == 1
    p = jnp.where(top, p * float(2.0 ** 64), p)
    p = jnp.where(top, p * float(2.0 ** 64), p)
    return jnp.where(bits < 0, -p, p), exp == 0


def _with_mantissa(u, signed_pow, zero_exp):
    low = jnp.where(u < 0, u, ~u) & F32_MANTISSA_MASK
    mant = 1.0 + low.astype(F32) * float(2.0 ** -F32_MANTISSA_BITS)
    return jnp.where(zero_exp, 0.0, signed_pow * mant)


def _gelu(x):
    return 0.5 * x * (1.0 + lax.erf(x * float(np.sqrt(0.5))))


def _silu(x):
    return x * jax.nn.sigmoid(x)


def _inproj_kernel(feature_major, xn_ref, w_ref, o_ref, wb_ref):
    @pl.when(pl.program_id(1) == 0)
    def _():
        wb_ref[...] = w_ref[...].astype(BF16)

    if feature_major:
        o_ref[...] = _dot_nt(wb_ref[...], xn_ref[...]).astype(o_ref.dtype)
    else:
        o_ref[...] = _dot_nt(xn_ref[...], wb_ref[...]).astype(o_ref.dtype)


def _inproj(xn, wt, row_start, n_rows, tn, tm, out_dtype, feature_major):
    rows = xn.shape[0]
    assert n_rows % tn == 0 and rows % tm == 0 and row_start % SUBLANES == 0
    if row_start % tn == 0:
        j0 = row_start // tn
        w_spec = pl.BlockSpec((tn, D_MODEL), lambda j, i: (j0 + j, 0))
    else:
        w_spec = pl.BlockSpec((pl.Element(tn), pl.Element(D_MODEL)), lambda j, i: ((row_start // SUBLANES + j * (tn // SUBLANES)) * SUBLANES, 0))
    if feature_major:
        out_spec = pl.BlockSpec((tn, tm), lambda j, i: (j, i))
        out_shape = (n_rows, rows)
    else:
        out_spec = pl.BlockSpec((tm, tn), lambda j, i: (i, j))
        out_shape = (rows, n_rows)
    return pl.pallas_call(
        functools.partial(_inproj_kernel, feature_major),
        grid=(n_rows // tn, rows // tm),
        in_specs=[pl.BlockSpec((tm, D_MODEL), lambda j, i: (i, 0)), w_spec],
        out_specs=out_spec,
        out_shape=jax.ShapeDtypeStruct(out_shape, out_dtype),
        scratch_shapes=[pltpu.VMEM((tn, D_MODEL), BF16)],
        compiler_params=pltpu.CompilerParams(
            dimension_semantics=("arbitrary", "arbitrary"), vmem_limit_bytes=VMEM_LIMIT),
        name="inproj_t" if feature_major else "inproj",
    )(xn, wt)


def _inproj_kvi_kernel(x_ref, g_ref, wkv_ref, wwi_ref, wki_ref, xn_ref, kv_ref, ki_ref, wit_ref, k3_ref, v3_ref,
                       wkvb_ref, wwib_ref, wkib_ref):
    @pl.when(pl.program_id(0) == 0)
    def _():
        wkvb_ref[...] = wkv_ref[...].astype(BF16)
        wwib_ref[...] = wwi_ref[...].astype(BF16)
        wkib_ref[...] = wki_ref[...].astype(BF16)

    x = x_ref[...]
    ms = jnp.mean(x * x, axis=-1, keepdims=True)
    xn = ((x * lax.rsqrt(ms + RMS_EPS)) * g_ref[...]).astype(BF16)
    xn_ref[...] = xn
    kv = _dot_nt(xn, wkvb_ref[...])
    kv_ref[...] = kv
    for g in range(N_KV_HEADS):
        k3_ref[:, g, :] = kv[:, g * HEAD_DIM:(g + 1) * HEAD_DIM]
        v3_ref[:, g, :] = kv[:, KV_WIDTH + g * HEAD_DIM:KV_WIDTH + (g + 1) * HEAD_DIM]
    ki_ref[...] = _dot_nt(xn, wkib_ref[...])
    wit_ref[...] = _dot_nt(wwib_ref[...], xn)


def _inproj_kvi(x2d, g, wt, tm):
    rows = x2d.shape[0]
    assert rows % tm == 0 and COL_K % (2 * KV_WIDTH) == 0 and COL_WI % IDX_HEADS == 0
    return pl.pallas_call(
        _inproj_kvi_kernel,
        grid=(rows // tm,),
        in_specs=[
            pl.BlockSpec((tm, D_MODEL), lambda i: (i, 0)),
            pl.BlockSpec((1, D_MODEL), lambda i: (0, 0)),
            pl.BlockSpec((2 * KV_WIDTH, D_MODEL), lambda i: (COL_K // (2 * KV_WIDTH), 0)),
            pl.BlockSpec((IDX_HEADS, D_MODEL), lambda i: (COL_WI // IDX_HEADS, 0)),
            pl.BlockSpec((pl.Element(IDX_DIM), pl.Element(D_MODEL)), lambda i: (COL_KI, 0)),
        ],
        out_specs=[
            pl.BlockSpec((tm, D_MODEL), lambda i: (i, 0)),
            pl.BlockSpec((tm, 2 * KV_WIDTH), lambda i: (i, 0)),
            pl.BlockSpec((tm, IDX_DIM), lambda i: (i, 0)),
            pl.BlockSpec((IDX_HEADS, tm), lambda i: (0, i)),
            pl.BlockSpec((tm, N_KV_HEADS, HEAD_DIM), lambda i: (i, 0, 0)),
            pl.BlockSpec((tm, N_KV_HEADS, HEAD_DIM), lambda i: (i, 0, 0)),
        ],
        out_shape=[
            jax.ShapeDtypeStruct((rows, D_MODEL), BF16),
            jax.ShapeDtypeStruct((rows, 2 * KV_WIDTH), F32),
            jax.ShapeDtypeStruct((rows, IDX_DIM), F32),
            jax.ShapeDtypeStruct((IDX_HEADS, rows), F32),
            jax.ShapeDtypeStruct((rows, N_KV_HEADS, HEAD_DIM), F32),
            jax.ShapeDtypeStruct((rows, N_KV_HEADS, HEAD_DIM), F32),
        ],
        scratch_shapes=[
            pltpu.VMEM((2 * KV_WIDTH, D_MODEL), BF16),
            pltpu.VMEM((IDX_HEADS, D_MODEL), BF16),
            pltpu.VMEM((IDX_DIM, D_MODEL), BF16),
        ],
        compiler_params=pltpu.CompilerParams(
            dimension_semantics=("arbitrary",), vmem_limit_bytes=VMEM_LIMIT),
        name="inproj_kvi",
    )(x2d, g, wt, wt, wt)


def _prompt_attn_kernel(qt_ref, qit0_ref, qit1_ref, qit2_ref, qit3_ref, vt_ref, wit_ref, kvf_ref, kif_ref,
                        o_ref, kb_ref, kib_ref, vtc_ref, sc_ref, sa_ref, sb_ref, acc_ref, ml_ref, u_ref, cnt_ref,
                        c0_ref):
    i = pl.program_id(1)
    seq = kvf_ref.shape[0]
    pos_bits = seq.bit_length()
    qit_refs = (qit0_ref, qit1_ref, qit2_ref, qit3_ref)
    heads_per_ref = IDX_HEADS // len(qit_refs)

    @pl.when(i == 0)
    def _():
        kb_ref[...] = kvf_ref[:, 0:KV_WIDTH].astype(BF16)
        kib_ref[...] = kif_ref[...].astype(BF16)
        for c in range(seq // KEY_CHUNK):
            for g in range(N_KV_HEADS):
                vtc_ref[c, g, :HEAD_DIM, :] = vt_ref[g * HEAD_DIM:(g + 1) * HEAD_DIM,
                                                     c * KEY_CHUNK:(c + 1) * KEY_CHUNK]
                vtc_ref[c, g, HEAD_DIM:, :] = jnp.ones((ONES_ROWS, KEY_CHUNK), BF16)

    n_chunks = (i * Q_BLOCK) // KEY_CHUNK + 1
    q_pos = i * Q_BLOCK + lax.broadcasted_iota(I32, (1, Q_BLOCK), 1)
    key0 = lax.broadcasted_iota(I32, (KEY_CHUNK, 1), 0)

    w_t = wit_ref[...] * IDX_W_SCALE
    qit_all = jnp.concatenate(
        [qit_refs[h // heads_per_ref][(h % heads_per_ref) * IDX_DIM:(h % heads_per_ref + 1) * IDX_DIM, :]
         for h in range(IDX_HEADS)], axis=1)

    def score_chunk(c):
        k0 = pl.multiple_of(c * KEY_CHUNK, KEY_CHUNK)
        res = jnp.dot(kib_ref[pl.ds(k0, KEY_CHUNK), :], qit_all, preferred_element_type=F32)
        acc = jnp.zeros((KEY_CHUNK, Q_BLOCK), F32)
        for h in range(IDX_HEADS):
            acc = acc + w_t[h:h + 1, :] * jnp.maximum(res[:, h * Q_BLOCK:(h + 1) * Q_BLOCK], 0.0)
        stored = jnp.where(k0 + key0 <= q_pos, acc, NEG_SENTINEL)
        sc_ref[c] = stored
        c0_ref[...] += jnp.where(stored >= 0.0, 1.0, 0.0).reshape(
            KEY_CHUNK // COUNT_ROWS, COUNT_ROWS, Q_BLOCK).sum(axis=0)

    c0_ref[...] = jnp.zeros(c0_ref.shape, F32)

    def score_pair(t, carry):
        score_chunk(2 * t)
        score_chunk(2 * t + 1)
        return carry

    lax.fori_loop(0, n_chunks // 2, score_pair, 0)

    @pl.when(n_chunks % 2 == 1)
    def _():
        score_chunk(n_chunks - 1)

    def count(pred):
        def body(c, cnt):
            ones = jnp.where(pred(sc_ref[c], c), 1.0, 0.0)
            return cnt + ones.reshape(KEY_CHUNK // COUNT_ROWS, COUNT_ROWS, Q_BLOCK).sum(axis=0)
        part = lax.fori_loop(0, n_chunks, body, jnp.zeros((COUNT_ROWS, Q_BLOCK), F32))
        return jnp.sum(part, axis=0, keepdims=True)

    def search(j0, n, u, cnt_u, decode):
        def body(j, carry):
            u, cnt_u = carry
            cand = u | lax.shift_left(jnp.int32(1), 31 - j)
            thr_c = decode(cand)
            cnt = count(lambda blk, c: blk >= thr_c)
            ok = cnt >= float(TOPK_MAX)
            return jnp.where(ok, cand, u), jnp.where(ok, cnt, cnt_u)
        return lax.fori_loop(j0, j0 + n, body, (u, cnt_u))

    u = jnp.zeros((1, Q_BLOCK), I32)
    cnt_u = (u + n_chunks * KEY_CHUNK).astype(F32)
    cnt_sign = jnp.sum(c0_ref[...], axis=0, keepdims=True)
    sign_ok = cnt_sign >= float(TOPK_MAX)
    u = jnp.where(sign_ok, INT_MIN, u)
    cnt_u = jnp.where(sign_ok, cnt_sign, cnt_u)
    u, cnt_u = search(1, SIGN_EXP_BITS - 1, u, cnt_u, _ordered_bits_to_f32)
    signed_pow, zero_exp = _sign_exp_parts(u)

    def decode_mantissa(cand):
        return _with_mantissa(cand, signed_pow, zero_exp)

    u, cnt_u = search(SIGN_EXP_BITS, SEARCH_ALWAYS_BITS - SIGN_EXP_BITS, u, cnt_u, decode_mantissa)
    u_ref[...] = u
    cnt_ref[...] = cnt_u
    for j0 in range(SEARCH_ALWAYS_BITS, 32, SEARCH_GROUP_BITS):
        @pl.when(jnp.max(jnp.where(cnt_ref[...] == float(TOPK_MAX), 0.0, 1.0)) > 0.0)
        def _():
            u_g, cnt_g = search(j0, SEARCH_GROUP_BITS, u_ref[...], cnt_ref[...], decode_mantissa)
            u_ref[...] = u_g
            cnt_ref[...] = cnt_g
    u = u_ref[...]
    n_ge = cnt_ref[...]
    thr = decode_mantissa(u)
    thr_adm = jnp.maximum(thr, NEG_ABOVE_SENTINEL)
    excess = jnp.max(jnp.where((n_ge > float(TOPK_MAX)) & (thr > NEG_SENTINEL), 1.0, 0.0)) > 0.0

    @pl.when(excess)
    def _():
        need = float(TOPK_MAX) - count(lambda blk, c: blk > thr)

        def jbit_body(j, jl):
            cand = jl | lax.shift_left(jnp.int32(1), pos_bits - 1 - j)
            cnt = count(lambda blk, c: (blk == thr) & (c * KEY_CHUNK + key0 < cand))
            return jnp.where(cnt <= need, cand, jl)
        jlim = lax.fori_loop(0, pos_bits, jbit_body, jnp.zeros((1, Q_BLOCK), I32))

        def demote_body(c, carry):
            blk = sc_ref[c]
            dropped = (blk == thr) & (thr > NEG_SENTINEL) & (c * KEY_CHUNK + key0 >= jlim)
            sc_ref[c] = jnp.where(dropped, NEG_SENTINEL, blk)
            return carry
        lax.fori_loop(0, n_chunks, demote_body, 0)

    acc_ref[...] = jnp.zeros(acc_ref.shape, F32)
    cols = Q_PER_KV * Q_BLOCK
    qt_groups = [
        jnp.concatenate([qt_ref[(g * Q_PER_KV + h) * HEAD_DIM:(g * Q_PER_KV + h + 1) * HEAD_DIM, :]
                         for h in range(Q_PER_KV)], axis=1)
        for g in range(N_KV_HEADS)
    ]

    eye = (lax.broadcasted_iota(I32, (Q_BLOCK, Q_BLOCK), 0)
           == lax.broadcasted_iota(I32, (Q_BLOCK, Q_BLOCK), 1))
    eye_heads = jnp.concatenate([jnp.where(eye, 1.0, 0.0).astype(BF16)] * Q_PER_KV, axis=1)
    qt_aug = [jnp.concatenate([qt_groups[g], eye_heads], axis=0) for g in range(N_KV_HEADS)]

    def qk_scores(c, dst_ref):
        k0 = pl.multiple_of(c * KEY_CHUNK, KEY_CHUNK)
        masked = jnp.where(sc_ref[c] >= thr_adm, 0.0, MASKED_SCORE).astype(BF16)
        for g in range(N_KV_HEADS):
            kc = kb_ref[pl.ds(k0, KEY_CHUNK), g * HEAD_DIM:(g + 1) * HEAD_DIM]
            dst_ref[g] = jnp.dot(jnp.concatenate([kc, masked], axis=1), qt_aug[g], preferred_element_type=F32)

    def softmax_pv(c, src_ref):
        for g in range(N_KV_HEADS):
            m_old = ml_ref[2 * g:2 * g + 1, :]
            l_old = ml_ref[2 * g + 1:2 * g + 2, :]
            s = src_ref[g] * ATTN_SCALE_LOG2E
            m_new = jnp.maximum(m_old, jnp.max(s, axis=0, keepdims=True))
            m_safe = jnp.where(m_new < MASKED_LEVEL, 0.0, m_new)
            alpha = jnp.exp2(m_old - m_safe)
            p = jnp.exp2(s - m_safe).astype(BF16)
            pv = jnp.dot(vtc_ref[c, g], p, preferred_element_type=F32)
            acc_ref[g] = alpha * acc_ref[g] + pv[:HEAD_DIM]
            ml_ref[2 * g:2 * g + 1, :] = m_new
            ml_ref[2 * g + 1:2 * g + 2, :] = alpha * l_old + pv[HEAD_DIM:HEAD_DIM + 1]

    def pair_body(t, carry):
        qk_scores(2 * t + 1, sb_ref)
        softmax_pv(2 * t, sa_ref)
        qk_scores(2 * t + 2, sa_ref)
        softmax_pv(2 * t + 1, sb_ref)
        return carry

    for g in range(N_KV_HEADS):
        ml_ref[2 * g:2 * g + 1, :] = jnp.full((1, cols), -jnp.inf, F32)
        ml_ref[2 * g + 1:2 * g + 2, :] = jnp.zeros((1, cols), F32)
    qk_scores(0, sa_ref)
    n_pairs = (n_chunks - 1) // 2
    lax.fori_loop(0, n_pairs, pair_body, 0)

    @pl.when(n_chunks % 2 == 1)
    def _():
        softmax_pv(n_chunks - 1, sa_ref)

    @pl.when(n_chunks % 2 == 0)
    def _():
        qk_scores(n_chunks - 1, sb_ref)
        softmax_pv(n_chunks - 2, sa_ref)
        softmax_pv(n_chunks - 1, sb_ref)

    for g in range(N_KV_HEADS):
        o = acc_ref[g] / ml_ref[2 * g + 1:2 * g + 2, :]
        for h in range(Q_PER_KV):
            hd = (g * Q_PER_KV + h) * HEAD_DIM
            o_ref[:, hd:hd + HEAD_DIM] = o[:, h * Q_BLOCK:(h + 1) * Q_BLOCK].T


def _prompt_attn(h_t, wi_t, kv, ki, batch, seq):
    nb = seq // Q_BLOCK
    cols = Q_PER_KV * Q_BLOCK
    qi_rows = IDX_HEADS * IDX_DIM // 4
    qi_b0 = COL_QI // qi_rows

    def qi_spec(t):
        return pl.BlockSpec((qi_rows, Q_BLOCK), lambda b, i: (qi_b0 + t, b * nb + i))

    return pl.pallas_call(
        _prompt_attn_kernel,
        grid=(batch, nb),
        in_specs=[
            pl.BlockSpec((ATTN_WIDTH, Q_BLOCK), lambda b, i: (COL_Q // ATTN_WIDTH, b * nb + i)),
            qi_spec(0), qi_spec(1), qi_spec(2), qi_spec(3),
            pl.BlockSpec((KV_WIDTH, seq), lambda b, i: (COL_V // KV_WIDTH, b)),
            pl.BlockSpec((IDX_HEADS, Q_BLOCK), lambda b, i: (0, b * nb + i)),
            pl.BlockSpec((seq, 2 * KV_WIDTH), lambda b, i: (b, 0)),
            pl.BlockSpec((seq, IDX_DIM), lambda b, i: (b, 0)),
        ],
        out_specs=pl.BlockSpec((Q_BLOCK, ATTN_WIDTH), lambda b, i: (b * nb + i, 0)),
        out_shape=jax.ShapeDtypeStruct((batch * seq, ATTN_WIDTH), F32),
        scratch_shapes=[
            pltpu.VMEM((seq, KV_WIDTH), BF16),
            pltpu.VMEM((seq, IDX_DIM), BF16),
            pltpu.VMEM((seq // KEY_CHUNK, N_KV_HEADS, HEAD_DIM + ONES_ROWS, KEY_CHUNK), BF16),
            pltpu.VMEM((seq // KEY_CHUNK, KEY_CHUNK, Q_BLOCK), F32),
            pltpu.VMEM((N_KV_HEADS, KEY_CHUNK, cols), F32),
            pltpu.VMEM((N_KV_HEADS, KEY_CHUNK, cols), F32),
            pltpu.VMEM((N_KV_HEADS, HEAD_DIM, cols), F32),
            pltpu.VMEM((2 * N_KV_HEADS, cols), F32),
            pltpu.VMEM((1, Q_BLOCK), I32),
            pltpu.VMEM((1, Q_BLOCK), F32),
            pltpu.VMEM((COUNT_ROWS, Q_BLOCK), F32),
        ],
        compiler_params=pltpu.CompilerParams(
            dimension_semantics=("arbitrary", "arbitrary"), vmem_limit_bytes=VMEM_LIMIT),
        name="prompt_attn",
    )(h_t, h_t, h_t, h_t, h_t, h_t, wi_t, kv, ki)


PAGES_PER_STEP = 64


def _sample_select_kernel(pt_ref, qi_ref, kvs_ref, *refs):
    page_refs = refs[:PAGES_PER_STEP]
    idx_ref, meta_ref, sc_ref, snew_ref = refs[PAGES_PER_STEP:]
    b = pl.program_id(0)
    p = pl.program_id(1)
    nb = pl.num_programs(0)
    n_steps = pl.num_programs(1)
    n_pages, db, _ = sc_ref.shape
    pos_bits = (n_pages * PAGE_SIZE).bit_length()

    qi = qi_ref[0]
    w_col = kvs_ref[0, :, KVI_WI:KVI_WI + 1] * IDX_W_SCALE

    def key_scores(keys_bf16):
        s = _dot_nt(qi, keys_bf16)
        return jnp.sum(w_col * jnp.maximum(s, 0.0), axis=0, keepdims=True)

    keys = jnp.concatenate([page_refs[r][0, 0] for r in range(PAGES_PER_STEP)], axis=0).astype(BF16)
    row = key_scores(keys)
    for r in range(PAGES_PER_STEP):
        sc_ref[p * PAGES_PER_STEP + r, pl.ds(b, 1), :] = row[:, r * PAGE_SIZE:(r + 1) * PAGE_SIZE]

    @pl.when(p == n_steps - 1)
    def _():
        ki_new = kvs_ref[0, :, KVI_KI:KVI_KI + IDX_DIM].astype(BF16)
        s_new = key_scores(ki_new)[:, 0:1]
        snew_ref[pl.ds(b, 1), :] = jnp.broadcast_to(s_new, (1, PAGE_SIZE))

    @pl.when((b == nb - 1) & (p == n_steps - 1))
    def _():
        sc = sc_ref[...]
        s_new = snew_ref[:, 0:1]

        def total(x):
            return jnp.sum(jnp.sum(x, axis=0), axis=1, keepdims=True)

        def bit_body(j, u):
            cand = u | lax.shift_left(jnp.int32(1), 31 - j)
            thr = _ordered_bits_to_f32(cand)
            cnt = total(jnp.where(sc >= thr[None], 1.0, 0.0)) + jnp.where(s_new >= thr, 1.0, 0.0)
            return jnp.where(cnt >= float(TOPK_MAX), cand, u)

        u = lax.fori_loop(0, 32, bit_body, jnp.zeros((db, 1), I32))
        thr = _ordered_bits_to_f32(u)
        n_gt = total(jnp.where(sc > thr[None], 1.0, 0.0)) + jnp.where(s_new > thr, 1.0, 0.0)
        need = float(TOPK_MAX) - n_gt
        flat = (lax.broadcasted_iota(I32, sc.shape, 0) * PAGE_SIZE
                + lax.broadcasted_iota(I32, sc.shape, 2))
        tie = sc == thr[None]

        def jbit_body(j, jl):
            cand = jl | lax.shift_left(jnp.int32(1), pos_bits - 1 - j)
            cnt = total(jnp.where(tie & (flat < cand[None]), 1.0, 0.0))
            return jnp.where(cnt <= need, cand, jl)

        jlim = lax.fori_loop(0, pos_bits, jbit_body, jnp.zeros((db, 1), I32))
        sel = (sc > thr[None]) | (tie & (flat < jlim[None]))
        self32 = jnp.where(sel, 1.0, 0.0)
        n_pool = total(self32)

        n_rows = n_pages * db
        selb = self32.astype(BF16).reshape(n_rows, PAGE_SIZE)
        ri = lax.broadcasted_iota(I32, (PAGE_SIZE, PAGE_SIZE), 0)
        ci = lax.broadcasted_iota(I32, (PAGE_SIZE, PAGE_SIZE), 1)
        incl = jnp.dot(selb, jnp.where(ri <= ci, 1.0, 0.0).astype(BF16), preferred_element_type=F32)
        tot = jnp.broadcast_to(incl[:, PAGE_SIZE - 1:PAGE_SIZE], (n_rows, PAGE_SIZE)).astype(BF16)
        rr = lax.broadcasted_iota(I32, (n_rows, n_rows), 0)
        cc = lax.broadcasted_iota(I32, (n_rows, n_rows), 1)
        earlier_page_same_row = ((rr & (db - 1)) == (cc & (db - 1))) & (cc < (rr & -db))
        offs = jnp.dot(jnp.where(earlier_page_same_row, 1.0, 0.0).astype(BF16), tot,
                       preferred_element_type=F32)
        ends_rows = offs + tot.astype(F32)
        r_i = lax.broadcasted_iota(I32, (n_rows, PAGE_SIZE), 0)
        l_i = lax.broadcasted_iota(I32, (n_rows, PAGE_SIZE), 1)
        log2_db = db.bit_length() - 1
        diag = jnp.where(l_i == lax.shift_right_logical(r_i, log2_db), ends_rows, 0.0).astype(BF16)
        pick = jnp.where((lax.broadcasted_iota(I32, (db, n_rows), 1) & (db - 1))
                         == lax.broadcasted_iota(I32, (db, n_rows), 0), 1.0, 0.0).astype(BF16)
        ends = jnp.dot(pick, diag, preferred_element_type=F32)

        table = jnp.concatenate([incl.astype(BF16), offs.astype(BF16)], axis=1)
        slot = lax.broadcasted_iota(I32, (TOPK_MAX, PAGE_SIZE), 0).astype(F32)
        rows_lane = lax.broadcasted_iota(I32, (TOPK_MAX, n_rows), 1)
        for row_i in range(db):
            page_s = jnp.sum(jnp.where(ends[row_i:row_i + 1, :] <= slot, 1.0, 0.0), axis=1, keepdims=True)
            mrow = page_s.astype(I32) * db + row_i
            onehot = jnp.where(rows_lane == mrow, 1.0, 0.0).astype(BF16)
            got = jnp.dot(onehot, table, preferred_element_type=F32)
            k = slot - got[:, PAGE_SIZE:]
            off_s = jnp.sum(jnp.where(got[:, :PAGE_SIZE] <= k, 1.0, 0.0), axis=1, keepdims=True)
            pos = page_s * float(PAGE_SIZE) + off_s
            valid = slot[:, 0:1] < n_pool[row_i:row_i + 1, :]
            idx_ref[row_i] = jnp.where(valid, pos, 0.0).astype(I32)
        meta_ref[...] = jnp.broadcast_to(n_pool, (db, PAGE_SIZE)).astype(I32)


def _sample_select(page_table, qi_s, kvi_s16, pool_ki):
    db, n_pages = page_table.shape
    assert db & (db - 1) == 0, "row index arithmetic in the rank matrix assumes a power-of-two row count"
    assert n_pages == PAGE_SIZE, "page ends are laid out one page per lane"
    n_steps = n_pages // PAGES_PER_STEP

    def page_spec(r):
        return pl.BlockSpec(
            (1, 1, PAGE_SIZE, IDX_DIM),
            lambda b, p, pt: (0, pt[b, p * PAGES_PER_STEP + r], 0, 0))

    grid_spec = pltpu.PrefetchScalarGridSpec(
        num_scalar_prefetch=1,
        grid=(db, n_steps),
        in_specs=[
            pl.BlockSpec((1, IDX_HEADS, IDX_DIM), lambda b, p, pt: (b, 0, 0)),
            pl.BlockSpec((1, IDX_HEADS, KVI_WIDTH), lambda b, p, pt: (b, 0, 0)),
        ] + [page_spec(r) for r in range(PAGES_PER_STEP)],
        out_specs=[
            pl.BlockSpec((db, TOPK_MAX, 1), lambda b, p, pt: (0, 0, 0)),
            pl.BlockSpec((db, PAGE_SIZE), lambda b, p, pt: (0, 0)),
        ],
        scratch_shapes=[
            pltpu.VMEM((n_pages, db, PAGE_SIZE), F32),
            pltpu.VMEM((db, PAGE_SIZE), F32),
        ],
    )
    return pl.pallas_call(
        _sample_select_kernel,
        grid_spec=grid_spec,
        out_shape=[
            jax.ShapeDtypeStruct((db, TOPK_MAX, 1), I32),
            jax.ShapeDtypeStruct((db, PAGE_SIZE), I32),
        ],
        compiler_params=pltpu.CompilerParams(
            dimension_semantics=("arbitrary", "arbitrary"), vmem_limit_bytes=VMEM_LIMIT),
        name="sample_select",
    )(page_table, qi_s, kvi_s16, *([pool_ki] * PAGES_PER_STEP))


def _sample_attend_kernel(idx_ref, npool_ref, pt_ref, q_ref, kvs_ref, pk_ref, pv_ref, o_ref,
                          kbuf, vbuf, kflat, vflat, sem):
    b = pl.program_id(0)

    def row_copies(r):
        ix = idx_ref[b, r]
        page = pt_ref[b, lax.shift_right_logical(ix, 7)]
        off = ix & (PAGE_SIZE - 1)
        return (pltpu.make_async_copy(pk_ref.at[0, page, off], kbuf.at[r], sem.at[0]),
                pltpu.make_async_copy(pv_ref.at[0, page, off], vbuf.at[r], sem.at[1]))

    def start_body(r, carry):
        ck, cv = row_copies(r)
        ck.start()
        cv.start()
        return carry

    def wait_body(r, carry):
        ck, cv = row_copies(r)
        ck.wait()
        cv.wait()
        return carry

    lax.fori_loop(0, TOPK_MAX, start_body, 0, unroll=8)
    lax.fori_loop(0, TOPK_MAX, wait_body, 0, unroll=8)

    def head_copies(g):
        return (pltpu.make_async_copy(kbuf.at[:, g, :], kflat.at[g], sem.at[2]),
                pltpu.make_async_copy(vbuf.at[:, g, :], vflat.at[g], sem.at[3]))

    for g in range(N_KV_HEADS):
        for cp in head_copies(g):
            cp.start()
    for g in range(N_KV_HEADS):
        for cp in head_copies(g):
            cp.wait()

    n_pool = npool_ref[b]
    slot = lax.broadcasted_iota(I32, (1, TOPK_MAX), 1)
    slot_ok = slot < n_pool
    new_ok = n_pool < TOPK_MAX
    q_all = q_ref[0].astype(F32)
    for g in range(N_KV_HEADS):
        qg = q_all[g * Q_PER_KV:(g + 1) * Q_PER_KV, :].astype(BF16)
        ks = kflat[g].astype(BF16)
        vs = vflat[g].astype(BF16)
        k_new = kvs_ref[0, 0:1, KVI_K + g * HEAD_DIM:KVI_K + (g + 1) * HEAD_DIM].astype(BF16)
        v_new = kvs_ref[0, 0:1, KVI_V + g * HEAD_DIM:KVI_V + (g + 1) * HEAD_DIM].astype(BF16)
        s = jnp.where(slot_ok, _dot_nt(qg, ks) * ATTN_SCALE, -jnp.inf)
        s_new = jnp.sum(qg.astype(F32) * k_new.astype(F32), axis=-1, keepdims=True) * ATTN_SCALE
        s_new = jnp.where(new_ok, s_new, -jnp.inf)
        m = jnp.maximum(jnp.max(s, axis=-1, keepdims=True), s_new)
        p = jnp.exp(s - m)
        p_new = jnp.exp(s_new - m)
        denom = jnp.sum(p, axis=-1, keepdims=True) + p_new
        pn = (p / denom).astype(BF16)
        pn_new = (p_new / denom).astype(BF16).astype(F32)
        o = jnp.dot(pn, vs, preferred_element_type=F32) + pn_new * v_new.astype(F32)
        o_ref[0, g * Q_PER_KV:(g + 1) * Q_PER_KV, :] = o


def _sample_attend(idx, n_pool, page_table, q_s, kvi_s16, pool_k, pool_v):
    db = page_table.shape[0]
    grid_spec = pltpu.PrefetchScalarGridSpec(
        num_scalar_prefetch=3,
        grid=(db,),
        in_specs=[
            pl.BlockSpec((1, N_HEADS, HEAD_DIM), lambda b, *_: (b, 0, 0)),
            pl.BlockSpec((1, IDX_HEADS, KVI_WIDTH), lambda b, *_: (b, 0, 0)),
            pl.BlockSpec(memory_space=pl.ANY),
            pl.BlockSpec(memory_space=pl.ANY),
        ],
        out_specs=pl.BlockSpec((1, N_HEADS, HEAD_DIM), lambda b, *_: (b, 0, 0)),
        scratch_shapes=[
            pltpu.VMEM((TOPK_MAX, N_KV_HEADS, HEAD_DIM), F32),
            pltpu.VMEM((TOPK_MAX, N_KV_HEADS, HEAD_DIM), F32),
            pltpu.VMEM((N_KV_HEADS, TOPK_MAX, HEAD_DIM), F32),
            pltpu.VMEM((N_KV_HEADS, TOPK_MAX, HEAD_DIM), F32),
            pltpu.SemaphoreType.DMA((4,)),
        ],
    )
    return pl.pallas_call(
        _sample_attend_kernel,
        grid_spec=grid_spec,
        out_shape=jax.ShapeDtypeStruct((db, N_HEADS, HEAD_DIM), F32),
        compiler_params=pltpu.CompilerParams(dimension_semantics=("arbitrary",)),
        name="sample_attend",
    )(idx, n_pool, page_table, q_s, kvi_s16, pool_k, pool_v)


def _merge_kernel(chunked, oa_ref, za_ref, u_ref, v_ref, zb_ref, ga_ref, gb_ref, x_ref,
                  wpa_ref, wpb_ref, wout_ref, lng_ref, lnb_ref, ws_ref, bs_ref, gf_ref, y_ref, *rest):
    vn_ref, ob_ref = (None, rest[0]) if chunked else rest
    tm = x_ref.shape[0]
    u = _gelu(u_ref[...])
    v = _gelu(v_ref[...])
    mu = jnp.mean(v, axis=-1, keepdims=True)
    vc = v - mu
    vn = (vc * lax.rsqrt(jnp.mean(vc * vc, axis=-1, keepdims=True) + LN_EPS)) * lng_ref[...] + lnb_ref[...]
    if vn_ref is not None:
        vn_ref[...] = vn

    if chunked:
        vnb = vn.astype(BF16)
        ri = lax.broadcasted_iota(I32, (CHUNK, CHUNK), 0)
        ci = lax.broadcasted_iota(I32, (CHUNK, CHUNK), 1)
        for g in range(GMLP_GROUPS):
            wm = jnp.where(ri >= ci, ws_ref[g], 0.0).astype(BF16)
            lo = g * GMLP_GROUP_DIM
            for c in range(tm // CHUNK):
                mixed = jnp.dot(wm, vnb[c * CHUNK:(c + 1) * CHUNK, lo:lo + GMLP_GROUP_DIM],
                                preferred_element_type=F32) + bs_ref[:, g:g + 1]
                ob_ref[c * CHUNK:(c + 1) * CHUNK, lo:lo + GMLP_GROUP_DIM] = (
                    u[c * CHUNK:(c + 1) * CHUNK, lo:lo + GMLP_GROUP_DIM] * mixed)
    else:
        ob_ref[...] = u * (ws_ref[...] * vn + bs_ref[...])

    ha = (oa_ref[...] * _silu(za_ref[...])).astype(BF16)
    hb = (ob_ref[...] * _silu(zb_ref[...])).astype(BF16)
    ya = jnp.dot(ha, wpa_ref[...], preferred_element_type=F32)
    yb = jnp.dot(hb, wpb_ref[...], preferred_element_type=F32)
    mix = jax.nn.sigmoid(ga_ref[...]) * ya + jax.nn.sigmoid(gb_ref[...]) * yb
    out = x_ref[...] + jnp.dot(mix.astype(BF16), wout_ref[...], preferred_element_type=F32)
    ms = jnp.mean(out * out, axis=-1, keepdims=True)
    y_ref[...] = (out * lax.rsqrt(ms + RMS_EPS)) * gf_ref[...]


def _merge(chunked, o_a, gate, x2d, wpa, wpb, wout, lng, lnb, ws, bs, gf, tm):
    rows = x2d.shape[0]
    gw = GMLP_WIDTH

    def const(shape):
        return pl.BlockSpec(shape, lambda i: (0,) * len(shape), pipeline_mode=pl.Buffered(1))

    return pl.pallas_call(
        functools.partial(_merge_kernel, chunked),
        grid=(rows // tm,),
        in_specs=[
            pl.BlockSpec((tm, ATTN_WIDTH), lambda i: (i, 0)),
            pl.BlockSpec((tm, gw), lambda i: (i, 0)),
            pl.BlockSpec((tm, gw), lambda i: (i, 1)),
            pl.BlockSpec((tm, gw), lambda i: (i, 2)),
            pl.BlockSpec((tm, gw), lambda i: (i, 3)),
            pl.BlockSpec((tm, D_MODEL), lambda i: (i, 2)),
            pl.BlockSpec((tm, D_MODEL), lambda i: (i, 3)),
            pl.BlockSpec((tm, D_MODEL), lambda i: (i, 0)),
            const(wpa.shape), const(wpb.shape), const(wout.shape),
            const(lng.shape), const(lnb.shape), const(ws.shape), const(bs.shape), const(gf.shape),
        ],
        out_specs=[pl.BlockSpec((tm, D_MODEL), lambda i: (i, 0))]
        + ([] if chunked else [pl.BlockSpec((tm, gw), lambda i: (i, 0))]),
        out_shape=[jax.ShapeDtypeStruct((rows, D_MODEL), F32)]
        + ([] if chunked else [jax.ShapeDtypeStruct((rows, gw), F32)]),
        scratch_shapes=[pltpu.VMEM((tm, gw), F32)],
        compiler_params=pltpu.CompilerParams(
            dimension_semantics=("arbitrary",), vmem_limit_bytes=VMEM_LIMIT),
        name="merge_prompt" if chunked else "merge_sample",
    )(o_a, gate, gate, gate, gate, gate, gate, x2d, wpa, wpb, wout, lng, lnb, ws, bs, gf)


def kernel(x_prompt, x_sample, cache_k, cache_v, cache_k_idx, page_table, norm_in_g, w_in,
           w_proj_a, w_proj_b, w_out, ln_g, ln_b, w_spatial, b_spatial, norm_f_g):
    depth = w_in.shape[0]
    assert depth == 1, "single trunk layer"
    batch, seq, _ = x_prompt.shape
    db, dseq, _ = x_sample.shape
    assert dseq == 1
    l = 0
    wpa = w_proj_a[l].astype(BF16)
    wpb = w_proj_b[l].astype(BF16)
    wout = w_out[l].astype(BF16)
    g_in = norm_in_g[l].reshape(1, D_MODEL)
    lng = ln_g[l].reshape(1, GMLP_WIDTH)
    lnb = ln_b[l].reshape(1, GMLP_WIDTH)
    gf = norm_f_g.reshape(1, D_MODEL)

    xp = x_prompt.reshape(batch * seq, D_MODEL)
    wt = w_in[l].T
    xn_p, kv_p, ki_p2, wi_t, k3_p, v3_p = _inproj_kvi(xp, g_in, wt, PROJ_TOKEN_TILE)
    h_t = _inproj(xn_p, wt, COL_Q, COL_WI, COL_WI // 4, PROJ_TOKEN_TILE, BF16, True)
    gate_p = _inproj(xn_p, wt, GATE_COL0, GATE_WIDTH, GATE_FEATURE_TILE, PROJ_TOKEN_TILE, F32, False)
    oa_p = _prompt_attn(h_t, wi_t, kv_p, ki_p2, batch, seq)
    bs_t = b_spatial[l].T
    (y_p,) = _merge(True, oa_p, gate_p, xp, wpa, wpb, wout, lng, lnb, w_spatial[l], bs_t, gf, MERGE_TOKEN_TILE)

    xs = x_sample.reshape(db, D_MODEL)
    xn_s, kv_s, ki_s2, wi_t_s, _, _ = _inproj_kvi(xs, g_in, wt, db)
    h_s = _inproj(xn_s, wt, COL_Q, COL_WI, 512, db, F32, False)
    gate_s = _inproj(xn_s, wt, GATE_COL0, GATE_WIDTH, GATE_FEATURE_TILE, db, F32, False)
    kvi_s = jnp.concatenate(
        [kv_s, ki_s2, wi_t_s.T, jnp.zeros((db, KVI_WIDTH - KVI_WI - IDX_HEADS), F32)], axis=1)
    q_s = h_s[:, COL_Q:COL_K].astype(BF16).reshape(db, N_HEADS, HEAD_DIM)
    qi_s = h_s[:, COL_QI:COL_WI].astype(BF16).reshape(db, IDX_HEADS, IDX_DIM)
    wi_col = kvi_s[:, KVI_WI:KVI_WI + IDX_HEADS].reshape(db, IDX_HEADS, 1)
    kvi_s16 = jnp.broadcast_to(kvi_s[:, None, :], (db, IDX_HEADS, KVI_WIDTH))
    kvi_s16 = lax.dynamic_update_slice(kvi_s16, wi_col, (0, 0, KVI_WI))
    idx, meta = _sample_select(page_table, qi_s, kvi_s16, cache_k_idx)
    oa_s = _sample_attend(idx.reshape(db, TOPK_MAX), meta[:, 0], page_table, q_s, kvi_s16,
                          cache_k, cache_v).reshape(db, ATTN_WIDTH)
    ws0 = jnp.repeat(w_spatial[l][:, 0, 0], GMLP_GROUP_DIM).reshape(1, GMLP_WIDTH)
    bs0 = jnp.repeat(b_spatial[l][:, 0], GMLP_GROUP_DIM).reshape(1, GMLP_WIDTH)
    y_s, vn_s = _merge(False, oa_s, gate_s, xs, wpa, wpb, wout, lng, lnb, ws0, bs0, gf, db)

    def kv_out(kvi, lead):
        k = kvi[:, KVI_K:KVI_K + KV_WIDTH].reshape((1,) + lead + (N_KV_HEADS, HEAD_DIM))
        v = kvi[:, KVI_V:KVI_V + KV_WIDTH].reshape((1,) + lead + (N_KV_HEADS, HEAD_DIM))
        ki = kvi[:, KVI_KI:KVI_KI + IDX_DIM].reshape((1,) + lead + (IDX_DIM,))
        return k, v, ki

    k_p = k3_p.reshape(1, batch, seq, N_KV_HEADS, HEAD_DIM)
    v_p = v3_p.reshape(1, batch, seq, N_KV_HEADS, HEAD_DIM)
    ki_p = ki_p2.reshape(1, batch, seq, IDX_DIM)
    k_s, v_s, ki_s = kv_out(kvi_s, (db, dseq))
    return (y_p.reshape(batch, seq, D_MODEL), y_s.reshape(db, dseq, D_MODEL),
            k_p, v_p, ki_p, k_s, v_s, ki_s, vn_s.reshape(1, db, dseq, GMLP_WIDTH))
```

```python
import functools

import jax
import jax.numpy as jnp
import numpy as np
from jax import lax
from jax.experimental import pallas as pl
from jax.experimental.pallas import tpu as pltpu

F32 = jnp.float32
BF16 = jnp.bfloat16
I32 = jnp.int32

D_MODEL = 2048
N_HEADS = 8
N_KV_HEADS = 2
HEAD_DIM = 128
Q_PER_KV = N_HEADS // N_KV_HEADS
ATTN_WIDTH = N_HEADS * HEAD_DIM
KV_WIDTH = N_KV_HEADS * HEAD_DIM
IDX_HEADS = 16
IDX_DIM = 128
TOPK_MAX = 256
PAGE_SIZE = 128
GMLP_WIDTH = 1024
GMLP_GROUPS = 8
GMLP_GROUP_DIM = GMLP_WIDTH // GMLP_GROUPS
CHUNK = 128
RMS_EPS = 1e-6
LN_EPS = 1e-5

SUBLANES = 8
COL_Q = 0
COL_K = COL_Q + ATTN_WIDTH
COL_V = COL_K + KV_WIDTH
COL_QI = COL_V + KV_WIDTH
COL_WI = COL_QI + IDX_HEADS * IDX_DIM
COL_KI = COL_WI + IDX_HEADS
GATE_COL0 = COL_KI + IDX_DIM
GATE_WIDTH = 4 * GMLP_WIDTH + 2 * D_MODEL
KVI_K = 0
KVI_V = KVI_K + KV_WIDTH
KVI_KI = KVI_V + KV_WIDTH
KVI_WI = KVI_KI + IDX_DIM
KVI_WIDTH = KVI_WI + 128

Q_BLOCK = 128
KEY_CHUNK = 512
COUNT_ROWS = 32
ONES_ROWS = 16
SIGN_EXP_BITS = 9
SEARCH_ALWAYS_BITS = 20
SEARCH_GROUP_BITS = 4
NEG_SENTINEL = float(np.finfo(np.float32).min)
NEG_ABOVE_SENTINEL = float(np.nextafter(np.float32(NEG_SENTINEL), np.float32(0.0)))
INT_MIN = -(2 ** 31)
FIXED_CANDIDATES = (0.0, 2.0, float(np.nextafter(np.float32(-2.0), np.float32(0.0))))
F32_MANTISSA_BITS = 23
F32_MANTISSA_MASK = (1 << F32_MANTISSA_BITS) - 1
F32_EXP_MASK = 0xFF
F32_MIN_NORMAL = float(2.0 ** -126)
IDX_W_SCALE = float(IDX_HEADS ** -0.5 * IDX_DIM ** -0.5)
ATTN_SCALE = float(HEAD_DIM ** -0.5)
ATTN_SCALE_LOG2E = float(HEAD_DIM ** -0.5 * np.log2(np.e))
MASKED_SCORE = float(jnp.finfo(jnp.bfloat16).min)
MASKED_LEVEL = -1e30
V7X_VMEM_BYTES = 64 * 1024 * 1024
VMEM_LIMIT = V7X_VMEM_BYTES - 8 * 1024 * 1024
PROJ_TOKEN_TILE = 1024
GATE_FEATURE_TILE = 1024
MERGE_TOKEN_TILE = 256


def _dot_nt(a, b):
    return lax.dot_general(a, b, (((1,), (1,)), ((), ())), preferred_element_type=F32)


def _ordered_bits_to_f32(u):
    signed_pow, zero_exp = _sign_exp_parts(u)
    return _with_mantissa(u, signed_pow, zero_exp)


def _sign_exp_parts(u):
    bits = jnp.where(u < 0, u ^ INT_MIN, ~u)
    exp = lax.shift_right_logical(bits, F32_MANTISSA_BITS) & F32_EXP_MASK
    e1 = exp - 1
    p = jnp.full(u.shape, F32_MIN_NORMAL, F32)
    for b in range(7):
        p = jnp.where(((e1 >> b) & 1) == 1, p * float(2.0 ** (2 ** b)), p)
    top = ((e1 >> 7) & 1) == 1
    p = jnp.where(top, p * float(2.0 ** 64), p)
    p = jnp.where(top, p * float(2.0 ** 64), p)
    return jnp.where(bits < 0, -p, p), exp == 0


def _with_mantissa(u, signed_pow, zero_exp):
    low = jnp.where(u < 0, u, ~u) & F32_MANTISSA_MASK
    mant = 1.0 + low.astype(F32) * float(2.0 ** -F32_MANTISSA_BITS)
    return jnp.where(zero_exp, 0.0, signed_pow * mant)


def _gelu(x):
    return 0.5 * x * (1.0 + lax.erf(x * float(np.sqrt(0.5))))


def _silu(x):
    return x * jax.nn.sigmoid(x)


def _inproj_kernel(feature_major, xn_ref, w_ref, o_ref, wb_ref):
    @pl.when(pl.program_id(1) == 0)
    def _():
        wb_ref[...] = w_ref[...].astype(BF16)

    if feature_major:
        o_ref[...] = _dot_nt(wb_ref[...], xn_ref[...]).astype(o_ref.dtype)
    else:
        o_ref[...] = _dot_nt(xn_ref[...], wb_ref[...]).astype(o_ref.dtype)


def _inproj(xn, wt, row_start, n_rows, tn, tm, out_dtype, feature_major):
    rows = xn.shape[0]
    assert n_rows % tn == 0 and rows % tm == 0 and row_start % SUBLANES == 0
    if row_start % tn == 0:
        j0 = row_start // tn
        w_spec = pl.BlockSpec((tn, D_MODEL), lambda j, i: (j0 + j, 0))
    else:
        w_spec = pl.BlockSpec((pl.Element(tn), pl.Element(D_MODEL)), lambda j, i: ((row_start // SUBLANES + j * (tn // SUBLANES)) * SUBLANES, 0))
    if feature_major:
        out_spec = pl.BlockSpec((tn, tm), lambda j, i: (j, i))
        out_shape = (n_rows, rows)
    else:
        out_spec = pl.BlockSpec((tm, tn), lambda j, i: (i, j))
        out_shape = (rows, n_rows)
    return pl.pallas_call(
        functools.partial(_inproj_kernel, feature_major),
        grid=(n_rows // tn, rows // tm),
        in_specs=[pl.BlockSpec((tm, D_MODEL), lambda j, i: (i, 0)), w_spec],
        out_specs=out_spec,
        out_shape=jax.ShapeDtypeStruct(out_shape, out_dtype),
        scratch_shapes=[pltpu.VMEM((tn, D_MODEL), BF16)],
        compiler_params=pltpu.CompilerParams(
            dimension_semantics=("arbitrary", "arbitrary"), vmem_limit_bytes=VMEM_LIMIT),
        name="inproj_t" if feature_major else "inproj",
    )(xn, wt)


def _inproj_kvi_kernel(x_ref, g_ref, wkv_ref, wwi_ref, wki_ref, xn_ref, kv_ref, ki_ref, wit_ref, k3_ref, v3_ref,
                       wkvb_ref, wwib_ref, wkib_ref):
    @pl.when(pl.program_id(0) == 0)
    def _():
        wkvb_ref[...] = wkv_ref[...].astype(BF16)
        wwib_ref[...] = wwi_ref[...].astype(BF16)
        wkib_ref[...] = wki_ref[...].astype(BF16)

    x = x_ref[...]
    ms = jnp.mean(x * x, axis=-1, keepdims=True)
    xn = ((x * lax.rsqrt(ms + RMS_EPS)) * g_ref[...]).astype(BF16)
    xn_ref[...] = xn
    kv = _dot_nt(xn, wkvb_ref[...])
    kv_ref[...] = kv
    for g in range(N_KV_HEADS):
        k3_ref[:, g, :] = kv[:, g * HEAD_DIM:(g + 1) * HEAD_DIM]
        v3_ref[:, g, :] = kv[:, KV_WIDTH + g * HEAD_DIM:KV_WIDTH + (g + 1) * HEAD_DIM]
    ki_ref[...] = _dot_nt(xn, wkib_ref[...])
    wit_ref[...] = _dot_nt(wwib_ref[...], xn)


def _inproj_kvi(x2d, g, wt, tm):
    rows = x2d.shape[0]
    assert rows % tm == 0 and COL_K % (2 * KV_WIDTH) == 0 and COL_WI % IDX_HEADS == 0
    return pl.pallas_call(
        _inproj_kvi_kernel,
        grid=(rows // tm,),
        in_specs=[
            pl.BlockSpec((tm, D_MODEL), lambda i: (i, 0)),
            pl.BlockSpec((1, D_MODEL), lambda i: (0, 0)),
            pl.BlockSpec((2 * KV_WIDTH, D_MODEL), lambda i: (COL_K // (2 * KV_WIDTH), 0)),
            pl.BlockSpec((IDX_HEADS, D_MODEL), lambda i: (COL_WI // IDX_HEADS, 0)),
            pl.BlockSpec((pl.Element(IDX_DIM), pl.Element(D_MODEL)), lambda i: (COL_KI, 0)),
        ],
        out_specs=[
            pl.BlockSpec((tm, D_MODEL), lambda i: (i, 0)),
            pl.BlockSpec((tm, 2 * KV_WIDTH), lambda i: (i, 0)),
            pl.BlockSpec((tm, IDX_DIM), lambda i: (i, 0)),
            pl.BlockSpec((IDX_HEADS, tm), lambda i: (0, i)),
            pl.BlockSpec((tm, N_KV_HEADS, HEAD_DIM), lambda i: (i, 0, 0)),
            pl.BlockSpec((tm, N_KV_HEADS, HEAD_DIM), lambda i: (i, 0, 0)),
        ],
        out_shape=[
            jax.ShapeDtypeStruct((rows, D_MODEL), BF16),
            jax.ShapeDtypeStruct((rows, 2 * KV_WIDTH), F32),
            jax.ShapeDtypeStruct((rows, IDX_DIM), F32),
            jax.ShapeDtypeStruct((IDX_HEADS, rows), F32),
            jax.ShapeDtypeStruct((rows, N_KV_HEADS, HEAD_DIM), F32),
            jax.ShapeDtypeStruct((rows, N_KV_HEADS, HEAD_DIM), F32),
        ],
        scratch_shapes=[
            pltpu.VMEM((2 * KV_WIDTH, D_MODEL), BF16),
            pltpu.VMEM((IDX_HEADS, D_MODEL), BF16),
            pltpu.VMEM((IDX_DIM, D_MODEL), BF16),
        ],
        compiler_params=pltpu.CompilerParams(
            dimension_semantics=("arbitrary",), vmem_limit_bytes=VMEM_LIMIT),
        name="inproj_kvi",
    )(x2d, g, wt, wt, wt)


def _prompt_attn_kernel(qt_ref, qit0_ref, qit1_ref, qit2_ref, qit3_ref, vt_ref, wit_ref, kvf_ref, kif_ref,
                        o_ref, kb_ref, kib_ref, vtc_ref, sc_ref, sa_ref, sb_ref, acc_ref, ml_ref, u_ref, cnt_ref,
                        c0_ref):
    i = pl.program_id(1)
    seq = kvf_ref.shape[0]
    pos_bits = seq.bit_length()
    qit_refs = (qit0_ref, qit1_ref, qit2_ref, qit3_ref)
    heads_per_ref = IDX_HEADS // len(qit_refs)

    @pl.when(i == 0)
    def _():
        kb_ref[...] = kvf_ref[:, 0:KV_WIDTH].astype(BF16)
        kib_ref[...] = kif_ref[...].astype(BF16)
        for c in range(seq // KEY_CHUNK):
            for g in range(N_KV_HEADS):
                vtc_ref[c, g, :HEAD_DIM, :] = vt_ref[g * HEAD_DIM:(g + 1) * HEAD_DIM,
                                                     c * KEY_CHUNK:(c + 1) * KEY_CHUNK]
                vtc_ref[c, g, HEAD_DIM:, :] = jnp.ones((ONES_ROWS, KEY_CHUNK), BF16)

    n_chunks = (i * Q_BLOCK) // KEY_CHUNK + 1
    q_pos = i * Q_BLOCK + lax.broadcasted_iota(I32, (1, Q_BLOCK), 1)
    key0 = lax.broadcasted_iota(I32, (KEY_CHUNK, 1), 0)

    w_t = wit_ref[...] * IDX_W_SCALE
    qit_all = jnp.concatenate(
        [qit_refs[h // heads_per_ref][(h % heads_per_ref) * IDX_DIM:(h % heads_per_ref + 1) * IDX_DIM, :]
         for h in range(IDX_HEADS)], axis=1)

    def score_chunk(c):
        k0 = pl.multiple_of(c * KEY_CHUNK, KEY_CHUNK)
        res = jnp.dot(kib_ref[pl.ds(k0, KEY_CHUNK), :], qit_all, preferred_element_type=F32)
        acc = jnp.zeros((KEY_CHUNK, Q_BLOCK), F32)
        for h in range(IDX_HEADS):
            acc = acc + w_t[h:h + 1, :] * jnp.maximum(res[:, h * Q_BLOCK:(h + 1) * Q_BLOCK], 0.0)
        stored = jnp.where(k0 + key0 <= q_pos, acc, NEG_SENTINEL)
        sc_ref[c] = stored
        for n, level in enumerate(FIXED_CANDIDATES):
            c0_ref[n] += jnp.where(stored >= level, 1.0, 0.0).reshape(
                KEY_CHUNK // COUNT_ROWS, COUNT_ROWS, Q_BLOCK).sum(axis=0)

    c0_ref[...] = jnp.zeros(c0_ref.shape, F32)

    def score_pair(t, carry):
        score_chunk(2 * t)
        score_chunk(2 * t + 1)
        return carry

    lax.fori_loop(0, n_chunks // 2, score_pair, 0)

    @pl.when(n_chunks % 2 == 1)
    def _():
        score_chunk(n_chunks - 1)

    def count(pred):
        def body(c, cnt):
            ones = jnp.where(pred(sc_ref[c], c), 1.0, 0.0)
            return cnt + ones.reshape(KEY_CHUNK // COUNT_ROWS, COUNT_ROWS, Q_BLOCK).sum(axis=0)
        part = lax.fori_loop(0, n_chunks, body, jnp.zeros((COUNT_ROWS, Q_BLOCK), F32))
        return jnp.sum(part, axis=0, keepdims=True)

    def search(j0, n, u, cnt_u, decode):
        def body(j, carry):
            u, cnt_u = carry
            cand = u | lax.shift_left(jnp.int32(1), 31 - j)
            thr_c = decode(cand)
            cnt = count(lambda blk, c: blk >= thr_c)
            ok = cnt >= float(TOPK_MAX)
            return jnp.where(ok, cand, u), jnp.where(ok, cnt, cnt_u)
        return lax.fori_loop(j0, j0 + n, body, (u, cnt_u))

    u = jnp.zeros((1, Q_BLOCK), I32)
    cnt_u = (u + n_chunks * KEY_CHUNK).astype(F32)
    cnt_sign, cnt_pos, cnt_neg = [jnp.sum(c0_ref[n], axis=0, keepdims=True) for n in range(3)]
    sign_ok = cnt_sign >= float(TOPK_MAX)
    u = jnp.where(sign_ok, INT_MIN, u)
    cnt_u = jnp.where(sign_ok, cnt_sign, cnt_u)
    cnt_next = jnp.where(sign_ok, cnt_pos, cnt_neg)
    next_ok = cnt_next >= float(TOPK_MAX)
    u = jnp.where(next_ok, u | (1 << 30), u)
    cnt_u = jnp.where(next_ok, cnt_next, cnt_u)
    u, cnt_u = search(2, SIGN_EXP_BITS - 2, u, cnt_u, _ordered_bits_to_f32)
    signed_pow, zero_exp = _sign_exp_parts(u)

    def decode_mantissa(cand):
        return _with_mantissa(cand, signed_pow, zero_exp)

    u, cnt_u = search(SIGN_EXP_BITS, SEARCH_ALWAYS_BITS - SIGN_EXP_BITS, u, cnt_u, decode_mantissa)
    u_ref[...] = u
    cnt_ref[...] = cnt_u
    for j0 in range(SEARCH_ALWAYS_BITS, 32, SEARCH_GROUP_BITS):
        @pl.when(jnp.max(jnp.where(cnt_ref[...] == float(TOPK_MAX), 0.0, 1.0)) > 0.0)
        def _():
            u_g, cnt_g = search(j0, SEARCH_GROUP_BITS, u_ref[...], cnt_ref[...], decode_mantissa)
            u_ref[...] = u_g
            cnt_ref[...] = cnt_g
    u = u_ref[...]
    n_ge = cnt_ref[...]
    thr = decode_mantissa(u)
    thr_adm = jnp.maximum(thr, NEG_ABOVE_SENTINEL)
    excess = jnp.max(jnp.where((n_ge > float(TOPK_MAX)) & (thr > NEG_SENTINEL), 1.0, 0.0)) > 0.0

    @pl.when(excess)
    def _():
        need = float(TOPK_MAX) - count(lambda blk, c: blk > thr)

        def jbit_body(j, jl):
            cand = jl | lax.shift_left(jnp.int32(1), pos_bits - 1 - j)
            cnt = count(lambda blk, c: (blk == thr) & (c * KEY_CHUNK + key0 < cand))
            return jnp.where(cnt <= need, cand, jl)
        jlim = lax.fori_loop(0, pos_bits, jbit_body, jnp.zeros((1, Q_BLOCK), I32))

        def demote_body(c, carry):
            blk = sc_ref[c]
            dropped = (blk == thr) & (thr > NEG_SENTINEL) & (c * KEY_CHUNK + key0 >= jlim)
            sc_ref[c] = jnp.where(dropped, NEG_SENTINEL, blk)
            return carry
        lax.fori_loop(0, n_chunks, demote_body, 0)

    acc_ref[...] = jnp.zeros(acc_ref.shape, F32)
    cols = Q_PER_KV * Q_BLOCK
    qt_groups = [
        jnp.concatenate([qt_ref[(g * Q_PER_KV + h) * HEAD_DIM:(g * Q_PER_KV + h + 1) * HEAD_DIM, :]
                         for h in range(Q_PER_KV)], axis=1)
        for g in range(N_KV_HEADS)
    ]

    eye = (lax.broadcasted_iota(I32, (Q_BLOCK, Q_BLOCK), 0)
           == lax.broadcasted_iota(I32, (Q_BLOCK, Q_BLOCK), 1))
    eye_heads = jnp.concatenate([jnp.where(eye, 1.0, 0.0).astype(BF16)] * Q_PER_KV, axis=1)
    qt_aug = [jnp.concatenate([qt_groups[g], eye_heads], axis=0) for g in range(N_KV_HEADS)]

    def qk_scores(c, dst_ref):
        k0 = pl.multiple_of(c * KEY_CHUNK, KEY_CHUNK)
        masked = jnp.where(sc_ref[c] >= thr_adm, 0.0, MASKED_SCORE).astype(BF16)
        for g in range(N_KV_HEADS):
            kc = kb_ref[pl.ds(k0, KEY_CHUNK), g * HEAD_DIM:(g + 1) * HEAD_DIM]
            dst_ref[g] = jnp.dot(jnp.concatenate([kc, masked], axis=1), qt_aug[g], preferred_element_type=F32)

    def softmax_pv(c, src_ref):
        for g in range(N_KV_HEADS):
            m_old = ml_ref[2 * g:2 * g + 1, :]
            l_old = ml_ref[2 * g + 1:2 * g + 2, :]
            s = src_ref[g] * ATTN_SCALE_LOG2E
            m_new = jnp.maximum(m_old, jnp.max(s, axis=0, keepdims=True))
            m_safe = jnp.where(m_new < MASKED_LEVEL, 0.0, m_new)
            alpha = jnp.exp2(m_old - m_safe)
            p = jnp.exp2(s - m_safe).astype(BF16)
            pv = jnp.dot(vtc_ref[c, g], p, preferred_element_type=F32)
            acc_ref[g] = alpha * acc_ref[g] + pv[:HEAD_DIM]
            ml_ref[2 * g:2 * g + 1, :] = m_new
            ml_ref[2 * g + 1:2 * g + 2, :] = alpha * l_old + pv[HEAD_DIM:HEAD_DIM + 1]

    def pair_body(t, carry):
        qk_scores(2 * t + 1, sb_ref)
        softmax_pv(2 * t, sa_ref)
        qk_scores(2 * t + 2, sa_ref)
        softmax_pv(2 * t + 1, sb_ref)
        return carry

    for g in range(N_KV_HEADS):
        ml_ref[2 * g:2 * g + 1, :] = jnp.full((1, cols), -jnp.inf, F32)
        ml_ref[2 * g + 1:2 * g + 2, :] = jnp.zeros((1, cols), F32)
    qk_scores(0, sa_ref)
    n_pairs = (n_chunks - 1) // 2
    lax.fori_loop(0, n_pairs, pair_body, 0)

    @pl.when(n_chunks % 2 == 1)
    def _():
        softmax_pv(n_chunks - 1, sa_ref)

    @pl.when(n_chunks % 2 == 0)
    def _():
        qk_scores(n_chunks - 1, sb_ref)
        softmax_pv(n_chunks - 2, sa_ref)
        softmax_pv(n_chunks - 1, sb_ref)

    for g in range(N_KV_HEADS):
        o = acc_ref[g] / ml_ref[2 * g + 1:2 * g + 2, :]
        for h in range(Q_PER_KV):
            hd = (g * Q_PER_KV + h) * HEAD_DIM
            o_ref[:, hd:hd + HEAD_DIM] = o[:, h * Q_BLOCK:(h + 1) * Q_BLOCK].T


def _prompt_attn(h_t, wi_t, kv, ki, batch, seq):
    nb = seq // Q_BLOCK
    cols = Q_PER_KV * Q_BLOCK
    qi_rows = IDX_HEADS * IDX_DIM // 4
    qi_b0 = COL_QI // qi_rows

    def qi_spec(t):
        return pl.BlockSpec((qi_rows, Q_BLOCK), lambda b, i: (qi_b0 + t, b * nb + i))

    return pl.pallas_call(
        _prompt_attn_kernel,
        grid=(batch, nb),
        in_specs=[
            pl.BlockSpec((ATTN_WIDTH, Q_BLOCK), lambda b, i: (COL_Q // ATTN_WIDTH, b * nb + i)),
            qi_spec(0), qi_spec(1), qi_spec(2), qi_spec(3),
            pl.BlockSpec((KV_WIDTH, seq), lambda b, i: (COL_V // KV_WIDTH, b)),
            pl.BlockSpec((IDX_HEADS, Q_BLOCK), lambda b, i: (0, b * nb + i)),
            pl.BlockSpec((seq, 2 * KV_WIDTH), lambda b, i: (b, 0)),
            pl.BlockSpec((seq, IDX_DIM), lambda b, i: (b, 0)),
        ],
        out_specs=pl.BlockSpec((Q_BLOCK, ATTN_WIDTH), lambda b, i: (b * nb + i, 0)),
        out_shape=jax.ShapeDtypeStruct((batch * seq, ATTN_WIDTH), F32),
        scratch_shapes=[
            pltpu.VMEM((seq, KV_WIDTH), BF16),
            pltpu.VMEM((seq, IDX_DIM), BF16),
            pltpu.VMEM((seq // KEY_CHUNK, N_KV_HEADS, HEAD_DIM + ONES_ROWS, KEY_CHUNK), BF16),
            pltpu.VMEM((seq // KEY_CHUNK, KEY_CHUNK, Q_BLOCK), F32),
            pltpu.VMEM((N_KV_HEADS, KEY_CHUNK, cols), F32),
            pltpu.VMEM((N_KV_HEADS, KEY_CHUNK, cols), F32),
            pltpu.VMEM((N_KV_HEADS, HEAD_DIM, cols), F32),
            pltpu.VMEM((2 * N_KV_HEADS, cols), F32),
            pltpu.VMEM((1, Q_BLOCK), I32),
            pltpu.VMEM((1, Q_BLOCK), F32),
            pltpu.VMEM((len(FIXED_CANDIDATES), COUNT_ROWS, Q_BLOCK), F32),
        ],
        compiler_params=pltpu.CompilerParams(
            dimension_semantics=("arbitrary", "arbitrary"), vmem_limit_bytes=VMEM_LIMIT),
        name="prompt_attn",
    )(h_t, h_t, h_t, h_t, h_t, h_t, wi_t, kv, ki)


PAGES_PER_STEP = 64


def _sample_select_kernel(pt_ref, qi_ref, kvs_ref, *refs):
    page_refs = refs[:PAGES_PER_STEP]
    idx_ref, meta_ref, sc_ref, snew_ref = refs[PAGES_PER_STEP:]
    b = pl.program_id(0)
    p = pl.program_id(1)
    nb = pl.num_programs(0)
    n_steps = pl.num_programs(1)
    n_pages, db, _ = sc_ref.shape
    pos_bits = (n_pages * PAGE_SIZE).bit_length()

    qi = qi_ref[0]
    w_col = kvs_ref[0, :, KVI_WI:KVI_WI + 1] * IDX_W_SCALE

    def key_scores(keys_bf16):
        s = _dot_nt(qi, keys_bf16)
        return jnp.sum(w_col * jnp.maximum(s, 0.0), axis=0, keepdims=True)

    keys = jnp.concatenate([page_refs[r][0, 0] for r in range(PAGES_PER_STEP)], axis=0).astype(BF16)
    row = key_scores(keys)
    for r in range(PAGES_PER_STEP):
        sc_ref[p * PAGES_PER_STEP + r, pl.ds(b, 1), :] = row[:, r * PAGE_SIZE:(r + 1) * PAGE_SIZE]

    @pl.when(p == n_steps - 1)
    def _():
        ki_new = kvs_ref[0, :, KVI_KI:KVI_KI + IDX_DIM].astype(BF16)
        s_new = key_scores(ki_new)[:, 0:1]
        snew_ref[pl.ds(b, 1), :] = jnp.broadcast_to(s_new, (1, PAGE_SIZE))

    @pl.when((b == nb - 1) & (p == n_steps - 1))
    def _():
        sc = sc_ref[...]
        s_new = snew_ref[:, 0:1]

        def total(x):
            return jnp.sum(jnp.sum(x, axis=0), axis=1, keepdims=True)

        def bit_body(j, u):
            cand = u | lax.shift_left(jnp.int32(1), 31 - j)
            thr = _ordered_bits_to_f32(cand)
            cnt = total(jnp.where(sc >= thr[None], 1.0, 0.0)) + jnp.where(s_new >= thr, 1.0, 0.0)
            return jnp.where(cnt >= float(TOPK_MAX), cand, u)

        u = lax.fori_loop(0, 32, bit_body, jnp.zeros((db, 1), I32))
        thr = _ordered_bits_to_f32(u)
        n_gt = total(jnp.where(sc > thr[None], 1.0, 0.0)) + jnp.where(s_new > thr, 1.0, 0.0)
        need = float(TOPK_MAX) - n_gt
        flat = (lax.broadcasted_iota(I32, sc.shape, 0) * PAGE_SIZE
                + lax.broadcasted_iota(I32, sc.shape, 2))
        tie = sc == thr[None]

        def jbit_body(j, jl):
            cand = jl | lax.shift_left(jnp.int32(1), pos_bits - 1 - j)
            cnt = total(jnp.where(tie & (flat < cand[None]), 1.0, 0.0))
            return jnp.where(cnt <= need, cand, jl)

        jlim = lax.fori_loop(0, pos_bits, jbit_body, jnp.zeros((db, 1), I32))
        sel = (sc > thr[None]) | (tie & (flat < jlim[None]))
        self32 = jnp.where(sel, 1.0, 0.0)
        n_pool = total(self32)

        n_rows = n_pages * db
        selb = self32.astype(BF16).reshape(n_rows, PAGE_SIZE)
        ri = lax.broadcasted_iota(I32, (PAGE_SIZE, PAGE_SIZE), 0)
        ci = lax.broadcasted_iota(I32, (PAGE_SIZE, PAGE_SIZE), 1)
        incl = jnp.dot(selb, jnp.where(ri <= ci, 1.0, 0.0).astype(BF16), preferred_element_type=F32)
        tot = jnp.broadcast_to(incl[:, PAGE_SIZE - 1:PAGE_SIZE], (n_rows, PAGE_SIZE)).astype(BF16)
        rr = lax.broadcasted_iota(I32, (n_rows, n_rows), 0)
        cc = lax.broadcasted_iota(I32, (n_rows, n_rows), 1)
        earlier_page_same_row = ((rr & (db - 1)) == (cc & (db - 1))) & (cc < (rr & -db))
        offs = jnp.dot(jnp.where(earlier_page_same_row, 1.0, 0.0).astype(BF16), tot,
                       preferred_element_type=F32)
        ends_rows = offs + tot.astype(F32)
        r_i = lax.broadcasted_iota(I32, (n_rows, PAGE_SIZE), 0)
        l_i = lax.broadcasted_iota(I32, (n_rows, PAGE_SIZE), 1)
        log2_db = db.bit_length() - 1
        diag = jnp.where(l_i == lax.shift_right_logical(r_i, log2_db), ends_rows, 0.0).astype(BF16)
        pick = jnp.where((lax.broadcasted_iota(I32, (db, n_rows), 1) & (db - 1))
                         == lax.broadcasted_iota(I32, (db, n_rows), 0), 1.0, 0.0).astype(BF16)
        ends = jnp.dot(pick, diag, preferred_element_type=F32)

        table = jnp.concatenate([incl.astype(BF16), offs.astype(BF16)], axis=1)
        slot = lax.broadcasted_iota(I32, (TOPK_MAX, PAGE_SIZE), 0).astype(F32)
        rows_lane = lax.broadcasted_iota(I32, (TOPK_MAX, n_rows), 1)
        for row_i in range(db):
            page_s = jnp.sum(jnp.where(ends[row_i:row_i + 1, :] <= slot, 1.0, 0.0), axis=1, keepdims=True)
            mrow = page_s.astype(I32) * db + row_i
            onehot = jnp.where(rows_lane == mrow, 1.0, 0.0).astype(BF16)
            got = jnp.dot(onehot, table, preferred_element_type=F32)
            k = slot - got[:, PAGE_SIZE:]
            off_s = jnp.sum(jnp.where(got[:, :PAGE_SIZE] <= k, 1.0, 0.0), axis=1, keepdims=True)
            pos = page_s * float(PAGE_SIZE) + off_s
            valid = slot[:, 0:1] < n_pool[row_i:row_i + 1, :]
            idx_ref[row_i] = jnp.where(valid, pos, 0.0).astype(I32)
        meta_ref[...] = jnp.broadcast_to(n_pool, (db, PAGE_SIZE)).astype(I32)


def _sample_select(page_table, qi_s, kvi_s16, pool_ki):
    db, n_pages = page_table.shape
    assert db & (db - 1) == 0, "row index arithmetic in the rank matrix assumes a power-of-two row count"
    assert n_pages == PAGE_SIZE, "page ends are laid out one page per lane"
    n_steps = n_pages // PAGES_PER_STEP

    def page_spec(r):
        return pl.BlockSpec(
            (1, 1, PAGE_SIZE, IDX_DIM),
            lambda b, p, pt: (0, pt[b, p * PAGES_PER_STEP + r], 0, 0))

    grid_spec = pltpu.PrefetchScalarGridSpec(
        num_scalar_prefetch=1,
        grid=(db, n_steps),
        in_specs=[
            pl.BlockSpec((1, IDX_HEADS, IDX_DIM), lambda b, p, pt: (b, 0, 0)),
            pl.BlockSpec((1, IDX_HEADS, KVI_WIDTH), lambda b, p, pt: (b, 0, 0)),
        ] + [page_spec(r) for r in range(PAGES_PER_STEP)],
        out_specs=[
            pl.BlockSpec((db, TOPK_MAX, 1), lambda b, p, pt: (0, 0, 0)),
            pl.BlockSpec((db, PAGE_SIZE), lambda b, p, pt: (0, 0)),
        ],
        scratch_shapes=[
            pltpu.VMEM((n_pages, db, PAGE_SIZE), F32),
            pltpu.VMEM((db, PAGE_SIZE), F32),
        ],
    )
    return pl.pallas_call(
        _sample_select_kernel,
        grid_spec=grid_spec,
        out_shape=[
            jax.ShapeDtypeStruct((db, TOPK_MAX, 1), I32),
            jax.ShapeDtypeStruct((db, PAGE_SIZE), I32),
        ],
        compiler_params=pltpu.CompilerParams(
            dimension_semantics=("arbitrary", "arbitrary"), vmem_limit_bytes=VMEM_LIMIT),
        name="sample_select",
    )(page_table, qi_s, kvi_s16, *([pool_ki] * PAGES_PER_STEP))


def _sample_attend_kernel(idx_ref, npool_ref, pt_ref, q_ref, kvs_ref, pk_ref, pv_ref, o_ref,
                          kbuf, vbuf, kflat, vflat, sem):
    b = pl.program_id(0)

    def row_copies(r):
        ix = idx_ref[b, r]
        page = pt_ref[b, lax.shift_right_logical(ix, 7)]
        off = ix & (PAGE_SIZE - 1)
        return (pltpu.make_async_copy(pk_ref.at[0, page, off], kbuf.at[r], sem.at[0]),
                pltpu.make_async_copy(pv_ref.at[0, page, off], vbuf.at[r], sem.at[1]))

    def start_body(r, carry):
        ck, cv = row_copies(r)
        ck.start()
        cv.start()
        return carry

    def wait_body(r, carry):
        ck, cv = row_copies(r)
        ck.wait()
        cv.wait()
        return carry

    lax.fori_loop(0, TOPK_MAX, start_body, 0, unroll=8)
    lax.fori_loop(0, TOPK_MAX, wait_body, 0, unroll=8)

    def head_copies(g):
        return (pltpu.make_async_copy(kbuf.at[:, g, :], kflat.at[g], sem.at[2]),
                pltpu.make_async_copy(vbuf.at[:, g, :], vflat.at[g], sem.at[3]))

    for g in range(N_KV_HEADS):
        for cp in head_copies(g):
            cp.start()
    for g in range(N_KV_HEADS):
        for cp in head_copies(g):
            cp.wait()

    n_pool = npool_ref[b]
    slot = lax.broadcasted_iota(I32, (1, TOPK_MAX), 1)
    slot_ok = slot < n_pool
    new_ok = n_pool < TOPK_MAX
    q_all = q_ref[0].astype(F32)
    for g in range(N_KV_HEADS):
        qg = q_all[g * Q_PER_KV:(g + 1) * Q_PER_KV, :].astype(BF16)
        ks = kflat[g].astype(BF16)
        vs = vflat[g].astype(BF16)
        k_new = kvs_ref[0, 0:1, KVI_K + g * HEAD_DIM:KVI_K + (g + 1) * HEAD_DIM].astype(BF16)
        v_new = kvs_ref[0, 0:1, KVI_V + g * HEAD_DIM:KVI_V + (g + 1) * HEAD_DIM].astype(BF16)
        s = jnp.where(slot_ok, _dot_nt(qg, ks) * ATTN_SCALE, -jnp.inf)
        s_new = jnp.sum(qg.astype(F32) * k_new.astype(F32), axis=-1, keepdims=True) * ATTN_SCALE
        s_new = jnp.where(new_ok, s_new, -jnp.inf)
        m = jnp.maximum(jnp.max(s, axis=-1, keepdims=True), s_new)
        p = jnp.exp(s - m)
        p_new = jnp.exp(s_new - m)
        denom = jnp.sum(p, axis=-1, keepdims=True) + p_new
        pn = (p / denom).astype(BF16)
        pn_new = (p_new / denom).astype(BF16).astype(F32)
        o = jnp.dot(pn, vs, preferred_element_type=F32) + pn_new * v_new.astype(F32)
        o_ref[0, g * Q_PER_KV:(g + 1) * Q_PER_KV, :] = o


def _sample_attend(idx, n_pool, page_table, q_s, kvi_s16, pool_k, pool_v):
    db = page_table.shape[0]
    grid_spec = pltpu.PrefetchScalarGridSpec(
        num_scalar_prefetch=3,
        grid=(db,),
        in_specs=[
            pl.BlockSpec((1, N_HEADS, HEAD_DIM), lambda b, *_: (b, 0, 0)),
            pl.BlockSpec((1, IDX_HEADS, KVI_WIDTH), lambda b, *_: (b, 0, 0)),
            pl.BlockSpec(memory_space=pl.ANY),
            pl.BlockSpec(memory_space=pl.ANY),
        ],
        out_specs=pl.BlockSpec((1, N_HEADS, HEAD_DIM), lambda b, *_: (b, 0, 0)),
        scratch_shapes=[
            pltpu.VMEM((TOPK_MAX, N_KV_HEADS, HEAD_DIM), F32),
            pltpu.VMEM((TOPK_MAX, N_KV_HEADS, HEAD_DIM), F32),
            pltpu.VMEM((N_KV_HEADS, TOPK_MAX, HEAD_DIM), F32),
            pltpu.VMEM((N_KV_HEADS, TOPK_MAX, HEAD_DIM), F32),
            pltpu.SemaphoreType.DMA((4,)),
        ],
    )
    return pl.pallas_call(
        _sample_attend_kernel,
        grid_spec=grid_spec,
        out_shape=jax.ShapeDtypeStruct((db, N_HEADS, HEAD_DIM), F32),
        compiler_params=pltpu.CompilerParams(dimension_semantics=("arbitrary",)),
        name="sample_attend",
    )(idx, n_pool, page_table, q_s, kvi_s16, pool_k, pool_v)


def _merge_kernel(chunked, oa_ref, za_ref, u_ref, v_ref, zb_ref, ga_ref, gb_ref, x_ref,
                  wpa_ref, wpb_ref, wout_ref, lng_ref, lnb_ref, ws_ref, bs_ref, gf_ref, y_ref, *rest):
    vn_ref, ob_ref = (None, rest[0]) if chunked else rest
    tm = x_ref.shape[0]
    u = _gelu(u_ref[...])
    v = _gelu(v_ref[...])
    mu = jnp.mean(v, axis=-1, keepdims=True)
    vc = v - mu
    vn = (vc * lax.rsqrt(jnp.mean(vc * vc, axis=-1, keepdims=True) + LN_EPS)) * lng_ref[...] + lnb_ref[...]
    if vn_ref is not None:
        vn_ref[...] = vn

    if chunked:
        vnb = vn.astype(BF16)
        ri = lax.broadcasted_iota(I32, (CHUNK, CHUNK), 0)
        ci = lax.broadcasted_iota(I32, (CHUNK, CHUNK), 1)
        for g in range(GMLP_GROUPS):
            wm = jnp.where(ri >= ci, ws_ref[g], 0.0).astype(BF16)
            lo = g * GMLP_GROUP_DIM
            for c in range(tm // CHUNK):
                mixed = jnp.dot(wm, vnb[c * CHUNK:(c + 1) * CHUNK, lo:lo + GMLP_GROUP_DIM],
                                preferred_element_type=F32) + bs_ref[:, g:g + 1]
                ob_ref[c * CHUNK:(c + 1) * CHUNK, lo:lo + GMLP_GROUP_DIM] = (
                    u[c * CHUNK:(c + 1) * CHUNK, lo:lo + GMLP_GROUP_DIM] * mixed)
    else:
        ob_ref[...] = u * (ws_ref[...] * vn + bs_ref[...])

    ha = (oa_ref[...] * _silu(za_ref[...])).astype(BF16)
    hb = (ob_ref[...] * _silu(zb_ref[...])).astype(BF16)
    ya = jnp.dot(ha, wpa_ref[...], preferred_element_type=F32)
    yb = jnp.dot(hb, wpb_ref[...], preferred_element_type=F32)
    mix = jax.nn.sigmoid(ga_ref[...]) * ya + jax.nn.sigmoid(gb_ref[...]) * yb
    out = x_ref[...] + jnp.dot(mix.astype(BF16), wout_ref[...], preferred_element_type=F32)
    ms = jnp.mean(out * out, axis=-1, keepdims=True)
    y_ref[...] = (out * lax.rsqrt(ms + RMS_EPS)) * gf_ref[...]


def _merge(chunked, o_a, gate, x2d, wpa, wpb, wout, lng, lnb, ws, bs, gf, tm):
    rows = x2d.shape[0]
    gw = GMLP_WIDTH

    def const(shape):
        return pl.BlockSpec(shape, lambda i: (0,) * len(shape), pipeline_mode=pl.Buffered(1))

    return pl.pallas_call(
        functools.partial(_merge_kernel, chunked),
        grid=(rows // tm,),
        in_specs=[
            pl.BlockSpec((tm, ATTN_WIDTH), lambda i: (i, 0)),
            pl.BlockSpec((tm, gw), lambda i: (i, 0)),
            pl.BlockSpec((tm, gw), lambda i: (i, 1)),
            pl.BlockSpec((tm, gw), lambda i: (i, 2)),
            pl.BlockSpec((tm, gw), lambda i: (i, 3)),
            pl.BlockSpec((tm, D_MODEL), lambda i: (i, 2)),
            pl.BlockSpec((tm, D_MODEL), lambda i: (i, 3)),
            pl.BlockSpec((tm, D_MODEL), lambda i: (i, 0)),
            const(wpa.shape), const(wpb.shape), const(wout.shape),
            const(lng.shape), const(lnb.shape), const(ws.shape), const(bs.shape), const(gf.shape),
        ],
        out_specs=[pl.BlockSpec((tm, D_MODEL), lambda i: (i, 0))]
        + ([] if chunked else [pl.BlockSpec((tm, gw), lambda i: (i, 0))]),
        out_shape=[jax.ShapeDtypeStruct((rows, D_MODEL), F32)]
        + ([] if chunked else [jax.ShapeDtypeStruct((rows, gw), F32)]),
        scratch_shapes=[pltpu.VMEM((tm, gw), F32)],
        compiler_params=pltpu.CompilerParams(
            dimension_semantics=("arbitrary",), vmem_limit_bytes=VMEM_LIMIT),
        name="merge_prompt" if chunked else "merge_sample",
    )(o_a, gate, gate, gate, gate, gate, gate, x2d, wpa, wpb, wout, lng, lnb, ws, bs, gf)


def kernel(x_prompt, x_sample, cache_k, cache_v, cache_k_idx, page_table, norm_in_g, w_in,
           w_proj_a, w_proj_b, w_out, ln_g, ln_b, w_spatial, b_spatial, norm_f_g):
    depth = w_in.shape[0]
    assert depth == 1, "single trunk layer"
    batch, seq, _ = x_prompt.shape
    db, dseq, _ = x_sample.shape
    assert dseq == 1
    l = 0
    wpa = w_proj_a[l].astype(BF16)
    wpb = w_proj_b[l].astype(BF16)
    wout = w_out[l].astype(BF16)
    g_in = norm_in_g[l].reshape(1, D_MODEL)
    lng = ln_g[l].reshape(1, GMLP_WIDTH)
    lnb = ln_b[l].reshape(1, GMLP_WIDTH)
    gf = norm_f_g.reshape(1, D_MODEL)

    xp = x_prompt.reshape(batch * seq, D_MODEL)
    wt = w_in[l].T
    xn_p, kv_p, ki_p2, wi_t, k3_p, v3_p = _inproj_kvi(xp, g_in, wt, PROJ_TOKEN_TILE)
    h_t = _inproj(xn_p, wt, COL_Q, COL_WI, COL_WI // 4, PROJ_TOKEN_TILE, BF16, True)
    gate_p = _inproj(xn_p, wt, GATE_COL0, GATE_WIDTH, GATE_FEATURE_TILE, PROJ_TOKEN_TILE, F32, False)
    oa_p = _prompt_attn(h_t, wi_t, kv_p, ki_p2, batch, seq)
    bs_t = b_spatial[l].T
    (y_p,) = _merge(True, oa_p, gate_p, xp, wpa, wpb, wout, lng, lnb, w_spatial[l], bs_t, gf, MERGE_TOKEN_TILE)

    xs = x_sample.reshape(db, D_MODEL)
    xn_s, kv_s, ki_s2, wi_t_s, _, _ = _inproj_kvi(xs, g_in, wt, db)
    h_s = _inproj(xn_s, wt, COL_Q, COL_WI, 512, db, F32, False)
    gate_s = _inproj(xn_s, wt, GATE_COL0, GATE_WIDTH, GATE_FEATURE_TILE, db, F32, False)
    kvi_s = jnp.concatenate(
        [kv_s, ki_s2, wi_t_s.T, jnp.zeros((db, KVI_WIDTH - KVI_WI - IDX_HEADS), F32)], axis=1)
    q_s = h_s[:, COL_Q:COL_K].astype(BF16).reshape(db, N_HEADS, HEAD_DIM)
    qi_s = h_s[:, COL_QI:COL_WI].astype(BF16).reshape(db, IDX_HEADS, IDX_DIM)
    wi_col = kvi_s[:, KVI_WI:KVI_WI + IDX_HEADS].reshape(db, IDX_HEADS, 1)
    kvi_s16 = jnp.broadcast_to(kvi_s[:, None, :], (db, IDX_HEADS, KVI_WIDTH))
    kvi_s16 = lax.dynamic_update_slice(kvi_s16, wi_col, (0, 0, KVI_WI))
    idx, meta = _sample_select(page_table, qi_s, kvi_s16, cache_k_idx)
    oa_s = _sample_attend(idx.reshape(db, TOPK_MAX), meta[:, 0], page_table, q_s, kvi_s16,
                          cache_k, cache_v).reshape(db, ATTN_WIDTH)
    ws0 = jnp.repeat(w_spatial[l][:, 0, 0], GMLP_GROUP_DIM).reshape(1, GMLP_WIDTH)
    bs0 = jnp.repeat(b_spatial[l][:, 0], GMLP_GROUP_DIM).reshape(1, GMLP_WIDTH)
    y_s, vn_s = _merge(False, oa_s, gate_s, xs, wpa, wpb, wout, lng, lnb, ws0, bs0, gf, db)

    def kv_out(kvi, lead):
        k = kvi[:, KVI_K:KVI_K + KV_WIDTH].reshape((1,) + lead + (N_KV_HEADS, HEAD_DIM))
        v = kvi[:, KVI_V:KVI_V + KV_WIDTH].reshape((1,) + lead + (N_KV_HEADS, HEAD_DIM))
        ki = kvi[:, KVI_KI:KVI_KI + IDX_DIM].reshape((1,) + lead + (IDX_DIM,))
        return k, v, ki

    k_p = k3_p.reshape(1, batch, seq, N_KV_HEADS, HEAD_DIM)
    v_p = v3_p.reshape(1, batch, seq, N_KV_HEADS, HEAD_DIM)
    ki_p = ki_p2.reshape(1, batch, seq, IDX_DIM)
    k_s, v_s, ki_s = kv_out(kvi_s, (db, dseq))
    return (y_p.reshape(batch, seq, D_MODEL), y_s.reshape(db, dseq, D_MODEL),
            k_p, v_p, ki_p, k_s, v_s, ki_s, vn_s.reshape(1, db, dseq, GMLP_WIDTH))
```
